```python
import math
import jax, jax.numpy as jnp
from jax import lax
import numpy as np

D_MODEL = 2048
BATCH = 4
SEQ = 2048
DEPTH = 1
DEC_BATCH = 128
DEC_SEQ = 4
PAST_LEN = 16384
PAGE_SIZE = 128

D_PE = 256
D_MIX = D_MODEL
D_POOL = D_MIX // 2
POOL_WINDOWS = (2, 4, 8, 16)
N_POOL_GROUPS = len(POOL_WINDOWS)
D_POOL_G = D_POOL // N_POOL_GROUPS
POOL_BUF = max(POOL_WINDOWS) - 1
D_SGU = D_MIX - D_POOL
CHUNK = 128
SGU_HEAD_DIM = 128
N_SGU_HEADS = D_SGU // SGU_HEAD_DIM
N_EXPERTS = 64
TOP_K = 8
N_EXPERT_GROUPS = 8
TOPK_GROUPS = 4
D_EXPERT = D_MODEL // 4
D_SHARED = D_EXPERT
ROUTE_SCALE = 2.5
LN_EPS = 1e-5
DEEPNORM_ALPHA = (2.0 * DEPTH) ** 0.25
DEEPNORM_BETA = (8.0 * DEPTH) ** -0.25

kernel_name = 'pool_sgu_moe_hybrid_decode_step'


def layer_norm(x, g, b):
    xf = x.astype(jnp.float32)
    mu = jnp.mean(xf, axis=-1, keepdims=True)
    xc = xf - mu
    var = jnp.mean(xc * xc, axis=-1, keepdims=True)
    return (xc * lax.rsqrt(var + LN_EPS) * g + b).astype(x.dtype)


def pool_mixer(a, buf, pos, w_pool, pool_scale):
    B, T, _ = a.shape
    z = jnp.concatenate([buf.astype(a.dtype), a], axis=1).astype(jnp.float32)
    cz = jnp.concatenate([jnp.zeros((B, 1, D_POOL), jnp.float32), jnp.cumsum(z, axis=1)], axis=1)
    end = cz[:, POOL_BUF + 1:]
    cur = z[:, POOL_BUF:]
    diffs = []
    for g, w in enumerate(POOL_WINDOWS):
        c0, c1 = g * D_POOL_G, (g + 1) * D_POOL_G
        start = cz[:, POOL_BUF + 1 - w: POOL_BUF + 1 - w + T, c0:c1]
        cnt = jnp.minimum(pos + 1, w).astype(jnp.float32)[None, :, None]
        diffs.append((end[..., c0:c1] - start) / cnt - cur[..., c0:c1])
    d = jnp.stack(diffs, axis=2)
    y = jnp.einsum('btgc,gce->btge', d, w_pool.astype(jnp.float32)).reshape(B, T, D_POOL)
    return (y * pool_scale).astype(a.dtype)


def sgu_mixer(u, v, sgu_ln_g, sgu_ln_b, w_s, b_s):
    B, T, H, Dh = v.shape
    L = min(T, CHUNK)
    n_chunks = T // L
    vn = layer_norm(v, sgu_ln_g, sgu_ln_b)
    ws = w_s[:, :L, :L] * jnp.tril(jnp.ones((L, L), w_s.dtype))
    vc = vn.reshape(B, n_chunks, L, H, Dh)
    mixed = jnp.einsum('hts,bcshd->bcthd', ws, vc) + b_s[:, :L].T[None, None, :, :, None]
    out = u * mixed.reshape(B, T, H, Dh)
    return out.reshape(B, T, H * Dh), vn


def moe(h, router_w, router_bias, w_gate, w_up, w_down, ws_gate, ws_up, ws_down):
    B, T, D = h.shape
    N = B * T
    hf = h.reshape(N, D)
    scores = jax.nn.sigmoid((hf @ router_w).astype(jnp.float32))
    biased = scores + router_bias.astype(jnp.float32)
    per_group = N_EXPERTS // N_EXPERT_GROUPS
    grp_score = lax.top_k(biased.reshape(N, N_EXPERT_GROUPS, per_group), 2)[0].sum(-1)
    _, top_g = lax.top_k(grp_score, TOPK_GROUPS)
    gmask = jax.nn.one_hot(top_g, N_EXPERT_GROUPS, dtype=jnp.float32).sum(1)
    emask = jnp.repeat(gmask, per_group, axis=1)
    _, top_e = lax.top_k(jnp.where(emask > 0, biased, -jnp.inf), TOP_K)
    sel = jnp.take_along_axis(scores, top_e, axis=1)
    wts = sel / jnp.sum(sel, axis=-1, keepdims=True) * ROUTE_SCALE
    combine = jnp.einsum('nk,nke->ne', wts,
                         jax.nn.one_hot(top_e, N_EXPERTS, dtype=jnp.float32)).astype(h.dtype)
    g = jnp.einsum('nd,edf->nef', hf, w_gate)
    u = jnp.einsum('nd,edf->nef', hf, w_up)
    act = jax.nn.silu(g) * u * combine[:, :, None]
    routed = jnp.einsum('nef,efd->nd', act, w_down)
    shared = (jax.nn.silu(hf @ ws_gate) * (hf @ ws_up)) @ ws_down
    return (routed + shared).reshape(B, T, D)


def decoder_layer(x, p, buf, pos, w_in, w_pool, pool_scale, sgu_ln_g, sgu_ln_b, w_s, b_s, w_out,
                  ln1_g, ln1_b, router_w, router_bias, w_gate, w_up, w_down, ws_gate, ws_up, ws_down,
                  ln2_g, ln2_b, w_pe, w_pgate, b_pgate):
    B, T, _ = x.shape
    proj = x @ w_in
    a = proj[..., :D_POOL]
    u = jax.nn.gelu(proj[..., D_POOL:D_POOL + D_SGU]).reshape(B, T, N_SGU_HEADS, SGU_HEAD_DIM)
    v = jax.nn.gelu(proj[..., D_POOL + D_SGU:]).reshape(B, T, N_SGU_HEADS, SGU_HEAD_DIM)
    pool_out = pool_mixer(a, buf, pos, w_pool, pool_scale)
    sgu_out, vn = sgu_mixer(u, v, sgu_ln_g, sgu_ln_b, w_s, b_s)
    mix = jnp.concatenate([pool_out, sgu_out], axis=-1) @ w_out
    h = layer_norm(DEEPNORM_ALPHA * x + mix, ln1_g, ln1_b)
    ffn = moe(h, router_w, router_bias, w_gate, w_up, w_down, ws_gate, ws_up, ws_down)
    h = layer_norm(DEEPNORM_ALPHA * h + ffn, ln2_g, ln2_b)
    h = h + jax.nn.sigmoid(h @ w_pgate + b_pgate) * (p @ w_pe)
    new_buf = jnp.concatenate([buf.astype(a.dtype), a], axis=1)[:, -POOL_BUF:]
    return h, new_buf, vn


def setup_inputs(seed: int = 0) -> dict:
    key = jax.random.key(seed)
    ks = jax.random.split(key, 32)

    def nrm(k, shape, scale):
        return jax.random.normal(k, shape, jnp.float32) * scale

    d_in = D_POOL + 2 * D_SGU
    return {
        'x_prompt': nrm(ks[0], (BATCH, SEQ, D_MODEL), 1.0),
        'x_sample': nrm(ks[1], (DEC_BATCH, DEC_SEQ, D_MODEL), 1.0),
        'state_pool': nrm(ks[2], (DEPTH, DEC_BATCH, POOL_BUF, D_POOL), 1.0),
        'p_prompt': nrm(ks[3], (DEPTH, BATCH, SEQ, D_PE), 1.0),
        'p_sample': nrm(ks[4], (DEPTH, DEC_BATCH, DEC_SEQ, D_PE), 1.0),
        'w_in': nrm(ks[5], (DEPTH, D_MODEL, d_in), D_MODEL ** -0.5),
        'w_pool': nrm(ks[6], (DEPTH, N_POOL_GROUPS, D_POOL_G, D_POOL_G), D_POOL_G ** -0.5),
        'pool_scale': 1.0 + nrm(ks[7], (DEPTH, D_POOL), 0.1),
        'sgu_ln_g': 1.0 + nrm(ks[8], (DEPTH, N_SGU_HEADS, SGU_HEAD_DIM), 0.05),
        'sgu_ln_b': nrm(ks[9], (DEPTH, N_SGU_HEADS, SGU_HEAD_DIM), 0.02),
        'w_s': nrm(ks[10], (DEPTH, N_SGU_HEADS, CHUNK, CHUNK), CHUNK ** -0.5),
        'b_s': 1.0 + nrm(ks[11], (DEPTH, N_SGU_HEADS, CHUNK), 0.1),
        'w_out': nrm(ks[12], (DEPTH, D_MIX, D_MODEL), DEEPNORM_BETA * D_MIX ** -0.5),
        'ln1_g': 1.0 + nrm(ks[13], (DEPTH, D_MODEL), 0.05),
        'ln1_b': nrm(ks[14], (DEPTH, D_MODEL), 0.02),
        'router_w': nrm(ks[15], (DEPTH, D_MODEL, N_EXPERTS), D_MODEL ** -0.5),
        'router_bias': nrm(ks[16], (DEPTH, N_EXPERTS), 0.01),
        'w_gate': nrm(ks[17], (DEPTH, N_EXPERTS, D_MODEL, D_EXPERT), D_MODEL ** -0.5),
        'w_up': nrm(ks[18], (DEPTH, N_EXPERTS, D_MODEL, D_EXPERT), D_MODEL ** -0.5),
        'w_down': nrm(ks[19], (DEPTH, N_EXPERTS, D_EXPERT, D_MODEL), DEEPNORM_BETA * D_EXPERT ** -0.5),
        'ws_gate': nrm(ks[20], (DEPTH, D_MODEL, D_SHARED), D_MODEL ** -0.5),
        'ws_up': nrm(ks[21], (DEPTH, D_MODEL, D_SHARED), D_MODEL ** -0.5),
        'ws_down': nrm(ks[22], (DEPTH, D_SHARED, D_MODEL), DEEPNORM_BETA * D_SHARED ** -0.5),
        'ln2_g': 1.0 + nrm(ks[23], (DEPTH, D_MODEL), 0.05),
        'ln2_b': nrm(ks[24], (DEPTH, D_MODEL), 0.02),
        'w_pe': nrm(ks[25], (DEPTH, D_PE, D_MODEL), D_PE ** -0.5),
        'w_pgate': nrm(ks[26], (DEPTH, D_MODEL, D_MODEL), D_MODEL ** -0.5),
        'b_pgate': nrm(ks[27], (DEPTH, D_MODEL), 0.02),
    }


def reference(x_prompt, x_sample, state_pool, p_prompt, p_sample, w_in, w_pool, pool_scale, sgu_ln_g,
              sgu_ln_b, w_s, b_s, w_out, ln1_g, ln1_b, router_w, router_bias, w_gate, w_up, w_down,
              ws_gate, ws_up, ws_down, ln2_g, ln2_b, w_pe, w_pgate, b_pgate):
    pos_prompt = jnp.arange(x_prompt.shape[1], dtype=jnp.int32)
    pos_sample = PAST_LEN + jnp.arange(x_sample.shape[1], dtype=jnp.int32)
    buf_prompt = jnp.zeros((x_prompt.shape[0], POOL_BUF, D_POOL), x_prompt.dtype)
    hp, hs = x_prompt, x_sample
    pool_p, pool_s, v_s = [], [], []
    for i in range(DEPTH):
        lw = (w_in[i], w_pool[i], pool_scale[i], sgu_ln_g[i], sgu_ln_b[i], w_s[i], b_s[i], w_out[i],
              ln1_g[i], ln1_b[i], router_w[i], router_bias[i], w_gate[i], w_up[i], w_down[i],
              ws_gate[i], ws_up[i], ws_down[i], ln2_g[i], ln2_b[i], w_pe[i], w_pgate[i], b_pgate[i])
        hp, bp, _ = decoder_layer(hp, p_prompt[i], buf_prompt, pos_prompt, *lw)
        hs, bs, vs = decoder_layer(hs, p_sample[i], state_pool[i], pos_sample, *lw)
        pool_p.append(bp)
        pool_s.append(bs)
        v_s.append(vs)
    return (hp, hs, jnp.stack(pool_p), jnp.stack(pool_s), jnp.stack(v_s))
```

```python
import functools

import jax
import jax.numpy as jnp
import numpy as np
from jax import lax
from jax.experimental import pallas as pl
from jax.experimental.pallas import tpu as pltpu

F32 = jnp.float32
BF16 = jnp.bfloat16
I32 = jnp.int32

POOL_WINDOWS = (2, 4, 8, 16)
POOL_BUF = max(POOL_WINDOWS) - 1
CHUNK = 128
SGU_HEAD_DIM = 128
N_EXPERTS = 64
TOP_K = 8
N_EXPERT_GROUPS = 8
TOPK_GROUPS = 4
ROUTE_SCALE = 2.5
LN_EPS = 1e-5
PAST_LEN = 16384

TM = 256
HALO = 16
TB = 256
TR = 256
SEG = 16
KC = 256
VMEM_LIMIT = 56 * 1024 * 1024


def _const_spec(shape):
    n = len(shape)
    return pl.BlockSpec(shape, lambda *_: (0,) * n, pipeline_mode=pl.Buffered(1))


def _ln_rows(x, g, b):
    mu = jnp.mean(x, axis=-1, keepdims=True)
    xc = x - mu
    var = jnp.mean(xc * xc, axis=-1, keepdims=True)
    return xc * lax.rsqrt(var + LN_EPS) * g + b


def _dot(a, b):
    return jnp.dot(a, b, preferred_element_type=F32)


def _mixer_prompt_body(alpha, n_tiles, x_ref, win_ref, band_ref, wpool_ref, pscale_ref, sg_ref, sb_ref,
                       ws_ref, bsb_ref, wout_ref, g1_ref, b1_ref,
                       h1_ref, h1b_ref, pool_ref, zcat_ref, mixin_ref):
    j = pl.program_id(1)
    d_pool = zcat_ref.shape[1]
    d_pool_g = d_pool // len(POOL_WINDOWS)
    d_sgu = mixin_ref.shape[1] - d_pool

    @pl.when(j == 0)
    def _():
        zcat_ref[0:HALO, :] = jnp.zeros((HALO, d_pool), BF16)

    x = x_ref[0]
    proj = _dot(x.astype(BF16), win_ref[...])
    a = proj[:, :d_pool]
    zcat_ref[HALO:, :] = a.astype(BF16)

    pos1 = (j * TM + lax.broadcasted_iota(I32, (TM, 1), 0) + 1).astype(F32)
    for g, w in enumerate(POOL_WINDOWS):
        c0, c1 = g * d_pool_g, (g + 1) * d_pool_g
        s = _dot(band_ref[g], zcat_ref[:, c0:c1])
        cnt = jnp.minimum(pos1, float(w))
        d = s / cnt - a[:, c0:c1]
        y = _dot(d.astype(BF16), wpool_ref[g])
        mixin_ref[:, c0:c1] = (y * pscale_ref[:, c0:c1]).astype(BF16)
    zcat_ref[0:HALO, :] = zcat_ref[TM:TM + HALO, :]

    @pl.when(j == n_tiles - 1)
    def _():
        pool_ref[0] = a[TM - HALO:, :]

    u = jax.nn.gelu(proj[:, d_pool:d_pool + d_sgu])
    v = jax.nn.gelu(proj[:, d_pool + d_sgu:])
    r = lax.broadcasted_iota(I32, (CHUNK, CHUNK), 0)
    c = lax.broadcasted_iota(I32, (CHUNK, CHUNK), 1)
    tril = r >= c
    for h in range(d_sgu // SGU_HEAD_DIM):
        l0, l1 = h * SGU_HEAD_DIM, (h + 1) * SGU_HEAD_DIM
        vn = _ln_rows(v[:, l0:l1], sg_ref[h:h + 1, :], sb_ref[h:h + 1, :]).astype(BF16)
        wst = jnp.where(tril, ws_ref[h], 0.0).astype(BF16)
        for ci in range(TM // CHUNK):
            r0, r1 = ci * CHUNK, (ci + 1) * CHUNK
            mixed = _dot(wst, vn[r0:r1]) + bsb_ref[h]
            mixin_ref[r0:r1, d_pool + l0:d_pool + l1] = (u[r0:r1, l0:l1] * mixed).astype(BF16)

    mix = _dot(mixin_ref[...], wout_ref[...])
    h1 = _ln_rows(alpha * x + mix, g1_ref[...], b1_ref[...])
    h1_ref[...] = h1
    h1b_ref[...] = h1.astype(BF16)


def _pool_band(tm):
    t = np.arange(tm)[:, None] + HALO
    s = np.arange(tm + HALO)[None, :]
    return jnp.asarray(np.stack([((s <= t) & (s >= t - w + 1)) for w in POOL_WINDOWS]).astype(np.float32), BF16)


def _mixer_prompt(x, win_b, wpool_b, pscale, sg, sb, ws, bsb, wout_b, g1, b1, n_rows_total, alpha):
    B, T, D = x.shape
    d_in = win_b.shape[1]
    d_pool = wpool_b.shape[0] * wpool_b.shape[1]
    n_tiles = T // TM
    body = functools.partial(_mixer_prompt_body, alpha, n_tiles)
    return pl.pallas_call(
        body,
        grid=(B, n_tiles),
        in_specs=[
            pl.BlockSpec((1, TM, D), lambda b, j: (b, j, 0)),
            _const_spec((D, d_in)),
            _const_spec((len(POOL_WINDOWS), TM, TM + HALO)),
            _const_spec(wpool_b.shape),
            _const_spec(pscale.shape),
            _const_spec(sg.shape),
            _const_spec(sb.shape),
            _const_spec(ws.shape),
            _const_spec(bsb.shape),
            _const_spec(wout_b.shape),
            _const_spec(g1.shape),
            _const_spec(b1.shape),
        ],
        out_specs=[
            pl.BlockSpec((TM, D), lambda b, j: (b * n_tiles + j, 0)),
            pl.BlockSpec((TM, D), lambda b, j: (b * n_tiles + j, 0)),
            pl.BlockSpec((1, HALO, d_pool), lambda b, j: (b, 0, 0)),
        ],
        out_shape=[
            jax.ShapeDtypeStruct((n_rows_total, D), F32),
            jax.ShapeDtypeStruct((n_rows_total, D), BF16),
            jax.ShapeDtypeStruct((B, HALO, d_pool), F32),
        ],
        scratch_shapes=[
            pltpu.VMEM((TM + HALO, d_pool), BF16),
            pltpu.VMEM((TM, D), BF16),
        ],
        compiler_params=pltpu.CompilerParams(
            dimension_semantics=("arbitrary", "arbitrary"), vmem_limit_bytes=VMEM_LIMIT),
        name="mixer_prompt",
    )(x, win_b, _pool_band(TM), wpool_b, pscale, sg, sb, ws, bsb, wout_b, g1, b1)


def _mixer_sample_body(alpha, x_ref, st_ref, win_ref, wpool_ref, pscale_ref, sg_ref, sb_ref,
                       wrow_ref, brow_ref, wout_ref, g1_ref, b1_ref, h1_in, h1b_in,
                       h1_ref, h1b_ref, a_ref, vn_ref, dbuf_ref, mixin_ref):
    del h1_in, h1b_in
    T, NB, D = x_ref.shape
    d_pool = dbuf_ref.shape[1]
    d_pool_g = d_pool // len(POOL_WINDOWS)
    d_sgu = mixin_ref.shape[1] - d_pool

    x = x_ref[...].reshape(T * NB, D)
    proj = _dot(x.astype(BF16), win_ref[...])
    a = proj[:, :d_pool]
    for t in range(T):
        a_ref[t] = a[t * NB:(t + 1) * NB]

    for g, w in enumerate(POOL_WINDOWS):
        c0, c1 = g * d_pool_g, (g + 1) * d_pool_g
        for t in range(T):
            cnt = float(min(PAST_LEN + t + 1, w))
            acc = None
            for s in range(POOL_BUF + t - w + 1, POOL_BUF + t + 1):
                if s < POOL_BUF:
                    term = st_ref[s, :, c0:c1]
                else:
                    term = a[(s - POOL_BUF) * NB:(s - POOL_BUF + 1) * NB, c0:c1]
                acc = term if acc is None else acc + term
            d = acc / cnt - a[t * NB:(t + 1) * NB, c0:c1]
            dbuf_ref[t * NB:(t + 1) * NB, c0:c1] = d.astype(BF16)
        y = _dot(dbuf_ref[:, c0:c1], wpool_ref[g])
        mixin_ref[:, c0:c1] = (y * pscale_ref[:, c0:c1]).astype(BF16)

    u = jax.nn.gelu(proj[:, d_pool:d_pool + d_sgu])
    v = jax.nn.gelu(proj[:, d_pool + d_sgu:])
    for h in range(d_sgu // SGU_HEAD_DIM):
        l0, l1 = h * SGU_HEAD_DIM, (h + 1) * SGU_HEAD_DIM
        vn = _ln_rows(v[:, l0:l1], sg_ref[h:h + 1, :], sb_ref[h:h + 1, :])
        for t in range(T):
            vn_ref[t, :, l0:l1] = vn[t * NB:(t + 1) * NB]
    for t in range(T):
        mixed = brow_ref[t:t + 1, :]
        for s in range(t + 1):
            mixed = mixed + wrow_ref[t, s:s + 1, :] * vn_ref[s]
        mixin_ref[t * NB:(t + 1) * NB, d_pool:] = (u[t * NB:(t + 1) * NB] * mixed).astype(BF16)

    mix = _dot(mixin_ref[...], wout_ref[...])
    h1 = _ln_rows(alpha * x + mix, g1_ref[...], b1_ref[...])
    h1_ref[...] = h1
    h1b_ref[...] = h1.astype(BF16)


def _mixer_sample(x_t, st_t, win_b, wpool_b, pscale, sg, sb, wrow, brow, wout_b, g1, b1, h1_all, h1b_all,
                  row0, alpha):
    T, NS, D = x_t.shape
    d_pool = st_t.shape[2]
    d_sgu = wrow.shape[2]
    NB = TB // T
    n_steps = NS // NB
    blk0 = row0 // TB
    body = functools.partial(_mixer_sample_body, alpha)
    return pl.pallas_call(
        body,
        grid=(n_steps,),
        in_specs=[
            pl.BlockSpec((T, NB, D), lambda i: (0, i, 0)),
            pl.BlockSpec((POOL_BUF, NB, d_pool), lambda i: (0, i, 0)),
            _const_spec(win_b.shape),
            _const_spec(wpool_b.shape),
            _const_spec(pscale.shape),
            _const_spec(sg.shape),
            _const_spec(sb.shape),
            _const_spec(wrow.shape),
            _const_spec(brow.shape),
            _const_spec(wout_b.shape),
            _const_spec(g1.shape),
            _const_spec(b1.shape),
            pl.BlockSpec(memory_space=pl.ANY),
            pl.BlockSpec(memory_space=pl.ANY),
        ],
        out_specs=[
            pl.BlockSpec((TB, D), lambda i: (blk0 + i, 0)),
            pl.BlockSpec((TB, D), lambda i: (blk0 + i, 0)),
            pl.BlockSpec((T, NB, d_pool), lambda i: (0, i, 0)),
            pl.BlockSpec((T, NB, d_sgu), lambda i: (0, i, 0)),
        ],
        out_shape=[
            jax.ShapeDtypeStruct(h1_all.shape, F32),
            jax.ShapeDtypeStruct(h1b_all.shape, BF16),
            jax.ShapeDtypeStruct((T, NS, d_pool), F32),
            jax.ShapeDtypeStruct((T, NS, d_sgu), F32),
        ],
        scratch_shapes=[
            pltpu.VMEM((TB, d_pool), BF16),
            pltpu.VMEM((TB, D), BF16),
        ],
        input_output_aliases={12: 0, 13: 1},
        compiler_params=pltpu.CompilerParams(
            dimension_semantics=("arbitrary",), vmem_limit_bytes=VMEM_LIMIT),
        name="mixer_sample",
    )(x_t, st_t, win_b, wpool_b, pscale, sg, sb, wrow, brow, wout_b, g1, b1, h1_all, h1b_all)


def _router_body(h_ref, rwt_ref, rb_ref, upper_ref, lstrict_ref, dest_ref, wsel_ref, cnt_ref):
    per_group = N_EXPERTS // N_EXPERT_GROUPS
    neg = -jnp.inf
    logits_t = lax.dot_general(rwt_ref[...], h_ref[...], (((1,), (1,)), ((), ())),
                               precision=lax.Precision.HIGHEST, preferred_element_type=F32)
    s_t = jax.nn.sigmoid(logits_t)
    b_t = s_t + rb_ref[...]

    io_g = lax.broadcasted_iota(I32, (per_group, TB), 0)
    gs = []
    for g in range(N_EXPERT_GROUPS):
        xg = b_t[g * per_group:(g + 1) * per_group, :]
        m1 = jnp.max(xg, axis=0, keepdims=True)
        i1 = jnp.min(jnp.where(xg == m1, io_g, per_group), axis=0, keepdims=True)
        m2 = jnp.max(jnp.where(io_g == i1, neg, xg), axis=0, keepdims=True)
        gs.append(m1 + m2)
    masked = []
    for g in range(N_EXPERT_GROUPS):
        rank = jnp.zeros((1, TB), F32)
        for g2 in range(N_EXPERT_GROUPS):
            if g2 == g:
                continue
            ahead = (gs[g2] > gs[g]) | ((gs[g2] == gs[g]) & (g2 < g)) if g2 < g else (gs[g2] > gs[g])
            rank = rank + jnp.where(ahead, 1.0, 0.0)
        keep = rank < float(TOPK_GROUPS)
        masked.append(jnp.where(keep, b_t[g * per_group:(g + 1) * per_group, :], neg))
    xm = jnp.concatenate(masked, axis=0)

    io_e = lax.broadcasted_iota(I32, (N_EXPERTS, TB), 0)
    onehots = []
    sel = jnp.zeros((N_EXPERTS, TB), F32)
    for _ in range(TOP_K):
        m = jnp.max(xm, axis=0, keepdims=True)
        idx = jnp.min(jnp.where(xm == m, io_e, N_EXPERTS), axis=0, keepdims=True)
        oh = io_e == idx
        onehots.append(oh)
        sel = jnp.where(oh, 1.0, sel)
        xm = jnp.where(oh, neg, xm)

    ssel = sel * s_t
    denom = jnp.sum(ssel, axis=0, keepdims=True)
    comb = ssel / denom * ROUTE_SCALE

    rank_t = _dot(sel.astype(BF16), upper_ref[...])
    cnt = jnp.sum(sel, axis=1, keepdims=True)
    cnt_i = cnt.astype(I32)
    seg16 = ((cnt_i + (SEG - 1)) // SEG).astype(F32)
    off16 = _dot(lstrict_ref[...], jnp.broadcast_to(seg16, (N_EXPERTS, 128)).astype(BF16))
    d_t = rank_t + off16[:, 0:1] * float(SEG)
    for k in range(TOP_K):
        dest_ref[0, k:k + 1, :] = jnp.sum(jnp.where(onehots[k], d_t, 0.0), axis=0, keepdims=True).astype(I32)
        wsel_ref[0, k:k + 1, :] = jnp.sum(jnp.where(onehots[k], comb, 0.0), axis=0, keepdims=True)
    cnt_ref[0] = jnp.broadcast_to(cnt_i, (N_EXPERTS, 128))


def _router(h1_all, rwt, rb_col):
    n, D = h1_all.shape
    nb = n // TB
    upper = jnp.asarray(np.triu(np.ones((TB, TB), np.float32), 1), BF16)
    lstrict = jnp.asarray(np.tril(np.ones((N_EXPERTS, N_EXPERTS), np.float32), -1), BF16)
    return pl.pallas_call(
        _router_body,
        grid=(nb,),
        in_specs=[
            pl.BlockSpec((TB, D), lambda i: (i, 0)),
            _const_spec(rwt.shape),
            _const_spec(rb_col.shape),
            _const_spec(upper.shape),
            _const_spec(lstrict.shape),
        ],
        out_specs=[
            pl.BlockSpec((1, TOP_K, TB), lambda i: (i, 0, 0)),
            pl.BlockSpec((1, TOP_K, TB), lambda i: (i, 0, 0)),
            pl.BlockSpec((1, N_EXPERTS, 128), lambda i: (i, 0, 0)),
        ],
        out_shape=[
            jax.ShapeDtypeStruct((nb, TOP_K, TB), I32),
            jax.ShapeDtypeStruct((nb, TOP_K, TB), F32),
            jax.ShapeDtypeStruct((nb, N_EXPERTS, 128), I32),
        ],
        compiler_params=pltpu.CompilerParams(dimension_semantics=("arbitrary",)),
        name="router",
    )(h1_all, rwt, rb_col, upper, lstrict)


def _seg_copies(b, seg_g, seg_l, seg_n, make_copy, wait):
    def per_expert(e, carry):
        idx = b * N_EXPERTS + e
        g0, l0, n = seg_g[idx], seg_l[idx], seg_n[idx]

        def per_granule(j, c):
            cp = make_copy(pl.multiple_of(l0 + j * SEG, SEG), pl.multiple_of(g0 + j * SEG, SEG))
            if wait:
                cp.wait()
            else:
                cp.start()
            return c

        lax.fori_loop(0, n, per_granule, 0)
        return carry

    lax.fori_loop(0, N_EXPERTS, per_expert, 0)


def _dispatch_body(seg_g, seg_l, seg_n, nch, tail_g, tail_n, dest_ref, h_ref, xs_hbm, xloc_ref, zero_ref, sem):
    b = pl.program_id(0)
    nb = pl.num_programs(0)
    dest = dest_ref[0]
    h = h_ref[...]

    def chunk(c, carry):
        r0 = pl.multiple_of(c * KC, KC)
        rows = r0 + lax.broadcasted_iota(I32, (KC, TB), 0)
        s = jnp.zeros((KC, TB), F32)
        for k in range(TOP_K):
            s = s + jnp.where(dest[k:k + 1, :] == rows, 1.0, 0.0)
        xloc_ref[pl.ds(r0, KC), :] = _dot(s.astype(BF16), h).astype(BF16)
        return carry

    lax.fori_loop(0, nch[b], chunk, 0)

    def seg_copy(l, g):
        return pltpu.make_async_copy(xloc_ref.at[pl.ds(l, SEG), :], xs_hbm.at[pl.ds(g, SEG), :], sem)

    _seg_copies(b, seg_g, seg_l, seg_n, seg_copy, wait=False)

    def tail_copy(g):
        return pltpu.make_async_copy(zero_ref, xs_hbm.at[pl.ds(g, SEG), :], sem)

    def tails(wait):
        def per_expert(e, carry):
            def per_granule(j, c):
                cp = tail_copy(pl.multiple_of(tail_g[e] + j * SEG, SEG))
                if wait:
                    cp.wait()
                else:
                    cp.start()
                return c
            lax.fori_loop(0, tail_n[e], per_granule, 0)
            return carry
        lax.fori_loop(0, N_EXPERTS, per_expert, 0)

    @pl.when(b == nb - 1)
    def _():
        zero_ref[...] = jnp.zeros(zero_ref.shape, BF16)
        tails(False)

    _seg_copies(b, seg_g, seg_l, seg_n, seg_copy, wait=True)

    @pl.when(b == nb - 1)
    def _():
        tails(True)


def _dispatch(meta, dest, h1b_all, n_rows_sorted, m_out_max):
    n, D = h1b_all.shape
    nb = n // TB
    grid_spec = pltpu.PrefetchScalarGridSpec(
        num_scalar_prefetch=6,
        grid=(nb,),
        in_specs=[
            pl.BlockSpec((1, TOP_K, TB), lambda i, *_: (i, 0, 0)),
            pl.BlockSpec((TB, D), lambda i, *_: (i, 0)),
        ],
        out_specs=pl.BlockSpec(memory_space=pl.ANY),
        scratch_shapes=[
            pltpu.VMEM((m_out_max, D), BF16),
            pltpu.VMEM((SEG, D), BF16),
            pltpu.SemaphoreType.DMA(()),
        ],
    )
    return pl.pallas_call(
        _dispatch_body,
        grid_spec=grid_spec,
        out_shape=jax.ShapeDtypeStruct((n_rows_sorted, D), BF16),
        compiler_params=pltpu.CompilerParams(
            dimension_semantics=("arbitrary",), vmem_limit_bytes=VMEM_LIMIT, has_side_effects=True),
        name="dispatch",
    )(meta["seg_g"], meta["seg_l"], meta["seg_n"], meta["nch"], meta["tail_g"], meta["tail_n"], dest, h1b_all)


def _ffn_body(tile_e, tile_src, tile_valid, x_ref, wg_ref, wu_ref, wd_ref, y_ref, wgu_b, wd_b):
    i = pl.program_id(0)
    f = wg_ref.shape[2]
    e = tile_e[i]
    e_prev = tile_e[jnp.maximum(i - 1, 0)]

    @pl.when((i == 0) | (e != e_prev))
    def _():
        wgu_b[:, :f] = wg_ref[0].astype(BF16)
        wgu_b[:, f:] = wu_ref[0].astype(BF16)
        wd_b[...] = wd_ref[0].astype(BF16)

    @pl.when(tile_valid[i] == 1)
    def _():
        gu = _dot(x_ref[...], wgu_b[...])
        act = (jax.nn.silu(gu[:, :f]) * gu[:, f:]).astype(BF16)
        y_ref[...] = _dot(act, wd_b[...]).astype(BF16)


def _expert_ffn(meta, xs, w_gate, w_up, w_down):
    R, D = xs.shape
    E, _, f = w_gate.shape
    nt = R // TR
    grid_spec = pltpu.PrefetchScalarGridSpec(
        num_scalar_prefetch=3,
        grid=(nt,),
        in_specs=[
            pl.BlockSpec((TR, D), lambda i, te, ts, tv: (ts[i], 0)),
            pl.BlockSpec((1, D, f), lambda i, te, ts, tv: (te[i], 0, 0)),
            pl.BlockSpec((1, D, f), lambda i, te, ts, tv: (te[i], 0, 0)),
            pl.BlockSpec((1, f, D), lambda i, te, ts, tv: (te[i], 0, 0)),
        ],
        out_specs=pl.BlockSpec((TR, D), lambda i, te, ts, tv: (ts[i], 0)),
        scratch_shapes=[
            pltpu.VMEM((D, 2 * f), BF16),
            pltpu.VMEM((f, D), BF16),
        ],
    )
    return pl.pallas_call(
        _ffn_body,
        grid_spec=grid_spec,
        out_shape=jax.ShapeDtypeStruct((R, D), BF16),
        compiler_params=pltpu.CompilerParams(
            dimension_semantics=("arbitrary",), vmem_limit_bytes=VMEM_LIMIT),
        name="expert_ffn",
    )(meta["tile_e"], meta["tile_src"], meta["tile_valid"], xs, w_gate, w_up, w_down)


def _combine_body(alpha, seg_g, seg_l, seg_n, nch, dest_ref, wsel_ref, h1_ref, h1b_ref, p_ref,
                  wsg_ref, wsu_ref, wsd_ref, g2_ref, b2_ref, wpg_ref, bpg_ref, wpe_ref, ys_hbm,
                  y_ref, yloc_ref, acc_ref, sem):
    b = pl.program_id(0)
    n_chunks = nch[b]

    last0 = pl.multiple_of((n_chunks - 1) * KC, KC)
    yloc_ref[pl.ds(last0, KC), :] = jnp.zeros((KC, yloc_ref.shape[1]), BF16)

    def seg_copy(l, g):
        return pltpu.make_async_copy(ys_hbm.at[pl.ds(g, SEG), :], yloc_ref.at[pl.ds(l, SEG), :], sem)

    _seg_copies(b, seg_g, seg_l, seg_n, seg_copy, wait=False)

    hb = h1b_ref[...]
    sh = (jax.nn.silu(_dot(hb, wsg_ref[...])) * _dot(hb, wsu_ref[...])).astype(BF16)
    acc_ref[...] = _dot(sh, wsd_ref[...])

    _seg_copies(b, seg_g, seg_l, seg_n, seg_copy, wait=True)

    dest = dest_ref[0]
    wsel = wsel_ref[0]

    def chunk(c, carry):
        r0 = pl.multiple_of(c * KC, KC)
        rows = r0 + lax.broadcasted_iota(I32, (KC, TB), 0)
        wm = jnp.zeros((KC, TB), F32)
        for k in range(TOP_K):
            wm = wm + jnp.where(dest[k:k + 1, :] == rows, wsel[k:k + 1, :], 0.0)
        acc_ref[...] += lax.dot_general(wm.astype(BF16), yloc_ref[pl.ds(r0, KC), :],
                                        (((0,), (0,)), ((), ())), preferred_element_type=F32)
        return carry

    lax.fori_loop(0, n_chunks, chunk, 0)

    h2 = _ln_rows(alpha * h1_ref[...] + acc_ref[...], g2_ref[...], b2_ref[...])
    gate = jax.nn.sigmoid(_dot(h2.astype(BF16), wpg_ref[...]) + bpg_ref[...])
    pe = _dot(p_ref[...].astype(BF16), wpe_ref[...])
    y_ref[...] = h2 + gate * pe


def _combine(meta, dest, wsel, h1_all, h1b_all, p_all, wsg_b, wsu_b, wsd_b, g2, b2, wpg_b, bpg, wpe_b, ys,
             m_out_max, alpha):
    n, D = h1_all.shape
    nb = n // TB
    d_pe = p_all.shape[1]
    cs = lambda shape: pl.BlockSpec(shape, lambda i, *_: (0,) * len(shape), pipeline_mode=pl.Buffered(1))
    grid_spec = pltpu.PrefetchScalarGridSpec(
        num_scalar_prefetch=4,
        grid=(nb,),
        in_specs=[
            pl.BlockSpec((1, TOP_K, TB), lambda i, *_: (i, 0, 0)),
            pl.BlockSpec((1, TOP_K, TB), lambda i, *_: (i, 0, 0)),
            pl.BlockSpec((TB, D), lambda i, *_: (i, 0)),
            pl.BlockSpec((TB, D), lambda i, *_: (i, 0)),
            pl.BlockSpec((TB, d_pe), lambda i, *_: (i, 0)),
            cs(wsg_b.shape), cs(wsu_b.shape), cs(wsd_b.shape), cs(g2.shape), cs(b2.shape),
            cs(wpg_b.shape), cs(bpg.shape), cs(wpe_b.shape),
            pl.BlockSpec(memory_space=pl.ANY),
        ],
        out_specs=pl.BlockSpec((TB, D), lambda i, *_: (i, 0)),
        scratch_shapes=[
            pltpu.VMEM((m_out_max, D), BF16),
            pltpu.VMEM((TB, D), F32),
            pltpu.SemaphoreType.DMA(()),
        ],
    )
    return pl.pallas_call(
        functools.partial(_combine_body, alpha),
        grid_spec=grid_spec,
        out_shape=jax.ShapeDtypeStruct((n, D), F32),
        compiler_params=pltpu.CompilerParams(
            dimension_semantics=("arbitrary",), vmem_limit_bytes=VMEM_LIMIT),
        name="combine",
    )(meta["seg_g"], meta["seg_l"], meta["seg_n"], meta["nch"], dest, wsel, h1_all, h1b_all, p_all,
      wsg_b, wsu_b, wsd_b, g2, b2, wpg_b, bpg, wpe_b, ys)


def _sort_meta(cnt, n_tiles_max):
    nb = cnt.shape[0]
    segp = (cnt + (SEG - 1)) // SEG * SEG
    loc = jnp.cumsum(segp, axis=1) - segp
    before = jnp.cumsum(segp, axis=0) - segp
    total = jnp.sum(segp, axis=0)
    total_p = (total + (TR - 1)) // TR * TR
    base = jnp.cumsum(total_p) - total_p
    glob = base[None, :] + before
    n_valid = jnp.sum(total_p) // TR
    tiles = jnp.arange(n_tiles_max, dtype=I32)
    tile_src = jnp.minimum(tiles, n_valid - 1)
    tile_e = jnp.searchsorted(base + total_p, tile_src * TR, side="right").astype(I32)
    return {
        "seg_g": glob.reshape(-1).astype(I32),
        "seg_l": loc.reshape(-1).astype(I32),
        "seg_n": (segp // SEG).reshape(-1).astype(I32),
        "nch": ((jnp.sum(segp, axis=1) + (KC - 1)) // KC).astype(I32),
        "tail_g": (base + total).astype(I32),
        "tail_n": ((total_p - total) // SEG).astype(I32),
        "tile_e": jnp.minimum(tile_e, N_EXPERTS - 1),
        "tile_src": tile_src.astype(I32),
        "tile_valid": (tiles < n_valid).astype(I32),
    }


def _layer(xp, xs, st, pp, ps, w_in, w_pool, pool_scale, sgu_ln_g, sgu_ln_b, w_s, b_s, w_out, ln1_g, ln1_b,
           router_w, router_bias, w_gate, w_up, w_down, ws_gate, ws_up, ws_down, ln2_g, ln2_b, w_pe, w_pgate,
           b_pgate, alpha):
    B, T, D = xp.shape
    NS, TS, _ = xs.shape
    n_prompt, n_sample = B * T, NS * TS
    n = n_prompt + n_sample
    d_pool = w_pool.shape[0] * w_pool.shape[1]
    n_heads = w_s.shape[0]
    row = lambda v: v.reshape(1, -1)

    win_b, wout_b, wpool_b = w_in.astype(BF16), w_out.astype(BF16), w_pool.astype(BF16)
    pscale, g1, b1 = row(pool_scale), row(ln1_g), row(ln1_b)
    bsb = jnp.broadcast_to(b_s[:, :, None], (n_heads, CHUNK, SGU_HEAD_DIM))
    h1_all, h1b_all, pool_p = _mixer_prompt(xp, win_b, wpool_b, pscale, sgu_ln_g, sgu_ln_b, w_s, bsb, wout_b,
                                            g1, b1, n, alpha)

    x_t = jnp.transpose(xs, (1, 0, 2))
    st_t = jnp.transpose(st, (1, 0, 2))
    wrow = jnp.repeat(jnp.transpose(w_s[:, :TS, :TS], (1, 2, 0)), SGU_HEAD_DIM, axis=2)
    brow = jnp.repeat(jnp.transpose(b_s[:, :TS], (1, 0)), SGU_HEAD_DIM, axis=1)
    h1_all, h1b_all, a_s, vn_s = _mixer_sample(x_t, st_t, win_b, wpool_b, pscale, sgu_ln_g, sgu_ln_b, wrow, brow,
                                               wout_b, g1, b1, h1_all, h1b_all, n_prompt, alpha)

    dest, wsel, cnt = _router(h1_all, jnp.transpose(router_w), router_bias.reshape(-1, 1))
    nb = n // TB
    m_out_max = TB * TOP_K + N_EXPERTS * SEG
    rows_max = n * TOP_K + nb * N_EXPERTS * (SEG - 1) + N_EXPERTS * (TR - SEG)
    n_tiles_max = -(-rows_max // TR)
    meta = _sort_meta(cnt[:, :, 0], n_tiles_max)
    xsort = _dispatch(meta, dest, h1b_all, n_tiles_max * TR, m_out_max)
    ysort = _expert_ffn(meta, xsort, w_gate, w_up, w_down)

    nbs = TB // TS
    ps_t = jnp.transpose(ps.reshape(NS // nbs, nbs, TS, -1), (0, 2, 1, 3)).reshape(n_sample, -1)
    p_all = jnp.concatenate([pp.reshape(n_prompt, -1), ps_t], axis=0)
    y_all = _combine(meta, dest, wsel, h1_all, h1b_all, p_all, ws_gate.astype(BF16), ws_up.astype(BF16),
                     ws_down.astype(BF16), row(ln2_g), row(ln2_b), w_pgate.astype(BF16), row(b_pgate),
                     w_pe.astype(BF16), ysort, m_out_max, alpha)

    yp = y_all[:n_prompt].reshape(B, T, D)
    ys = jnp.transpose(y_all[n_prompt:].reshape(NS // nbs, TS, nbs, D), (0, 2, 1, 3)).reshape(NS, TS, D)
    new_pool_p = pool_p[:, HALO - POOL_BUF:, :]
    new_pool_s = jnp.concatenate([st, jnp.transpose(a_s, (1, 0, 2))], axis=1)[:, -POOL_BUF:]
    vn = jnp.transpose(vn_s, (1, 0, 2)).reshape(NS, TS, n_heads, SGU_HEAD_DIM)
    return yp, ys, new_pool_p, new_pool_s, vn


def kernel(x_prompt, x_sample, state_pool, p_prompt, p_sample, w_in, w_pool, pool_scale, sgu_ln_g, sgu_ln_b, w_s, b_s, w_out, ln1_g, ln1_b, router_w, router_bias, w_gate, w_up, w_down, ws_gate, ws_up, ws_down, ln2_g, ln2_b, w_pe, w_pgate, b_pgate):
    depth = w_in.shape[0]
    alpha = (2.0 * depth) ** 0.25
    hp, hs = x_prompt, x_sample
    pool_p, pool_s, v_s = [], [], []
    for i in range(depth):
        hp, hs, bp, bs, vs = _layer(
            hp, hs, state_pool[i], p_prompt[i], p_sample[i], w_in[i], w_pool[i], pool_scale[i], sgu_ln_g[i],
            sgu_ln_b[i], w_s[i], b_s[i], w_out[i], ln1_g[i], ln1_b[i], router_w[i], router_bias[i], w_gate[i],
            w_up[i], w_down[i], ws_gate[i], ws_up[i], ws_down[i], ln2_g[i], ln2_b[i], w_pe[i], w_pgate[i],
            b_pgate[i], alpha)
        pool_p.append(bp)
        pool_s.append(bs)
        v_s.append(vs)
    return hp, hs, jnp.stack(pool_p), jnp.stack(pool_s), jnp.stack(v_s)
```

```python
import functools

import jax
import jax.numpy as jnp
import numpy as np
from jax import lax
from jax.experimental import pallas as pl
from jax.experimental.pallas import tpu as pltpu

F32 = jnp.float32
BF16 = jnp.bfloat16
I32 = jnp.int32

POOL_WINDOWS = (2, 4, 8, 16)
POOL_BUF = max(POOL_WINDOWS) - 1
CHUNK = 128
SGU_HEAD_DIM = 128
N_EXPERTS = 64
TOP_K = 8
N_EXPERT_GROUPS = 8
TOPK_GROUPS = 4
ROUTE_SCALE = 2.5
LN_EPS = 1e-5
PAST_LEN = 16384

TB = 256
HALO = 16
TR = 256
SEG = 16
KC = 256
VMEM_LIMIT = 56 * 1024 * 1024


def _const_spec(shape):
    n = len(shape)
    return pl.BlockSpec(shape, lambda *_: (0,) * n, pipeline_mode=pl.Buffered(1))


def _ln_rows(x, g, b):
    mu = jnp.mean(x, axis=-1, keepdims=True)
    xc = x - mu
    var = jnp.mean(xc * xc, axis=-1, keepdims=True)
    return xc * lax.rsqrt(var + LN_EPS) * g + b


def _dot(a, b):
    return jnp.dot(a, b, preferred_element_type=F32)


def _prompt_tile(j, n_tiles, x, proj, band_ref, wpool_ref, pscale_ref, sg_ref, sb_ref, ws_ref, bsb_ref,
                 pool_ref, zcat_ref, mixin_ref):
    d_pool = zcat_ref.shape[1]
    d_pool_g = d_pool // len(POOL_WINDOWS)
    d_sgu = mixin_ref.shape[1] - d_pool

    @pl.when(j == 0)
    def _():
        zcat_ref[0:HALO, :] = jnp.zeros((HALO, d_pool), BF16)

    a = proj[:, :d_pool]
    zcat_ref[HALO:, :] = a.astype(BF16)

    pos1 = (j * TB + lax.broadcasted_iota(I32, (TB, 1), 0) + 1).astype(F32)
    for g, w in enumerate(POOL_WINDOWS):
        c0, c1 = g * d_pool_g, (g + 1) * d_pool_g
        s = _dot(band_ref[g], zcat_ref[:, c0:c1])
        cnt = jnp.minimum(pos1, float(w))
        d = s / cnt - a[:, c0:c1]
        y = _dot(d.astype(BF16), wpool_ref[g])
        mixin_ref[:, c0:c1] = (y * pscale_ref[:, c0:c1]).astype(BF16)
    zcat_ref[0:HALO, :] = zcat_ref[TB:TB + HALO, :]

    @pl.when(j == n_tiles - 1)
    def _():
        pool_ref[0] = a[TB - HALO:, :]

    u = jax.nn.gelu(proj[:, d_pool:d_pool + d_sgu])
    v = jax.nn.gelu(proj[:, d_pool + d_sgu:])
    r = lax.broadcasted_iota(I32, (CHUNK, CHUNK), 0)
    c = lax.broadcasted_iota(I32, (CHUNK, CHUNK), 1)
    tril = r >= c
    for h in range(d_sgu // SGU_HEAD_DIM):
        l0, l1 = h * SGU_HEAD_DIM, (h + 1) * SGU_HEAD_DIM
        vn = _ln_rows(v[:, l0:l1], sg_ref[h:h + 1, :], sb_ref[h:h + 1, :]).astype(BF16)
        wst = jnp.where(tril, ws_ref[h], 0.0).astype(BF16)
        for ci in range(TB // CHUNK):
            r0, r1 = ci * CHUNK, (ci + 1) * CHUNK
            mixed = _dot(wst, vn[r0:r1]) + bsb_ref[h]
            mixin_ref[r0:r1, d_pool + l0:d_pool + l1] = (u[r0:r1, l0:l1] * mixed).astype(BF16)


def _sample_tile(T, NB, proj, st_ref, wpool_ref, pscale_ref, sg_ref, sb_ref, wrow_ref, brow_ref,
                 a_ref, vn_ref, dbuf_ref, mixin_ref):
    d_pool = dbuf_ref.shape[1]
    d_pool_g = d_pool // len(POOL_WINDOWS)
    d_sgu = mixin_ref.shape[1] - d_pool

    a = proj[:, :d_pool]
    for t in range(T):
        a_ref[t] = a[t * NB:(t + 1) * NB]

    for g, w in enumerate(POOL_WINDOWS):
        c0, c1 = g * d_pool_g, (g + 1) * d_pool_g
        for t in range(T):
            cnt = float(min(PAST_LEN + t + 1, w))
            acc = None
            for s in range(POOL_BUF + t - w + 1, POOL_BUF + t + 1):
                if s < POOL_BUF:
                    term = st_ref[s, :, c0:c1]
                else:
                    term = a[(s - POOL_BUF) * NB:(s - POOL_BUF + 1) * NB, c0:c1]
                acc = term if acc is None else acc + term
            d = acc / cnt - a[t * NB:(t + 1) * NB, c0:c1]
            dbuf_ref[t * NB:(t + 1) * NB, c0:c1] = d.astype(BF16)
        y = _dot(dbuf_ref[0:T * NB, c0:c1], wpool_ref[g])
        mixin_ref[:, c0:c1] = (y * pscale_ref[:, c0:c1]).astype(BF16)

    u = jax.nn.gelu(proj[:, d_pool:d_pool + d_sgu])
    v = jax.nn.gelu(proj[:, d_pool + d_sgu:])
    for h in range(d_sgu // SGU_HEAD_DIM):
        l0, l1 = h * SGU_HEAD_DIM, (h + 1) * SGU_HEAD_DIM
        vn = _ln_rows(v[:, l0:l1], sg_ref[h:h + 1, :], sb_ref[h:h + 1, :])
        for t in range(T):
            vn_ref[t, :, l0:l1] = vn[t * NB:(t + 1) * NB]
    for t in range(T):
        mixed = brow_ref[t:t + 1, :]
        for s in range(t + 1):
            mixed = mixed + wrow_ref[t, s:s + 1, :] * vn_ref[s]
        mixin_ref[t * NB:(t + 1) * NB, d_pool:] = (u[t * NB:(t + 1) * NB] * mixed).astype(BF16)


def _mixer_body(alpha, n_pt, tiles_per_seq, xp_ref, xs_ref, st_ref, win_ref, band_ref, wpool_ref, pscale_ref,
                sg_ref, sb_ref, ws_ref, bsb_ref, wrow_ref, brow_ref, wout_ref, g1_ref, b1_ref,
                h1_ref, h1b_ref, pool_ref, a_ref, vn_ref, zcat_ref, mixin_ref):
    i = pl.program_id(0)

    def finish(x):
        mix = _dot(mixin_ref[...], wout_ref[...])
        h1 = _ln_rows(alpha * x + mix, g1_ref[...], b1_ref[...])
        h1_ref[...] = h1
        h1b_ref[...] = h1.astype(BF16)

    @pl.when(i < n_pt)
    def _():
        x = xp_ref[0]
        proj = _dot(x.astype(BF16), win_ref[...])
        _prompt_tile(i % tiles_per_seq, tiles_per_seq, x, proj, band_ref, wpool_ref, pscale_ref, sg_ref, sb_ref,
                     ws_ref, bsb_ref, pool_ref, zcat_ref, mixin_ref)
        finish(x)

    @pl.when(i >= n_pt)
    def _():
        T, NB, D = xs_ref.shape
        x = xs_ref[...].reshape(T * NB, D)
        proj = _dot(x.astype(BF16), win_ref[...])
        _sample_tile(T, NB, proj, st_ref, wpool_ref, pscale_ref, sg_ref, sb_ref, wrow_ref, brow_ref,
                     a_ref, vn_ref, zcat_ref, mixin_ref)
        finish(x)


def _pool_band(tm):
    t = np.arange(tm)[:, None] + HALO
    s = np.arange(tm + HALO)[None, :]
    return jnp.asarray(np.stack([((s <= t) & (s >= t - w + 1)) for w in POOL_WINDOWS]).astype(np.float32), BF16)


def _mixer(xp, xs_t, st_t, win_b, wpool_b, pscale, sg, sb, ws, bsb, wrow, brow, wout_b, g1, b1, alpha):
    B, T, D = xp.shape
    TS, NS, _ = xs_t.shape
    d_pool = st_t.shape[2]
    d_sgu = wrow.shape[2]
    tps = T // TB
    n_pt = B * tps
    NB = TB // TS
    n_st = NS // NB
    n = B * T + NS * TS
    pt = lambda i: jnp.minimum(i, n_pt - 1)
    stile = lambda i: jnp.maximum(i - n_pt, 0)
    body = functools.partial(_mixer_body, alpha, n_pt, tps)
    consts = (win_b, _pool_band(TB), wpool_b, pscale, sg, sb, ws, bsb, wrow, brow, wout_b, g1, b1)
    return pl.pallas_call(
        body,
        grid=(n_pt + n_st,),
        in_specs=[
            pl.BlockSpec((1, TB, D), lambda i: (pt(i) // tps, pt(i) % tps, 0)),
            pl.BlockSpec((TS, NB, D), lambda i: (0, stile(i), 0)),
            pl.BlockSpec((POOL_BUF, NB, d_pool), lambda i: (0, stile(i), 0)),
        ] + [_const_spec(c.shape) for c in consts],
        out_specs=[
            pl.BlockSpec((TB, D), lambda i: (i, 0)),
            pl.BlockSpec((TB, D), lambda i: (i, 0)),
            pl.BlockSpec((1, HALO, d_pool), lambda i: (pt(i) // tps, 0, 0)),
            pl.BlockSpec((TS, NB, d_pool), lambda i: (0, stile(i), 0)),
            pl.BlockSpec((TS, NB, d_sgu), lambda i: (0, stile(i), 0)),
        ],
        out_shape=[
            jax.ShapeDtypeStruct((n, D), F32),
            jax.ShapeDtypeStruct((n, D), BF16),
            jax.ShapeDtypeStruct((B, HALO, d_pool), F32),
            jax.ShapeDtypeStruct((TS, NS, d_pool), F32),
            jax.ShapeDtypeStruct((TS, NS, d_sgu), F32),
        ],
        scratch_shapes=[
            pltpu.VMEM((TB + HALO, d_pool), BF16),
            pltpu.VMEM((TB, D), BF16),
        ],
        compiler_params=pltpu.CompilerParams(
            dimension_semantics=("arbitrary",), vmem_limit_bytes=VMEM_LIMIT),
        name="mixer",
    )(xp, xs_t, st_t, *consts)


def _router_body(h_ref, rwt_ref, rb_ref, upper_ref, lstrict_ref, dest_ref, wsel_ref, cnt_ref):
    per_group = N_EXPERTS // N_EXPERT_GROUPS
    neg = -jnp.inf
    logits_t = lax.dot_general(rwt_ref[...], h_ref[...], (((1,), (1,)), ((), ())),
                               precision=lax.Precision.HIGHEST, preferred_element_type=F32)
    s_t = jax.nn.sigmoid(logits_t)
    b_t = s_t + rb_ref[...]

    io_g = lax.broadcasted_iota(I32, (per_group, TB), 0)
    gs = []
    for g in range(N_EXPERT_GROUPS):
        xg = b_t[g * per_group:(g + 1) * per_group, :]
        m1 = jnp.max(xg, axis=0, keepdims=True)
        i1 = jnp.min(jnp.where(xg == m1, io_g, per_group), axis=0, keepdims=True)
        m2 = jnp.max(jnp.where(io_g == i1, neg, xg), axis=0, keepdims=True)
        gs.append(m1 + m2)
    masked = []
    for g in range(N_EXPERT_GROUPS):
        rank = jnp.zeros((1, TB), F32)
        for g2 in range(N_EXPERT_GROUPS):
            if g2 != g:
                ahead = (gs[g2] >= gs[g]) if g2 < g else (gs[g2] > gs[g])
                rank = rank + jnp.where(ahead, 1.0, 0.0)
        keep = rank < float(TOPK_GROUPS)
        masked.append(jnp.where(keep, b_t[g * per_group:(g + 1) * per_group, :], neg))
    xm = jnp.concatenate(masked, axis=0)

    io_e = lax.broadcasted_iota(I32, (N_EXPERTS, TB), 0)
    onehots = []
    sel = jnp.zeros((N_EXPERTS, TB), F32)
    for _ in range(TOP_K):
        m = jnp.max(xm, axis=0, keepdims=True)
        idx = jnp.min(jnp.where(xm == m, io_e, N_EXPERTS), axis=0, keepdims=True)
        oh = io_e == idx
        onehots.append(oh)
        sel = jnp.where(oh, 1.0, sel)
        xm = jnp.where(oh, neg, xm)

    ssel = sel * s_t
    denom = jnp.sum(ssel, axis=0, keepdims=True)
    comb = ssel / denom * ROUTE_SCALE

    rank_t = _dot(sel.astype(BF16), upper_ref[...])
    cnt = jnp.sum(sel, axis=1, keepdims=True)
    cnt_i = cnt.astype(I32)
    seg16 = ((cnt_i + (SEG - 1)) // SEG).astype(F32)
    off16 = _dot(lstrict_ref[...], jnp.broadcast_to(seg16, (N_EXPERTS, 128)).astype(BF16))
    d_t = rank_t + off16[:, 0:1] * float(SEG)
    for k in range(TOP_K):
        dest_ref[0, k:k + 1, :] = jnp.sum(jnp.where(onehots[k], d_t, 0.0), axis=0, keepdims=True).astype(I32)
        wsel_ref[0, k:k + 1, :] = jnp.sum(jnp.where(onehots[k], comb, 0.0), axis=0, keepdims=True)
    cnt_ref[0] = jnp.broadcast_to(cnt_i, (N_EXPERTS, 128))


def _router(h1_all, rwt, rb_col):
    n, D = h1_all.shape
    nb = n // TB
    upper = jnp.asarray(np.triu(np.ones((TB, TB), np.float32), 1), BF16)
    lstrict = jnp.asarray(np.tril(np.ones((N_EXPERTS, N_EXPERTS), np.float32), -1), BF16)
    return pl.pallas_call(
        _router_body,
        grid=(nb,),
        in_specs=[
            pl.BlockSpec((TB, D), lambda i: (i, 0)),
            _const_spec(rwt.shape),
            _const_spec(rb_col.shape),
            _const_spec(upper.shape),
            _const_spec(lstrict.shape),
        ],
        out_specs=[
            pl.BlockSpec((1, TOP_K, TB), lambda i: (i, 0, 0)),
            pl.BlockSpec((1, TOP_K, TB), lambda i: (i, 0, 0)),
            pl.BlockSpec((1, N_EXPERTS, 128), lambda i: (i, 0, 0)),
        ],
        out_shape=[
            jax.ShapeDtypeStruct((nb, TOP_K, TB), I32),
            jax.ShapeDtypeStruct((nb, TOP_K, TB), F32),
            jax.ShapeDtypeStruct((nb, N_EXPERTS, 128), I32),
        ],
        compiler_params=pltpu.CompilerParams(dimension_semantics=("arbitrary",)),
        name="router",
    )(h1_all, rwt, rb_col, upper, lstrict)


def _seg_copies(b, seg_g, seg_l, seg_n, make_copy, wait):
    def per_expert(e, carry):
        idx = b * N_EXPERTS + e
        g0, l0, n = seg_g[idx], seg_l[idx], seg_n[idx]

        def per_granule(j, c):
            cp = make_copy(pl.multiple_of(l0 + j * SEG, SEG), pl.multiple_of(g0 + j * SEG, SEG))
            if wait:
                cp.wait()
            else:
                cp.start()
            return c

        lax.fori_loop(0, n, per_granule, 0)
        return carry

    lax.fori_loop(0, N_EXPERTS, per_expert, 0)


def _dispatch_body(seg_g, seg_l, seg_n, nch, tail_g, tail_n, dest_ref, h_ref, xs_hbm, xloc_ref, zero_ref, sem):
    b = pl.program_id(0)
    nb = pl.num_programs(0)
    dest = dest_ref[0]
    h = h_ref[...]

    def chunk(c, carry):
        r0 = pl.multiple_of(c * KC, KC)
        rows = r0 + lax.broadcasted_iota(I32, (KC, TB), 0)
        s = jnp.zeros((KC, TB), F32)
        for k in range(TOP_K):
            s = s + jnp.where(dest[k:k + 1, :] == rows, 1.0, 0.0)
        xloc_ref[pl.ds(r0, KC), :] = _dot(s.astype(BF16), h).astype(BF16)
        return carry

    lax.fori_loop(0, nch[b], chunk, 0)

    def seg_copy(l, g):
        return pltpu.make_async_copy(xloc_ref.at[pl.ds(l, SEG), :], xs_hbm.at[pl.ds(g, SEG), :], sem)

    _seg_copies(b, seg_g, seg_l, seg_n, seg_copy, wait=False)

    def tail_copy(g):
        return pltpu.make_async_copy(zero_ref, xs_hbm.at[pl.ds(g, SEG), :], sem)

    def tails(wait):
        def per_expert(e, carry):
            def per_granule(j, c):
                cp = tail_copy(pl.multiple_of(tail_g[e] + j * SEG, SEG))
                if wait:
                    cp.wait()
                else:
                    cp.start()
                return c
            lax.fori_loop(0, tail_n[e], per_granule, 0)
            return carry
        lax.fori_loop(0, N_EXPERTS, per_expert, 0)

    @pl.when(b == nb - 1)
    def _():
        zero_ref[...] = jnp.zeros(zero_ref.shape, BF16)
        tails(False)

    _seg_copies(b, seg_g, seg_l, seg_n, seg_copy, wait=True)

    @pl.when(b == nb - 1)
    def _():
        tails(True)


def _dispatch(meta, dest, h1b_all, n_rows_sorted, m_out_max):
    n, D = h1b_all.shape
    nb = n // TB
    grid_spec = pltpu.PrefetchScalarGridSpec(
        num_scalar_prefetch=6,
        grid=(nb,),
        in_specs=[
            pl.BlockSpec((1, TOP_K, TB), lambda i, *_: (i, 0, 0)),
            pl.BlockSpec((TB, D), lambda i, *_: (i, 0)),
        ],
        out_specs=pl.BlockSpec(memory_space=pl.ANY),
        scratch_shapes=[
            pltpu.VMEM((m_out_max, D), BF16),
            pltpu.VMEM((SEG, D), BF16),
            pltpu.SemaphoreType.DMA(()),
        ],
    )
    return pl.pallas_call(
        _dispatch_body,
        grid_spec=grid_spec,
        out_shape=jax.ShapeDtypeStruct((n_rows_sorted, D), BF16),
        compiler_params=pltpu.CompilerParams(
            dimension_semantics=("arbitrary",), vmem_limit_bytes=VMEM_LIMIT, has_side_effects=True),
        name="dispatch",
    )(meta["seg_g"], meta["seg_l"], meta["seg_n"], meta["nch"], meta["tail_g"], meta["tail_n"], dest, h1b_all)


def _ffn_body(tile_e, tile_src, tile_valid, e_slot, e_next, x_ref, wg_hbm, wu_hbm, wd_hbm, y_ref,
              wg_f, wu_f, wd_f, wgu_b, wd_b, sems):
    i = pl.program_id(0)
    f = wg_f.shape[2]
    e = tile_e[i]
    e_prev = tile_e[jnp.maximum(i - 1, 0)]

    def weight_copies(expert, slot):
        return (pltpu.make_async_copy(wg_hbm.at[expert], wg_f.at[slot], sems.at[slot, 0]),
                pltpu.make_async_copy(wu_hbm.at[expert], wu_f.at[slot], sems.at[slot, 1]),
                pltpu.make_async_copy(wd_hbm.at[expert], wd_f.at[slot], sems.at[slot, 2]))

    @pl.when((i == 0) | (e != e_prev))
    def _():
        slot = e_slot[e]

        @pl.when(i == 0)
        def _():
            for cp in weight_copies(e, slot):
                cp.start()

        for cp in weight_copies(e, slot):
            cp.wait()
        nxt = e_next[e]

        @pl.when(nxt >= 0)
        def _():
            for cp in weight_copies(nxt, 1 - slot):
                cp.start()

        wgu_b[:, :f] = wg_f[slot].astype(BF16)
        wgu_b[:, f:] = wu_f[slot].astype(BF16)
        wd_b[...] = wd_f[slot].astype(BF16)

    @pl.when(tile_valid[i] == 1)
    def _():
        gu = _dot(x_ref[...], wgu_b[...])
        act = (jax.nn.silu(gu[:, :f]) * gu[:, f:]).astype(BF16)
        y_ref[...] = _dot(act, wd_b[...]).astype(BF16)


def _expert_ffn(meta, xs, w_gate, w_up, w_down):
    R, D = xs.shape
    E, _, f = w_gate.shape
    nt = R // TR
    grid_spec = pltpu.PrefetchScalarGridSpec(
        num_scalar_prefetch=5,
        grid=(nt,),
        in_specs=[
            pl.BlockSpec((TR, D), lambda i, te, ts, *_: (ts[i], 0)),
            pl.BlockSpec(memory_space=pl.ANY),
            pl.BlockSpec(memory_space=pl.ANY),
            pl.BlockSpec(memory_space=pl.ANY),
        ],
        out_specs=pl.BlockSpec((TR, D), lambda i, te, ts, *_: (ts[i], 0)),
        scratch_shapes=[
            pltpu.VMEM((2, D, f), F32),
            pltpu.VMEM((2, D, f), F32),
            pltpu.VMEM((2, f, D), F32),
            pltpu.VMEM((D, 2 * f), BF16),
            pltpu.VMEM((f, D), BF16),
            pltpu.SemaphoreType.DMA((2, 3)),
        ],
    )
    return pl.pallas_call(
        _ffn_body,
        grid_spec=grid_spec,
        out_shape=jax.ShapeDtypeStruct((R, D), BF16),
        compiler_params=pltpu.CompilerParams(
            dimension_semantics=("arbitrary",), vmem_limit_bytes=VMEM_LIMIT),
        name="expert_ffn",
    )(meta["tile_e"], meta["tile_src"], meta["tile_valid"], meta["e_slot"], meta["e_next"],
      xs, w_gate, w_up, w_down)


def _combine_body(alpha, n_pb, seg_g, seg_l, seg_n, nch, dest_ref, wsel_ref, h1_ref, h1b_ref, p_ref,
                  wsg_ref, wsu_ref, wsd_ref, g2_ref, b2_ref, wpg_ref, bpg_ref, wpe_ref, ys_hbm,
                  yp_ref, ysm_ref, yloc_ref, acc_ref, sem):
    b = pl.program_id(0)
    n_chunks = nch[b]

    last0 = pl.multiple_of((n_chunks - 1) * KC, KC)
    yloc_ref[pl.ds(last0, KC), :] = jnp.zeros((KC, yloc_ref.shape[1]), BF16)

    def seg_copy(l, g):
        return pltpu.make_async_copy(ys_hbm.at[pl.ds(g, SEG), :], yloc_ref.at[pl.ds(l, SEG), :], sem)

    _seg_copies(b, seg_g, seg_l, seg_n, seg_copy, wait=False)

    hb = h1b_ref[...]
    sh = (jax.nn.silu(_dot(hb, wsg_ref[...])) * _dot(hb, wsu_ref[...])).astype(BF16)
    acc_ref[...] = _dot(sh, wsd_ref[...])

    _seg_copies(b, seg_g, seg_l, seg_n, seg_copy, wait=True)

    dest = dest_ref[0]
    wsel = wsel_ref[0]

    def chunk(c, carry):
        r0 = pl.multiple_of(c * KC, KC)
        rows = r0 + lax.broadcasted_iota(I32, (KC, TB), 0)
        wm = jnp.zeros((KC, TB), F32)
        for k in range(TOP_K):
            wm = wm + jnp.where(dest[k:k + 1, :] == rows, wsel[k:k + 1, :], 0.0)
        acc_ref[...] += lax.dot_general(wm.astype(BF16), yloc_ref[pl.ds(r0, KC), :],
                                        (((0,), (0,)), ((), ())), preferred_element_type=F32)
        return carry

    lax.fori_loop(0, n_chunks, chunk, 0)

    h2 = _ln_rows(alpha * h1_ref[...] + acc_ref[...], g2_ref[...], b2_ref[...])
    gate = jax.nn.sigmoid(_dot(h2.astype(BF16), wpg_ref[...]) + bpg_ref[...])
    pe = _dot(p_ref[...].astype(BF16), wpe_ref[...])
    y = h2 + gate * pe

    @pl.when(b < n_pb)
    def _():
        yp_ref[...] = y

    @pl.when(b >= n_pb)
    def _():
        ysm_ref[...] = y


def _combine(meta, dest, wsel, h1_all, h1b_all, p_all, wsg_b, wsu_b, wsd_b, g2, b2, wpg_b, bpg, wpe_b, ys,
             m_out_max, n_prompt, alpha):
    n, D = h1_all.shape
    nb = n // TB
    n_pb = n_prompt // TB
    d_pe = p_all.shape[1]
    cs = lambda shape: pl.BlockSpec(shape, lambda i, *_: (0,) * len(shape), pipeline_mode=pl.Buffered(1))
    grid_spec = pltpu.PrefetchScalarGridSpec(
        num_scalar_prefetch=4,
        grid=(nb,),
        in_specs=[
            pl.BlockSpec((1, TOP_K, TB), lambda i, *_: (i, 0, 0)),
            pl.BlockSpec((1, TOP_K, TB), lambda i, *_: (i, 0, 0)),
            pl.BlockSpec((TB, D), lambda i, *_: (i, 0)),
            pl.BlockSpec((TB, D), lambda i, *_: (i, 0)),
            pl.BlockSpec((TB, d_pe), lambda i, *_: (i, 0)),
            cs(wsg_b.shape), cs(wsu_b.shape), cs(wsd_b.shape), cs(g2.shape), cs(b2.shape),
            cs(wpg_b.shape), cs(bpg.shape), cs(wpe_b.shape),
            pl.BlockSpec(memory_space=pl.ANY),
        ],
        out_specs=[
            pl.BlockSpec((TB, D), lambda i, *_: (jnp.minimum(i, n_pb - 1), 0)),
            pl.BlockSpec((TB, D), lambda i, *_: (jnp.maximum(i - n_pb, 0), 0)),
        ],
        scratch_shapes=[
            pltpu.VMEM((m_out_max, D), BF16),
            pltpu.VMEM((TB, D), F32),
            pltpu.SemaphoreType.DMA(()),
        ],
    )
    return pl.pallas_call(
        functools.partial(_combine_body, alpha, n_pb),
        grid_spec=grid_spec,
        out_shape=[jax.ShapeDtypeStruct((n_prompt, D), F32), jax.ShapeDtypeStruct((n - n_prompt, D), F32)],
        compiler_params=pltpu.CompilerParams(
            dimension_semantics=("arbitrary",), vmem_limit_bytes=VMEM_LIMIT),
        name="combine",
    )(meta["seg_g"], meta["seg_l"], meta["seg_n"], meta["nch"], dest, wsel, h1_all, h1b_all, p_all,
      wsg_b, wsu_b, wsd_b, g2, b2, wpg_b, bpg, wpe_b, ys)


def _sort_meta(cnt, n_tiles_max):
    segp = (cnt + (SEG - 1)) // SEG * SEG
    loc = jnp.cumsum(segp, axis=1) - segp
    before = jnp.cumsum(segp, axis=0) - segp
    total = jnp.sum(segp, axis=0)
    total_p = (total + (TR - 1)) // TR * TR
    ends = jnp.cumsum(total_p)
    base = ends - total_p
    glob = base[None, :] + before
    n_valid = ends[-1] // TR
    tiles = jnp.arange(n_tiles_max, dtype=I32)
    tile_src = jnp.minimum(tiles, n_valid - 1)
    tile_e = jnp.sum((ends[None, :] <= (tile_src * TR)[:, None]).astype(I32), axis=1)
    experts = jnp.arange(N_EXPERTS, dtype=I32)
    active = total_p > 0
    later = active[None, :] & (experts[None, :] > experts[:, None])
    e_next = jnp.min(jnp.where(later, experts[None, :], N_EXPERTS), axis=1)
    return {
        "seg_g": glob.reshape(-1).astype(I32),
        "seg_l": loc.reshape(-1).astype(I32),
        "seg_n": (segp // SEG).reshape(-1).astype(I32),
        "nch": ((jnp.sum(segp, axis=1) + (KC - 1)) // KC).astype(I32),
        "tail_g": (base + total).astype(I32),
        "tail_n": ((total_p - total) // SEG).astype(I32),
        "tile_e": jnp.minimum(tile_e, N_EXPERTS - 1).astype(I32),
        "tile_src": tile_src.astype(I32),
        "tile_valid": (tiles < n_valid).astype(I32),
        "e_slot": ((jnp.cumsum(active.astype(I32)) - 1) % 2).astype(I32),
        "e_next": jnp.where(e_next < N_EXPERTS, e_next, -1).astype(I32),
    }


def _layer(xp, xs, st, pp, ps, w_in, w_pool, pool_scale, sgu_ln_g, sgu_ln_b, w_s, b_s, w_out, ln1_g, ln1_b,
           router_w, router_bias, w_gate, w_up, w_down, ws_gate, ws_up, ws_down, ln2_g, ln2_b, w_pe, w_pgate,
           b_pgate, alpha):
    B, T, D = xp.shape
    NS, TS, _ = xs.shape
    n_prompt, n_sample = B * T, NS * TS
    n = n_prompt + n_sample
    n_heads = w_s.shape[0]
    row = lambda v: v.reshape(1, -1)

    x_t = jnp.transpose(xs, (1, 0, 2))
    st_t = jnp.transpose(st, (1, 0, 2))
    bsb = jnp.broadcast_to(b_s[:, :, None], (n_heads, CHUNK, SGU_HEAD_DIM))
    wrow = jnp.repeat(jnp.transpose(w_s[:, :TS, :TS], (1, 2, 0)), SGU_HEAD_DIM, axis=2)
    brow = jnp.repeat(jnp.transpose(b_s[:, :TS], (1, 0)), SGU_HEAD_DIM, axis=1)
    h1_all, h1b_all, pool_p, a_s, vn_s = _mixer(
        xp, x_t, st_t, w_in.astype(BF16), w_pool.astype(BF16), row(pool_scale), sgu_ln_g, sgu_ln_b, w_s, bsb,
        wrow, brow, w_out.astype(BF16), row(ln1_g), row(ln1_b), alpha)

    dest, wsel, cnt = _router(h1_all, jnp.transpose(router_w), router_bias.reshape(-1, 1))
    nb = n // TB
    m_out_max = TB * TOP_K + N_EXPERTS * SEG
    rows_max = n * TOP_K + nb * N_EXPERTS * (SEG - 1) + N_EXPERTS * (TR - SEG)
    n_tiles_max = -(-rows_max // TR)
    meta = _sort_meta(cnt[:, :, 0], n_tiles_max)
    xsort = _dispatch(meta, dest, h1b_all, n_tiles_max * TR, m_out_max)
    ysort = _expert_ffn(meta, xsort, w_gate, w_up, w_down)

    nbs = TB // TS
    ps_t = jnp.transpose(ps.reshape(NS // nbs, nbs, TS, -1), (0, 2, 1, 3)).reshape(n_sample, -1)
    p_all = jnp.concatenate([pp.reshape(n_prompt, -1), ps_t], axis=0)
    y_p, y_s = _combine(meta, dest, wsel, h1_all, h1b_all, p_all, ws_gate.astype(BF16), ws_up.astype(BF16),
                        ws_down.astype(BF16), row(ln2_g), row(ln2_b), w_pgate.astype(BF16), row(b_pgate),
                        w_pe.astype(BF16), ysort, m_out_max, n_prompt, alpha)

    yp = y_p.reshape(B, T, D)
    ys = jnp.transpose(y_s.reshape(NS // nbs, TS, nbs, D), (0, 2, 1, 3)).reshape(NS, TS, D)
    new_pool_p = pool_p[:, HALO - POOL_BUF:, :]
    new_pool_s = jnp.concatenate([st, jnp.transpose(a_s, (1, 0, 2))], axis=1)[:, -POOL_BUF:]
    vn = jnp.transpose(vn_s, (1, 0, 2)).reshape(NS, TS, n_heads, SGU_HEAD_DIM)
    return yp, ys, new_pool_p, new_pool_s, vn


def kernel(x_prompt, x_sample, state_pool, p_prompt, p_sample, w_in, w_pool, pool_scale, sgu_ln_g, sgu_ln_b, w_s, b_s, w_out, ln1_g, ln1_b, router_w, router_bias, w_gate, w_up, w_down, ws_gate, ws_up, ws_down, ln2_g, ln2_b, w_pe, w_pgate, b_pgate):
    depth = w_in.shape[0]
    alpha = (2.0 * depth) ** 0.25
    hp, hs = x_prompt, x_sample
    pool_p, pool_s, v_s = [], [], []
    for i in range(depth):
        hp, hs, bp, bs, vs = _layer(
            hp, hs, state_pool[i], p_prompt[i], p_sample[i], w_in[i], w_pool[i], pool_scale[i], sgu_ln_g[i],
            sgu_ln_b[i], w_s[i], b_s[i], w_out[i], ln1_g[i], ln1_b[i], router_w[i], router_bias[i], w_gate[i],
            w_up[i], w_down[i], ws_gate[i], ws_up[i], ws_down[i], ln2_g[i], ln2_b[i], w_pe[i], w_pgate[i],
            b_pgate[i], alpha)
        pool_p.append(bp)
        pool_s.append(bs)
        v_s.append(vs)
    return hp, hs, jnp.stack(pool_p), jnp.stack(pool_s), jnp.stack(v_s)
```

```python
import functools

import jax
import jax.numpy as jnp
import numpy as np
from jax import lax
from jax.experimental import pallas as pl
from jax.experimental.pallas import tpu as pltpu

F32 = jnp.float32
BF16 = jnp.bfloat16
I32 = jnp.int32

POOL_WINDOWS = (2, 4, 8, 16)
POOL_BUF = max(POOL_WINDOWS) - 1
CHUNK = 128
SGU_HEAD_DIM = 128
N_EXPERTS = 64
TOP_K = 8
N_EXPERT_GROUPS = 8
TOPK_GROUPS = 4
ROUTE_SCALE = 2.5
LN_EPS = 1e-5
PAST_LEN = 16384

TB = 256
HALO = 16
TR = 256
SEG = 16
KC = 256
KCC = 512
VMEM_LIMIT = 56 * 1024 * 1024


def _const_spec(shape):
    n = len(shape)
    return pl.BlockSpec(shape, lambda *_: (0,) * n, pipeline_mode=pl.Buffered(1))


def _ln_rows(x, g, b):
    mu = jnp.mean(x, axis=-1, keepdims=True)
    xc = x - mu
    var = jnp.mean(xc * xc, axis=-1, keepdims=True)
    return xc * lax.rsqrt(var + LN_EPS) * g + b


def _dot(a, b):
    return jnp.dot(a, b, preferred_element_type=F32)


def _prompt_tile(j, n_tiles, x, proj, band_ref, wpool_ref, pscale_ref, sg_ref, sb_ref, ws_ref, bsb_ref,
                 pool_ref, zcat_ref, mixin_ref):
    d_pool = zcat_ref.shape[1]
    d_pool_g = d_pool // len(POOL_WINDOWS)
    d_sgu = mixin_ref.shape[1] - d_pool

    @pl.when(j == 0)
    def _():
        zcat_ref[0:HALO, :] = jnp.zeros((HALO, d_pool), BF16)

    a = proj[:, :d_pool]
    zcat_ref[HALO:, :] = a.astype(BF16)

    pos1 = (j * TB + lax.broadcasted_iota(I32, (TB, 1), 0) + 1).astype(F32)
    for g, w in enumerate(POOL_WINDOWS):
        c0, c1 = g * d_pool_g, (g + 1) * d_pool_g
        s = _dot(band_ref[g], zcat_ref[:, c0:c1])
        cnt = jnp.minimum(pos1, float(w))
        d = s / cnt - a[:, c0:c1]
        y = _dot(d.astype(BF16), wpool_ref[g])
        mixin_ref[:, c0:c1] = (y * pscale_ref[:, c0:c1]).astype(BF16)
    zcat_ref[0:HALO, :] = zcat_ref[TB:TB + HALO, :]

    @pl.when(j == n_tiles - 1)
    def _():
        pool_ref[0] = a[TB - HALO:, :]

    u = jax.nn.gelu(proj[:, d_pool:d_pool + d_sgu])
    v = jax.nn.gelu(proj[:, d_pool + d_sgu:])
    r = lax.broadcasted_iota(I32, (CHUNK, CHUNK), 0)
    c = lax.broadcasted_iota(I32, (CHUNK, CHUNK), 1)
    tril = r >= c
    for h in range(d_sgu // SGU_HEAD_DIM):
        l0, l1 = h * SGU_HEAD_DIM, (h + 1) * SGU_HEAD_DIM
        vn = _ln_rows(v[:, l0:l1], sg_ref[h:h + 1, :], sb_ref[h:h + 1, :]).astype(BF16)
        wst = jnp.where(tril, ws_ref[h], 0.0).astype(BF16)
        for ci in range(TB // CHUNK):
            r0, r1 = ci * CHUNK, (ci + 1) * CHUNK
            mixed = _dot(wst, vn[r0:r1]) + bsb_ref[h]
            mixin_ref[r0:r1, d_pool + l0:d_pool + l1] = (u[r0:r1, l0:l1] * mixed).astype(BF16)


def _sample_tile(T, NB, proj, st_ref, wpool_ref, pscale_ref, sg_ref, sb_ref, wrow_ref, brow_ref,
                 a_ref, vn_ref, dbuf_ref, mixin_ref):
    d_pool = dbuf_ref.shape[1]
    d_pool_g = d_pool // len(POOL_WINDOWS)
    d_sgu = mixin_ref.shape[1] - d_pool

    a = proj[:, :d_pool]
    for t in range(T):
        a_ref[t] = a[t * NB:(t + 1) * NB]

    for g, w in enumerate(POOL_WINDOWS):
        c0, c1 = g * d_pool_g, (g + 1) * d_pool_g
        for t in range(T):
            cnt = float(min(PAST_LEN + t + 1, w))
            acc = None
            for s in range(POOL_BUF + t - w + 1, POOL_BUF + t + 1):
                if s < POOL_BUF:
                    term = st_ref[s, :, c0:c1]
                else:
                    term = a[(s - POOL_BUF) * NB:(s - POOL_BUF + 1) * NB, c0:c1]
                acc = term if acc is None else acc + term
            d = acc / cnt - a[t * NB:(t + 1) * NB, c0:c1]
            dbuf_ref[t * NB:(t + 1) * NB, c0:c1] = d.astype(BF16)
        y = _dot(dbuf_ref[0:T * NB, c0:c1], wpool_ref[g])
        mixin_ref[:, c0:c1] = (y * pscale_ref[:, c0:c1]).astype(BF16)

    u = jax.nn.gelu(proj[:, d_pool:d_pool + d_sgu])
    v = jax.nn.gelu(proj[:, d_pool + d_sgu:])
    for h in range(d_sgu // SGU_HEAD_DIM):
        l0, l1 = h * SGU_HEAD_DIM, (h + 1) * SGU_HEAD_DIM
        vn = _ln_rows(v[:, l0:l1], sg_ref[h:h + 1, :], sb_ref[h:h + 1, :])
        for t in range(T):
            vn_ref[t, :, l0:l1] = vn[t * NB:(t + 1) * NB]
    for t in range(T):
        mixed = brow_ref[t:t + 1, :]
        for s in range(t + 1):
            mixed = mixed + wrow_ref[t, s:s + 1, :] * vn_ref[s]
        mixin_ref[t * NB:(t + 1) * NB, d_pool:] = (u[t * NB:(t + 1) * NB] * mixed).astype(BF16)


def _mixer_body(alpha, n_pt, tiles_per_seq, xp_ref, xs_ref, st_ref, win_ref, band_ref, wpool_ref, pscale_ref,
                sg_ref, sb_ref, ws_ref, bsb_ref, wrow_ref, brow_ref, wout_ref, g1_ref, b1_ref,
                h1_ref, h1b_ref, pool_ref, a_ref, vn_ref, zcat_ref, mixin_ref):
    i = pl.program_id(0)

    def finish(x):
        mix = _dot(mixin_ref[...], wout_ref[...])
        h1 = _ln_rows(alpha * x + mix, g1_ref[...], b1_ref[...])
        h1_ref[...] = h1
        h1b_ref[...] = h1.astype(BF16)

    @pl.when(i < n_pt)
    def _():
        x = xp_ref[0]
        proj = _dot(x.astype(BF16), win_ref[...])
        _prompt_tile(i % tiles_per_seq, tiles_per_seq, x, proj, band_ref, wpool_ref, pscale_ref, sg_ref, sb_ref,
                     ws_ref, bsb_ref, pool_ref, zcat_ref, mixin_ref)
        finish(x)

    @pl.when(i >= n_pt)
    def _():
        T, NB, D = xs_ref.shape
        x = xs_ref[...].reshape(T * NB, D)
        proj = _dot(x.astype(BF16), win_ref[...])
        _sample_tile(T, NB, proj, st_ref, wpool_ref, pscale_ref, sg_ref, sb_ref, wrow_ref, brow_ref,
                     a_ref, vn_ref, zcat_ref, mixin_ref)
        finish(x)


def _pool_band(tm):
    t = np.arange(tm)[:, None] + HALO
    s = np.arange(tm + HALO)[None, :]
    return jnp.asarray(np.stack([((s <= t) & (s >= t - w + 1)) for w in POOL_WINDOWS]).astype(np.float32), BF16)


def _mixer(xp, xs_t, st_t, win_b, wpool_b, pscale, sg, sb, ws, bsb, wrow, brow, wout_b, g1, b1, alpha):
    B, T, D = xp.shape
    TS, NS, _ = xs_t.shape
    d_pool = st_t.shape[2]
    d_sgu = wrow.shape[2]
    tps = T // TB
    n_pt = B * tps
    NB = TB // TS
    n_st = NS // NB
    n = B * T + NS * TS
    pt = lambda i: jnp.minimum(i, n_pt - 1)
    stile = lambda i: jnp.maximum(i - n_pt, 0)
    body = functools.partial(_mixer_body, alpha, n_pt, tps)
    consts = (win_b, _pool_band(TB), wpool_b, pscale, sg, sb, ws, bsb, wrow, brow, wout_b, g1, b1)
    return pl.pallas_call(
        body,
        grid=(n_pt + n_st,),
        in_specs=[
            pl.BlockSpec((1, TB, D), lambda i: (pt(i) // tps, pt(i) % tps, 0)),
            pl.BlockSpec((TS, NB, D), lambda i: (0, stile(i), 0)),
            pl.BlockSpec((POOL_BUF, NB, d_pool), lambda i: (0, stile(i), 0)),
        ] + [_const_spec(c.shape) for c in consts],
        out_specs=[
            pl.BlockSpec((TB, D), lambda i: (i, 0)),
            pl.BlockSpec((TB, D), lambda i: (i, 0)),
            pl.BlockSpec((1, HALO, d_pool), lambda i: (pt(i) // tps, 0, 0)),
            pl.BlockSpec((TS, NB, d_pool), lambda i: (0, stile(i), 0)),
            pl.BlockSpec((TS, NB, d_sgu), lambda i: (0, stile(i), 0)),
        ],
        out_shape=[
            jax.ShapeDtypeStruct((n, D), F32),
            jax.ShapeDtypeStruct((n, D), BF16),
            jax.ShapeDtypeStruct((B, HALO, d_pool), F32),
            jax.ShapeDtypeStruct((TS, NS, d_pool), F32),
            jax.ShapeDtypeStruct((TS, NS, d_sgu), F32),
        ],
        scratch_shapes=[
            pltpu.VMEM((TB + HALO, d_pool), BF16),
            pltpu.VMEM((TB, D), BF16),
        ],
        compiler_params=pltpu.CompilerParams(
            dimension_semantics=("arbitrary",), vmem_limit_bytes=VMEM_LIMIT),
        name="mixer",
    )(xp, xs_t, st_t, *consts)


def _router_body(h_ref, rwt_ref, rb_ref, upper_ref, lstrict_ref, dest_ref, wsel_ref, cnt_ref):
    per_group = N_EXPERTS // N_EXPERT_GROUPS
    neg = -jnp.inf
    logits_t = lax.dot_general(rwt_ref[...], h_ref[...], (((1,), (1,)), ((), ())),
                               precision=lax.Precision.HIGHEST, preferred_element_type=F32)
    s_t = jax.nn.sigmoid(logits_t)
    b_t = s_t + rb_ref[...]

    io_g = lax.broadcasted_iota(I32, (per_group, TB), 0)
    gs = []
    for g in range(N_EXPERT_GROUPS):
        xg = b_t[g * per_group:(g + 1) * per_group, :]
        m1 = jnp.max(xg, axis=0, keepdims=True)
        i1 = jnp.min(jnp.where(xg == m1, io_g, per_group), axis=0, keepdims=True)
        m2 = jnp.max(jnp.where(io_g == i1, neg, xg), axis=0, keepdims=True)
        gs.append(m1 + m2)
    masked = []
    for g in range(N_EXPERT_GROUPS):
        rank = jnp.zeros((1, TB), F32)
        for g2 in range(N_EXPERT_GROUPS):
            if g2 != g:
                ahead = (gs[g2] >= gs[g]) if g2 < g else (gs[g2] > gs[g])
                rank = rank + jnp.where(ahead, 1.0, 0.0)
        keep = rank < float(TOPK_GROUPS)
        masked.append(jnp.where(keep, b_t[g * per_group:(g + 1) * per_group, :], neg))
    xm = jnp.concatenate(masked, axis=0)

    io_e = lax.broadcasted_iota(I32, (N_EXPERTS, TB), 0)
    onehots = []
    sel = jnp.zeros((N_EXPERTS, TB), F32)
    for _ in range(TOP_K):
        m = jnp.max(xm, axis=0, keepdims=True)
        idx = jnp.min(jnp.where(xm == m, io_e, N_EXPERTS), axis=0, keepdims=True)
        oh = io_e == idx
        onehots.append(oh)
        sel = jnp.where(oh, 1.0, sel)
        xm = jnp.where(oh, neg, xm)

    ssel = sel * s_t
    denom = jnp.sum(ssel, axis=0, keepdims=True)
    comb = ssel / denom * ROUTE_SCALE

    rank_t = _dot(sel.astype(BF16), upper_ref[...])
    cnt = jnp.sum(sel, axis=1, keepdims=True)
    cnt_i = cnt.astype(I32)
    seg16 = ((cnt_i + (SEG - 1)) // SEG).astype(F32)
    off16 = _dot(lstrict_ref[...], jnp.broadcast_to(seg16, (N_EXPERTS, 128)).astype(BF16))
    d_t = rank_t + off16[:, 0:1] * float(SEG)
    for k in range(TOP_K):
        dest_ref[0, k:k + 1, :] = jnp.sum(jnp.where(onehots[k], d_t, 0.0), axis=0, keepdims=True).astype(I32)
        wsel_ref[0, k:k + 1, :] = jnp.sum(jnp.where(onehots[k], comb, 0.0), axis=0, keepdims=True)
    cnt_ref[0] = jnp.broadcast_to(cnt_i, (N_EXPERTS, 128))


def _router(h1_all, rwt, rb_col):
    n, D = h1_all.shape
    nb = n // TB
    upper = jnp.asarray(np.triu(np.ones((TB, TB), np.float32), 1), BF16)
    lstrict = jnp.asarray(np.tril(np.ones((N_EXPERTS, N_EXPERTS), np.float32), -1), BF16)
    return pl.pallas_call(
        _router_body,
        grid=(nb,),
        in_specs=[
            pl.BlockSpec((TB, D), lambda i: (i, 0)),
            _const_spec(rwt.shape),
            _const_spec(rb_col.shape),
            _const_spec(upper.shape),
            _const_spec(lstrict.shape),
        ],
        out_specs=[
            pl.BlockSpec((1, TOP_K, TB), lambda i: (i, 0, 0)),
            pl.BlockSpec((1, TOP_K, TB), lambda i: (i, 0, 0)),
            pl.BlockSpec((1, N_EXPERTS, 128), lambda i: (i, 0, 0)),
        ],
        out_shape=[
            jax.ShapeDtypeStruct((nb, TOP_K, TB), I32),
            jax.ShapeDtypeStruct((nb, TOP_K, TB), F32),
            jax.ShapeDtypeStruct((nb, N_EXPERTS, 128), I32),
        ],
        compiler_params=pltpu.CompilerParams(dimension_semantics=("arbitrary",)),
        name="router",
    )(h1_all, rwt, rb_col, upper, lstrict)


def _seg_copies(b, seg_g, seg_l, seg_n, make_copy, wait):
    def per_expert(e, carry):
        idx = b * N_EXPERTS + e
        g0, l0, n = seg_g[idx], seg_l[idx], seg_n[idx]

        def per_granule(j, c):
            cp = make_copy(pl.multiple_of(l0 + j * SEG, SEG), pl.multiple_of(g0 + j * SEG, SEG))
            if wait:
                cp.wait()
            else:
                cp.start()
            return c

        lax.fori_loop(0, n, per_granule, 0)
        return carry

    lax.fori_loop(0, N_EXPERTS, per_expert, 0)


def _dispatch_body(seg_g, seg_l, seg_n, nch, tail_g, tail_n, dest_ref, wsel_ref, h_ref, tokid_ref, ones_ref,
                   xs_hbm, src_ref, wrow_ref, xloc_ref, zero_ref, sem):
    b = pl.program_id(0)
    nb = pl.num_programs(0)
    dest = dest_ref[0]
    wsel = wsel_ref[0]
    h = h_ref[...]
    src_ref[...] = jnp.zeros(src_ref.shape, F32)
    wrow_ref[...] = jnp.zeros(wrow_ref.shape, F32)

    def chunk(c, carry):
        r0 = pl.multiple_of(c * KC, KC)
        rows = r0 + lax.broadcasted_iota(I32, (KC, TB), 0)
        s = jnp.zeros((KC, TB), F32)
        wm = jnp.zeros((KC, TB), F32)
        for k in range(TOP_K):
            hit = dest[k:k + 1, :] == rows
            s = jnp.where(hit, 1.0, s)
            wm = jnp.where(hit, wsel[k:k + 1, :], wm)
        sb = s.astype(BF16)
        xloc_ref[pl.ds(r0, KC), :] = _dot(sb, h).astype(BF16)
        src_ref[0, pl.ds(r0, KC), :] = _dot(sb, tokid_ref[...])
        wrow_ref[0, pl.ds(r0, KC), :] = _dot(wm.astype(BF16), ones_ref[...])
        return carry

    lax.fori_loop(0, nch[b], chunk, 0)

    def seg_copy(l, g):
        return pltpu.make_async_copy(xloc_ref.at[pl.ds(l, SEG), :], xs_hbm.at[pl.ds(g, SEG), :], sem)

    _seg_copies(b, seg_g, seg_l, seg_n, seg_copy, wait=False)

    def tail_copy(g):
        return pltpu.make_async_copy(zero_ref, xs_hbm.at[pl.ds(g, SEG), :], sem)

    def tails(wait):
        def per_expert(e, carry):
            def per_granule(j, c):
                cp = tail_copy(pl.multiple_of(tail_g[e] + j * SEG, SEG))
                if wait:
                    cp.wait()
                else:
                    cp.start()
                return c
            lax.fori_loop(0, tail_n[e], per_granule, 0)
            return carry
        lax.fori_loop(0, N_EXPERTS, per_expert, 0)

    @pl.when(b == nb - 1)
    def _():
        zero_ref[...] = jnp.zeros(zero_ref.shape, BF16)
        tails(False)

    _seg_copies(b, seg_g, seg_l, seg_n, seg_copy, wait=True)

    @pl.when(b == nb - 1)
    def _():
        tails(True)


def _dispatch(meta, dest, wsel, h1b_all, n_rows_sorted, m_out_max):
    n, D = h1b_all.shape
    nb = n // TB
    tokid = jnp.asarray(np.broadcast_to(np.arange(TB, dtype=np.float32)[:, None], (TB, 128)), BF16)
    ones = jnp.ones((TB, 128), BF16)
    cs = lambda shape: pl.BlockSpec(shape, lambda i, *_: (0,) * len(shape), pipeline_mode=pl.Buffered(1))
    grid_spec = pltpu.PrefetchScalarGridSpec(
        num_scalar_prefetch=6,
        grid=(nb,),
        in_specs=[
            pl.BlockSpec((1, TOP_K, TB), lambda i, *_: (i, 0, 0)),
            pl.BlockSpec((1, TOP_K, TB), lambda i, *_: (i, 0, 0)),
            pl.BlockSpec((TB, D), lambda i, *_: (i, 0)),
            cs(tokid.shape), cs(ones.shape),
        ],
        out_specs=[
            pl.BlockSpec(memory_space=pl.ANY),
            pl.BlockSpec((1, m_out_max, 128), lambda i, *_: (i, 0, 0)),
            pl.BlockSpec((1, m_out_max, 128), lambda i, *_: (i, 0, 0)),
        ],
        scratch_shapes=[
            pltpu.VMEM((m_out_max, D), BF16),
            pltpu.VMEM((SEG, D), BF16),
            pltpu.SemaphoreType.DMA(()),
        ],
    )
    return pl.pallas_call(
        _dispatch_body,
        grid_spec=grid_spec,
        out_shape=[
            jax.ShapeDtypeStruct((n_rows_sorted, D), BF16),
            jax.ShapeDtypeStruct((nb, m_out_max, 128), F32),
            jax.ShapeDtypeStruct((nb, m_out_max, 128), F32),
        ],
        compiler_params=pltpu.CompilerParams(
            dimension_semantics=("arbitrary",), vmem_limit_bytes=VMEM_LIMIT, has_side_effects=True),
        name="dispatch",
    )(meta["seg_g"], meta["seg_l"], meta["seg_n"], meta["nch"], meta["tail_g"], meta["tail_n"], dest, wsel,
      h1b_all, tokid, ones)


def _ffn_body(tile_e, tile_src, tile_valid, e_slot, e_next, x_ref, wg_hbm, wu_hbm, wd_hbm, y_ref,
              wg_f, wu_f, wd_f, wgu_b, wd_b, sems):
    i = pl.program_id(0)
    f = wg_f.shape[2]
    e = tile_e[i]
    e_prev = tile_e[jnp.maximum(i - 1, 0)]

    def weight_copies(expert, slot):
        return (pltpu.make_async_copy(wg_hbm.at[expert], wg_f.at[slot], sems.at[slot, 0]),
                pltpu.make_async_copy(wu_hbm.at[expert], wu_f.at[slot], sems.at[slot, 1]),
                pltpu.make_async_copy(wd_hbm.at[expert], wd_f.at[slot], sems.at[slot, 2]))

    @pl.when((i == 0) | (e != e_prev))
    def _():
        slot = e_slot[e]

        @pl.when(i == 0)
        def _():
            for cp in weight_copies(e, slot):
                cp.start()

        for cp in weight_copies(e, slot):
            cp.wait()
        nxt = e_next[e]

        @pl.when(nxt >= 0)
        def _():
            for cp in weight_copies(nxt, 1 - slot):
                cp.start(priority=1)

        wgu_b[:, :f] = wg_f[slot].astype(BF16)
        wgu_b[:, f:] = wu_f[slot].astype(BF16)
        wd_b[...] = wd_f[slot].astype(BF16)

    @pl.when(tile_valid[i] == 1)
    def _():
        gu = _dot(x_ref[...], wgu_b[...])
        act = (jax.nn.silu(gu[:, :f]) * gu[:, f:]).astype(BF16)
        y_ref[...] = _dot(act, wd_b[...]).astype(BF16)


def _expert_ffn(meta, xs, w_gate, w_up, w_down):
    R, D = xs.shape
    E, _, f = w_gate.shape
    nt = R // TR
    grid_spec = pltpu.PrefetchScalarGridSpec(
        num_scalar_prefetch=5,
        grid=(nt,),
        in_specs=[
            pl.BlockSpec((TR, D), lambda i, te, ts, *_: (ts[i], 0)),
            pl.BlockSpec(memory_space=pl.ANY),
            pl.BlockSpec(memory_space=pl.ANY),
            pl.BlockSpec(memory_space=pl.ANY),
        ],
        out_specs=pl.BlockSpec((TR, D), lambda i, te, ts, *_: (ts[i], 0)),
        scratch_shapes=[
            pltpu.VMEM((2, D, f), F32),
            pltpu.VMEM((2, D, f), F32),
            pltpu.VMEM((2, f, D), F32),
            pltpu.VMEM((D, 2 * f), BF16),
            pltpu.VMEM((f, D), BF16),
            pltpu.SemaphoreType.DMA((2, 3)),
        ],
    )
    return pl.pallas_call(
        _ffn_body,
        grid_spec=grid_spec,
        out_shape=jax.ShapeDtypeStruct((R, D), BF16),
        compiler_params=pltpu.CompilerParams(
            dimension_semantics=("arbitrary",), vmem_limit_bytes=VMEM_LIMIT),
        name="expert_ffn",
    )(meta["tile_e"], meta["tile_src"], meta["tile_valid"], meta["e_slot"], meta["e_next"],
      xs, w_gate, w_up, w_down)


def _combine_body(alpha, n_pb, seg_g, seg_l, seg_n, nch, src_ref, wrow_ref, h1_ref, h1b_ref, p_ref,
                  wsg_ref, wsu_ref, wsd_ref, g2_ref, b2_ref, wpg_ref, bpg_ref, wpe_ref, ys_hbm,
                  yp_ref, ysm_ref, yloc_ref, acc_ref, sem):
    b = pl.program_id(0)
    n_chunks = nch[b]

    last0 = pl.multiple_of((n_chunks - 1) * KCC, KCC)
    yloc_ref[pl.ds(last0, KCC), :] = jnp.zeros((KCC, yloc_ref.shape[1]), BF16)

    def seg_copy(l, g):
        return pltpu.make_async_copy(ys_hbm.at[pl.ds(g, SEG), :], yloc_ref.at[pl.ds(l, SEG), :], sem)

    _seg_copies(b, seg_g, seg_l, seg_n, seg_copy, wait=False)

    hb = h1b_ref[...]
    sh = (jax.nn.silu(_dot(hb, wsg_ref[...])) * _dot(hb, wsu_ref[...])).astype(BF16)
    acc_ref[...] = _dot(sh, wsd_ref[...])

    _seg_copies(b, seg_g, seg_l, seg_n, seg_copy, wait=True)

    def chunk(c, carry):
        r0 = pl.multiple_of(c * KCC, KCC)
        src = src_ref[0, pl.ds(r0, KCC), :]
        w = wrow_ref[0, pl.ds(r0, KCC), :]
        lane = lax.broadcasted_iota(I32, src.shape, 1).astype(F32)
        wm = jnp.concatenate([jnp.where(src == lane + float(o), w, 0.0) for o in range(0, TB, src.shape[1])],
                             axis=1).astype(BF16)
        acc_ref[...] += lax.dot_general(wm, yloc_ref[pl.ds(r0, KCC), :],
                                        (((0,), (0,)), ((), ())), preferred_element_type=F32)
        return carry

    lax.fori_loop(0, n_chunks, chunk, 0)

    h2 = _ln_rows(alpha * h1_ref[...] + acc_ref[...], g2_ref[...], b2_ref[...])
    gate = jax.nn.sigmoid(_dot(h2.astype(BF16), wpg_ref[...]) + bpg_ref[...])
    pe = _dot(p_ref[...].astype(BF16), wpe_ref[...])
    y = h2 + gate * pe

    @pl.when(b < n_pb)
    def _():
        yp_ref[...] = y

    @pl.when(b >= n_pb)
    def _():
        ysm_ref[...] = y


def _combine(meta, src_rep, w_rep, h1_all, h1b_all, p_all, wsg_b, wsu_b, wsd_b, g2, b2, wpg_b, bpg, wpe_b, ys,
             m_out_max, n_prompt, alpha):
    n, D = h1_all.shape
    nb = n // TB
    n_pb = n_prompt // TB
    d_pe = p_all.shape[1]
    cs = lambda shape: pl.BlockSpec(shape, lambda i, *_: (0,) * len(shape), pipeline_mode=pl.Buffered(1))
    grid_spec = pltpu.PrefetchScalarGridSpec(
        num_scalar_prefetch=4,
        grid=(nb,),
        in_specs=[
            pl.BlockSpec((1, m_out_max, 128), lambda i, *_: (i, 0, 0)),
            pl.BlockSpec((1, m_out_max, 128), lambda i, *_: (i, 0, 0)),
            pl.BlockSpec((TB, D), lambda i, *_: (i, 0)),
            pl.BlockSpec((TB, D), lambda i, *_: (i, 0)),
            pl.BlockSpec((TB, d_pe), lambda i, *_: (i, 0)),
            cs(wsg_b.shape), cs(wsu_b.shape), cs(wsd_b.shape), cs(g2.shape), cs(b2.shape),
            cs(wpg_b.shape), cs(bpg.shape), cs(wpe_b.shape),
            pl.BlockSpec(memory_space=pl.ANY),
        ],
        out_specs=[
            pl.BlockSpec((TB, D), lambda i, *_: (jnp.minimum(i, n_pb - 1), 0)),
            pl.BlockSpec((TB, D), lambda i, *_: (jnp.maximum(i - n_pb, 0), 0)),
        ],
        scratch_shapes=[
            pltpu.VMEM((m_out_max, D), BF16),
            pltpu.VMEM((TB, D), F32),
            pltpu.SemaphoreType.DMA(()),
        ],
    )
    return pl.pallas_call(
        functools.partial(_combine_body, alpha, n_pb),
        grid_spec=grid_spec,
        out_shape=[jax.ShapeDtypeStruct((n_prompt, D), F32), jax.ShapeDtypeStruct((n - n_prompt, D), F32)],
        compiler_params=pltpu.CompilerParams(
            dimension_semantics=("arbitrary",), vmem_limit_bytes=VMEM_LIMIT),
        name="combine",
    )(meta["seg_g"], meta["seg_l"], meta["seg_n"], meta["nchc"], src_rep, w_rep, h1_all, h1b_all, p_all,
      wsg_b, wsu_b, wsd_b, g2, b2, wpg_b, bpg, wpe_b, ys)


def _sort_meta(cnt, n_tiles_max):
    segp = (cnt + (SEG - 1)) // SEG * SEG
    loc = jnp.cumsum(segp, axis=1) - segp
    before = jnp.cumsum(segp, axis=0) - segp
    total = jnp.sum(segp, axis=0)
    total_p = (total + (TR - 1)) // TR * TR
    ends = jnp.cumsum(total_p)
    base = ends - total_p
    glob = base[None, :] + before
    n_valid = ends[-1] // TR
    tiles = jnp.arange(n_tiles_max, dtype=I32)
    tile_src = jnp.minimum(tiles, n_valid - 1)
    tile_e = jnp.sum((ends[None, :] <= (tile_src * TR)[:, None]).astype(I32), axis=1)
    experts = jnp.arange(N_EXPERTS, dtype=I32)
    active = total_p > 0
    later = active[None, :] & (experts[None, :] > experts[:, None])
    e_next = jnp.min(jnp.where(later, experts[None, :], N_EXPERTS), axis=1)
    return {
        "seg_g": glob.reshape(-1).astype(I32),
        "seg_l": loc.reshape(-1).astype(I32),
        "seg_n": (segp // SEG).reshape(-1).astype(I32),
        "nch": ((jnp.sum(segp, axis=1) + (KC - 1)) // KC).astype(I32),
        "nchc": ((jnp.sum(segp, axis=1) + (KCC - 1)) // KCC).astype(I32),
        "tail_g": (base + total).astype(I32),
        "tail_n": ((total_p - total) // SEG).astype(I32),
        "tile_e": jnp.minimum(tile_e, N_EXPERTS - 1).astype(I32),
        "tile_src": tile_src.astype(I32),
        "tile_valid": (tiles < n_valid).astype(I32),
        "e_slot": ((jnp.cumsum(active.astype(I32)) - 1) % 2).astype(I32),
        "e_next": jnp.where(e_next < N_EXPERTS, e_next, -1).astype(I32),
    }


def _layer(xp, xs, st, pp, ps, w_in, w_pool, pool_scale, sgu_ln_g, sgu_ln_b, w_s, b_s, w_out, ln1_g, ln1_b,
           router_w, router_bias, w_gate, w_up, w_down, ws_gate, ws_up, ws_down, ln2_g, ln2_b, w_pe, w_pgate,
           b_pgate, alpha):
    B, T, D = xp.shape
    NS, TS, _ = xs.shape
    n_prompt, n_sample = B * T, NS * TS
    n = n_prompt + n_sample
    n_heads = w_s.shape[0]
    row = lambda v: v.reshape(1, -1)

    x_t = jnp.transpose(xs, (1, 0, 2))
    st_t = jnp.transpose(st, (1, 0, 2))
    bsb = jnp.broadcast_to(b_s[:, :, None], (n_heads, CHUNK, SGU_HEAD_DIM))
    wrow = jnp.repeat(jnp.transpose(w_s[:, :TS, :TS], (1, 2, 0)), SGU_HEAD_DIM, axis=2)
    brow = jnp.repeat(jnp.transpose(b_s[:, :TS], (1, 0)), SGU_HEAD_DIM, axis=1)
    h1_all, h1b_all, pool_p, a_s, vn_s = _mixer(
        xp, x_t, st_t, w_in.astype(BF16), w_pool.astype(BF16), row(pool_scale), sgu_ln_g, sgu_ln_b, w_s, bsb,
        wrow, brow, w_out.astype(BF16), row(ln1_g), row(ln1_b), alpha)

    dest, wsel, cnt = _router(h1_all, jnp.transpose(router_w), router_bias.reshape(-1, 1))
    nb = n // TB
    m_out_max = TB * TOP_K + N_EXPERTS * SEG
    rows_max = n * TOP_K + nb * N_EXPERTS * (SEG - 1) + N_EXPERTS * (TR - SEG)
    n_tiles_max = -(-rows_max // TR)
    meta = _sort_meta(cnt[:, :, 0], n_tiles_max)
    xsort, src_rep, w_rep = _dispatch(meta, dest, wsel, h1b_all, n_tiles_max * TR, m_out_max)
    ysort = _expert_ffn(meta, xsort, w_gate, w_up, w_down)

    nbs = TB // TS
    ps_t = jnp.transpose(ps.reshape(NS // nbs, nbs, TS, -1), (0, 2, 1, 3)).reshape(n_sample, -1)
    p_all = jnp.concatenate([pp.reshape(n_prompt, -1), ps_t], axis=0)
    y_p, y_s = _combine(meta, src_rep, w_rep, h1_all, h1b_all, p_all, ws_gate.astype(BF16), ws_up.astype(BF16),
                        ws_down.astype(BF16), row(ln2_g), row(ln2_b), w_pgate.astype(BF16), row(b_pgate),
                        w_pe.astype(BF16), ysort, m_out_max, n_prompt, alpha)

    yp = y_p.reshape(B, T, D)
    ys = jnp.transpose(y_s.reshape(NS // nbs, TS, nbs, D), (0, 2, 1, 3)).reshape(NS, TS, D)
    new_pool_p = pool_p[:, HALO - POOL_BUF:, :]
    new_pool_s = jnp.concatenate([st, jnp.transpose(a_s, (1, 0, 2))], axis=1)[:, -POOL_BUF:]
    vn = jnp.transpose(vn_s, (1, 0, 2)).reshape(NS, TS, n_heads, SGU_HEAD_DIM)
    return yp, ys, new_pool_p, new_pool_s, vn


def kernel(x_prompt, x_sample, state_pool, p_prompt, p_sample, w_in, w_pool, pool_scale, sgu_ln_g, sgu_ln_b, w_s, b_s, w_out, ln1_g, ln1_b, router_w, router_bias, w_gate, w_up, w_down, ws_gate, ws_up, ws_down, ln2_g, ln2_b, w_pe, w_pgate, b_pgate):
    depth = w_in.shape[0]
    alpha = (2.0 * depth) ** 0.25
    hp, hs = x_prompt, x_sample
    pool_p, pool_s, v_s = [], [], []
    for i in range(depth):
        hp, hs, bp, bs, vs = _layer(
            hp, hs, state_pool[i], p_prompt[i], p_sample[i], w_in[i], w_pool[i], pool_scale[i], sgu_ln_g[i],
            sgu_ln_b[i], w_s[i], b_s[i], w_out[i], ln1_g[i], ln1_b[i], router_w[i], router_bias[i], w_gate[i],
            w_up[i], w_down[i], ws_gate[i], ws_up[i], ws_down[i], ln2_g[i], ln2_b[i], w_pe[i], w_pgate[i],
            b_pgate[i], alpha)
        pool_p.append(bp)
        pool_s.append(bs)
        v_s.append(vs)
    return hp, hs, jnp.stack(pool_p), jnp.stack(pool_s), jnp.stack(v_s)
```

```python
import functools

import jax
import jax.numpy as jnp
import numpy as np
from jax import lax
from jax.experimental import pallas as pl
from jax.experimental.pallas import tpu as pltpu

F32 = jnp.float32
BF16 = jnp.bfloat16
I32 = jnp.int32

POOL_WINDOWS = (2, 4, 8, 16)
POOL_BUF = max(POOL_WINDOWS) - 1
CHUNK = 128
SGU_HEAD_DIM = 128
N_EXPERTS = 64
TOP_K = 8
N_EXPERT_GROUPS = 8
TOPK_GROUPS = 4
ROUTE_SCALE = 2.5
LN_EPS = 1e-5
PAST_LEN = 16384

TB = 256
HALO = 16
TR = 256
SEG = 16
KC = 256
KCC = 512
VMEM_LIMIT = 56 * 1024 * 1024


def _const_spec(shape):
    n = len(shape)
    return pl.BlockSpec(shape, lambda *_: (0,) * n, pipeline_mode=pl.Buffered(1))


def _ln_rows(x, g, b):
    mu = jnp.mean(x, axis=-1, keepdims=True)
    xc = x - mu
    var = jnp.mean(xc * xc, axis=-1, keepdims=True)
    return xc * lax.rsqrt(var + LN_EPS) * g + b


def _dot(a, b):
    return jnp.dot(a, b, preferred_element_type=F32)


def _prompt_tile(j, n_tiles, x, proj, band_ref, wpool_ref, pscale_ref, sg_ref, sb_ref, ws_ref, bsb_ref,
                 pool_ref, zcat_ref, mixin_ref):
    d_pool = zcat_ref.shape[1]
    d_pool_g = d_pool // len(POOL_WINDOWS)
    d_sgu = mixin_ref.shape[1] - d_pool

    @pl.when(j == 0)
    def _():
        zcat_ref[0:HALO, :] = jnp.zeros((HALO, d_pool), BF16)

    a = proj[:, :d_pool]
    zcat_ref[HALO:, :] = a.astype(BF16)

    pos1 = (j * TB + lax.broadcasted_iota(I32, (TB, 1), 0) + 1).astype(F32)
    for g, w in enumerate(POOL_WINDOWS):
        c0, c1 = g * d_pool_g, (g + 1) * d_pool_g
        s = _dot(band_ref[g], zcat_ref[:, c0:c1])
        cnt = jnp.minimum(pos1, float(w))
        d = s / cnt - a[:, c0:c1]
        y = _dot(d.astype(BF16), wpool_ref[g])
        mixin_ref[:, c0:c1] = (y * pscale_ref[:, c0:c1]).astype(BF16)
    zcat_ref[0:HALO, :] = zcat_ref[TB:TB + HALO, :]

    @pl.when(j == n_tiles - 1)
    def _():
        pool_ref[0] = a[TB - HALO:, :]

    u = jax.nn.gelu(proj[:, d_pool:d_pool + d_sgu])
    v = jax.nn.gelu(proj[:, d_pool + d_sgu:])
    r = lax.broadcasted_iota(I32, (CHUNK, CHUNK), 0)
    c = lax.broadcasted_iota(I32, (CHUNK, CHUNK), 1)
    tril = r >= c
    for h in range(d_sgu // SGU_HEAD_DIM):
        l0, l1 = h * SGU_HEAD_DIM, (h + 1) * SGU_HEAD_DIM
        vn = _ln_rows(v[:, l0:l1], sg_ref[h:h + 1, :], sb_ref[h:h + 1, :]).astype(BF16)
        wst = jnp.where(tril, ws_ref[h], 0.0).astype(BF16)
        for ci in range(TB // CHUNK):
            r0, r1 = ci * CHUNK, (ci + 1) * CHUNK
            mixed = _dot(wst, vn[r0:r1]) + bsb_ref[h]
            mixin_ref[r0:r1, d_pool + l0:d_pool + l1] = (u[r0:r1, l0:l1] * mixed).astype(BF16)


def _sample_tile(T, NB, proj, st_ref, wpool_ref, pscale_ref, sg_ref, sb_ref, wrow_ref, brow_ref,
                 a_ref, vn_ref, dbuf_ref, mixin_ref):
    d_pool = dbuf_ref.shape[1]
    d_pool_g = d_pool // len(POOL_WINDOWS)
    d_sgu = mixin_ref.shape[1] - d_pool

    a = proj[:, :d_pool]
    for t in range(T):
        a_ref[t] = a[t * NB:(t + 1) * NB]

    for g, w in enumerate(POOL_WINDOWS):
        c0, c1 = g * d_pool_g, (g + 1) * d_pool_g
        for t in range(T):
            cnt = float(min(PAST_LEN + t + 1, w))
            acc = None
            for s in range(POOL_BUF + t - w + 1, POOL_BUF + t + 1):
                if s < POOL_BUF:
                    term = st_ref[s, :, c0:c1]
                else:
                    term = a[(s - POOL_BUF) * NB:(s - POOL_BUF + 1) * NB, c0:c1]
                acc = term if acc is None else acc + term
            d = acc / cnt - a[t * NB:(t + 1) * NB, c0:c1]
            dbuf_ref[t * NB:(t + 1) * NB, c0:c1] = d.astype(BF16)
        y = _dot(dbuf_ref[0:T * NB, c0:c1], wpool_ref[g])
        mixin_ref[:, c0:c1] = (y * pscale_ref[:, c0:c1]).astype(BF16)

    u = jax.nn.gelu(proj[:, d_pool:d_pool + d_sgu])
    v = jax.nn.gelu(proj[:, d_pool + d_sgu:])
    for h in range(d_sgu // SGU_HEAD_DIM):
        l0, l1 = h * SGU_HEAD_DIM, (h + 1) * SGU_HEAD_DIM
        vn = _ln_rows(v[:, l0:l1], sg_ref[h:h + 1, :], sb_ref[h:h + 1, :])
        for t in range(T):
            vn_ref[t, :, l0:l1] = vn[t * NB:(t + 1) * NB]
    for t in range(T):
        mixed = brow_ref[t:t + 1, :]
        for s in range(t + 1):
            mixed = mixed + wrow_ref[t, s:s + 1, :] * vn_ref[s]
        mixin_ref[t * NB:(t + 1) * NB, d_pool:] = (u[t * NB:(t + 1) * NB] * mixed).astype(BF16)


def _mixer_body(alpha, n_pt, tiles_per_seq, xp_ref, xs_ref, st_ref, win_ref, band_ref, wpool_ref, pscale_ref,
                sg_ref, sb_ref, ws_ref, bsb_ref, wrow_ref, brow_ref, wout_ref, g1_ref, b1_ref,
                h1_ref, h1b_ref, pool_ref, a_ref, vn_ref, zcat_ref, mixin_ref):
    i = pl.program_id(0)

    def finish(x):
        mix = _dot(mixin_ref[...], wout_ref[...])
        h1 = _ln_rows(alpha * x + mix, g1_ref[...], b1_ref[...])
        h1_ref[...] = h1
        h1b_ref[...] = h1.astype(BF16)

    @pl.when(i < n_pt)
    def _():
        x = xp_ref[0]
        proj = _dot(x.astype(BF16), win_ref[...])
        _prompt_tile(i % tiles_per_seq, tiles_per_seq, x, proj, band_ref, wpool_ref, pscale_ref, sg_ref, sb_ref,
                     ws_ref, bsb_ref, pool_ref, zcat_ref, mixin_ref)
        finish(x)

    @pl.when(i >= n_pt)
    def _():
        T, NB, D = xs_ref.shape
        x = xs_ref[...].reshape(T * NB, D)
        proj = _dot(x.astype(BF16), win_ref[...])
        _sample_tile(T, NB, proj, st_ref, wpool_ref, pscale_ref, sg_ref, sb_ref, wrow_ref, brow_ref,
                     a_ref, vn_ref, zcat_ref, mixin_ref)
        finish(x)


def _pool_band(tm):
    t = np.arange(tm)[:, None] + HALO
    s = np.arange(tm + HALO)[None, :]
    return jnp.asarray(np.stack([((s <= t) & (s >= t - w + 1)) for w in POOL_WINDOWS]).astype(np.float32), BF16)


def _mixer(xp, xs_t, st_t, win_b, wpool_b, pscale, sg, sb, ws, bsb, wrow, brow, wout_b, g1, b1, alpha):
    B, T, D = xp.shape
    TS, NS, _ = xs_t.shape
    d_pool = st_t.shape[2]
    d_sgu = wrow.shape[2]
    tps = T // TB
    n_pt = B * tps
    NB = TB // TS
    n_st = NS // NB
    n = B * T + NS * TS
    pt = lambda i: jnp.minimum(i, n_pt - 1)
    stile = lambda i: jnp.maximum(i - n_pt, 0)
    body = functools.partial(_mixer_body, alpha, n_pt, tps)
    consts = (win_b, _pool_band(TB), wpool_b, pscale, sg, sb, ws, bsb, wrow, brow, wout_b, g1, b1)
    return pl.pallas_call(
        body,
        grid=(n_pt + n_st,),
        in_specs=[
            pl.BlockSpec((1, TB, D), lambda i: (pt(i) // tps, pt(i) % tps, 0)),
            pl.BlockSpec((TS, NB, D), lambda i: (0, stile(i), 0)),
            pl.BlockSpec((POOL_BUF, NB, d_pool), lambda i: (0, stile(i), 0)),
        ] + [_const_spec(c.shape) for c in consts],
        out_specs=[
            pl.BlockSpec((TB, D), lambda i: (i, 0)),
            pl.BlockSpec((TB, D), lambda i: (i, 0)),
            pl.BlockSpec((1, HALO, d_pool), lambda i: (pt(i) // tps, 0, 0)),
            pl.BlockSpec((TS, NB, d_pool), lambda i: (0, stile(i), 0)),
            pl.BlockSpec((TS, NB, d_sgu), lambda i: (0, stile(i), 0)),
        ],
        out_shape=[
            jax.ShapeDtypeStruct((n, D), F32),
            jax.ShapeDtypeStruct((n, D), BF16),
            jax.ShapeDtypeStruct((B, HALO, d_pool), F32),
            jax.ShapeDtypeStruct((TS, NS, d_pool), F32),
            jax.ShapeDtypeStruct((TS, NS, d_sgu), F32),
        ],
        scratch_shapes=[
            pltpu.VMEM((TB + HALO, d_pool), BF16),
            pltpu.VMEM((TB, D), BF16),
        ],
        compiler_params=pltpu.CompilerParams(
            dimension_semantics=("arbitrary",), vmem_limit_bytes=VMEM_LIMIT),
        name="mixer",
    )(xp, xs_t, st_t, *consts)


def _router_body(h_ref, rwt_ref, rb_ref, upper_ref, lstrict_ref, dest_ref, wsel_ref, cnt_ref):
    per_group = N_EXPERTS // N_EXPERT_GROUPS
    neg = -jnp.inf
    logits_t = lax.dot_general(rwt_ref[...], h_ref[...], (((1,), (1,)), ((), ())),
                               precision=lax.Precision.HIGHEST, preferred_element_type=F32)
    s_t = jax.nn.sigmoid(logits_t)
    b_t = s_t + rb_ref[...]

    io_g = lax.broadcasted_iota(I32, (per_group, TB), 0)
    gs = []
    for g in range(N_EXPERT_GROUPS):
        xg = b_t[g * per_group:(g + 1) * per_group, :]
        m1 = jnp.max(xg, axis=0, keepdims=True)
        i1 = jnp.min(jnp.where(xg == m1, io_g, per_group), axis=0, keepdims=True)
        m2 = jnp.max(jnp.where(io_g == i1, neg, xg), axis=0, keepdims=True)
        gs.append(m1 + m2)
    masked = []
    for g in range(N_EXPERT_GROUPS):
        rank = jnp.zeros((1, TB), F32)
        for g2 in range(N_EXPERT_GROUPS):
            if g2 != g:
                ahead = (gs[g2] >= gs[g]) if g2 < g else (gs[g2] > gs[g])
                rank = rank + jnp.where(ahead, 1.0, 0.0)
        keep = rank < float(TOPK_GROUPS)
        masked.append(jnp.where(keep, b_t[g * per_group:(g + 1) * per_group, :], neg))
    xm = jnp.concatenate(masked, axis=0)

    io_e = lax.broadcasted_iota(I32, (N_EXPERTS, TB), 0)
    onehots = []
    sel = jnp.zeros((N_EXPERTS, TB), F32)
    for _ in range(TOP_K):
        m = jnp.max(xm, axis=0, keepdims=True)
        idx = jnp.min(jnp.where(xm == m, io_e, N_EXPERTS), axis=0, keepdims=True)
        oh = io_e == idx
        onehots.append(oh)
        sel = jnp.where(oh, 1.0, sel)
        xm = jnp.where(oh, neg, xm)

    ssel = sel * s_t
    denom = jnp.sum(ssel, axis=0, keepdims=True)
    comb = ssel / denom * ROUTE_SCALE

    rank_t = _dot(sel.astype(BF16), upper_ref[...])
    cnt = jnp.sum(sel, axis=1, keepdims=True)
    cnt_i = cnt.astype(I32)
    seg16 = ((cnt_i + (SEG - 1)) // SEG).astype(F32)
    off16 = _dot(lstrict_ref[...], jnp.broadcast_to(seg16, (N_EXPERTS, 128)).astype(BF16))
    d_t = rank_t + off16[:, 0:1] * float(SEG)
    for k in range(TOP_K):
        dest_ref[0, k:k + 1, :] = jnp.sum(jnp.where(onehots[k], d_t, 0.0), axis=0, keepdims=True).astype(I32)
        wsel_ref[0, k:k + 1, :] = jnp.sum(jnp.where(onehots[k], comb, 0.0), axis=0, keepdims=True)
    cnt_ref[0] = jnp.broadcast_to(cnt_i, (N_EXPERTS, 128))


def _router(h1_all, rwt, rb_col):
    n, D = h1_all.shape
    nb = n // TB
    upper = jnp.asarray(np.triu(np.ones((TB, TB), np.float32), 1), BF16)
    lstrict = jnp.asarray(np.tril(np.ones((N_EXPERTS, N_EXPERTS), np.float32), -1), BF16)
    return pl.pallas_call(
        _router_body,
        grid=(nb,),
        in_specs=[
            pl.BlockSpec((TB, D), lambda i: (i, 0)),
            _const_spec(rwt.shape),
            _const_spec(rb_col.shape),
            _const_spec(upper.shape),
            _const_spec(lstrict.shape),
        ],
        out_specs=[
            pl.BlockSpec((1, TOP_K, TB), lambda i: (i, 0, 0)),
            pl.BlockSpec((1, TOP_K, TB), lambda i: (i, 0, 0)),
            pl.BlockSpec((1, N_EXPERTS, 128), lambda i: (i, 0, 0)),
        ],
        out_shape=[
            jax.ShapeDtypeStruct((nb, TOP_K, TB), I32),
            jax.ShapeDtypeStruct((nb, TOP_K, TB), F32),
            jax.ShapeDtypeStruct((nb, N_EXPERTS, 128), I32),
        ],
        compiler_params=pltpu.CompilerParams(dimension_semantics=("arbitrary",)),
        name="router",
    )(h1_all, rwt, rb_col, upper, lstrict)


def _start_granule_copies(b, gran_l, gran_g, n_gran, make_copy):
    gmax = gran_l.shape[0] // n_gran.shape[0]

    def per_granule(j, c):
        idx = b * gmax + j
        make_copy(pl.multiple_of(gran_l[idx], SEG), pl.multiple_of(gran_g[idx], SEG)).start()
        return c

    lax.fori_loop(0, n_gran[b], per_granule, 0)


def _wait_granule_copies(count, gmax, make_wait_copy):
    for k in range(int(gmax).bit_length()):
        @pl.when(((count >> k) & 1) == 1)
        def _():
            make_wait_copy(SEG << k).wait()


def _dispatch_body(gran_l, gran_g, n_gran, nch, tail_g, tail_n, dest_ref, wsel_ref, h_ref, tokid_ref, ones_ref,
                   xs_hbm, src_ref, wrow_ref, xloc_ref, zero_ref, sem, tail_sem):
    b = pl.program_id(0)
    nb = pl.num_programs(0)
    gmax = gran_l.shape[0] // n_gran.shape[0]
    dest = dest_ref[0]
    wsel = wsel_ref[0]
    h = h_ref[...]
    src_ref[...] = jnp.zeros(src_ref.shape, F32)
    wrow_ref[...] = jnp.zeros(wrow_ref.shape, F32)

    row_in_chunk = lax.broadcasted_iota(I32, (KC, TB), 0).astype(F32).astype(BF16)
    wsel_b = wsel.astype(BF16)
    one = jnp.ones((1, TB), BF16)

    def chunk(c):
        r0 = pl.multiple_of(c * KC, KC)
        rel = (dest - r0).astype(F32).astype(BF16)
        sb = jnp.zeros((KC, TB), BF16)
        wm = jnp.zeros((KC, TB), BF16)
        for k in range(TOP_K):
            hit = rel[k:k + 1, :] == row_in_chunk
            sb = jnp.where(hit, one, sb)
            wm = jnp.where(hit, wsel_b[k:k + 1, :], wm)
        xloc_ref[pl.ds(r0, KC), :] = _dot(sb, h).astype(BF16)
        src_ref[0, pl.ds(r0, KC), :] = _dot(sb, tokid_ref[...])
        wrow_ref[0, pl.ds(r0, KC), :] = _dot(wm, ones_ref[...])

    def chunk_pair(cp, carry):
        chunk(2 * cp)
        chunk(2 * cp + 1)
        return carry

    lax.fori_loop(0, (nch[b] + 1) // 2, chunk_pair, 0)

    def seg_copy(l, g):
        return pltpu.make_async_copy(xloc_ref.at[pl.ds(l, SEG), :], xs_hbm.at[pl.ds(g, SEG), :], sem)

    _start_granule_copies(b, gran_l, gran_g, n_gran, seg_copy)

    def tail_copy(g):
        return pltpu.make_async_copy(zero_ref, xs_hbm.at[pl.ds(g, SEG), :], tail_sem)

    def tails(wait):
        def per_expert(e, carry):
            def per_granule(j, c):
                cp = tail_copy(pl.multiple_of(tail_g[e] + j * SEG, SEG))
                if wait:
                    cp.wait()
                else:
                    cp.start()
                return c
            lax.fori_loop(0, tail_n[e], per_granule, 0)
            return carry
        lax.fori_loop(0, N_EXPERTS, per_expert, 0)

    @pl.when(b == nb - 1)
    def _():
        zero_ref[...] = jnp.zeros(zero_ref.shape, BF16)
        tails(False)

    _wait_granule_copies(
        n_gran[b], gmax,
        lambda rows: pltpu.make_async_copy(xloc_ref.at[pl.ds(0, rows), :], xs_hbm.at[pl.ds(0, rows), :], sem))

    @pl.when(b == nb - 1)
    def _():
        tails(True)


def _dispatch(meta, dest, wsel, h1b_all, n_rows_sorted, m_out_max):
    n, D = h1b_all.shape
    nb = n // TB
    tokid = jnp.asarray(np.broadcast_to(np.arange(TB, dtype=np.float32)[:, None], (TB, 128)), BF16)
    ones = jnp.ones((TB, 128), BF16)
    cs = lambda shape: pl.BlockSpec(shape, lambda i, *_: (0,) * len(shape), pipeline_mode=pl.Buffered(1))
    grid_spec = pltpu.PrefetchScalarGridSpec(
        num_scalar_prefetch=6,
        grid=(nb,),
        in_specs=[
            pl.BlockSpec((1, TOP_K, TB), lambda i, *_: (i, 0, 0)),
            pl.BlockSpec((1, TOP_K, TB), lambda i, *_: (i, 0, 0)),
            pl.BlockSpec((TB, D), lambda i, *_: (i, 0)),
            cs(tokid.shape), cs(ones.shape),
        ],
        out_specs=[
            pl.BlockSpec(memory_space=pl.ANY),
            pl.BlockSpec((1, m_out_max, 128), lambda i, *_: (i, 0, 0)),
            pl.BlockSpec((1, m_out_max, 128), lambda i, *_: (i, 0, 0)),
        ],
        scratch_shapes=[
            pltpu.VMEM((m_out_max, D), BF16),
            pltpu.VMEM((SEG, D), BF16),
            pltpu.SemaphoreType.DMA(()),
            pltpu.SemaphoreType.DMA(()),
        ],
    )
    return pl.pallas_call(
        _dispatch_body,
        grid_spec=grid_spec,
        out_shape=[
            jax.ShapeDtypeStruct((n_rows_sorted, D), BF16),
            jax.ShapeDtypeStruct((nb, m_out_max, 128), F32),
            jax.ShapeDtypeStruct((nb, m_out_max, 128), F32),
        ],
        compiler_params=pltpu.CompilerParams(
            dimension_semantics=("arbitrary",), vmem_limit_bytes=VMEM_LIMIT, has_side_effects=True),
        name="dispatch",
    )(meta["gran_l"], meta["gran_g"], meta["n_gran"], meta["nch"], meta["tail_g"], meta["tail_n"], dest, wsel,
      h1b_all, tokid, ones)


def _ffn_body(tile_e, tile_src, tile_valid, e_slot, e_next, x_ref, wg_hbm, wu_hbm, wd_hbm, y_ref,
              wg_f, wu_f, wd_f, wgu_b, wd_b, sems):
    i = pl.program_id(0)
    f = wg_f.shape[2]
    e = tile_e[i]
    e_prev = tile_e[jnp.maximum(i - 1, 0)]

    def weight_copies(expert, slot):
        return (pltpu.make_async_copy(wg_hbm.at[expert], wg_f.at[slot], sems.at[slot, 0]),
                pltpu.make_async_copy(wu_hbm.at[expert], wu_f.at[slot], sems.at[slot, 1]),
                pltpu.make_async_copy(wd_hbm.at[expert], wd_f.at[slot], sems.at[slot, 2]))

    @pl.when((i == 0) | (e != e_prev))
    def _():
        slot = e_slot[e]

        @pl.when(i == 0)
        def _():
            for cp in weight_copies(e, slot):
                cp.start()

        for cp in weight_copies(e, slot):
            cp.wait()
        nxt = e_next[e]

        @pl.when(nxt >= 0)
        def _():
            for cp in weight_copies(nxt, 1 - slot):
                cp.start(priority=1)

        wgu_b[:, :f] = wg_f[slot].astype(BF16)
        wgu_b[:, f:] = wu_f[slot].astype(BF16)
        wd_b[...] = wd_f[slot].astype(BF16)

    @pl.when(tile_valid[i] == 1)
    def _():
        gu = _dot(x_ref[...], wgu_b[...])
        act = (jax.nn.silu(gu[:, :f]) * gu[:, f:]).astype(BF16)
        y_ref[...] = _dot(act, wd_b[...]).astype(BF16)


def _expert_ffn(meta, xs, w_gate, w_up, w_down):
    R, D = xs.shape
    E, _, f = w_gate.shape
    nt = R // TR
    grid_spec = pltpu.PrefetchScalarGridSpec(
        num_scalar_prefetch=5,
        grid=(nt,),
        in_specs=[
            pl.BlockSpec((TR, D), lambda i, te, ts, *_: (ts[i], 0)),
            pl.BlockSpec(memory_space=pl.ANY),
            pl.BlockSpec(memory_space=pl.ANY),
            pl.BlockSpec(memory_space=pl.ANY),
        ],
        out_specs=pl.BlockSpec((TR, D), lambda i, te, ts, *_: (ts[i], 0)),
        scratch_shapes=[
            pltpu.VMEM((2, D, f), F32),
            pltpu.VMEM((2, D, f), F32),
            pltpu.VMEM((2, f, D), F32),
            pltpu.VMEM((D, 2 * f), BF16),
            pltpu.VMEM((f, D), BF16),
            pltpu.SemaphoreType.DMA((2, 3)),
        ],
    )
    return pl.pallas_call(
        _ffn_body,
        grid_spec=grid_spec,
        out_shape=jax.ShapeDtypeStruct((R, D), BF16),
        compiler_params=pltpu.CompilerParams(
            dimension_semantics=("arbitrary",), vmem_limit_bytes=VMEM_LIMIT),
        name="expert_ffn",
    )(meta["tile_e"], meta["tile_src"], meta["tile_valid"], meta["e_slot"], meta["e_next"],
      xs, w_gate, w_up, w_down)


def _combine_body(alpha, n_pb, gran_l, gran_g, n_gran, nch, src_ref, wrow_ref, h1_ref, h1b_ref, p_ref,
                  wsg_ref, wsu_ref, wsd_ref, g2_ref, b2_ref, wpg_ref, bpg_ref, wpe_ref, ys_hbm,
                  yp_ref, ysm_ref, yloc_ref, acc_ref, sem):
    b = pl.program_id(0)
    n_chunks = nch[b]
    gmax = gran_l.shape[0] // n_gran.shape[0]

    last0 = pl.multiple_of((n_chunks - 1) * KCC, KCC)
    yloc_ref[pl.ds(last0, KCC), :] = jnp.zeros((KCC, yloc_ref.shape[1]), BF16)

    def seg_copy(l, g):
        return pltpu.make_async_copy(ys_hbm.at[pl.ds(g, SEG), :], yloc_ref.at[pl.ds(l, SEG), :], sem)

    _start_granule_copies(b, gran_l, gran_g, n_gran, seg_copy)

    hb = h1b_ref[...]
    sh = (jax.nn.silu(_dot(hb, wsg_ref[...])) * _dot(hb, wsu_ref[...])).astype(BF16)
    acc_ref[...] = _dot(sh, wsd_ref[...])

    _wait_granule_copies(
        n_gran[b], gmax,
        lambda rows: pltpu.make_async_copy(ys_hbm.at[pl.ds(0, rows), :], yloc_ref.at[pl.ds(0, rows), :], sem))

    def chunk(c, carry):
        r0 = pl.multiple_of(c * KCC, KCC)
        src = src_ref[0, pl.ds(r0, KCC), :]
        w = wrow_ref[0, pl.ds(r0, KCC), :]
        lane = lax.broadcasted_iota(I32, src.shape, 1).astype(F32)
        wm = jnp.concatenate([jnp.where(src == lane + float(o), w, 0.0) for o in range(0, TB, src.shape[1])],
                             axis=1).astype(BF16)
        acc_ref[...] += lax.dot_general(wm, yloc_ref[pl.ds(r0, KCC), :],
                                        (((0,), (0,)), ((), ())), preferred_element_type=F32)
        return carry

    lax.fori_loop(0, n_chunks, chunk, 0)

    h2 = _ln_rows(alpha * h1_ref[...] + acc_ref[...], g2_ref[...], b2_ref[...])
    gate = jax.nn.sigmoid(_dot(h2.astype(BF16), wpg_ref[...]) + bpg_ref[...])
    pe = _dot(p_ref[...].astype(BF16), wpe_ref[...])
    y = h2 + gate * pe

    @pl.when(b < n_pb)
    def _():
        yp_ref[...] = y

    @pl.when(b >= n_pb)
    def _():
        ysm_ref[...] = y


def _combine(meta, src_rep, w_rep, h1_all, h1b_all, p_all, wsg_b, wsu_b, wsd_b, g2, b2, wpg_b, bpg, wpe_b, ys,
             m_out_max, n_prompt, alpha):
    n, D = h1_all.shape
    nb = n // TB
    n_pb = n_prompt // TB
    d_pe = p_all.shape[1]
    cs = lambda shape: pl.BlockSpec(shape, lambda i, *_: (0,) * len(shape), pipeline_mode=pl.Buffered(1))
    grid_spec = pltpu.PrefetchScalarGridSpec(
        num_scalar_prefetch=4,
        grid=(nb,),
        in_specs=[
            pl.BlockSpec((1, m_out_max, 128), lambda i, *_: (i, 0, 0)),
            pl.BlockSpec((1, m_out_max, 128), lambda i, *_: (i, 0, 0)),
            pl.BlockSpec((TB, D), lambda i, *_: (i, 0)),
            pl.BlockSpec((TB, D), lambda i, *_: (i, 0)),
            pl.BlockSpec((TB, d_pe), lambda i, *_: (i, 0)),
            cs(wsg_b.shape), cs(wsu_b.shape), cs(wsd_b.shape), cs(g2.shape), cs(b2.shape),
            cs(wpg_b.shape), cs(bpg.shape), cs(wpe_b.shape),
            pl.BlockSpec(memory_space=pl.ANY),
        ],
        out_specs=[
            pl.BlockSpec((TB, D), lambda i, *_: (jnp.minimum(i, n_pb - 1), 0)),
            pl.BlockSpec((TB, D), lambda i, *_: (jnp.maximum(i - n_pb, 0), 0)),
        ],
        scratch_shapes=[
            pltpu.VMEM((m_out_max, D), BF16),
            pltpu.VMEM((TB, D), F32),
            pltpu.SemaphoreType.DMA(()),
        ],
    )
    return pl.pallas_call(
        functools.partial(_combine_body, alpha, n_pb),
        grid_spec=grid_spec,
        out_shape=[jax.ShapeDtypeStruct((n_prompt, D), F32), jax.ShapeDtypeStruct((n - n_prompt, D), F32)],
        compiler_params=pltpu.CompilerParams(
            dimension_semantics=("arbitrary",), vmem_limit_bytes=VMEM_LIMIT),
        name="combine",
    )(meta["gran_l"], meta["gran_g"], meta["n_gran"], meta["nchc"], src_rep, w_rep, h1_all, h1b_all, p_all,
      wsg_b, wsu_b, wsd_b, g2, b2, wpg_b, bpg, wpe_b, ys)


def _sort_meta(cnt, n_tiles_max, m_out_max):
    segp = (cnt + (SEG - 1)) // SEG * SEG
    loc = jnp.cumsum(segp, axis=1) - segp
    before = jnp.cumsum(segp, axis=0) - segp
    total = jnp.sum(segp, axis=0)
    total_p = (total + (TR - 1)) // TR * TR
    ends = jnp.cumsum(total_p)
    base = ends - total_p
    glob = base[None, :] + before
    n_valid = ends[-1] // TR
    tiles = jnp.arange(n_tiles_max, dtype=I32)
    tile_src = jnp.minimum(tiles, n_valid - 1)
    tile_e = jnp.sum((ends[None, :] <= (tile_src * TR)[:, None]).astype(I32), axis=1)
    experts = jnp.arange(N_EXPERTS, dtype=I32)
    active = total_p > 0
    later = active[None, :] & (experts[None, :] > experts[:, None])
    e_next = jnp.min(jnp.where(later, experts[None, :], N_EXPERTS), axis=1)
    gmax = m_out_max // SEG
    n16 = segp // SEG
    g_end = jnp.cumsum(n16, axis=1)
    g_start = g_end - n16
    j = jnp.arange(gmax, dtype=I32)
    e_j = jnp.sum((g_end[:, None, :] <= j[None, :, None]).astype(I32), axis=2)
    e_j = jnp.minimum(e_j, N_EXPERTS - 1)
    within = (j[None, :] - jnp.take_along_axis(g_start, e_j, axis=1)) * SEG
    gran_l = jnp.take_along_axis(loc, e_j, axis=1) + within
    gran_g = jnp.take_along_axis(glob, e_j, axis=1) + within
    return {
        "gran_l": gran_l.reshape(-1).astype(I32),
        "gran_g": gran_g.reshape(-1).astype(I32),
        "n_gran": g_end[:, -1].astype(I32),
        "nch": ((jnp.sum(segp, axis=1) + (KC - 1)) // KC).astype(I32),
        "nchc": ((jnp.sum(segp, axis=1) + (KCC - 1)) // KCC).astype(I32),
        "tail_g": (base + total).astype(I32),
        "tail_n": ((total_p - total) // SEG).astype(I32),
        "tile_e": jnp.minimum(tile_e, N_EXPERTS - 1).astype(I32),
        "tile_src": tile_src.astype(I32),
        "tile_valid": (tiles < n_valid).astype(I32),
        "e_slot": ((jnp.cumsum(active.astype(I32)) - 1) % 2).astype(I32),
        "e_next": jnp.where(e_next < N_EXPERTS, e_next, -1).astype(I32),
    }


def _layer(xp, xs, st, pp, ps, w_in, w_pool, pool_scale, sgu_ln_g, sgu_ln_b, w_s, b_s, w_out, ln1_g, ln1_b,
           router_w, router_bias, w_gate, w_up, w_down, ws_gate, ws_up, ws_down, ln2_g, ln2_b, w_pe, w_pgate,
           b_pgate, alpha):
    B, T, D = xp.shape
    NS, TS, _ = xs.shape
    n_prompt, n_sample = B * T, NS * TS
    n = n_prompt + n_sample
    n_heads = w_s.shape[0]
    row = lambda v: v.reshape(1, -1)

    x_t = jnp.transpose(xs, (1, 0, 2))
    st_t = jnp.transpose(st, (1, 0, 2))
    bsb = jnp.broadcast_to(b_s[:, :, None], (n_heads, CHUNK, SGU_HEAD_DIM))
    wrow = jnp.repeat(jnp.transpose(w_s[:, :TS, :TS], (1, 2, 0)), SGU_HEAD_DIM, axis=2)
    brow = jnp.repeat(jnp.transpose(b_s[:, :TS], (1, 0)), SGU_HEAD_DIM, axis=1)
    h1_all, h1b_all, pool_p, a_s, vn_s = _mixer(
        xp, x_t, st_t, w_in.astype(BF16), w_pool.astype(BF16), row(pool_scale), sgu_ln_g, sgu_ln_b, w_s, bsb,
        wrow, brow, w_out.astype(BF16), row(ln1_g), row(ln1_b), alpha)

    dest, wsel, cnt = _router(h1_all, jnp.transpose(router_w), router_bias.reshape(-1, 1))
    nb = n // TB
    m_out_max = TB * TOP_K + N_EXPERTS * SEG
    rows_max = n * TOP_K + nb * N_EXPERTS * (SEG - 1) + N_EXPERTS * (TR - SEG)
    n_tiles_max = -(-rows_max // TR)
    meta = _sort_meta(cnt[:, :, 0], n_tiles_max, m_out_max)
    xsort, src_rep, w_rep = _dispatch(meta, dest, wsel, h1b_all, n_tiles_max * TR, m_out_max)
    ysort = _expert_ffn(meta, xsort, w_gate, w_up, w_down)

    nbs = TB // TS
    ps_t = jnp.transpose(ps.reshape(NS // nbs, nbs, TS, -1), (0, 2, 1, 3)).reshape(n_sample, -1)
    p_all = jnp.concatenate([pp.reshape(n_prompt, -1), ps_t], axis=0)
    y_p, y_s = _combine(meta, src_rep, w_rep, h1_all, h1b_all, p_all, ws_gate.astype(BF16), ws_up.astype(BF16),
                        ws_down.astype(BF16), row(ln2_g), row(ln2_b), w_pgate.astype(BF16), row(b_pgate),
                        w_pe.astype(BF16), ysort, m_out_max, n_prompt, alpha)

    yp = y_p.reshape(B, T, D)
    ys = jnp.transpose(y_s.reshape(NS // nbs, TS, nbs, D), (0, 2, 1, 3)).reshape(NS, TS, D)
    new_pool_p = pool_p[:, HALO - POOL_BUF:, :]
    new_pool_s = jnp.concatenate([st, jnp.transpose(a_s, (1, 0, 2))], axis=1)[:, -POOL_BUF:]
    vn = jnp.transpose(vn_s, (1, 0, 2)).reshape(NS, TS, n_heads, SGU_HEAD_DIM)
    return yp, ys, new_pool_p, new_pool_s, vn


def kernel(x_prompt, x_sample, state_pool, p_prompt, p_sample, w_in, w_pool, pool_scale, sgu_ln_g, sgu_ln_b, w_s, b_s, w_out, ln1_g, ln1_b, router_w, router_bias, w_gate, w_up, w_down, ws_gate, ws_up, ws_down, ln2_g, ln2_b, w_pe, w_pgate, b_pgate):
    depth = w_in.shape[0]
    alpha = (2.0 * depth) ** 0.25
    hp, hs = x_prompt, x_sample
    pool_p, pool_s, v_s = [], [], []
    for i in range(depth):
        hp, hs, bp, bs, vs = _layer(
            hp, hs, state_pool[i], p_prompt[i], p_sample[i], w_in[i], w_pool[i], pool_scale[i], sgu_ln_g[i],
            sgu_ln_b[i], w_s[i], b_s[i], w_out[i], ln1_g[i], ln1_b[i], router_w[i], router_bias[i], w_gate[i],
            w_up[i], w_down[i], ws_gate[i], ws_up[i], ws_down[i], ln2_g[i], ln2_b[i], w_pe[i], w_pgate[i],
            b_pgate[i], alpha)
        pool_p.append(bp)
        pool_s.append(bs)
        v_s.append(vs)
    return hp, hs, jnp.stack(pool_p), jnp.stack(pool_s), jnp.stack(v_s)
```

```python
import functools

import jax
import jax.numpy as jnp
import numpy as np
from jax import lax
from jax.experimental import pallas as pl
from jax.experimental.pallas import tpu as pltpu

F32 = jnp.float32
BF16 = jnp.bfloat16
I32 = jnp.int32

POOL_WINDOWS = (2, 4, 8, 16)
POOL_BUF = max(POOL_WINDOWS) - 1
CHUNK = 128
SGU_HEAD_DIM = 128
N_EXPERTS = 64
TOP_K = 8
N_EXPERT_GROUPS = 8
TOPK_GROUPS = 4
ROUTE_SCALE = 2.5
LN_EPS = 1e-5
PAST_LEN = 16384

TB = 256
HALO = 16
TR = 256
SEG = 16
KC = 256
KCC = 512
VMEM_LIMIT = 56 * 1024 * 1024


def _const_spec(shape):
    n = len(shape)
    return pl.BlockSpec(shape, lambda *_: (0,) * n, pipeline_mode=pl.Buffered(1))


def _ln_rows(x, g, b):
    mu = jnp.mean(x, axis=-1, keepdims=True)
    xc = x - mu
    var = jnp.mean(xc * xc, axis=-1, keepdims=True)
    return xc * lax.rsqrt(var + LN_EPS) * g + b


def _dot(a, b):
    return jnp.dot(a, b, preferred_element_type=F32)


def _prompt_tile(j, n_tiles, x, proj, band_ref, wpool_ref, pscale_ref, sg_ref, sb_ref, ws_ref, bsb_ref,
                 pool_ref, zcat_ref, mixin_ref):
    d_pool = zcat_ref.shape[1]
    d_pool_g = d_pool // len(POOL_WINDOWS)
    d_sgu = mixin_ref.shape[1] - d_pool

    @pl.when(j == 0)
    def _():
        zcat_ref[0:HALO, :] = jnp.zeros((HALO, d_pool), BF16)

    a = proj[:, :d_pool]
    zcat_ref[HALO:, :] = a.astype(BF16)

    pos1 = (j * TB + lax.broadcasted_iota(I32, (TB, 1), 0) + 1).astype(F32)
    for g, w in enumerate(POOL_WINDOWS):
        c0, c1 = g * d_pool_g, (g + 1) * d_pool_g
        s = _dot(band_ref[g], zcat_ref[:, c0:c1])
        cnt = jnp.minimum(pos1, float(w))
        d = s / cnt - a[:, c0:c1]
        y = _dot(d.astype(BF16), wpool_ref[g])
        mixin_ref[:, c0:c1] = (y * pscale_ref[:, c0:c1]).astype(BF16)
    zcat_ref[0:HALO, :] = zcat_ref[TB:TB + HALO, :]

    @pl.when(j == n_tiles - 1)
    def _():
        pool_ref[0] = a[TB - HALO:, :]

    u = jax.nn.gelu(proj[:, d_pool:d_pool + d_sgu])
    v = jax.nn.gelu(proj[:, d_pool + d_sgu:])
    r = lax.broadcasted_iota(I32, (CHUNK, CHUNK), 0)
    c = lax.broadcasted_iota(I32, (CHUNK, CHUNK), 1)
    tril = r >= c
    for h in range(d_sgu // SGU_HEAD_DIM):
        l0, l1 = h * SGU_HEAD_DIM, (h + 1) * SGU_HEAD_DIM
        vn = _ln_rows(v[:, l0:l1], sg_ref[h:h + 1, :], sb_ref[h:h + 1, :]).astype(BF16)
        wst = jnp.where(tril, ws_ref[h], 0.0).astype(BF16)
        for ci in range(TB // CHUNK):
            r0, r1 = ci * CHUNK, (ci + 1) * CHUNK
            mixed = _dot(wst, vn[r0:r1]) + bsb_ref[h]
            mixin_ref[r0:r1, d_pool + l0:d_pool + l1] = (u[r0:r1, l0:l1] * mixed).astype(BF16)


def _sample_tile(T, NB, proj, st_ref, wpool_ref, pscale_ref, sg_ref, sb_ref, wrow_ref, brow_ref,
                 a_ref, vn_ref, dbuf_ref, mixin_ref):
    d_pool = dbuf_ref.shape[1]
    d_pool_g = d_pool // len(POOL_WINDOWS)
    d_sgu = mixin_ref.shape[1] - d_pool

    a = proj[:, :d_pool]
    for t in range(T):
        a_ref[t] = a[t * NB:(t + 1) * NB]

    for g, w in enumerate(POOL_WINDOWS):
        c0, c1 = g * d_pool_g, (g + 1) * d_pool_g
        for t in range(T):
            cnt = float(min(PAST_LEN + t + 1, w))
            acc = None
            for s in range(POOL_BUF + t - w + 1, POOL_BUF + t + 1):
                if s < POOL_BUF:
                    term = st_ref[s, :, c0:c1]
                else:
                    term = a[(s - POOL_BUF) * NB:(s - POOL_BUF + 1) * NB, c0:c1]
                acc = term if acc is None else acc + term
            d = acc / cnt - a[t * NB:(t + 1) * NB, c0:c1]
            dbuf_ref[t * NB:(t + 1) * NB, c0:c1] = d.astype(BF16)
        y = _dot(dbuf_ref[0:T * NB, c0:c1], wpool_ref[g])
        mixin_ref[:, c0:c1] = (y * pscale_ref[:, c0:c1]).astype(BF16)

    u = jax.nn.gelu(proj[:, d_pool:d_pool + d_sgu])
    v = jax.nn.gelu(proj[:, d_pool + d_sgu:])
    for h in range(d_sgu // SGU_HEAD_DIM):
        l0, l1 = h * SGU_HEAD_DIM, (h + 1) * SGU_HEAD_DIM
        vn = _ln_rows(v[:, l0:l1], sg_ref[h:h + 1, :], sb_ref[h:h + 1, :])
        for t in range(T):
            vn_ref[t, :, l0:l1] = vn[t * NB:(t + 1) * NB]
    for t in range(T):
        mixed = brow_ref[t:t + 1, :]
        for s in range(t + 1):
            mixed = mixed + wrow_ref[t, s:s + 1, :] * vn_ref[s]
        mixin_ref[t * NB:(t + 1) * NB, d_pool:] = (u[t * NB:(t + 1) * NB] * mixed).astype(BF16)


def _mixer_body(alpha, n_pt, tiles_per_seq, xp_ref, xs_ref, st_ref, win_ref, band_ref, wpool_ref, pscale_ref,
                sg_ref, sb_ref, ws_ref, bsb_ref, wrow_ref, brow_ref, wout_ref, g1_ref, b1_ref,
                h1_ref, h1b_ref, pool_ref, a_ref, vn_ref, zcat_ref, mixin_ref):
    i = pl.program_id(0)

    def finish(x):
        mix = _dot(mixin_ref[...], wout_ref[...])
        h1 = _ln_rows(alpha * x + mix, g1_ref[...], b1_ref[...])
        h1_ref[...] = h1
        h1b_ref[...] = h1.astype(BF16)

    @pl.when(i < n_pt)
    def _():
        x = xp_ref[0]
        proj = _dot(x.astype(BF16), win_ref[...])
        _prompt_tile(i % tiles_per_seq, tiles_per_seq, x, proj, band_ref, wpool_ref, pscale_ref, sg_ref, sb_ref,
                     ws_ref, bsb_ref, pool_ref, zcat_ref, mixin_ref)
        finish(x)

    @pl.when(i >= n_pt)
    def _():
        T, NB, D = xs_ref.shape
        x = xs_ref[...].reshape(T * NB, D)
        proj = _dot(x.astype(BF16), win_ref[...])
        _sample_tile(T, NB, proj, st_ref, wpool_ref, pscale_ref, sg_ref, sb_ref, wrow_ref, brow_ref,
                     a_ref, vn_ref, zcat_ref, mixin_ref)
        finish(x)


def _pool_band(tm):
    t = np.arange(tm)[:, None] + HALO
    s = np.arange(tm + HALO)[None, :]
    return jnp.asarray(np.stack([((s <= t) & (s >= t - w + 1)) for w in POOL_WINDOWS]).astype(np.float32), BF16)


def _mixer(xp, xs_t, st_t, win_b, wpool_b, pscale, sg, sb, ws, bsb, wrow, brow, wout_b, g1, b1, alpha):
    B, T, D = xp.shape
    TS, NS, _ = xs_t.shape
    d_pool = st_t.shape[2]
    d_sgu = wrow.shape[2]
    tps = T // TB
    n_pt = B * tps
    NB = TB // TS
    n_st = NS // NB
    n = B * T + NS * TS
    pt = lambda i: jnp.minimum(i, n_pt - 1)
    stile = lambda i: jnp.maximum(i - n_pt, 0)
    body = functools.partial(_mixer_body, alpha, n_pt, tps)
    consts = (win_b, _pool_band(TB), wpool_b, pscale, sg, sb, ws, bsb, wrow, brow, wout_b, g1, b1)
    return pl.pallas_call(
        body,
        grid=(n_pt + n_st,),
        in_specs=[
            pl.BlockSpec((1, TB, D), lambda i: (pt(i) // tps, pt(i) % tps, 0)),
            pl.BlockSpec((TS, NB, D), lambda i: (0, stile(i), 0)),
            pl.BlockSpec((POOL_BUF, NB, d_pool), lambda i: (0, stile(i), 0)),
        ] + [_const_spec(c.shape) for c in consts],
        out_specs=[
            pl.BlockSpec((TB, D), lambda i: (i, 0)),
            pl.BlockSpec((TB, D), lambda i: (i, 0)),
            pl.BlockSpec((1, HALO, d_pool), lambda i: (pt(i) // tps, 0, 0)),
            pl.BlockSpec((TS, NB, d_pool), lambda i: (0, stile(i), 0)),
            pl.BlockSpec((TS, NB, d_sgu), lambda i: (0, stile(i), 0)),
        ],
        out_shape=[
            jax.ShapeDtypeStruct((n, D), F32),
            jax.ShapeDtypeStruct((n, D), BF16),
            jax.ShapeDtypeStruct((B, HALO, d_pool), F32),
            jax.ShapeDtypeStruct((TS, NS, d_pool), F32),
            jax.ShapeDtypeStruct((TS, NS, d_sgu), F32),
        ],
        scratch_shapes=[
            pltpu.VMEM((TB + HALO, d_pool), BF16),
            pltpu.VMEM((TB, D), BF16),
        ],
        compiler_params=pltpu.CompilerParams(
            dimension_semantics=("arbitrary",), vmem_limit_bytes=VMEM_LIMIT),
        name="mixer",
    )(xp, xs_t, st_t, *consts)


def _router_body(h_ref, rwt_ref, rb_ref, upper_ref, lstrict_ref, dest_ref, wsel_ref, cnt_ref):
    per_group = N_EXPERTS // N_EXPERT_GROUPS
    neg = -jnp.inf
    logits_t = lax.dot_general(rwt_ref[...], h_ref[...], (((1,), (1,)), ((), ())),
                               precision=lax.Precision.HIGHEST, preferred_element_type=F32)
    s_t = jax.nn.sigmoid(logits_t)
    b_t = s_t + rb_ref[...]

    io_g = lax.broadcasted_iota(I32, (per_group, TB), 0)
    gs = []
    for g in range(N_EXPERT_GROUPS):
        xg = b_t[g * per_group:(g + 1) * per_group, :]
        m1 = jnp.max(xg, axis=0, keepdims=True)
        i1 = jnp.min(jnp.where(xg == m1, io_g, per_group), axis=0, keepdims=True)
        m2 = jnp.max(jnp.where(io_g == i1, neg, xg), axis=0, keepdims=True)
        gs.append(m1 + m2)
    masked = []
    for g in range(N_EXPERT_GROUPS):
        rank = jnp.zeros((1, TB), F32)
        for g2 in range(N_EXPERT_GROUPS):
            if g2 != g:
                ahead = (gs[g2] >= gs[g]) if g2 < g else (gs[g2] > gs[g])
                rank = rank + jnp.where(ahead, 1.0, 0.0)
        keep = rank < float(TOPK_GROUPS)
        masked.append(jnp.where(keep, b_t[g * per_group:(g + 1) * per_group, :], neg))
    xm = jnp.concatenate(masked, axis=0)

    io_e = lax.broadcasted_iota(I32, (N_EXPERTS, TB), 0)
    onehots = []
    sel = jnp.zeros((N_EXPERTS, TB), F32)
    for _ in range(TOP_K):
        m = jnp.max(xm, axis=0, keepdims=True)
        idx = jnp.min(jnp.where(xm == m, io_e, N_EXPERTS), axis=0, keepdims=True)
        oh = io_e == idx
        onehots.append(oh)
        sel = jnp.where(oh, 1.0, sel)
        xm = jnp.where(oh, neg, xm)

    ssel = sel * s_t
    denom = jnp.sum(ssel, axis=0, keepdims=True)
    comb = ssel / denom * ROUTE_SCALE

    rank_t = _dot(sel.astype(BF16), upper_ref[...])
    cnt = jnp.sum(sel, axis=1, keepdims=True)
    cnt_i = cnt.astype(I32)
    seg16 = ((cnt_i + (SEG - 1)) // SEG).astype(F32)
    off16 = _dot(lstrict_ref[...], jnp.broadcast_to(seg16, (N_EXPERTS, 128)).astype(BF16))
    d_t = rank_t + off16[:, 0:1] * float(SEG)
    for k in range(TOP_K):
        dest_ref[0, k:k + 1, :] = jnp.sum(jnp.where(onehots[k], d_t, 0.0), axis=0, keepdims=True).astype(I32)
        wsel_ref[0, k:k + 1, :] = jnp.sum(jnp.where(onehots[k], comb, 0.0), axis=0, keepdims=True)
    cnt_ref[0] = jnp.broadcast_to(cnt_i, (N_EXPERTS, 128))


def _router(h1_all, rwt, rb_col):
    n, D = h1_all.shape
    nb = n // TB
    upper = jnp.asarray(np.triu(np.ones((TB, TB), np.float32), 1), BF16)
    lstrict = jnp.asarray(np.tril(np.ones((N_EXPERTS, N_EXPERTS), np.float32), -1), BF16)
    return pl.pallas_call(
        _router_body,
        grid=(nb,),
        in_specs=[
            pl.BlockSpec((TB, D), lambda i: (i, 0)),
            _const_spec(rwt.shape),
            _const_spec(rb_col.shape),
            _const_spec(upper.shape),
            _const_spec(lstrict.shape),
        ],
        out_specs=[
            pl.BlockSpec((1, TOP_K, TB), lambda i: (i, 0, 0)),
            pl.BlockSpec((1, TOP_K, TB), lambda i: (i, 0, 0)),
            pl.BlockSpec((1, N_EXPERTS, 128), lambda i: (i, 0, 0)),
        ],
        out_shape=[
            jax.ShapeDtypeStruct((nb, TOP_K, TB), I32),
            jax.ShapeDtypeStruct((nb, TOP_K, TB), F32),
            jax.ShapeDtypeStruct((nb, N_EXPERTS, 128), I32),
        ],
        compiler_params=pltpu.CompilerParams(dimension_semantics=("arbitrary",)),
        name="router",
    )(h1_all, rwt, rb_col, upper, lstrict)


def _start_granule_copies(b, j0, j1, gran_g, gmax, make_copy):
    def per_granule(j, c):
        make_copy(pl.multiple_of(j * SEG, SEG), pl.multiple_of(gran_g[b * gmax + j], SEG)).start()
        return c

    lax.fori_loop(j0, j1, per_granule, 0)


def _wait_granule_copies(count, max_count, make_wait_copy):
    for k in range(int(max_count).bit_length()):
        @pl.when(((count >> k) & 1) == 1)
        def _():
            make_wait_copy(SEG << k).wait()


def _dispatch_body(gran_g, n_gran, nch, tail_g, tail_n, dest_ref, wsel_ref, h_ref, tokid_ref, ones_ref,
                   xs_hbm, src_ref, wrow_ref, xloc_ref, zero_ref, sems, tail_sem):
    b = pl.program_id(0)
    nb = pl.num_programs(0)
    gmax = gran_g.shape[0] // n_gran.shape[0]
    gpp = 2 * KC // SEG
    slot = b % 2
    n_b = n_gran[b]

    def seg_copy(sl):
        return lambda l, g: pltpu.make_async_copy(
            xloc_ref.at[sl, pl.ds(l, SEG), :], xs_hbm.at[pl.ds(g, SEG), :], sems.at[sl])

    def wait_block(count, sl):
        _wait_granule_copies(count, gmax, lambda rows: pltpu.make_async_copy(
            xloc_ref.at[sl, pl.ds(0, rows), :], xs_hbm.at[pl.ds(0, rows), :], sems.at[sl]))

    @pl.when(b >= 2)
    def _():
        wait_block(n_gran[jnp.maximum(b - 2, 0)], slot)

    dest = dest_ref[0]
    wsel = wsel_ref[0]
    h = h_ref[...]
    src_ref[...] = jnp.zeros(src_ref.shape, F32)
    wrow_ref[...] = jnp.zeros(wrow_ref.shape, F32)

    row_in_chunk = lax.broadcasted_iota(I32, (KC, TB), 0).astype(F32).astype(BF16)
    wsel_b = wsel.astype(BF16)
    one = jnp.ones((1, TB), BF16)

    def chunk(c):
        r0 = pl.multiple_of(c * KC, KC)
        rel = (dest - r0).astype(F32).astype(BF16)
        sb = jnp.zeros((KC, TB), BF16)
        wm = jnp.zeros((KC, TB), BF16)
        for k in range(TOP_K):
            hit = rel[k:k + 1, :] == row_in_chunk
            sb = jnp.where(hit, one, sb)
            wm = jnp.where(hit, wsel_b[k:k + 1, :], wm)
        xloc_ref[slot, pl.ds(r0, KC), :] = _dot(sb, h).astype(BF16)
        src_ref[0, pl.ds(r0, KC), :] = _dot(sb, tokid_ref[...])
        wrow_ref[0, pl.ds(r0, KC), :] = _dot(wm, ones_ref[...])

    def chunk_pair(cp, carry):
        chunk(2 * cp)
        chunk(2 * cp + 1)
        _start_granule_copies(b, cp * gpp, jnp.minimum((cp + 1) * gpp, n_b), gran_g, gmax, seg_copy(slot))
        return carry

    lax.fori_loop(0, (nch[b] + 1) // 2, chunk_pair, 0)

    def tail_copy(g):
        return pltpu.make_async_copy(zero_ref, xs_hbm.at[pl.ds(g, SEG), :], tail_sem)

    def tails(wait):
        def per_expert(e, carry):
            def per_granule(j, c):
                cp = tail_copy(pl.multiple_of(tail_g[e] + j * SEG, SEG))
                if wait:
                    cp.wait()
                else:
                    cp.start()
                return c
            lax.fori_loop(0, tail_n[e], per_granule, 0)
            return carry
        lax.fori_loop(0, N_EXPERTS, per_expert, 0)

    @pl.when(b == nb - 1)
    def _():
        zero_ref[...] = jnp.zeros(zero_ref.shape, BF16)
        tails(False)

        @pl.when(b >= 1)
        def _():
            wait_block(n_gran[jnp.maximum(b - 1, 0)], 1 - slot)

        wait_block(n_b, slot)
        tails(True)


def _dispatch(meta, dest, wsel, h1b_all, n_rows_sorted, m_out_max):
    n, D = h1b_all.shape
    nb = n // TB
    tokid = jnp.asarray(np.broadcast_to(np.arange(TB, dtype=np.float32)[:, None], (TB, 128)), BF16)
    ones = jnp.ones((TB, 128), BF16)
    cs = lambda shape: pl.BlockSpec(shape, lambda i, *_: (0,) * len(shape), pipeline_mode=pl.Buffered(1))
    grid_spec = pltpu.PrefetchScalarGridSpec(
        num_scalar_prefetch=5,
        grid=(nb,),
        in_specs=[
            pl.BlockSpec((1, TOP_K, TB), lambda i, *_: (i, 0, 0)),
            pl.BlockSpec((1, TOP_K, TB), lambda i, *_: (i, 0, 0)),
            pl.BlockSpec((TB, D), lambda i, *_: (i, 0)),
            cs(tokid.shape), cs(ones.shape),
        ],
        out_specs=[
            pl.BlockSpec(memory_space=pl.ANY),
            pl.BlockSpec((1, m_out_max, 128), lambda i, *_: (i, 0, 0)),
            pl.BlockSpec((1, m_out_max, 128), lambda i, *_: (i, 0, 0)),
        ],
        scratch_shapes=[
            pltpu.VMEM((2, m_out_max, D), BF16),
            pltpu.VMEM((SEG, D), BF16),
            pltpu.SemaphoreType.DMA((2,)),
            pltpu.SemaphoreType.DMA(()),
        ],
    )
    return pl.pallas_call(
        _dispatch_body,
        grid_spec=grid_spec,
        out_shape=[
            jax.ShapeDtypeStruct((n_rows_sorted, D), BF16),
            jax.ShapeDtypeStruct((nb, m_out_max, 128), F32),
            jax.ShapeDtypeStruct((nb, m_out_max, 128), F32),
        ],
        compiler_params=pltpu.CompilerParams(
            dimension_semantics=("arbitrary",), vmem_limit_bytes=VMEM_LIMIT, has_side_effects=True),
        name="dispatch",
    )(meta["gran_g"], meta["n_gran"], meta["nch"], meta["tail_g"], meta["tail_n"], dest, wsel,
      h1b_all, tokid, ones)


def _ffn_body(tile_e, tile_src, tile_valid, e_slot, e_next, x_ref, wg_hbm, wu_hbm, wd_hbm, y_ref,
              wg_f, wu_f, wd_f, wgu_b, wd_b, sems):
    i = pl.program_id(0)
    f = wg_f.shape[2]
    e = tile_e[i]
    e_prev = tile_e[jnp.maximum(i - 1, 0)]

    def weight_copies(expert, slot):
        return (pltpu.make_async_copy(wg_hbm.at[expert], wg_f.at[slot], sems.at[slot, 0]),
                pltpu.make_async_copy(wu_hbm.at[expert], wu_f.at[slot], sems.at[slot, 1]),
                pltpu.make_async_copy(wd_hbm.at[expert], wd_f.at[slot], sems.at[slot, 2]))

    @pl.when((i == 0) | (e != e_prev))
    def _():
        slot = e_slot[e]

        @pl.when(i == 0)
        def _():
            for cp in weight_copies(e, slot):
                cp.start()

        for cp in weight_copies(e, slot):
            cp.wait()
        nxt = e_next[e]

        @pl.when(nxt >= 0)
        def _():
            for cp in weight_copies(nxt, 1 - slot):
                cp.start(priority=1)

        wgu_b[:, :f] = wg_f[slot].astype(BF16)
        wgu_b[:, f:] = wu_f[slot].astype(BF16)
        wd_b[...] = wd_f[slot].astype(BF16)

    @pl.when(tile_valid[i] == 1)
    def _():
        gu = _dot(x_ref[...], wgu_b[...])
        act = (jax.nn.silu(gu[:, :f]) * gu[:, f:]).astype(BF16)
        y_ref[...] = _dot(act, wd_b[...]).astype(BF16)


def _expert_ffn(meta, xs, w_gate, w_up, w_down):
    R, D = xs.shape
    E, _, f = w_gate.shape
    nt = R // TR
    grid_spec = pltpu.PrefetchScalarGridSpec(
        num_scalar_prefetch=5,
        grid=(nt,),
        in_specs=[
            pl.BlockSpec((TR, D), lambda i, te, ts, *_: (ts[i], 0)),
            pl.BlockSpec(memory_space=pl.ANY),
            pl.BlockSpec(memory_space=pl.ANY),
            pl.BlockSpec(memory_space=pl.ANY),
        ],
        out_specs=pl.BlockSpec((TR, D), lambda i, te, ts, *_: (ts[i], 0)),
        scratch_shapes=[
            pltpu.VMEM((2, D, f), F32),
            pltpu.VMEM((2, D, f), F32),
            pltpu.VMEM((2, f, D), F32),
            pltpu.VMEM((D, 2 * f), BF16),
            pltpu.VMEM((f, D), BF16),
            pltpu.SemaphoreType.DMA((2, 3)),
        ],
    )
    return pl.pallas_call(
        _ffn_body,
        grid_spec=grid_spec,
        out_shape=jax.ShapeDtypeStruct((R, D), BF16),
        compiler_params=pltpu.CompilerParams(
            dimension_semantics=("arbitrary",), vmem_limit_bytes=VMEM_LIMIT),
        name="expert_ffn",
    )(meta["tile_e"], meta["tile_src"], meta["tile_valid"], meta["e_slot"], meta["e_next"],
      xs, w_gate, w_up, w_down)


def _combine_body(alpha, n_pb, gran_g, n_gran, nch, src_ref, wrow_ref, h1_ref, h1b_ref, p_ref,
                  wsg_ref, wsu_ref, wsd_ref, g2_ref, b2_ref, wpg_ref, bpg_ref, wpe_ref, ys_hbm,
                  yp_ref, ysm_ref, yloc_ref, acc_ref, sems):
    b = pl.program_id(0)
    n_chunks = nch[b]
    n_b = n_gran[b]
    gmax = gran_g.shape[0] // n_gran.shape[0]
    gpc = KCC // SEG

    last0 = pl.multiple_of((n_chunks - 1) * KCC, KCC)
    yloc_ref[pl.ds(last0, KCC), :] = jnp.zeros((KCC, yloc_ref.shape[1]), BF16)

    def seg_copy(l, g):
        return pltpu.make_async_copy(ys_hbm.at[pl.ds(g, SEG), :], yloc_ref.at[pl.ds(l, SEG), :],
                                     sems.at[l // KCC])

    _start_granule_copies(b, 0, n_b, gran_g, gmax, seg_copy)

    hb = h1b_ref[...]
    sh = (jax.nn.silu(_dot(hb, wsg_ref[...])) * _dot(hb, wsu_ref[...])).astype(BF16)
    acc_ref[...] = _dot(sh, wsd_ref[...])

    def chunk(c, carry):
        def wait_rows(rows):
            return pltpu.make_async_copy(ys_hbm.at[pl.ds(0, rows), :], yloc_ref.at[pl.ds(0, rows), :], sems.at[c])

        _wait_granule_copies(jnp.minimum(n_b - c * gpc, gpc), gpc, wait_rows)

        r0 = pl.multiple_of(c * KCC, KCC)
        src = src_ref[0, pl.ds(r0, KCC), :]
        w = wrow_ref[0, pl.ds(r0, KCC), :]
        lane = lax.broadcasted_iota(I32, src.shape, 1).astype(F32)
        wm = jnp.concatenate([jnp.where(src == lane + float(o), w, 0.0) for o in range(0, TB, src.shape[1])],
                             axis=1).astype(BF16)
        acc_ref[...] += lax.dot_general(wm, yloc_ref[pl.ds(r0, KCC), :],
                                        (((0,), (0,)), ((), ())), preferred_element_type=F32)
        return carry

    lax.fori_loop(0, n_chunks, chunk, 0)

    h2 = _ln_rows(alpha * h1_ref[...] + acc_ref[...], g2_ref[...], b2_ref[...])
    gate = jax.nn.sigmoid(_dot(h2.astype(BF16), wpg_ref[...]) + bpg_ref[...])
    pe = _dot(p_ref[...].astype(BF16), wpe_ref[...])
    y = h2 + gate * pe

    @pl.when(b < n_pb)
    def _():
        yp_ref[...] = y

    @pl.when(b >= n_pb)
    def _():
        ysm_ref[...] = y


def _combine(meta, src_rep, w_rep, h1_all, h1b_all, p_all, wsg_b, wsu_b, wsd_b, g2, b2, wpg_b, bpg, wpe_b, ys,
             m_out_max, n_prompt, alpha):
    n, D = h1_all.shape
    nb = n // TB
    n_pb = n_prompt // TB
    d_pe = p_all.shape[1]
    cs = lambda shape: pl.BlockSpec(shape, lambda i, *_: (0,) * len(shape), pipeline_mode=pl.Buffered(1))
    grid_spec = pltpu.PrefetchScalarGridSpec(
        num_scalar_prefetch=3,
        grid=(nb,),
        in_specs=[
            pl.BlockSpec((1, m_out_max, 128), lambda i, *_: (i, 0, 0)),
            pl.BlockSpec((1, m_out_max, 128), lambda i, *_: (i, 0, 0)),
            pl.BlockSpec((TB, D), lambda i, *_: (i, 0)),
            pl.BlockSpec((TB, D), lambda i, *_: (i, 0)),
            pl.BlockSpec((TB, d_pe), lambda i, *_: (i, 0)),
            cs(wsg_b.shape), cs(wsu_b.shape), cs(wsd_b.shape), cs(g2.shape), cs(b2.shape),
            cs(wpg_b.shape), cs(bpg.shape), cs(wpe_b.shape),
            pl.BlockSpec(memory_space=pl.ANY),
        ],
        out_specs=[
            pl.BlockSpec((TB, D), lambda i, *_: (jnp.minimum(i, n_pb - 1), 0)),
            pl.BlockSpec((TB, D), lambda i, *_: (jnp.maximum(i - n_pb, 0), 0)),
        ],
        scratch_shapes=[
            pltpu.VMEM((m_out_max, D), BF16),
            pltpu.VMEM((TB, D), F32),
            pltpu.SemaphoreType.DMA((m_out_max // KCC,)),
        ],
    )
    return pl.pallas_call(
        functools.partial(_combine_body, alpha, n_pb),
        grid_spec=grid_spec,
        out_shape=[jax.ShapeDtypeStruct((n_prompt, D), F32), jax.ShapeDtypeStruct((n - n_prompt, D), F32)],
        compiler_params=pltpu.CompilerParams(
            dimension_semantics=("arbitrary",), vmem_limit_bytes=VMEM_LIMIT),
        name="combine",
    )(meta["gran_g"], meta["n_gran"], meta["nchc"], src_rep, w_rep, h1_all, h1b_all, p_all,
      wsg_b, wsu_b, wsd_b, g2, b2, wpg_b, bpg, wpe_b, ys)


def _sort_meta(cnt, n_tiles_max, m_out_max):
    segp = (cnt + (SEG - 1)) // SEG * SEG
    loc = jnp.cumsum(segp, axis=1) - segp
    before = jnp.cumsum(segp, axis=0) - segp
    total = jnp.sum(segp, axis=0)
    total_p = (total + (TR - 1)) // TR * TR
    ends = jnp.cumsum(total_p)
    base = ends - total_p
    glob = base[None, :] + before
    n_valid = ends[-1] // TR
    tiles = jnp.arange(n_tiles_max, dtype=I32)
    tile_src = jnp.minimum(tiles, n_valid - 1)
    tile_e = jnp.sum((ends[None, :] <= (tile_src * TR)[:, None]).astype(I32), axis=1)
    experts = jnp.arange(N_EXPERTS, dtype=I32)
    active = total_p > 0
    later = active[None, :] & (experts[None, :] > experts[:, None])
    e_next = jnp.min(jnp.where(later, experts[None, :], N_EXPERTS), axis=1)
    gmax = m_out_max // SEG
    n16 = segp // SEG
    g_end = jnp.cumsum(n16, axis=1)
    g_start = g_end - n16
    j = jnp.arange(gmax, dtype=I32)[None, :, None]
    in_seg = (g_start[:, None, :] <= j) & (j < g_end[:, None, :])
    gran_g = jnp.sum(jnp.where(in_seg, (glob - loc)[:, None, :], 0), axis=2) + j[:, :, 0] * SEG
    return {
        "gran_g": gran_g.reshape(-1).astype(I32),
        "n_gran": g_end[:, -1].astype(I32),
        "nch": ((jnp.sum(segp, axis=1) + (KC - 1)) // KC).astype(I32),
        "nchc": ((jnp.sum(segp, axis=1) + (KCC - 1)) // KCC).astype(I32),
        "tail_g": (base + total).astype(I32),
        "tail_n": ((total_p - total) // SEG).astype(I32),
        "tile_e": jnp.minimum(tile_e, N_EXPERTS - 1).astype(I32),
        "tile_src": tile_src.astype(I32),
        "tile_valid": (tiles < n_valid).astype(I32),
        "e_slot": ((jnp.cumsum(active.astype(I32)) - 1) % 2).astype(I32),
        "e_next": jnp.where(e_next < N_EXPERTS, e_next, -1).astype(I32),
    }


def _layer(xp, xs, st, pp, ps, w_in, w_pool, pool_scale, sgu_ln_g, sgu_ln_b, w_s, b_s, w_out, ln1_g, ln1_b,
           router_w, router_bias, w_gate, w_up, w_down, ws_gate, ws_up, ws_down, ln2_g, ln2_b, w_pe, w_pgate,
           b_pgate, alpha):
    B, T, D = xp.shape
    NS, TS, _ = xs.shape
    n_prompt, n_sample = B * T, NS * TS
    n = n_prompt + n_sample
    n_heads = w_s.shape[0]
    row = lambda v: v.reshape(1, -1)

    x_t = jnp.transpose(xs, (1, 0, 2))
    st_t = jnp.transpose(st, (1, 0, 2))
    bsb = jnp.broadcast_to(b_s[:, :, None], (n_heads, CHUNK, SGU_HEAD_DIM))
    wrow = jnp.repeat(jnp.transpose(w_s[:, :TS, :TS], (1, 2, 0)), SGU_HEAD_DIM, axis=2)
    brow = jnp.repeat(jnp.transpose(b_s[:, :TS], (1, 0)), SGU_HEAD_DIM, axis=1)
    h1_all, h1b_all, pool_p, a_s, vn_s = _mixer(
        xp, x_t, st_t, w_in.astype(BF16), w_pool.astype(BF16), row(pool_scale), sgu_ln_g, sgu_ln_b, w_s, bsb,
        wrow, brow, w_out.astype(BF16), row(ln1_g), row(ln1_b), alpha)

    dest, wsel, cnt = _router(h1_all, jnp.transpose(router_w), router_bias.reshape(-1, 1))
    nb = n // TB
    m_out_max = TB * TOP_K + N_EXPERTS * SEG
    rows_max = n * TOP_K + nb * N_EXPERTS * (SEG - 1) + N_EXPERTS * (TR - SEG)
    n_tiles_max = -(-rows_max // TR)
    meta = _sort_meta(cnt[:, :, 0], n_tiles_max, m_out_max)
    xsort, src_rep, w_rep = _dispatch(meta, dest, wsel, h1b_all, n_tiles_max * TR, m_out_max)
    ysort = _expert_ffn(meta, xsort, w_gate, w_up, w_down)

    nbs = TB // TS
    ps_t = jnp.transpose(ps.reshape(NS // nbs, nbs, TS, -1), (0, 2, 1, 3)).reshape(n_sample, -1)
    p_all = jnp.concatenate([pp.reshape(n_prompt, -1), ps_t], axis=0)
    y_p, y_s = _combine(meta, src_rep, w_rep, h1_all, h1b_all, p_all, ws_gate.astype(BF16), ws_up.astype(BF16),
                        ws_down.astype(BF16), row(ln2_g), row(ln2_b), w_pgate.astype(BF16), row(b_pgate),
                        w_pe.astype(BF16), ysort, m_out_max, n_prompt, alpha)

    yp = y_p.reshape(B, T, D)
    ys = jnp.transpose(y_s.reshape(NS // nbs, TS, nbs, D), (0, 2, 1, 3)).reshape(NS, TS, D)
    new_pool_p = pool_p[:, HALO - POOL_BUF:, :]
    new_pool_s = jnp.concatenate([st, jnp.transpose(a_s, (1, 0, 2))], axis=1)[:, -POOL_BUF:]
    vn = jnp.transpose(vn_s, (1, 0, 2)).reshape(NS, TS, n_heads, SGU_HEAD_DIM)
    return yp, ys, new_pool_p, new_pool_s, vn


def kernel(x_prompt, x_sample, state_pool, p_prompt, p_sample, w_in, w_pool, pool_scale, sgu_ln_g, sgu_ln_b, w_s, b_s, w_out, ln1_g, ln1_b, router_w, router_bias, w_gate, w_up, w_down, ws_gate, ws_up, ws_down, ln2_g, ln2_b, w_pe, w_pgate, b_pgate):
    depth = w_in.shape[0]
    alpha = (2.0 * depth) ** 0.25
    hp, hs = x_prompt, x_sample
    pool_p, pool_s, v_s = [], [], []
    for i in range(depth):
        hp, hs, bp, bs, vs = _layer(
            hp, hs, state_pool[i], p_prompt[i], p_sample[i], w_in[i], w_pool[i], pool_scale[i], sgu_ln_g[i],
            sgu_ln_b[i], w_s[i], b_s[i], w_out[i], ln1_g[i], ln1_b[i], router_w[i], router_bias[i], w_gate[i],
            w_up[i], w_down[i], ws_gate[i], ws_up[i], ws_down[i], ln2_g[i], ln2_b[i], w_pe[i], w_pgate[i],
            b_pgate[i], alpha)
        pool_p.append(bp)
        pool_s.append(bs)
        v_s.append(vs)
    return hp, hs, jnp.stack(pool_p), jnp.stack(pool_s), jnp.stack(v_s)
```

```python
import functools

import jax
import jax.numpy as jnp
import numpy as np
from jax import lax
from jax.experimental import pallas as pl
from jax.experimental.pallas import tpu as pltpu

F32 = jnp.float32
BF16 = jnp.bfloat16
I32 = jnp.int32

POOL_WINDOWS = (2, 4, 8, 16)
POOL_BUF = max(POOL_WINDOWS) - 1
CHUNK = 128
SGU_HEAD_DIM = 128
N_EXPERTS = 64
TOP_K = 8
N_EXPERT_GROUPS = 8
TOPK_GROUPS = 4
ROUTE_SCALE = 2.5
LN_EPS = 1e-5
PAST_LEN = 16384

TB = 256
HALO = 16
TR = 256
SEG = 16
KC = 256
KCC = 512
VMEM_LIMIT = 56 * 1024 * 1024


def _const_spec(shape):
    n = len(shape)
    return pl.BlockSpec(shape, lambda *_: (0,) * n, pipeline_mode=pl.Buffered(1))


def _ln_rows(x, g, b):
    mu = jnp.mean(x, axis=-1, keepdims=True)
    xc = x - mu
    var = jnp.mean(xc * xc, axis=-1, keepdims=True)
    return xc * lax.rsqrt(var + LN_EPS) * g + b


def _dot(a, b):
    return jnp.dot(a, b, preferred_element_type=F32)


def _prompt_tile(j, n_tiles, x, proj, band_ref, wpool_ref, pscale_ref, sg_ref, sb_ref, ws_ref, bsb_ref,
                 pool_ref, zcat_ref, mixin_ref):
    d_pool = zcat_ref.shape[1]
    d_pool_g = d_pool // len(POOL_WINDOWS)
    d_sgu = mixin_ref.shape[1] - d_pool

    a = proj[:, :d_pool]
    zcat_ref[HALO:, :] = a.astype(BF16)

    pos1 = (j * TB + lax.broadcasted_iota(I32, (TB, 1), 0) + 1).astype(F32)
    for g, w in enumerate(POOL_WINDOWS):
        c0, c1 = g * d_pool_g, (g + 1) * d_pool_g
        s = _dot(band_ref[g], zcat_ref[:, c0:c1])
        cnt = jnp.minimum(pos1, float(w))
        d = s / cnt - a[:, c0:c1]
        y = _dot(d.astype(BF16), wpool_ref[g])
        mixin_ref[:, c0:c1] = (y * pscale_ref[:, c0:c1]).astype(BF16)
    zcat_ref[0:HALO, :] = jnp.where(j == n_tiles - 1, jnp.zeros((HALO, d_pool), BF16), zcat_ref[TB:TB + HALO, :])
    pool_ref[0] = a[TB - HALO:, :]

    u = jax.nn.gelu(proj[:, d_pool:d_pool + d_sgu])
    v = jax.nn.gelu(proj[:, d_pool + d_sgu:])
    r = lax.broadcasted_iota(I32, (CHUNK, CHUNK), 0)
    c = lax.broadcasted_iota(I32, (CHUNK, CHUNK), 1)
    tril = r >= c
    for h in range(d_sgu // SGU_HEAD_DIM):
        l0, l1 = h * SGU_HEAD_DIM, (h + 1) * SGU_HEAD_DIM
        vn = _ln_rows(v[:, l0:l1], sg_ref[h:h + 1, :], sb_ref[h:h + 1, :]).astype(BF16)
        wst = jnp.where(tril, ws_ref[h], 0.0).astype(BF16)
        n_ch = TB // CHUNK
        mixed_all = _dot(wst, jnp.concatenate([vn[ci * CHUNK:(ci + 1) * CHUNK] for ci in range(n_ch)], axis=1))
        for ci in range(n_ch):
            r0, r1 = ci * CHUNK, (ci + 1) * CHUNK
            mixed = mixed_all[:, ci * SGU_HEAD_DIM:(ci + 1) * SGU_HEAD_DIM] + bsb_ref[h]
            mixin_ref[r0:r1, d_pool + l0:d_pool + l1] = (u[r0:r1, l0:l1] * mixed).astype(BF16)


def _sample_tile(T, NB, proj, st_ref, wpool_ref, pscale_ref, sg_ref, sb_ref, wrow_ref, brow_ref,
                 a_ref, vn_ref, dbuf_ref, mixin_ref):
    d_pool = dbuf_ref.shape[1]
    d_pool_g = d_pool // len(POOL_WINDOWS)
    d_sgu = mixin_ref.shape[1] - d_pool

    a = proj[:, :d_pool]
    for t in range(T):
        a_ref[t] = a[t * NB:(t + 1) * NB]

    for g, w in enumerate(POOL_WINDOWS):
        c0, c1 = g * d_pool_g, (g + 1) * d_pool_g
        for t in range(T):
            cnt = float(min(PAST_LEN + t + 1, w))
            acc = None
            for s in range(POOL_BUF + t - w + 1, POOL_BUF + t + 1):
                if s < POOL_BUF:
                    term = st_ref[s, :, c0:c1]
                else:
                    term = a[(s - POOL_BUF) * NB:(s - POOL_BUF + 1) * NB, c0:c1]
                acc = term if acc is None else acc + term
            d = acc / cnt - a[t * NB:(t + 1) * NB, c0:c1]
            dbuf_ref[t * NB:(t + 1) * NB, c0:c1] = d.astype(BF16)
        y = _dot(dbuf_ref[0:T * NB, c0:c1], wpool_ref[g])
        mixin_ref[:, c0:c1] = (y * pscale_ref[:, c0:c1]).astype(BF16)

    u = jax.nn.gelu(proj[:, d_pool:d_pool + d_sgu])
    v = jax.nn.gelu(proj[:, d_pool + d_sgu:])
    for h in range(d_sgu // SGU_HEAD_DIM):
        l0, l1 = h * SGU_HEAD_DIM, (h + 1) * SGU_HEAD_DIM
        vn = _ln_rows(v[:, l0:l1], sg_ref[h:h + 1, :], sb_ref[h:h + 1, :])
        for t in range(T):
            vn_ref[t, :, l0:l1] = vn[t * NB:(t + 1) * NB]
    for t in range(T):
        mixed = brow_ref[t:t + 1, :]
        for s in range(t + 1):
            mixed = mixed + wrow_ref[t, s:s + 1, :] * vn_ref[s]
        mixin_ref[t * NB:(t + 1) * NB, d_pool:] = (u[t * NB:(t + 1) * NB] * mixed).astype(BF16)


def _mixer_body(alpha, n_pt, tiles_per_seq, xp_ref, xs_ref, st_ref, win_ref, band_ref, wpool_ref, pscale_ref,
                sg_ref, sb_ref, ws_ref, bsb_ref, wrow_ref, brow_ref, wout_ref, g1_ref, b1_ref,
                rwh_ref, rwl_ref, rb_ref, upper_ref, lstrict_ref,
                h1_ref, h1b_ref, pool_ref, a_ref, vn_ref, dest_ref, wsel_ref, cnt_ref,
                zcat_ref, mixin_ref, hprev_ref):
    i = pl.program_id(0)
    n_tiles = pl.num_programs(0) - 1

    def route_previous_tile():
        h = hprev_ref[...]
        _route_tile(h, h.astype(BF16), rwh_ref, rwl_ref, rb_ref, upper_ref, lstrict_ref,
                    dest_ref, wsel_ref, cnt_ref)

    @pl.when(i == 0)
    def _():
        zcat_ref[0:HALO, :] = jnp.zeros((HALO, zcat_ref.shape[1]), BF16)
        hprev_ref[...] = jnp.zeros(hprev_ref.shape, F32)

    @pl.when(i < n_pt)
    def _():
        x = xp_ref[0]
        proj = _dot(x.astype(BF16), win_ref[...])
        route_previous_tile()
        _prompt_tile(i % tiles_per_seq, tiles_per_seq, x, proj, band_ref, wpool_ref, pscale_ref, sg_ref, sb_ref,
                     ws_ref, bsb_ref, pool_ref, zcat_ref, mixin_ref)
        h1_ref[...] = alpha * x

    @pl.when(i >= n_pt)
    def _():
        route_previous_tile()

    @pl.when((i >= n_pt) & (i < n_tiles))
    def _():
        T, NB, D = xs_ref.shape
        x = xs_ref[...].reshape(T * NB, D)
        proj = _dot(x.astype(BF16), win_ref[...])
        _sample_tile(T, NB, proj, st_ref, wpool_ref, pscale_ref, sg_ref, sb_ref, wrow_ref, brow_ref,
                     a_ref, vn_ref, zcat_ref, mixin_ref)
        h1_ref[...] = alpha * x

    @pl.when(i < n_tiles)
    def _():
        mix = _dot(mixin_ref[...], wout_ref[...])
        h1 = _ln_rows(h1_ref[...] + mix, g1_ref[...], b1_ref[...])
        h1_ref[...] = h1
        h1b_ref[...] = h1.astype(BF16)
        hprev_ref[...] = h1


def _pool_band(tm):
    t = np.arange(tm)[:, None] + HALO
    s = np.arange(tm + HALO)[None, :]
    return jnp.asarray(np.stack([((s <= t) & (s >= t - w + 1)) for w in POOL_WINDOWS]).astype(np.float32), BF16)


def _mixer(xp, xs_t, st_t, win_b, wpool_b, pscale, sg, sb, ws, bsb, wrow, brow, wout_b, g1, b1, rwt, rb_col, alpha):
    B, T, D = xp.shape
    TS, NS, _ = xs_t.shape
    d_pool = st_t.shape[2]
    d_sgu = wrow.shape[2]
    tps = T // TB
    n_pt = B * tps
    NB = TB // TS
    n_st = NS // NB
    n = B * T + NS * TS
    nblk = n_pt + n_st
    pt = lambda i: jnp.minimum(i, n_pt - 1)
    stile = lambda i: jnp.clip(i - n_pt, 0, n_st - 1)
    blk = lambda i: jnp.minimum(i, nblk - 1)
    routed = lambda i: jnp.maximum(i - 1, 0)
    body = functools.partial(_mixer_body, alpha, n_pt, tps)
    rw_hi = rwt.astype(BF16)
    rw_lo = (rwt - rw_hi.astype(F32)).astype(BF16)
    upper = jnp.asarray(np.triu(np.ones((TB, TB), np.float32), 1), BF16)
    lstrict = jnp.asarray(np.tril(np.ones((N_EXPERTS, N_EXPERTS), np.float32), -1), BF16)
    consts = (win_b, _pool_band(TB), wpool_b, pscale, sg, sb, ws, bsb, wrow, brow, wout_b, g1, b1,
              rw_hi, rw_lo, rb_col, upper, lstrict)
    return pl.pallas_call(
        body,
        grid=(nblk + 1,),
        in_specs=[
            pl.BlockSpec((1, TB, D), lambda i: (pt(i) // tps, pt(i) % tps, 0)),
            pl.BlockSpec((TS, NB, D), lambda i: (0, stile(i), 0), pipeline_mode=pl.Buffered(1)),
            pl.BlockSpec((POOL_BUF, NB, d_pool), lambda i: (0, stile(i), 0), pipeline_mode=pl.Buffered(1)),
        ] + [_const_spec(c.shape) for c in consts],
        out_specs=[
            pl.BlockSpec((TB, D), lambda i: (blk(i), 0)),
            pl.BlockSpec((TB, D), lambda i: (blk(i), 0)),
            pl.BlockSpec((1, HALO, d_pool), lambda i: (pt(i) // tps, 0, 0)),
            pl.BlockSpec((TS, NB, d_pool), lambda i: (0, stile(i), 0)),
            pl.BlockSpec((TS, NB, d_sgu), lambda i: (0, stile(i), 0)),
            pl.BlockSpec((1, TOP_K, TB), lambda i: (routed(i), 0, 0)),
            pl.BlockSpec((1, TOP_K, TB), lambda i: (routed(i), 0, 0)),
            pl.BlockSpec((1, N_EXPERTS, 128), lambda i: (routed(i), 0, 0)),
        ],
        out_shape=[
            jax.ShapeDtypeStruct((n, D), F32),
            jax.ShapeDtypeStruct((n, D), BF16),
            jax.ShapeDtypeStruct((B, HALO, d_pool), F32),
            jax.ShapeDtypeStruct((TS, NS, d_pool), F32),
            jax.ShapeDtypeStruct((TS, NS, d_sgu), F32),
            jax.ShapeDtypeStruct((nblk, TOP_K, TB), I32),
            jax.ShapeDtypeStruct((nblk, TOP_K, TB), F32),
            jax.ShapeDtypeStruct((nblk, N_EXPERTS, 128), I32),
        ],
        scratch_shapes=[
            pltpu.VMEM((TB + HALO, d_pool), BF16),
            pltpu.VMEM((TB, D), BF16),
            pltpu.VMEM((TB, D), F32),
        ],
        compiler_params=pltpu.CompilerParams(
            dimension_semantics=("arbitrary",), vmem_limit_bytes=VMEM_LIMIT),
        name="mixer",
    )(xp, xs_t, st_t, *consts)


def _route_tile(h1, h1b, rwh_ref, rwl_ref, rb_ref, upper_ref, lstrict_ref, dest_ref, wsel_ref, cnt_ref):
    per_group = N_EXPERTS // N_EXPERT_GROUPS
    neg = -jnp.inf
    nt = (((1,), (1,)), ((), ()))
    h_lo = (h1 - h1b.astype(F32)).astype(BF16)
    rwh = rwh_ref[...]
    logits_t = (lax.dot_general(rwh, h1b, nt, preferred_element_type=F32)
                + lax.dot_general(rwh, h_lo, nt, preferred_element_type=F32)
                + lax.dot_general(rwl_ref[...], h1b, nt, preferred_element_type=F32))
    s_t = jax.nn.sigmoid(logits_t)
    b_t = s_t + rb_ref[...]

    io_g = lax.broadcasted_iota(I32, (per_group, TB), 0)
    gs = []
    for g in range(N_EXPERT_GROUPS):
        xg = b_t[g * per_group:(g + 1) * per_group, :]
        m1 = jnp.max(xg, axis=0, keepdims=True)
        i1 = jnp.min(jnp.where(xg == m1, io_g, per_group), axis=0, keepdims=True)
        m2 = jnp.max(jnp.where(io_g == i1, neg, xg), axis=0, keepdims=True)
        gs.append(m1 + m2)
    masked = []
    for g in range(N_EXPERT_GROUPS):
        rank = jnp.zeros((1, TB), F32)
        for g2 in range(N_EXPERT_GROUPS):
            if g2 != g:
                ahead = (gs[g2] >= gs[g]) if g2 < g else (gs[g2] > gs[g])
                rank = rank + jnp.where(ahead, 1.0, 0.0)
        keep = rank < float(TOPK_GROUPS)
        masked.append(jnp.where(keep, b_t[g * per_group:(g + 1) * per_group, :], neg))
    xm = jnp.concatenate(masked, axis=0)

    io_e = lax.broadcasted_iota(I32, (N_EXPERTS, TB), 0)
    onehots = []
    sel = jnp.zeros((N_EXPERTS, TB), F32)
    for _ in range(TOP_K):
        m = jnp.max(xm, axis=0, keepdims=True)
        idx = jnp.min(jnp.where(xm == m, io_e, N_EXPERTS), axis=0, keepdims=True)
        oh = io_e == idx
        onehots.append(oh)
        sel = jnp.where(oh, 1.0, sel)
        xm = jnp.where(oh, neg, xm)

    ssel = sel * s_t
    denom = jnp.sum(ssel, axis=0, keepdims=True)
    comb = ssel / denom * ROUTE_SCALE

    rank_t = _dot(sel.astype(BF16), upper_ref[...])
    cnt = jnp.sum(sel, axis=1, keepdims=True)
    cnt_i = cnt.astype(I32)
    seg16 = ((cnt_i + (SEG - 1)) // SEG).astype(F32)
    off16 = _dot(lstrict_ref[...], jnp.broadcast_to(seg16, (N_EXPERTS, 128)).astype(BF16))
    d_t = rank_t + off16[:, 0:1] * float(SEG)
    for k in range(TOP_K):
        dest_ref[0, k:k + 1, :] = jnp.sum(jnp.where(onehots[k], d_t, 0.0), axis=0, keepdims=True).astype(I32)
        wsel_ref[0, k:k + 1, :] = jnp.sum(jnp.where(onehots[k], comb, 0.0), axis=0, keepdims=True)
    cnt_ref[0] = jnp.broadcast_to(cnt_i, (N_EXPERTS, 128))


def _start_granule_copies(b, j0, j1, gran_g, gmax, make_copy):
    def per_granule(j, c):
        make_copy(pl.multiple_of(j * SEG, SEG), pl.multiple_of(gran_g[b * gmax + j], SEG)).start()
        return c

    lax.fori_loop(j0, j1, per_granule, 0)


def _wait_granule_copies(count, max_count, make_wait_copy):
    for k in range(int(max_count).bit_length()):
        @pl.when(((count >> k) & 1) == 1)
        def _():
            make_wait_copy(SEG << k).wait()


def _dispatch_body(gran_g, n_gran, nch, tail_g, tail_n, dest_ref, wsel_ref, h_ref, tokid_ref, ones_ref,
                   xs_hbm, src_ref, wrow_ref, xloc_ref, zero_ref, sems, tail_sem):
    b = pl.program_id(0)
    nb = pl.num_programs(0)
    gmax = gran_g.shape[0] // n_gran.shape[0]
    gpp = 2 * KC // SEG
    slot = b % 2
    n_b = n_gran[b]

    def seg_copy(sl):
        return lambda l, g: pltpu.make_async_copy(
            xloc_ref.at[sl, pl.ds(l, SEG), :], xs_hbm.at[pl.ds(g, SEG), :], sems.at[sl])

    def wait_block(count, sl):
        _wait_granule_copies(count, gmax, lambda rows: pltpu.make_async_copy(
            xloc_ref.at[sl, pl.ds(0, rows), :], xs_hbm.at[pl.ds(0, rows), :], sems.at[sl]))

    @pl.when(b >= 2)
    def _():
        wait_block(n_gran[jnp.maximum(b - 2, 0)], slot)

    dest = dest_ref[0]
    wsel = wsel_ref[0]
    h = h_ref[...]
    src_ref[...] = jnp.zeros(src_ref.shape, F32)
    wrow_ref[...] = jnp.zeros(wrow_ref.shape, F32)

    row_in_chunk = lax.broadcasted_iota(I32, (KC, TB), 0).astype(F32).astype(BF16)
    wsel_b = wsel.astype(BF16)
    one = jnp.ones((1, TB), BF16)

    def chunk(c):
        r0 = pl.multiple_of(c * KC, KC)
        rel = (dest - r0).astype(F32).astype(BF16)
        sb = jnp.zeros((KC, TB), BF16)
        wm = jnp.zeros((KC, TB), BF16)
        for k in range(TOP_K):
            hit = rel[k:k + 1, :] == row_in_chunk
            sb = jnp.where(hit, one, sb)
            wm = jnp.where(hit, wsel_b[k:k + 1, :], wm)
        xloc_ref[slot, pl.ds(r0, KC), :] = _dot(sb, h).astype(BF16)
        src_ref[0, pl.ds(r0, KC), :] = _dot(sb, tokid_ref[...])
        wrow_ref[0, pl.ds(r0, KC), :] = _dot(wm, ones_ref[...])

    def chunk_pair(cp, carry):
        chunk(2 * cp)
        chunk(2 * cp + 1)
        _start_granule_copies(b, cp * gpp, jnp.minimum((cp + 1) * gpp, n_b), gran_g, gmax, seg_copy(slot))
        return carry

    lax.fori_loop(0, (nch[b] + 1) // 2, chunk_pair, 0)

    def tail_copy(g):
        return pltpu.make_async_copy(zero_ref, xs_hbm.at[pl.ds(g, SEG), :], tail_sem)

    def tails(wait):
        def per_expert(e, carry):
            def per_granule(j, c):
                cp = tail_copy(pl.multiple_of(tail_g[e] + j * SEG, SEG))
                if wait:
                    cp.wait()
                else:
                    cp.start()
                return c
            lax.fori_loop(0, tail_n[e], per_granule, 0)
            return carry
        lax.fori_loop(0, N_EXPERTS, per_expert, 0)

    @pl.when(b == nb - 1)
    def _():
        zero_ref[...] = jnp.zeros(zero_ref.shape, BF16)
        tails(False)

        @pl.when(b >= 1)
        def _():
            wait_block(n_gran[jnp.maximum(b - 1, 0)], 1 - slot)

        wait_block(n_b, slot)
        tails(True)


def _dispatch(meta, dest, wsel, h1b_all, n_rows_sorted, m_out_max):
    n, D = h1b_all.shape
    nb = n // TB
    tokid = jnp.asarray(np.broadcast_to(np.arange(TB, dtype=np.float32)[:, None], (TB, 128)), BF16)
    ones = jnp.ones((TB, 128), BF16)
    cs = lambda shape: pl.BlockSpec(shape, lambda i, *_: (0,) * len(shape), pipeline_mode=pl.Buffered(1))
    grid_spec = pltpu.PrefetchScalarGridSpec(
        num_scalar_prefetch=5,
        grid=(nb,),
        in_specs=[
            pl.BlockSpec((1, TOP_K, TB), lambda i, *_: (i, 0, 0)),
            pl.BlockSpec((1, TOP_K, TB), lambda i, *_: (i, 0, 0)),
            pl.BlockSpec((TB, D), lambda i, *_: (i, 0)),
            cs(tokid.shape), cs(ones.shape),
        ],
        out_specs=[
            pl.BlockSpec(memory_space=pl.ANY),
            pl.BlockSpec((1, m_out_max, 128), lambda i, *_: (i, 0, 0)),
            pl.BlockSpec((1, m_out_max, 128), lambda i, *_: (i, 0, 0)),
        ],
        scratch_shapes=[
            pltpu.VMEM((2, m_out_max, D), BF16),
            pltpu.VMEM((SEG, D), BF16),
            pltpu.SemaphoreType.DMA((2,)),
            pltpu.SemaphoreType.DMA(()),
        ],
    )
    return pl.pallas_call(
        _dispatch_body,
        grid_spec=grid_spec,
        out_shape=[
            jax.ShapeDtypeStruct((n_rows_sorted, D), BF16),
            jax.ShapeDtypeStruct((nb, m_out_max, 128), F32),
            jax.ShapeDtypeStruct((nb, m_out_max, 128), F32),
        ],
        compiler_params=pltpu.CompilerParams(
            dimension_semantics=("arbitrary",), vmem_limit_bytes=VMEM_LIMIT, has_side_effects=True),
        name="dispatch",
    )(meta["gran_g"], meta["n_gran"], meta["nch"], meta["tail_g"], meta["tail_n"], dest, wsel,
      h1b_all, tokid, ones)


def _ffn_body(tile_e, tile_src, tile_valid, e_slot, e_next, x_ref, wg_hbm, wu_hbm, wd_hbm, y_ref,
              wg_f, wu_f, wd_f, wgu_b, wd_b, sems):
    i = pl.program_id(0)
    f = wg_f.shape[2]
    e = tile_e[i]
    e_prev = tile_e[jnp.maximum(i - 1, 0)]

    def weight_copies(expert, slot):
        return (pltpu.make_async_copy(wg_hbm.at[expert], wg_f.at[slot], sems.at[slot, 0]),
                pltpu.make_async_copy(wu_hbm.at[expert], wu_f.at[slot], sems.at[slot, 1]),
                pltpu.make_async_copy(wd_hbm.at[expert], wd_f.at[slot], sems.at[slot, 2]))

    @pl.when((i == 0) | (e != e_prev))
    def _():
        slot = e_slot[e]

        @pl.when(i == 0)
        def _():
            for cp in weight_copies(e, slot):
                cp.start()

        for cp in weight_copies(e, slot):
            cp.wait()
        nxt = e_next[e]

        @pl.when(nxt >= 0)
        def _():
            for cp in weight_copies(nxt, 1 - slot):
                cp.start(priority=1)

        wgu_b[:, :f] = wg_f[slot].astype(BF16)
        wgu_b[:, f:] = wu_f[slot].astype(BF16)
        wd_b[...] = wd_f[slot].astype(BF16)

    @pl.when(tile_valid[i] == 1)
    def _():
        gu = _dot(x_ref[...], wgu_b[...])
        act = (jax.nn.silu(gu[:, :f]) * gu[:, f:]).astype(BF16)
        y_ref[...] = _dot(act, wd_b[...]).astype(BF16)


def _expert_ffn(meta, xs, w_gate, w_up, w_down):
    R, D = xs.shape
    E, _, f = w_gate.shape
    nt = R // TR
    grid_spec = pltpu.PrefetchScalarGridSpec(
        num_scalar_prefetch=5,
        grid=(nt,),
        in_specs=[
            pl.BlockSpec((TR, D), lambda i, te, ts, *_: (ts[i], 0)),
            pl.BlockSpec(memory_space=pl.ANY),
            pl.BlockSpec(memory_space=pl.ANY),
            pl.BlockSpec(memory_space=pl.ANY),
        ],
        out_specs=pl.BlockSpec((TR, D), lambda i, te, ts, *_: (ts[i], 0)),
        scratch_shapes=[
            pltpu.VMEM((2, D, f), F32),
            pltpu.VMEM((2, D, f), F32),
            pltpu.VMEM((2, f, D), F32),
            pltpu.VMEM((D, 2 * f), BF16),
            pltpu.VMEM((f, D), BF16),
            pltpu.SemaphoreType.DMA((2, 3)),
        ],
    )
    return pl.pallas_call(
        _ffn_body,
        grid_spec=grid_spec,
        out_shape=jax.ShapeDtypeStruct((R, D), BF16),
        compiler_params=pltpu.CompilerParams(
            dimension_semantics=("arbitrary",), vmem_limit_bytes=VMEM_LIMIT),
        name="expert_ffn",
    )(meta["tile_e"], meta["tile_src"], meta["tile_valid"], meta["e_slot"], meta["e_next"],
      xs, w_gate, w_up, w_down)


def _combine_body(alpha, n_pb, gran_g, n_gran, nch, src_ref, wrow_ref, h1_ref, h1b_ref, p_ref,
                  wsg_ref, wsu_ref, wsd_ref, g2_ref, b2_ref, wpg_ref, bpg_ref, wpe_ref, ys_hbm,
                  yp_ref, ysm_ref, yloc_ref, acc_ref, sems):
    b = pl.program_id(0)
    n_chunks = nch[b]
    n_b = n_gran[b]
    gmax = gran_g.shape[0] // n_gran.shape[0]
    gpc = KCC // SEG

    last0 = pl.multiple_of((n_chunks - 1) * KCC, KCC)
    yloc_ref[pl.ds(last0, KCC), :] = jnp.zeros((KCC, yloc_ref.shape[1]), BF16)

    def seg_copy(l, g):
        return pltpu.make_async_copy(ys_hbm.at[pl.ds(g, SEG), :], yloc_ref.at[pl.ds(l, SEG), :],
                                     sems.at[l // KCC])

    _start_granule_copies(b, 0, n_b, gran_g, gmax, seg_copy)

    hb = h1b_ref[...]
    sh = (jax.nn.silu(_dot(hb, wsg_ref[...])) * _dot(hb, wsu_ref[...])).astype(BF16)
    acc_ref[...] = _dot(sh, wsd_ref[...])

    def chunk(c, carry):
        def wait_rows(rows):
            return pltpu.make_async_copy(ys_hbm.at[pl.ds(0, rows), :], yloc_ref.at[pl.ds(0, rows), :], sems.at[c])

        _wait_granule_copies(jnp.minimum(n_b - c * gpc, gpc), gpc, wait_rows)

        r0 = pl.multiple_of(c * KCC, KCC)
        src = src_ref[0, pl.ds(r0, KCC), :]
        w = wrow_ref[0, pl.ds(r0, KCC), :]
        lane = lax.broadcasted_iota(I32, src.shape, 1).astype(F32)
        wm = jnp.concatenate([jnp.where(src == lane + float(o), w, 0.0) for o in range(0, TB, src.shape[1])],
                             axis=1).astype(BF16)
        acc_ref[...] += lax.dot_general(wm, yloc_ref[pl.ds(r0, KCC), :],
                                        (((0,), (0,)), ((), ())), preferred_element_type=F32)
        return carry

    lax.fori_loop(0, n_chunks, chunk, 0)

    h2 = _ln_rows(alpha * h1_ref[...] + acc_ref[...], g2_ref[...], b2_ref[...])
    gate = jax.nn.sigmoid(_dot(h2.astype(BF16), wpg_ref[...]) + bpg_ref[...])
    pe = _dot(p_ref[...].astype(BF16), wpe_ref[...])
    y = h2 + gate * pe

    @pl.when(b < n_pb)
    def _():
        yp_ref[...] = y

    @pl.when(b >= n_pb)
    def _():
        ysm_ref[...] = y


def _combine(meta, src_rep, w_rep, h1_all, h1b_all, p_all, wsg_b, wsu_b, wsd_b, g2, b2, wpg_b, bpg, wpe_b, ys,
             m_out_max, n_prompt, alpha):
    n, D = h1_all.shape
    nb = n // TB
    n_pb = n_prompt // TB
    d_pe = p_all.shape[1]
    cs = lambda shape: pl.BlockSpec(shape, lambda i, *_: (0,) * len(shape), pipeline_mode=pl.Buffered(1))
    grid_spec = pltpu.PrefetchScalarGridSpec(
        num_scalar_prefetch=3,
        grid=(nb,),
        in_specs=[
            pl.BlockSpec((1, m_out_max, 128), lambda i, *_: (i, 0, 0)),
            pl.BlockSpec((1, m_out_max, 128), lambda i, *_: (i, 0, 0)),
            pl.BlockSpec((TB, D), lambda i, *_: (i, 0)),
            pl.BlockSpec((TB, D), lambda i, *_: (i, 0)),
            pl.BlockSpec((TB, d_pe), lambda i, *_: (i, 0)),
            cs(wsg_b.shape), cs(wsu_b.shape), cs(wsd_b.shape), cs(g2.shape), cs(b2.shape),
            cs(wpg_b.shape), cs(bpg.shape), cs(wpe_b.shape),
            pl.BlockSpec(memory_space=pl.ANY),
        ],
        out_specs=[
            pl.BlockSpec((TB, D), lambda i, *_: (jnp.minimum(i, n_pb - 1), 0)),
            pl.BlockSpec((TB, D), lambda i, *_: (jnp.maximum(i - n_pb, 0), 0)),
        ],
        scratch_shapes=[
            pltpu.VMEM((m_out_max, D), BF16),
            pltpu.VMEM((TB, D), F32),
            pltpu.SemaphoreType.DMA((m_out_max // KCC,)),
        ],
    )
    return pl.pallas_call(
        functools.partial(_combine_body, alpha, n_pb),
        grid_spec=grid_spec,
        out_shape=[jax.ShapeDtypeStruct((n_prompt, D), F32), jax.ShapeDtypeStruct((n - n_prompt, D), F32)],
        compiler_params=pltpu.CompilerParams(
            dimension_semantics=("arbitrary",), vmem_limit_bytes=VMEM_LIMIT),
        name="combine",
    )(meta["gran_g"], meta["n_gran"], meta["nchc"], src_rep, w_rep, h1_all, h1b_all, p_all,
      wsg_b, wsu_b, wsd_b, g2, b2, wpg_b, bpg, wpe_b, ys)


def _sort_meta(cnt, n_tiles_max, m_out_max):
    segp = (cnt + (SEG - 1)) // SEG * SEG
    loc = jnp.cumsum(segp, axis=1) - segp
    before = jnp.cumsum(segp, axis=0) - segp
    total = jnp.sum(segp, axis=0)
    total_p = (total + (TR - 1)) // TR * TR
    ends = jnp.cumsum(total_p)
    base = ends - total_p
    glob = base[None, :] + before
    n_valid = ends[-1] // TR
    tiles = jnp.arange(n_tiles_max, dtype=I32)
    tile_src = jnp.minimum(tiles, n_valid - 1)
    tile_e = jnp.sum((ends[None, :] <= (tile_src * TR)[:, None]).astype(I32), axis=1)
    experts = jnp.arange(N_EXPERTS, dtype=I32)
    active = total_p > 0
    later = active[None, :] & (experts[None, :] > experts[:, None])
    e_next = jnp.min(jnp.where(later, experts[None, :], N_EXPERTS), axis=1)
    gmax = m_out_max // SEG
    n16 = segp // SEG
    g_end = jnp.cumsum(n16, axis=1)
    g_start = g_end - n16
    j = jnp.arange(gmax, dtype=I32)[None, :, None]
    in_seg = (g_start[:, None, :] <= j) & (j < g_end[:, None, :])
    gran_g = jnp.sum(jnp.where(in_seg, (glob - loc)[:, None, :], 0), axis=2) + j[:, :, 0] * SEG
    return {
        "gran_g": gran_g.reshape(-1).astype(I32),
        "n_gran": g_end[:, -1].astype(I32),
        "nch": ((jnp.sum(segp, axis=1) + (KC - 1)) // KC).astype(I32),
        "nchc": ((jnp.sum(segp, axis=1) + (KCC - 1)) // KCC).astype(I32),
        "tail_g": (base + total).astype(I32),
        "tail_n": ((total_p - total) // SEG).astype(I32),
        "tile_e": jnp.minimum(tile_e, N_EXPERTS - 1).astype(I32),
        "tile_src": tile_src.astype(I32),
        "tile_valid": (tiles < n_valid).astype(I32),
        "e_slot": ((jnp.cumsum(active.astype(I32)) - 1) % 2).astype(I32),
        "e_next": jnp.where(e_next < N_EXPERTS, e_next, -1).astype(I32),
    }


def _layer(xp, xs, st, pp, ps, w_in, w_pool, pool_scale, sgu_ln_g, sgu_ln_b, w_s, b_s, w_out, ln1_g, ln1_b,
           router_w, router_bias, w_gate, w_up, w_down, ws_gate, ws_up, ws_down, ln2_g, ln2_b, w_pe, w_pgate,
           b_pgate, alpha):
    B, T, D = xp.shape
    NS, TS, _ = xs.shape
    n_prompt, n_sample = B * T, NS * TS
    n = n_prompt + n_sample
    n_heads = w_s.shape[0]
    row = lambda v: v.reshape(1, -1)

    x_t = jnp.transpose(xs, (1, 0, 2))
    st_t = jnp.transpose(st, (1, 0, 2))
    bsb = jnp.broadcast_to(b_s[:, :, None], (n_heads, CHUNK, SGU_HEAD_DIM))
    wrow = jnp.repeat(jnp.transpose(w_s[:, :TS, :TS], (1, 2, 0)), SGU_HEAD_DIM, axis=2)
    brow = jnp.repeat(jnp.transpose(b_s[:, :TS], (1, 0)), SGU_HEAD_DIM, axis=1)
    h1_all, h1b_all, pool_p, a_s, vn_s, dest, wsel, cnt = _mixer(
        xp, x_t, st_t, w_in.astype(BF16), w_pool.astype(BF16), row(pool_scale), sgu_ln_g, sgu_ln_b, w_s, bsb,
        wrow, brow, w_out.astype(BF16), row(ln1_g), row(ln1_b), jnp.transpose(router_w),
        router_bias.reshape(-1, 1), alpha)
    nb = n // TB
    m_out_max = TB * TOP_K + N_EXPERTS * SEG
    rows_max = n * TOP_K + nb * N_EXPERTS * (SEG - 1) + N_EXPERTS * (TR - SEG)
    n_tiles_max = -(-rows_max // TR)
    meta = _sort_meta(cnt[:, :, 0], n_tiles_max, m_out_max)
    xsort, src_rep, w_rep = _dispatch(meta, dest, wsel, h1b_all, n_tiles_max * TR, m_out_max)
    ysort = _expert_ffn(meta, xsort, w_gate, w_up, w_down)

    nbs = TB // TS
    ps_t = jnp.transpose(ps.reshape(NS // nbs, nbs, TS, -1), (0, 2, 1, 3)).reshape(n_sample, -1)
    p_all = jnp.concatenate([pp.reshape(n_prompt, -1), ps_t], axis=0)
    y_p, y_s = _combine(meta, src_rep, w_rep, h1_all, h1b_all, p_all, ws_gate.astype(BF16), ws_up.astype(BF16),
                        ws_down.astype(BF16), row(ln2_g), row(ln2_b), w_pgate.astype(BF16), row(b_pgate),
                        w_pe.astype(BF16), ysort, m_out_max, n_prompt, alpha)

    yp = y_p.reshape(B, T, D)
    ys = jnp.transpose(y_s.reshape(NS // nbs, TS, nbs, D), (0, 2, 1, 3)).reshape(NS, TS, D)
    new_pool_p = pool_p[:, HALO - POOL_BUF:, :]
    new_pool_s = jnp.concatenate([st, jnp.transpose(a_s, (1, 0, 2))], axis=1)[:, -POOL_BUF:]
    vn = jnp.transpose(vn_s, (1, 0, 2)).reshape(NS, TS, n_heads, SGU_HEAD_DIM)
    return yp, ys, new_pool_p, new_pool_s, vn


def kernel(x_prompt, x_sample, state_pool, p_prompt, p_sample, w_in, w_pool, pool_scale, sgu_ln_g, sgu_ln_b, w_s, b_s, w_out, ln1_g, ln1_b, router_w, router_bias, w_gate, w_up, w_down, ws_gate, ws_up, ws_down, ln2_g, ln2_b, w_pe, w_pgate, b_pgate):
    depth = w_in.shape[0]
    alpha = (2.0 * depth) ** 0.25
    hp, hs = x_prompt, x_sample
    pool_p, pool_s, v_s = [], [], []
    for i in range(depth):
        hp, hs, bp, bs, vs = _layer(
            hp, hs, state_pool[i], p_prompt[i], p_sample[i], w_in[i], w_pool[i], pool_scale[i], sgu_ln_g[i],
            sgu_ln_b[i], w_s[i], b_s[i], w_out[i], ln1_g[i], ln1_b[i], router_w[i], router_bias[i], w_gate[i],
            w_up[i], w_down[i], ws_gate[i], ws_up[i], ws_down[i], ln2_g[i], ln2_b[i], w_pe[i], w_pgate[i],
            b_pgate[i], alpha)
        pool_p.append(bp)
        pool_s.append(bs)
        v_s.append(vs)
    return hp, hs, jnp.stack(pool_p), jnp.stack(pool_s), jnp.stack(v_s)
```

```python
import functools

import jax
import jax.numpy as jnp
import numpy as np
from jax import lax
from jax.experimental import pallas as pl
from jax.experimental.pallas import tpu as pltpu

F32 = jnp.float32
BF16 = jnp.bfloat16
I32 = jnp.int32

POOL_WINDOWS = (2, 4, 8, 16)
POOL_BUF = max(POOL_WINDOWS) - 1
CHUNK = 128
SGU_HEAD_DIM = 128
N_EXPERTS = 64
TOP_K = 8
N_EXPERT_GROUPS = 8
TOPK_GROUPS = 4
ROUTE_SCALE = 2.5
LN_EPS = 1e-5
PAST_LEN = 16384

TB = 256
HALO = 16
TR = 512
TRC = 256
SEG = 16
KC = 256
KCC = 512
VMEM_LIMIT = 56 * 1024 * 1024


def _const_spec(shape):
    n = len(shape)
    return pl.BlockSpec(shape, lambda *_: (0,) * n, pipeline_mode=pl.Buffered(1))


def _ln_rows(x, g, b):
    mu = jnp.mean(x, axis=-1, keepdims=True)
    xc = x - mu
    var = jnp.mean(xc * xc, axis=-1, keepdims=True)
    return xc * lax.rsqrt(var + LN_EPS) * g + b


def _dot(a, b):
    return jnp.dot(a, b, preferred_element_type=F32)


def _prompt_tile(j, n_tiles, x, proj, band_ref, wpool_ref, pscale_ref, sg_ref, sb_ref, ws_ref, bsb_ref,
                 pool_ref, zcat_ref, mixin_ref):
    d_pool = zcat_ref.shape[1]
    d_pool_g = d_pool // len(POOL_WINDOWS)
    d_sgu = mixin_ref.shape[1] - d_pool

    a = proj[:, :d_pool]
    zcat_ref[HALO:, :] = a.astype(BF16)

    pos1 = (j * TB + lax.broadcasted_iota(I32, (TB, 1), 0) + 1).astype(F32)
    for g, w in enumerate(POOL_WINDOWS):
        c0, c1 = g * d_pool_g, (g + 1) * d_pool_g
        s = _dot(band_ref[g], zcat_ref[:, c0:c1])
        cnt = jnp.minimum(pos1, float(w))
        d = s / cnt - a[:, c0:c1]
        y = _dot(d.astype(BF16), wpool_ref[g])
        mixin_ref[:, c0:c1] = (y * pscale_ref[:, c0:c1]).astype(BF16)
    zcat_ref[0:HALO, :] = jnp.where(j == n_tiles - 1, jnp.zeros((HALO, d_pool), BF16), zcat_ref[TB:TB + HALO, :])
    pool_ref[0] = a[TB - HALO:, :]

    u = jax.nn.gelu(proj[:, d_pool:d_pool + d_sgu])
    v = jax.nn.gelu(proj[:, d_pool + d_sgu:])
    r = lax.broadcasted_iota(I32, (CHUNK, CHUNK), 0)
    c = lax.broadcasted_iota(I32, (CHUNK, CHUNK), 1)
    tril = r >= c
    for h in range(d_sgu // SGU_HEAD_DIM):
        l0, l1 = h * SGU_HEAD_DIM, (h + 1) * SGU_HEAD_DIM
        vn = _ln_rows(v[:, l0:l1], sg_ref[h:h + 1, :], sb_ref[h:h + 1, :]).astype(BF16)
        wst = jnp.where(tril, ws_ref[h], 0.0).astype(BF16)
        n_ch = TB // CHUNK
        mixed_all = _dot(wst, jnp.concatenate([vn[ci * CHUNK:(ci + 1) * CHUNK] for ci in range(n_ch)], axis=1))
        for ci in range(n_ch):
            r0, r1 = ci * CHUNK, (ci + 1) * CHUNK
            mixed = mixed_all[:, ci * SGU_HEAD_DIM:(ci + 1) * SGU_HEAD_DIM] + bsb_ref[h]
            mixin_ref[r0:r1, d_pool + l0:d_pool + l1] = (u[r0:r1, l0:l1] * mixed).astype(BF16)


def _sample_tile(T, NB, proj, st_ref, wpool_ref, pscale_ref, sg_ref, sb_ref, wrow_ref, brow_ref,
                 a_ref, vn_ref, dbuf_ref, mixin_ref):
    d_pool = dbuf_ref.shape[1]
    d_pool_g = d_pool // len(POOL_WINDOWS)
    d_sgu = mixin_ref.shape[1] - d_pool

    a = proj[:, :d_pool]
    for t in range(T):
        a_ref[t] = a[t * NB:(t + 1) * NB]

    for g, w in enumerate(POOL_WINDOWS):
        c0, c1 = g * d_pool_g, (g + 1) * d_pool_g
        for t in range(T):
            cnt = float(min(PAST_LEN + t + 1, w))
            acc = None
            for s in range(POOL_BUF + t - w + 1, POOL_BUF + t + 1):
                if s < POOL_BUF:
                    term = st_ref[s, :, c0:c1]
                else:
                    term = a[(s - POOL_BUF) * NB:(s - POOL_BUF + 1) * NB, c0:c1]
                acc = term if acc is None else acc + term
            d = acc / cnt - a[t * NB:(t + 1) * NB, c0:c1]
            dbuf_ref[t * NB:(t + 1) * NB, c0:c1] = d.astype(BF16)
        y = _dot(dbuf_ref[0:T * NB, c0:c1], wpool_ref[g])
        mixin_ref[:, c0:c1] = (y * pscale_ref[:, c0:c1]).astype(BF16)

    u = jax.nn.gelu(proj[:, d_pool:d_pool + d_sgu])
    v = jax.nn.gelu(proj[:, d_pool + d_sgu:])
    for h in range(d_sgu // SGU_HEAD_DIM):
        l0, l1 = h * SGU_HEAD_DIM, (h + 1) * SGU_HEAD_DIM
        vn = _ln_rows(v[:, l0:l1], sg_ref[h:h + 1, :], sb_ref[h:h + 1, :])
        for t in range(T):
            vn_ref[t, :, l0:l1] = vn[t * NB:(t + 1) * NB]
    for t in range(T):
        mixed = brow_ref[t:t + 1, :]
        for s in range(t + 1):
            mixed = mixed + wrow_ref[t, s:s + 1, :] * vn_ref[s]
        mixin_ref[t * NB:(t + 1) * NB, d_pool:] = (u[t * NB:(t + 1) * NB] * mixed).astype(BF16)


def _mixer_body(alpha, n_pt, tiles_per_seq, xp_ref, xs_ref, st_ref, win_ref, band_ref, wpool_ref, pscale_ref,
                sg_ref, sb_ref, ws_ref, bsb_ref, wrow_ref, brow_ref, wout_ref, g1_ref, b1_ref,
                rwh_ref, rwl_ref, rb_ref, upper_ref, lstrict_ref,
                h1_ref, h1b_ref, pool_ref, a_ref, vn_ref, dest_ref, wsel_ref, cnt_ref,
                zcat_ref, mixin_ref, hprev_ref):
    i = pl.program_id(0)
    n_tiles = pl.num_programs(0) - 1

    def route_previous_tile():
        h = hprev_ref[...]
        _route_tile(h, h.astype(BF16), rwh_ref, rwl_ref, rb_ref, upper_ref, lstrict_ref,
                    dest_ref, wsel_ref, cnt_ref)

    @pl.when(i == 0)
    def _():
        zcat_ref[0:HALO, :] = jnp.zeros((HALO, zcat_ref.shape[1]), BF16)
        hprev_ref[...] = jnp.zeros(hprev_ref.shape, F32)

    @pl.when(i < n_pt)
    def _():
        x = xp_ref[0]
        proj = _dot(x.astype(BF16), win_ref[...])
        route_previous_tile()
        _prompt_tile(i % tiles_per_seq, tiles_per_seq, x, proj, band_ref, wpool_ref, pscale_ref, sg_ref, sb_ref,
                     ws_ref, bsb_ref, pool_ref, zcat_ref, mixin_ref)
        h1_ref[...] = alpha * x

    @pl.when(i >= n_pt)
    def _():
        route_previous_tile()

    @pl.when((i >= n_pt) & (i < n_tiles))
    def _():
        T, NB, D = xs_ref.shape
        x = xs_ref[...].reshape(T * NB, D)
        proj = _dot(x.astype(BF16), win_ref[...])
        _sample_tile(T, NB, proj, st_ref, wpool_ref, pscale_ref, sg_ref, sb_ref, wrow_ref, brow_ref,
                     a_ref, vn_ref, zcat_ref, mixin_ref)
        h1_ref[...] = alpha * x

    @pl.when(i < n_tiles)
    def _():
        mix = _dot(mixin_ref[...], wout_ref[...])
        h1 = _ln_rows(h1_ref[...] + mix, g1_ref[...], b1_ref[...])
        h1_ref[...] = h1
        h1b_ref[...] = h1.astype(BF16)
        hprev_ref[...] = h1


def _pool_band(tm):
    t = np.arange(tm)[:, None] + HALO
    s = np.arange(tm + HALO)[None, :]
    return jnp.asarray(np.stack([((s <= t) & (s >= t - w + 1)) for w in POOL_WINDOWS]).astype(np.float32), BF16)


def _mixer(xp, xs_t, st_t, win_b, wpool_b, pscale, sg, sb, ws, bsb, wrow, brow, wout_b, g1, b1, rwt, rb_col, alpha):
    B, T, D = xp.shape
    TS, NS, _ = xs_t.shape
    d_pool = st_t.shape[2]
    d_sgu = wrow.shape[2]
    tps = T // TB
    n_pt = B * tps
    NB = TB // TS
    n_st = NS // NB
    n = B * T + NS * TS
    nblk = n_pt + n_st
    pt = lambda i: jnp.minimum(i, n_pt - 1)
    stile = lambda i: jnp.clip(i - n_pt, 0, n_st - 1)
    blk = lambda i: jnp.minimum(i, nblk - 1)
    routed = lambda i: jnp.maximum(i - 1, 0)
    body = functools.partial(_mixer_body, alpha, n_pt, tps)
    rw_hi = rwt.astype(BF16)
    rw_lo = (rwt - rw_hi.astype(F32)).astype(BF16)
    upper = jnp.asarray(np.triu(np.ones((TB, TB), np.float32), 1), BF16)
    lstrict = jnp.asarray(np.tril(np.ones((N_EXPERTS, N_EXPERTS), np.float32), -1), BF16)
    consts = (win_b, _pool_band(TB), wpool_b, pscale, sg, sb, ws, bsb, wrow, brow, wout_b, g1, b1,
              rw_hi, rw_lo, rb_col, upper, lstrict)
    return pl.pallas_call(
        body,
        grid=(nblk + 1,),
        in_specs=[
            pl.BlockSpec((1, TB, D), lambda i: (pt(i) // tps, pt(i) % tps, 0)),
            pl.BlockSpec((TS, NB, D), lambda i: (0, stile(i), 0), pipeline_mode=pl.Buffered(1)),
            pl.BlockSpec((POOL_BUF, NB, d_pool), lambda i: (0, stile(i), 0), pipeline_mode=pl.Buffered(1)),
        ] + [_const_spec(c.shape) for c in consts],
        out_specs=[
            pl.BlockSpec((TB, D), lambda i: (blk(i), 0)),
            pl.BlockSpec((TB, D), lambda i: (blk(i), 0)),
            pl.BlockSpec((1, HALO, d_pool), lambda i: (pt(i) // tps, 0, 0)),
            pl.BlockSpec((TS, NB, d_pool), lambda i: (0, stile(i), 0)),
            pl.BlockSpec((TS, NB, d_sgu), lambda i: (0, stile(i), 0)),
            pl.BlockSpec((1, TOP_K, TB), lambda i: (routed(i), 0, 0)),
            pl.BlockSpec((1, TOP_K, TB), lambda i: (routed(i), 0, 0)),
            pl.BlockSpec((1, N_EXPERTS, 128), lambda i: (routed(i), 0, 0)),
        ],
        out_shape=[
            jax.ShapeDtypeStruct((n, D), F32),
            jax.ShapeDtypeStruct((n, D), BF16),
            jax.ShapeDtypeStruct((B, HALO, d_pool), F32),
            jax.ShapeDtypeStruct((TS, NS, d_pool), F32),
            jax.ShapeDtypeStruct((TS, NS, d_sgu), F32),
            jax.ShapeDtypeStruct((nblk, TOP_K, TB), I32),
            jax.ShapeDtypeStruct((nblk, TOP_K, TB), F32),
            jax.ShapeDtypeStruct((nblk, N_EXPERTS, 128), I32),
        ],
        scratch_shapes=[
            pltpu.VMEM((TB + HALO, d_pool), BF16),
            pltpu.VMEM((TB, D), BF16),
            pltpu.VMEM((TB, D), F32),
        ],
        compiler_params=pltpu.CompilerParams(
            dimension_semantics=("arbitrary",), vmem_limit_bytes=VMEM_LIMIT),
        name="mixer",
    )(xp, xs_t, st_t, *consts)


def _route_tile(h1, h1b, rwh_ref, rwl_ref, rb_ref, upper_ref, lstrict_ref, dest_ref, wsel_ref, cnt_ref):
    per_group = N_EXPERTS // N_EXPERT_GROUPS
    neg = -jnp.inf
    nt = (((1,), (1,)), ((), ()))
    h_lo = (h1 - h1b.astype(F32)).astype(BF16)
    rwh = rwh_ref[...]
    logits_t = (lax.dot_general(rwh, h1b, nt, preferred_element_type=F32)
                + lax.dot_general(rwh, h_lo, nt, preferred_element_type=F32)
                + lax.dot_general(rwl_ref[...], h1b, nt, preferred_element_type=F32))
    s_t = jax.nn.sigmoid(logits_t)
    b_t = s_t + rb_ref[...]

    io_g = lax.broadcasted_iota(I32, (per_group, TB), 0)
    gs = []
    for g in range(N_EXPERT_GROUPS):
        xg = b_t[g * per_group:(g + 1) * per_group, :]
        m1 = jnp.max(xg, axis=0, keepdims=True)
        i1 = jnp.min(jnp.where(xg == m1, io_g, per_group), axis=0, keepdims=True)
        m2 = jnp.max(jnp.where(io_g == i1, neg, xg), axis=0, keepdims=True)
        gs.append(m1 + m2)
    masked = []
    for g in range(N_EXPERT_GROUPS):
        rank = jnp.zeros((1, TB), F32)
        for g2 in range(N_EXPERT_GROUPS):
            if g2 != g:
                ahead = (gs[g2] >= gs[g]) if g2 < g else (gs[g2] > gs[g])
                rank = rank + jnp.where(ahead, 1.0, 0.0)
        keep = rank < float(TOPK_GROUPS)
        masked.append(jnp.where(keep, b_t[g * per_group:(g + 1) * per_group, :], neg))
    xm = jnp.concatenate(masked, axis=0)

    io_e = lax.broadcasted_iota(I32, (N_EXPERTS, TB), 0)
    onehots = []
    sel = jnp.zeros((N_EXPERTS, TB), F32)
    for _ in range(TOP_K):
        m = jnp.max(xm, axis=0, keepdims=True)
        idx = jnp.min(jnp.where(xm == m, io_e, N_EXPERTS), axis=0, keepdims=True)
        oh = io_e == idx
        onehots.append(oh)
        sel = jnp.where(oh, 1.0, sel)
        xm = jnp.where(oh, neg, xm)

    ssel = sel * s_t
    denom = jnp.sum(ssel, axis=0, keepdims=True)
    comb = ssel / denom * ROUTE_SCALE

    rank_t = _dot(sel.astype(BF16), upper_ref[...])
    cnt = jnp.sum(sel, axis=1, keepdims=True)
    cnt_i = cnt.astype(I32)
    seg16 = ((cnt_i + (SEG - 1)) // SEG).astype(F32)
    off16 = _dot(lstrict_ref[...], jnp.broadcast_to(seg16, (N_EXPERTS, 128)).astype(BF16))
    d_t = rank_t + off16[:, 0:1] * float(SEG)
    for k in range(TOP_K):
        dest_ref[0, k:k + 1, :] = jnp.sum(jnp.where(onehots[k], d_t, 0.0), axis=0, keepdims=True).astype(I32)
        wsel_ref[0, k:k + 1, :] = jnp.sum(jnp.where(onehots[k], comb, 0.0), axis=0, keepdims=True)
    cnt_ref[0] = jnp.broadcast_to(cnt_i, (N_EXPERTS, 128))


def _start_granule_copies(b, j0, j1, gran_g, gmax, make_copy):
    def per_granule(j, c):
        make_copy(pl.multiple_of(j * SEG, SEG), pl.multiple_of(gran_g[b * gmax + j], SEG)).start()
        return c

    lax.fori_loop(j0, j1, per_granule, 0)


def _wait_granule_copies(count, max_count, make_wait_copy):
    for k in range(int(max_count).bit_length()):
        @pl.when(((count >> k) & 1) == 1)
        def _():
            make_wait_copy(SEG << k).wait()


def _dispatch_body(gran_g, n_gran, nch, tail_g, tail_n, dest_ref, wsel_ref, h_ref, tokid_ref, ones_ref,
                   xs_hbm, src_ref, wrow_ref, xloc_ref, zero_ref, sems, tail_sem):
    b = pl.program_id(0)
    nb = pl.num_programs(0)
    gmax = gran_g.shape[0] // n_gran.shape[0]
    gpp = 2 * KC // SEG
    slot = b % 2
    n_b = n_gran[b]

    def seg_copy(sl):
        return lambda l, g: pltpu.make_async_copy(
            xloc_ref.at[sl, pl.ds(l, SEG), :], xs_hbm.at[pl.ds(g, SEG), :], sems.at[sl])

    def wait_block(count, sl):
        _wait_granule_copies(count, gmax, lambda rows: pltpu.make_async_copy(
            xloc_ref.at[sl, pl.ds(0, rows), :], xs_hbm.at[pl.ds(0, rows), :], sems.at[sl]))

    @pl.when(b >= 2)
    def _():
        wait_block(n_gran[jnp.maximum(b - 2, 0)], slot)

    dest = dest_ref[0]
    wsel = wsel_ref[0]
    h = h_ref[...]
    src_ref[...] = jnp.zeros(src_ref.shape, F32)
    wrow_ref[...] = jnp.zeros(wrow_ref.shape, F32)

    row_in_chunk = lax.broadcasted_iota(I32, (KC, TB), 0).astype(F32).astype(BF16)
    wsel_b = wsel.astype(BF16)
    one = jnp.ones((1, TB), BF16)

    def chunk(c):
        r0 = pl.multiple_of(c * KC, KC)
        rel = (dest - r0).astype(F32).astype(BF16)
        sb = jnp.zeros((KC, TB), BF16)
        wm = jnp.zeros((KC, TB), BF16)
        for k in range(TOP_K):
            hit = rel[k:k + 1, :] == row_in_chunk
            sb = jnp.where(hit, one, sb)
            wm = jnp.where(hit, wsel_b[k:k + 1, :], wm)
        xloc_ref[slot, pl.ds(r0, KC), :] = _dot(sb, h).astype(BF16)
        src_ref[0, pl.ds(r0, KC), :] = _dot(sb, tokid_ref[...])
        wrow_ref[0, pl.ds(r0, KC), :] = _dot(wm, ones_ref[...])

    def chunk_pair(cp, carry):
        chunk(2 * cp)
        chunk(2 * cp + 1)
        _start_granule_copies(b, cp * gpp, jnp.minimum((cp + 1) * gpp, n_b), gran_g, gmax, seg_copy(slot))
        return carry

    lax.fori_loop(0, (nch[b] + 1) // 2, chunk_pair, 0)

    def tail_copy(g):
        return pltpu.make_async_copy(zero_ref, xs_hbm.at[pl.ds(g, SEG), :], tail_sem)

    def tails(wait):
        def per_expert(e, carry):
            def per_granule(j, c):
                cp = tail_copy(pl.multiple_of(tail_g[e] + j * SEG, SEG))
                if wait:
                    cp.wait()
                else:
                    cp.start()
                return c
            lax.fori_loop(0, tail_n[e], per_granule, 0)
            return carry
        lax.fori_loop(0, N_EXPERTS, per_expert, 0)

    @pl.when(b == nb - 1)
    def _():
        zero_ref[...] = jnp.zeros(zero_ref.shape, BF16)
        tails(False)

        @pl.when(b >= 1)
        def _():
            wait_block(n_gran[jnp.maximum(b - 1, 0)], 1 - slot)

        wait_block(n_b, slot)
        tails(True)


def _dispatch(meta, dest, wsel, h1b_all, n_rows_sorted, m_out_max):
    n, D = h1b_all.shape
    nb = n // TB
    tokid = jnp.asarray(np.broadcast_to(np.arange(TB, dtype=np.float32)[:, None], (TB, 128)), BF16)
    ones = jnp.ones((TB, 128), BF16)
    cs = lambda shape: pl.BlockSpec(shape, lambda i, *_: (0,) * len(shape), pipeline_mode=pl.Buffered(1))
    grid_spec = pltpu.PrefetchScalarGridSpec(
        num_scalar_prefetch=5,
        grid=(nb,),
        in_specs=[
            pl.BlockSpec((1, TOP_K, TB), lambda i, *_: (i, 0, 0)),
            pl.BlockSpec((1, TOP_K, TB), lambda i, *_: (i, 0, 0)),
            pl.BlockSpec((TB, D), lambda i, *_: (i, 0)),
            cs(tokid.shape), cs(ones.shape),
        ],
        out_specs=[
            pl.BlockSpec(memory_space=pl.ANY),
            pl.BlockSpec((1, m_out_max, 128), lambda i, *_: (i, 0, 0)),
            pl.BlockSpec((1, m_out_max, 128), lambda i, *_: (i, 0, 0)),
        ],
        scratch_shapes=[
            pltpu.VMEM((2, m_out_max, D), BF16),
            pltpu.VMEM((SEG, D), BF16),
            pltpu.SemaphoreType.DMA((2,)),
            pltpu.SemaphoreType.DMA(()),
        ],
    )
    return pl.pallas_call(
        _dispatch_body,
        grid_spec=grid_spec,
        out_shape=[
            jax.ShapeDtypeStruct((n_rows_sorted, D), BF16),
            jax.ShapeDtypeStruct((nb, m_out_max, 128), F32),
            jax.ShapeDtypeStruct((nb, m_out_max, 128), F32),
        ],
        compiler_params=pltpu.CompilerParams(
            dimension_semantics=("arbitrary",), vmem_limit_bytes=VMEM_LIMIT, has_side_effects=True),
        name="dispatch",
    )(meta["gran_g"], meta["n_gran"], meta["nch"], meta["tail_g"], meta["tail_n"], dest, wsel,
      h1b_all, tokid, ones)


def _ffn_body(tile_e, tile_src, tile_rows, e_slot, e_next, x_ref, wg_hbm, wu_hbm, wd_hbm, y_ref,
              wg_f, wu_f, wd_f, wgu_b, wd_b, sems):
    i = pl.program_id(0)
    f = wg_f.shape[2]
    e = tile_e[i]
    e_prev = tile_e[jnp.maximum(i - 1, 0)]

    def weight_copies(expert, slot):
        return (pltpu.make_async_copy(wg_hbm.at[expert], wg_f.at[slot], sems.at[slot, 0]),
                pltpu.make_async_copy(wu_hbm.at[expert], wu_f.at[slot], sems.at[slot, 1]),
                pltpu.make_async_copy(wd_hbm.at[expert], wd_f.at[slot], sems.at[slot, 2]))

    @pl.when((i == 0) | (e != e_prev))
    def _():
        slot = e_slot[e]

        @pl.when(i == 0)
        def _():
            for cp in weight_copies(e, slot):
                cp.start()

        for cp in weight_copies(e, slot):
            cp.wait()
        nxt = e_next[e]

        @pl.when(nxt >= 0)
        def _():
            for cp in weight_copies(nxt, 1 - slot):
                cp.start(priority=1)

        wgu_b[:, :f] = wg_f[slot].astype(BF16)
        wgu_b[:, f:] = wu_f[slot].astype(BF16)
        wd_b[...] = wd_f[slot].astype(BF16)

    def swiglu(x):
        gu = _dot(x, wgu_b[...])
        act = (jax.nn.silu(gu[:, :f]) * gu[:, f:]).astype(BF16)
        return _dot(act, wd_b[...]).astype(BF16)

    @pl.when(tile_rows[i] == TR)
    def _():
        y_ref[...] = swiglu(x_ref[...])

    @pl.when(tile_rows[i] == TRC)
    def _():
        y_ref[0:TRC, :] = swiglu(x_ref[0:TRC, :])
        y_ref[TRC:, :] = jnp.zeros((TR - TRC, y_ref.shape[1]), BF16)


def _expert_ffn(meta, xs, w_gate, w_up, w_down):
    R, D = xs.shape
    E, _, f = w_gate.shape
    nt = R // TR
    grid_spec = pltpu.PrefetchScalarGridSpec(
        num_scalar_prefetch=5,
        grid=(nt,),
        in_specs=[
            pl.BlockSpec((TR, D), lambda i, te, ts, *_: (ts[i], 0)),
            pl.BlockSpec(memory_space=pl.ANY),
            pl.BlockSpec(memory_space=pl.ANY),
            pl.BlockSpec(memory_space=pl.ANY),
        ],
        out_specs=pl.BlockSpec((TR, D), lambda i, te, ts, *_: (ts[i], 0)),
        scratch_shapes=[
            pltpu.VMEM((2, D, f), F32),
            pltpu.VMEM((2, D, f), F32),
            pltpu.VMEM((2, f, D), F32),
            pltpu.VMEM((D, 2 * f), BF16),
            pltpu.VMEM((f, D), BF16),
            pltpu.SemaphoreType.DMA((2, 3)),
        ],
    )
    return pl.pallas_call(
        _ffn_body,
        grid_spec=grid_spec,
        out_shape=jax.ShapeDtypeStruct((R, D), BF16),
        compiler_params=pltpu.CompilerParams(
            dimension_semantics=("arbitrary",), vmem_limit_bytes=VMEM_LIMIT),
        name="expert_ffn",
    )(meta["tile_e"], meta["tile_src"], meta["tile_rows"], meta["e_slot"], meta["e_next"],
      xs, w_gate, w_up, w_down)


def _combine_body(alpha, n_pb, gran_g, n_gran, nch, src_ref, wrow_ref, h1_ref, h1b_ref, p_ref,
                  wsg_ref, wsu_ref, wsd_ref, g2_ref, b2_ref, wpg_ref, bpg_ref, wpe_ref, ys_hbm,
                  yp_ref, ysm_ref, yloc_ref, acc_ref, sems):
    b = pl.program_id(0)
    n_chunks = nch[b]
    n_b = n_gran[b]
    gmax = gran_g.shape[0] // n_gran.shape[0]
    gpc = KCC // SEG

    del n_b
    for j in range(gmax):
        pltpu.make_async_copy(ys_hbm.at[pl.ds(pl.multiple_of(gran_g[b * gmax + j], SEG), SEG), :],
                              yloc_ref.at[pl.ds(j * SEG, SEG), :], sems.at[j // gpc]).start()

    hb = h1b_ref[...]
    sh = (jax.nn.silu(_dot(hb, wsg_ref[...])) * _dot(hb, wsu_ref[...])).astype(BF16)
    acc_ref[...] = _dot(sh, wsd_ref[...])

    def wait_chunk(c):
        pltpu.make_async_copy(ys_hbm.at[pl.ds(0, KCC), :], yloc_ref.at[pl.ds(0, KCC), :], sems.at[c]).wait()

    def chunk(c, carry):
        wait_chunk(c)
        r0 = pl.multiple_of(c * KCC, KCC)
        src = src_ref[0, pl.ds(r0, KCC), :]
        w = wrow_ref[0, pl.ds(r0, KCC), :]
        lane = lax.broadcasted_iota(I32, src.shape, 1).astype(F32)
        wm = jnp.concatenate([jnp.where(src == lane + float(o), w, 0.0) for o in range(0, TB, src.shape[1])],
                             axis=1).astype(BF16)
        acc_ref[...] += lax.dot_general(wm, yloc_ref[pl.ds(r0, KCC), :],
                                        (((0,), (0,)), ((), ())), preferred_element_type=F32)
        return carry

    lax.fori_loop(0, n_chunks, chunk, 0)

    for c in range(gmax // gpc):
        @pl.when(c >= n_chunks)
        def _():
            wait_chunk(c)

    h2 = _ln_rows(alpha * h1_ref[...] + acc_ref[...], g2_ref[...], b2_ref[...])
    gate = jax.nn.sigmoid(_dot(h2.astype(BF16), wpg_ref[...]) + bpg_ref[...])
    pe = _dot(p_ref[...].astype(BF16), wpe_ref[...])
    y = h2 + gate * pe

    @pl.when(b < n_pb)
    def _():
        yp_ref[...] = y

    @pl.when(b >= n_pb)
    def _():
        ysm_ref[...] = y


def _combine(meta, src_rep, w_rep, h1_all, h1b_all, p_all, wsg_b, wsu_b, wsd_b, g2, b2, wpg_b, bpg, wpe_b, ys,
             m_out_max, n_prompt, alpha):
    n, D = h1_all.shape
    nb = n // TB
    n_pb = n_prompt // TB
    d_pe = p_all.shape[1]
    cs = lambda shape: pl.BlockSpec(shape, lambda i, *_: (0,) * len(shape), pipeline_mode=pl.Buffered(1))
    grid_spec = pltpu.PrefetchScalarGridSpec(
        num_scalar_prefetch=3,
        grid=(nb,),
        in_specs=[
            pl.BlockSpec((1, m_out_max, 128), lambda i, *_: (i, 0, 0)),
            pl.BlockSpec((1, m_out_max, 128), lambda i, *_: (i, 0, 0)),
            pl.BlockSpec((TB, D), lambda i, *_: (i, 0)),
            pl.BlockSpec((TB, D), lambda i, *_: (i, 0)),
            pl.BlockSpec((TB, d_pe), lambda i, *_: (i, 0)),
            cs(wsg_b.shape), cs(wsu_b.shape), cs(wsd_b.shape), cs(g2.shape), cs(b2.shape),
            cs(wpg_b.shape), cs(bpg.shape), cs(wpe_b.shape),
            pl.BlockSpec(memory_space=pl.ANY),
        ],
        out_specs=[
            pl.BlockSpec((TB, D), lambda i, *_: (jnp.minimum(i, n_pb - 1), 0)),
            pl.BlockSpec((TB, D), lambda i, *_: (jnp.maximum(i - n_pb, 0), 0)),
        ],
        scratch_shapes=[
            pltpu.VMEM((m_out_max, D), BF16),
            pltpu.VMEM((TB, D), F32),
            pltpu.SemaphoreType.DMA((m_out_max // KCC,)),
        ],
    )
    return pl.pallas_call(
        functools.partial(_combine_body, alpha, n_pb),
        grid_spec=grid_spec,
        out_shape=[jax.ShapeDtypeStruct((n_prompt, D), F32), jax.ShapeDtypeStruct((n - n_prompt, D), F32)],
        compiler_params=pltpu.CompilerParams(
            dimension_semantics=("arbitrary",), vmem_limit_bytes=VMEM_LIMIT),
        name="combine",
    )(meta["gran_g"], meta["n_gran"], meta["nchc"], src_rep, w_rep, h1_all, h1b_all, p_all,
      wsg_b, wsu_b, wsd_b, g2, b2, wpg_b, bpg, wpe_b, ys)


def _sort_meta(cnt, n_tiles_max, m_out_max):
    segp = (cnt + (SEG - 1)) // SEG * SEG
    loc = jnp.cumsum(segp, axis=1) - segp
    before = jnp.cumsum(segp, axis=0) - segp
    total = jnp.sum(segp, axis=0)
    total_c = (total + (TRC - 1)) // TRC * TRC
    total_p = (total + (TR - 1)) // TR * TR
    ends = jnp.cumsum(total_p)
    base = ends - total_p
    glob = base[None, :] + before
    n_valid = ends[-1] // TR
    tiles = jnp.arange(n_tiles_max, dtype=I32)
    tile_src = jnp.minimum(tiles, n_valid - 1)
    tile_e = jnp.minimum(jnp.sum((ends[None, :] <= (tile_src * TR)[:, None]).astype(I32), axis=1), N_EXPERTS - 1)
    left = jnp.sum(jnp.where(tile_e[:, None] == jnp.arange(N_EXPERTS, dtype=I32)[None, :],
                             (base + total_c)[None, :], 0), axis=1) - tile_src * TR
    tile_rows = jnp.where(tiles < n_valid, jnp.clip(left, 0, TR), 0)
    experts = jnp.arange(N_EXPERTS, dtype=I32)
    active = total_p > 0
    later = active[None, :] & (experts[None, :] > experts[:, None])
    e_next = jnp.min(jnp.where(later, experts[None, :], N_EXPERTS), axis=1)
    gmax = m_out_max // SEG
    n16 = segp // SEG
    g_end = jnp.cumsum(n16, axis=1)
    g_start = g_end - n16
    j = jnp.arange(gmax, dtype=I32)[None, :, None]
    in_seg = (g_start[:, None, :] <= j) & (j < g_end[:, None, :])
    gran_g = jnp.sum(jnp.where(in_seg, (glob - loc)[:, None, :] + j * SEG, 0), axis=2)
    return {
        "gran_g": gran_g.reshape(-1).astype(I32),
        "n_gran": g_end[:, -1].astype(I32),
        "nch": ((jnp.sum(segp, axis=1) + (KC - 1)) // KC).astype(I32),
        "nchc": ((jnp.sum(segp, axis=1) + (KCC - 1)) // KCC).astype(I32),
        "tail_g": (base + total).astype(I32),
        "tail_n": ((total_c - total) // SEG).astype(I32),
        "tile_e": tile_e.astype(I32),
        "tile_src": tile_src.astype(I32),
        "tile_rows": tile_rows.astype(I32),
        "e_slot": ((jnp.cumsum(active.astype(I32)) - 1) % 2).astype(I32),
        "e_next": jnp.where(e_next < N_EXPERTS, e_next, -1).astype(I32),
    }


def _layer(xp, xs, st, pp, ps, w_in, w_pool, pool_scale, sgu_ln_g, sgu_ln_b, w_s, b_s, w_out, ln1_g, ln1_b,
           router_w, router_bias, w_gate, w_up, w_down, ws_gate, ws_up, ws_down, ln2_g, ln2_b, w_pe, w_pgate,
           b_pgate, alpha):
    B, T, D = xp.shape
    NS, TS, _ = xs.shape
    n_prompt, n_sample = B * T, NS * TS
    n = n_prompt + n_sample
    n_heads = w_s.shape[0]
    row = lambda v: v.reshape(1, -1)

    x_t = jnp.transpose(xs, (1, 0, 2))
    st_t = jnp.transpose(st, (1, 0, 2))
    bsb = jnp.broadcast_to(b_s[:, :, None], (n_heads, CHUNK, SGU_HEAD_DIM))
    wrow = jnp.repeat(jnp.transpose(w_s[:, :TS, :TS], (1, 2, 0)), SGU_HEAD_DIM, axis=2)
    brow = jnp.repeat(jnp.transpose(b_s[:, :TS], (1, 0)), SGU_HEAD_DIM, axis=1)
    h1_all, h1b_all, pool_p, a_s, vn_s, dest, wsel, cnt = _mixer(
        xp, x_t, st_t, w_in.astype(BF16), w_pool.astype(BF16), row(pool_scale), sgu_ln_g, sgu_ln_b, w_s, bsb,
        wrow, brow, w_out.astype(BF16), row(ln1_g), row(ln1_b), jnp.transpose(router_w),
        router_bias.reshape(-1, 1), alpha)
    nb = n // TB
    m_out_max = TB * TOP_K + N_EXPERTS * SEG
    rows_max = n * TOP_K + nb * N_EXPERTS * (SEG - 1) + N_EXPERTS * (TR - SEG)
    n_tiles_max = -(-rows_max // TR)
    meta = _sort_meta(cnt[:, :, 0], n_tiles_max, m_out_max)
    xsort, src_rep, w_rep = _dispatch(meta, dest, wsel, h1b_all, n_tiles_max * TR, m_out_max)
    ysort = _expert_ffn(meta, xsort, w_gate, w_up, w_down)

    nbs = TB // TS
    ps_t = jnp.transpose(ps.reshape(NS // nbs, nbs, TS, -1), (0, 2, 1, 3)).reshape(n_sample, -1)
    p_all = jnp.concatenate([pp.reshape(n_prompt, -1), ps_t], axis=0)
    y_p, y_s = _combine(meta, src_rep, w_rep, h1_all, h1b_all, p_all, ws_gate.astype(BF16), ws_up.astype(BF16),
                        ws_down.astype(BF16), row(ln2_g), row(ln2_b), w_pgate.astype(BF16), row(b_pgate),
                        w_pe.astype(BF16), ysort, m_out_max, n_prompt, alpha)

    yp = y_p.reshape(B, T, D)
    ys = jnp.transpose(y_s.reshape(NS // nbs, TS, nbs, D), (0, 2, 1, 3)).reshape(NS, TS, D)
    new_pool_p = pool_p[:, HALO - POOL_BUF:, :]
    new_pool_s = jnp.concatenate([st, jnp.transpose(a_s, (1, 0, 2))], axis=1)[:, -POOL_BUF:]
    vn = jnp.transpose(vn_s, (1, 0, 2)).reshape(NS, TS, n_heads, SGU_HEAD_DIM)
    return yp, ys, new_pool_p, new_pool_s, vn


def kernel(x_prompt, x_sample, state_pool, p_prompt, p_sample, w_in, w_pool, pool_scale, sgu_ln_g, sgu_ln_b, w_s, b_s, w_out, ln1_g, ln1_b, router_w, router_bias, w_gate, w_up, w_down, ws_gate, ws_up, ws_down, ln2_g, ln2_b, w_pe, w_pgate, b_pgate):
    depth = w_in.shape[0]
    alpha = (2.0 * depth) ** 0.25
    hp, hs = x_prompt, x_sample
    pool_p, pool_s, v_s = [], [], []
    for i in range(depth):
        hp, hs, bp, bs, vs = _layer(
            hp, hs, state_pool[i], p_prompt[i], p_sample[i], w_in[i], w_pool[i], pool_scale[i], sgu_ln_g[i],
            sgu_ln_b[i], w_s[i], b_s[i], w_out[i], ln1_g[i], ln1_b[i], router_w[i], router_bias[i], w_gate[i],
            w_up[i], w_down[i], ws_gate[i], ws_up[i], ws_down[i], ln2_g[i], ln2_b[i], w_pe[i], w_pgate[i],
            b_pgate[i], alpha)
        pool_p.append(bp)
        pool_s.append(bs)
        v_s.append(vs)
    return hp, hs, jnp.stack(pool_p), jnp.stack(pool_s), jnp.stack(v_s)
```

```python
import functools

import jax
import jax.numpy as jnp
import numpy as np
from jax import lax
from jax.experimental import pallas as pl
from jax.experimental.pallas import tpu as pltpu

F32 = jnp.float32
BF16 = jnp.bfloat16
I32 = jnp.int32

POOL_WINDOWS = (2, 4, 8, 16)
POOL_BUF = max(POOL_WINDOWS) - 1
CHUNK = 128
SGU_HEAD_DIM = 128
N_EXPERTS = 64
TOP_K = 8
N_EXPERT_GROUPS = 8
TOPK_GROUPS = 4
ROUTE_SCALE = 2.5
LN_EPS = 1e-5
PAST_LEN = 16384

TB = 256
HALO = 16
TR = 512
TRC = 256
SEG = 8
KC = 256
KCC = 512
VMEM_LIMIT = 56 * 1024 * 1024


def _const_spec(shape):
    n = len(shape)
    return pl.BlockSpec(shape, lambda *_: (0,) * n, pipeline_mode=pl.Buffered(1))


def _ln_rows(x, g, b):
    mu = jnp.mean(x, axis=-1, keepdims=True)
    xc = x - mu
    var = jnp.mean(xc * xc, axis=-1, keepdims=True)
    return xc * lax.rsqrt(var + LN_EPS) * g + b


def _dot(a, b):
    return jnp.dot(a, b, preferred_element_type=F32)


def _prompt_tile(j, n_tiles, x, proj, band_ref, wpool_ref, pscale_ref, sg_ref, sb_ref, ws_ref, bsb_ref,
                 pool_ref, zcat_ref, mixin_ref):
    d_pool = zcat_ref.shape[1]
    d_pool_g = d_pool // len(POOL_WINDOWS)
    d_sgu = mixin_ref.shape[1] - d_pool

    a = proj[:, :d_pool]
    zcat_ref[HALO:, :] = a.astype(BF16)

    pos1 = (j * TB + lax.broadcasted_iota(I32, (TB, 1), 0) + 1).astype(F32)
    for g, w in enumerate(POOL_WINDOWS):
        c0, c1 = g * d_pool_g, (g + 1) * d_pool_g
        s = _dot(band_ref[g], zcat_ref[:, c0:c1])
        cnt = jnp.minimum(pos1, float(w))
        d = s / cnt - a[:, c0:c1]
        y = _dot(d.astype(BF16), wpool_ref[g])
        mixin_ref[:, c0:c1] = (y * pscale_ref[:, c0:c1]).astype(BF16)
    zcat_ref[0:HALO, :] = jnp.where(j == n_tiles - 1, jnp.zeros((HALO, d_pool), BF16), zcat_ref[TB:TB + HALO, :])
    pool_ref[0] = a[TB - HALO:, :]

    u = jax.nn.gelu(proj[:, d_pool:d_pool + d_sgu])
    v = jax.nn.gelu(proj[:, d_pool + d_sgu:])
    r = lax.broadcasted_iota(I32, (CHUNK, CHUNK), 0)
    c = lax.broadcasted_iota(I32, (CHUNK, CHUNK), 1)
    tril = r >= c
    for h in range(d_sgu // SGU_HEAD_DIM):
        l0, l1 = h * SGU_HEAD_DIM, (h + 1) * SGU_HEAD_DIM
        vn = _ln_rows(v[:, l0:l1], sg_ref[h:h + 1, :], sb_ref[h:h + 1, :]).astype(BF16)
        wst = jnp.where(tril, ws_ref[h], 0.0).astype(BF16)
        n_ch = TB // CHUNK
        mixed_all = _dot(wst, jnp.concatenate([vn[ci * CHUNK:(ci + 1) * CHUNK] for ci in range(n_ch)], axis=1))
        for ci in range(n_ch):
            r0, r1 = ci * CHUNK, (ci + 1) * CHUNK
            mixed = mixed_all[:, ci * SGU_HEAD_DIM:(ci + 1) * SGU_HEAD_DIM] + bsb_ref[h]
            mixin_ref[r0:r1, d_pool + l0:d_pool + l1] = (u[r0:r1, l0:l1] * mixed).astype(BF16)


def _sample_tile(T, NB, proj, st_ref, wpool_ref, pscale_ref, sg_ref, sb_ref, wrow_ref, brow_ref,
                 a_ref, vn_ref, dbuf_ref, mixin_ref):
    d_pool = dbuf_ref.shape[1]
    d_pool_g = d_pool // len(POOL_WINDOWS)
    d_sgu = mixin_ref.shape[1] - d_pool

    a = proj[:, :d_pool]
    for t in range(T):
        a_ref[t] = a[t * NB:(t + 1) * NB]

    for g, w in enumerate(POOL_WINDOWS):
        c0, c1 = g * d_pool_g, (g + 1) * d_pool_g
        for t in range(T):
            cnt = float(min(PAST_LEN + t + 1, w))
            acc = None
            for s in range(POOL_BUF + t - w + 1, POOL_BUF + t + 1):
                if s < POOL_BUF:
                    term = st_ref[s, :, c0:c1]
                else:
                    term = a[(s - POOL_BUF) * NB:(s - POOL_BUF + 1) * NB, c0:c1]
                acc = term if acc is None else acc + term
            d = acc / cnt - a[t * NB:(t + 1) * NB, c0:c1]
            dbuf_ref[t * NB:(t + 1) * NB, c0:c1] = d.astype(BF16)
        y = _dot(dbuf_ref[0:T * NB, c0:c1], wpool_ref[g])
        mixin_ref[:, c0:c1] = (y * pscale_ref[:, c0:c1]).astype(BF16)

    u = jax.nn.gelu(proj[:, d_pool:d_pool + d_sgu])
    v = jax.nn.gelu(proj[:, d_pool + d_sgu:])
    for h in range(d_sgu // SGU_HEAD_DIM):
        l0, l1 = h * SGU_HEAD_DIM, (h + 1) * SGU_HEAD_DIM
        vn = _ln_rows(v[:, l0:l1], sg_ref[h:h + 1, :], sb_ref[h:h + 1, :])
        for t in range(T):
            vn_ref[t, :, l0:l1] = vn[t * NB:(t + 1) * NB]
    for t in range(T):
        mixed = brow_ref[t:t + 1, :]
        for s in range(t + 1):
            mixed = mixed + wrow_ref[t, s:s + 1, :] * vn_ref[s]
        mixin_ref[t * NB:(t + 1) * NB, d_pool:] = (u[t * NB:(t + 1) * NB] * mixed).astype(BF16)


def _mixer_body(alpha, n_pt, tiles_per_seq, xp_ref, xs_ref, st_ref, win_ref, band_ref, wpool_ref, pscale_ref,
                sg_ref, sb_ref, ws_ref, bsb_ref, wrow_ref, brow_ref, wout_ref, g1_ref, b1_ref,
                rwh_ref, rwl_ref, rb_ref, upper_ref, lstrict_ref,
                h1_ref, h1b_ref, pool_ref, a_ref, vn_ref, dest_ref, wsel_ref, cnt_ref,
                zcat_ref, mixin_ref, hprev_ref):
    i = pl.program_id(0)
    n_tiles = pl.num_programs(0) - 1

    def route_previous_tile():
        h = hprev_ref[...]
        _route_tile(h, h.astype(BF16), rwh_ref, rwl_ref, rb_ref, upper_ref, lstrict_ref,
                    dest_ref, wsel_ref, cnt_ref)

    @pl.when(i == 0)
    def _():
        zcat_ref[0:HALO, :] = jnp.zeros((HALO, zcat_ref.shape[1]), BF16)
        hprev_ref[...] = jnp.zeros(hprev_ref.shape, F32)

    @pl.when(i < n_pt)
    def _():
        x = xp_ref[0]
        proj = _dot(x.astype(BF16), win_ref[...])
        route_previous_tile()
        _prompt_tile(i % tiles_per_seq, tiles_per_seq, x, proj, band_ref, wpool_ref, pscale_ref, sg_ref, sb_ref,
                     ws_ref, bsb_ref, pool_ref, zcat_ref, mixin_ref)
        h1_ref[...] = alpha * x

    @pl.when(i >= n_pt)
    def _():
        route_previous_tile()

    @pl.when((i >= n_pt) & (i < n_tiles))
    def _():
        T, NB, D = xs_ref.shape
        x = xs_ref[...].reshape(T * NB, D)
        proj = _dot(x.astype(BF16), win_ref[...])
        _sample_tile(T, NB, proj, st_ref, wpool_ref, pscale_ref, sg_ref, sb_ref, wrow_ref, brow_ref,
                     a_ref, vn_ref, zcat_ref, mixin_ref)
        h1_ref[...] = alpha * x

    @pl.when(i < n_tiles)
    def _():
        mix = _dot(mixin_ref[...], wout_ref[...])
        h1 = _ln_rows(h1_ref[...] + mix, g1_ref[...], b1_ref[...])
        h1_ref[...] = h1
        h1b_ref[...] = h1.astype(BF16)
        hprev_ref[...] = h1


def _pool_band(tm):
    t = np.arange(tm)[:, None] + HALO
    s = np.arange(tm + HALO)[None, :]
    return jnp.asarray(np.stack([((s <= t) & (s >= t - w + 1)) for w in POOL_WINDOWS]).astype(np.float32), BF16)


def _mixer(xp, xs_t, st_t, win_b, wpool_b, pscale, sg, sb, ws, bsb, wrow, brow, wout_b, g1, b1, rwt, rb_col, alpha):
    B, T, D = xp.shape
    TS, NS, _ = xs_t.shape
    d_pool = st_t.shape[2]
    d_sgu = wrow.shape[2]
    tps = T // TB
    n_pt = B * tps
    NB = TB // TS
    n_st = NS // NB
    n = B * T + NS * TS
    nblk = n_pt + n_st
    pt = lambda i: jnp.minimum(i, n_pt - 1)
    stile = lambda i: jnp.clip(i - n_pt, 0, n_st - 1)
    blk = lambda i: jnp.minimum(i, nblk - 1)
    routed = lambda i: jnp.maximum(i - 1, 0)
    body = functools.partial(_mixer_body, alpha, n_pt, tps)
    rw_hi = rwt.astype(BF16)
    rw_lo = (rwt - rw_hi.astype(F32)).astype(BF16)
    upper = jnp.asarray(np.triu(np.ones((TB, TB), np.float32), 1), BF16)
    lstrict = jnp.asarray(np.tril(np.ones((N_EXPERTS, N_EXPERTS), np.float32), -1), BF16)
    consts = (win_b, _pool_band(TB), wpool_b, pscale, sg, sb, ws, bsb, wrow, brow, wout_b, g1, b1,
              rw_hi, rw_lo, rb_col, upper, lstrict)
    return pl.pallas_call(
        body,
        grid=(nblk + 1,),
        in_specs=[
            pl.BlockSpec((1, TB, D), lambda i: (pt(i) // tps, pt(i) % tps, 0)),
            pl.BlockSpec((TS, NB, D), lambda i: (0, stile(i), 0), pipeline_mode=pl.Buffered(1)),
            pl.BlockSpec((POOL_BUF, NB, d_pool), lambda i: (0, stile(i), 0), pipeline_mode=pl.Buffered(1)),
        ] + [_const_spec(c.shape) for c in consts],
        out_specs=[
            pl.BlockSpec((TB, D), lambda i: (blk(i), 0)),
            pl.BlockSpec((TB, D), lambda i: (blk(i), 0)),
            pl.BlockSpec((1, HALO, d_pool), lambda i: (pt(i) // tps, 0, 0)),
            pl.BlockSpec((TS, NB, d_pool), lambda i: (0, stile(i), 0)),
            pl.BlockSpec((TS, NB, d_sgu), lambda i: (0, stile(i), 0)),
            pl.BlockSpec((1, TOP_K, TB), lambda i: (routed(i), 0, 0)),
            pl.BlockSpec((1, TOP_K, TB), lambda i: (routed(i), 0, 0)),
            pl.BlockSpec((1, N_EXPERTS, 128), lambda i: (routed(i), 0, 0)),
        ],
        out_shape=[
            jax.ShapeDtypeStruct((n, D), F32),
            jax.ShapeDtypeStruct((n, D), BF16),
            jax.ShapeDtypeStruct((B, HALO, d_pool), F32),
            jax.ShapeDtypeStruct((TS, NS, d_pool), F32),
            jax.ShapeDtypeStruct((TS, NS, d_sgu), F32),
            jax.ShapeDtypeStruct((nblk, TOP_K, TB), I32),
            jax.ShapeDtypeStruct((nblk, TOP_K, TB), F32),
            jax.ShapeDtypeStruct((nblk, N_EXPERTS, 128), I32),
        ],
        scratch_shapes=[
            pltpu.VMEM((TB + HALO, d_pool), BF16),
            pltpu.VMEM((TB, D), BF16),
            pltpu.VMEM((TB, D), F32),
        ],
        compiler_params=pltpu.CompilerParams(
            dimension_semantics=("arbitrary",), vmem_limit_bytes=VMEM_LIMIT),
        name="mixer",
    )(xp, xs_t, st_t, *consts)


def _route_tile(h1, h1b, rwh_ref, rwl_ref, rb_ref, upper_ref, lstrict_ref, dest_ref, wsel_ref, cnt_ref):
    per_group = N_EXPERTS // N_EXPERT_GROUPS
    neg = -jnp.inf
    nt = (((1,), (1,)), ((), ()))
    h_lo = (h1 - h1b.astype(F32)).astype(BF16)
    rwh = rwh_ref[...]
    logits_t = (lax.dot_general(rwh, h1b, nt, preferred_element_type=F32)
                + lax.dot_general(rwh, h_lo, nt, preferred_element_type=F32)
                + lax.dot_general(rwl_ref[...], h1b, nt, preferred_element_type=F32))
    s_t = jax.nn.sigmoid(logits_t)
    b_t = s_t + rb_ref[...]

    io_g = lax.broadcasted_iota(I32, (per_group, TB), 0)
    gs = []
    for g in range(N_EXPERT_GROUPS):
        xg = b_t[g * per_group:(g + 1) * per_group, :]
        m1 = jnp.max(xg, axis=0, keepdims=True)
        i1 = jnp.min(jnp.where(xg == m1, io_g, per_group), axis=0, keepdims=True)
        m2 = jnp.max(jnp.where(io_g == i1, neg, xg), axis=0, keepdims=True)
        gs.append(m1 + m2)
    masked = []
    for g in range(N_EXPERT_GROUPS):
        rank = jnp.zeros((1, TB), F32)
        for g2 in range(N_EXPERT_GROUPS):
            if g2 != g:
                ahead = (gs[g2] >= gs[g]) if g2 < g else (gs[g2] > gs[g])
                rank = rank + jnp.where(ahead, 1.0, 0.0)
        keep = rank < float(TOPK_GROUPS)
        masked.append(jnp.where(keep, b_t[g * per_group:(g + 1) * per_group, :], neg))
    xm = jnp.concatenate(masked, axis=0)

    io_e = lax.broadcasted_iota(I32, (N_EXPERTS, TB), 0)
    onehots = []
    sel = jnp.zeros((N_EXPERTS, TB), F32)
    for _ in range(TOP_K):
        m = jnp.max(xm, axis=0, keepdims=True)
        idx = jnp.min(jnp.where(xm == m, io_e, N_EXPERTS), axis=0, keepdims=True)
        oh = io_e == idx
        onehots.append(oh)
        sel = jnp.where(oh, 1.0, sel)
        xm = jnp.where(oh, neg, xm)

    ssel = sel * s_t
    denom = jnp.sum(ssel, axis=0, keepdims=True)
    comb = ssel / denom * ROUTE_SCALE

    rank_t = _dot(sel.astype(BF16), upper_ref[...])
    cnt = jnp.sum(sel, axis=1, keepdims=True)
    cnt_i = cnt.astype(I32)
    seg16 = ((cnt_i + (SEG - 1)) // SEG).astype(F32)
    off16 = _dot(lstrict_ref[...], jnp.broadcast_to(seg16, (N_EXPERTS, 128)).astype(BF16))
    d_t = rank_t + off16[:, 0:1] * float(SEG)
    for k in range(TOP_K):
        dest_ref[0, k:k + 1, :] = jnp.sum(jnp.where(onehots[k], d_t, 0.0), axis=0, keepdims=True).astype(I32)
        wsel_ref[0, k:k + 1, :] = jnp.sum(jnp.where(onehots[k], comb, 0.0), axis=0, keepdims=True)
    cnt_ref[0] = jnp.broadcast_to(cnt_i, (N_EXPERTS, 128))


def _start_granule_copies(b, j0, j1, gran_g, gmax, make_copy):
    def per_granule(j, c):
        make_copy(pl.multiple_of(j * SEG, SEG), pl.multiple_of(gran_g[b * gmax + j], SEG)).start()
        return c

    lax.fori_loop(j0, j1, per_granule, 0)


def _wait_granule_copies(count, max_count, make_wait_copy):
    for k in range(int(max_count).bit_length()):
        @pl.when(((count >> k) & 1) == 1)
        def _():
            make_wait_copy(SEG << k).wait()


def _dispatch_body(gran_g, n_gran, nch, tail_g, tail_n, dest_ref, wsel_ref, h_ref, tokid_ref, ones_ref,
                   xs_hbm, src_ref, wrow_ref, xloc_ref, zero_ref, sems, tail_sem):
    b = pl.program_id(0)
    nb = pl.num_programs(0)
    gmax = gran_g.shape[0] // n_gran.shape[0]
    gpp = 2 * KC // SEG
    slot = b % 2
    n_b = n_gran[b]

    def seg_copy(sl):
        return lambda l, g: pltpu.make_async_copy(
            xloc_ref.at[sl, pl.ds(l, SEG), :], xs_hbm.at[pl.ds(g, SEG), :], sems.at[sl])

    def wait_block(count, sl):
        _wait_granule_copies(count, gmax, lambda rows: pltpu.make_async_copy(
            xloc_ref.at[sl, pl.ds(0, rows), :], xs_hbm.at[pl.ds(0, rows), :], sems.at[sl]))

    @pl.when(b >= 2)
    def _():
        wait_block(n_gran[jnp.maximum(b - 2, 0)], slot)

    dest = dest_ref[0]
    wsel = wsel_ref[0]
    h = h_ref[...]
    src_ref[...] = jnp.zeros(src_ref.shape, F32)
    wrow_ref[...] = jnp.zeros(wrow_ref.shape, F32)

    row_in_chunk = lax.broadcasted_iota(I32, (KC, TB), 0).astype(F32).astype(BF16)
    wsel_b = wsel.astype(BF16)
    one = jnp.ones((1, TB), BF16)

    def chunk(c):
        r0 = pl.multiple_of(c * KC, KC)
        rel = (dest - r0).astype(F32).astype(BF16)
        sb = jnp.zeros((KC, TB), BF16)
        wm = jnp.zeros((KC, TB), BF16)
        for k in range(TOP_K):
            hit = rel[k:k + 1, :] == row_in_chunk
            sb = jnp.where(hit, one, sb)
            wm = jnp.where(hit, wsel_b[k:k + 1, :], wm)
        xloc_ref[slot, pl.ds(r0, KC), :] = _dot(sb, h).astype(BF16)
        src_ref[0, pl.ds(r0, KC), :] = _dot(sb, tokid_ref[...])
        wrow_ref[0, pl.ds(r0, KC), :] = _dot(wm, ones_ref[...])

    def chunk_pair(cp, carry):
        chunk(2 * cp)
        chunk(2 * cp + 1)
        _start_granule_copies(b, cp * gpp, jnp.minimum((cp + 1) * gpp, n_b), gran_g, gmax, seg_copy(slot))
        return carry

    lax.fori_loop(0, (nch[b] + 1) // 2, chunk_pair, 0)

    def tail_copy(g):
        return pltpu.make_async_copy(zero_ref, xs_hbm.at[pl.ds(g, SEG), :], tail_sem)

    def tails(wait):
        def per_expert(e, carry):
            def per_granule(j, c):
                cp = tail_copy(pl.multiple_of(tail_g[e] + j * SEG, SEG))
                if wait:
                    cp.wait()
                else:
                    cp.start()
                return c
            lax.fori_loop(0, tail_n[e], per_granule, 0)
            return carry
        lax.fori_loop(0, N_EXPERTS, per_expert, 0)

    @pl.when(b == nb - 1)
    def _():
        zero_ref[...] = jnp.zeros(zero_ref.shape, BF16)
        tails(False)

        @pl.when(b >= 1)
        def _():
            wait_block(n_gran[jnp.maximum(b - 1, 0)], 1 - slot)

        wait_block(n_b, slot)
        tails(True)


def _dispatch(meta, dest, wsel, h1b_all, n_rows_sorted, m_out_max):
    n, D = h1b_all.shape
    nb = n // TB
    tokid = jnp.asarray(np.broadcast_to(np.arange(TB, dtype=np.float32)[:, None], (TB, 128)), BF16)
    ones = jnp.ones((TB, 128), BF16)
    cs = lambda shape: pl.BlockSpec(shape, lambda i, *_: (0,) * len(shape), pipeline_mode=pl.Buffered(1))
    grid_spec = pltpu.PrefetchScalarGridSpec(
        num_scalar_prefetch=5,
        grid=(nb,),
        in_specs=[
            pl.BlockSpec((1, TOP_K, TB), lambda i, *_: (i, 0, 0)),
            pl.BlockSpec((1, TOP_K, TB), lambda i, *_: (i, 0, 0)),
            pl.BlockSpec((TB, D), lambda i, *_: (i, 0)),
            cs(tokid.shape), cs(ones.shape),
        ],
        out_specs=[
            pl.BlockSpec(memory_space=pl.ANY),
            pl.BlockSpec((1, m_out_max, 128), lambda i, *_: (i, 0, 0)),
            pl.BlockSpec((1, m_out_max, 128), lambda i, *_: (i, 0, 0)),
        ],
        scratch_shapes=[
            pltpu.VMEM((2, m_out_max, D), BF16),
            pltpu.VMEM((SEG, D), BF16),
            pltpu.SemaphoreType.DMA((2,)),
            pltpu.SemaphoreType.DMA(()),
        ],
    )
    return pl.pallas_call(
        _dispatch_body,
        grid_spec=grid_spec,
        out_shape=[
            jax.ShapeDtypeStruct((n_rows_sorted, D), BF16),
            jax.ShapeDtypeStruct((nb, m_out_max, 128), F32),
            jax.ShapeDtypeStruct((nb, m_out_max, 128), F32),
        ],
        compiler_params=pltpu.CompilerParams(
            dimension_semantics=("arbitrary",), vmem_limit_bytes=VMEM_LIMIT, has_side_effects=True),
        name="dispatch",
    )(meta["gran_g"], meta["n_gran"], meta["nch"], meta["tail_g"], meta["tail_n"], dest, wsel,
      h1b_all, tokid, ones)


def _ffn_body(tile_e, tile_src, tile_rows, e_slot, e_next, x_ref, wg_hbm, wu_hbm, wd_hbm, y_ref,
              wg_f, wu_f, wd_f, wgu_b, wd_b, sems):
    i = pl.program_id(0)
    f = wg_f.shape[2]
    e = tile_e[i]
    e_prev = tile_e[jnp.maximum(i - 1, 0)]

    def weight_copies(expert, slot):
        return (pltpu.make_async_copy(wg_hbm.at[expert], wg_f.at[slot], sems.at[slot, 0]),
                pltpu.make_async_copy(wu_hbm.at[expert], wu_f.at[slot], sems.at[slot, 1]),
                pltpu.make_async_copy(wd_hbm.at[expert], wd_f.at[slot], sems.at[slot, 2]))

    @pl.when((i == 0) | (e != e_prev))
    def _():
        slot = e_slot[e]

        @pl.when(i == 0)
        def _():
            for cp in weight_copies(e, slot):
                cp.start()

        for cp in weight_copies(e, slot):
            cp.wait()
        nxt = e_next[e]

        @pl.when(nxt >= 0)
        def _():
            for cp in weight_copies(nxt, 1 - slot):
                cp.start(priority=1)

        wgu_b[:, :f] = wg_f[slot].astype(BF16)
        wgu_b[:, f:] = wu_f[slot].astype(BF16)
        wd_b[...] = wd_f[slot].astype(BF16)

    def swiglu(x):
        gu = _dot(x, wgu_b[...])
        act = (jax.nn.silu(gu[:, :f]) * gu[:, f:]).astype(BF16)
        return _dot(act, wd_b[...]).astype(BF16)

    @pl.when(tile_rows[i] == TR)
    def _():
        y_ref[...] = swiglu(x_ref[...])

    @pl.when(tile_rows[i] == TRC)
    def _():
        y_ref[0:TRC, :] = swiglu(x_ref[0:TRC, :])
        y_ref[TRC:, :] = jnp.zeros((TR - TRC, y_ref.shape[1]), BF16)


def _expert_ffn(meta, xs, w_gate, w_up, w_down):
    R, D = xs.shape
    E, _, f = w_gate.shape
    nt = R // TR
    grid_spec = pltpu.PrefetchScalarGridSpec(
        num_scalar_prefetch=5,
        grid=(nt,),
        in_specs=[
            pl.BlockSpec((TR, D), lambda i, te, ts, *_: (ts[i], 0)),
            pl.BlockSpec(memory_space=pl.ANY),
            pl.BlockSpec(memory_space=pl.ANY),
            pl.BlockSpec(memory_space=pl.ANY),
        ],
        out_specs=pl.BlockSpec((TR, D), lambda i, te, ts, *_: (ts[i], 0)),
        scratch_shapes=[
            pltpu.VMEM((2, D, f), F32),
            pltpu.VMEM((2, D, f), F32),
            pltpu.VMEM((2, f, D), F32),
            pltpu.VMEM((D, 2 * f), BF16),
            pltpu.VMEM((f, D), BF16),
            pltpu.SemaphoreType.DMA((2, 3)),
        ],
    )
    return pl.pallas_call(
        _ffn_body,
        grid_spec=grid_spec,
        out_shape=jax.ShapeDtypeStruct((R, D), BF16),
        compiler_params=pltpu.CompilerParams(
            dimension_semantics=("arbitrary",), vmem_limit_bytes=VMEM_LIMIT),
        name="expert_ffn",
    )(meta["tile_e"], meta["tile_src"], meta["tile_rows"], meta["e_slot"], meta["e_next"],
      xs, w_gate, w_up, w_down)


def _combine_body(alpha, n_pb, gran_g, n_gran, nch, src_ref, wrow_ref, h1_ref, h1b_ref, p_ref,
                  wsg_ref, wsu_ref, wsd_ref, g2_ref, b2_ref, wpg_ref, bpg_ref, wpe_ref, ys_hbm,
                  yp_ref, ysm_ref, yloc_ref, acc_ref, sems):
    b = pl.program_id(0)
    n_chunks = nch[b]
    n_b = n_gran[b]
    gmax = gran_g.shape[0] // n_gran.shape[0]
    gpc = KCC // SEG

    last0 = pl.multiple_of((n_chunks - 1) * KCC, KCC)
    yloc_ref[pl.ds(last0, KCC), :] = jnp.zeros((KCC, yloc_ref.shape[1]), BF16)

    def seg_copy(l, g):
        return pltpu.make_async_copy(ys_hbm.at[pl.ds(g, SEG), :], yloc_ref.at[pl.ds(l, SEG), :],
                                     sems.at[l // KCC])

    _start_granule_copies(b, 0, n_b, gran_g, gmax, seg_copy)

    hb = h1b_ref[...]
    sh = (jax.nn.silu(_dot(hb, wsg_ref[...])) * _dot(hb, wsu_ref[...])).astype(BF16)
    acc_ref[...] = _dot(sh, wsd_ref[...])

    def chunk(c, carry):
        def wait_rows(rows):
            return pltpu.make_async_copy(ys_hbm.at[pl.ds(0, rows), :], yloc_ref.at[pl.ds(0, rows), :], sems.at[c])

        _wait_granule_copies(jnp.minimum(n_b - c * gpc, gpc), gpc, wait_rows)

        r0 = pl.multiple_of(c * KCC, KCC)
        src = src_ref[0, pl.ds(r0, KCC), :]
        w = wrow_ref[0, pl.ds(r0, KCC), :]
        lane = lax.broadcasted_iota(I32, src.shape, 1).astype(F32)
        wm = jnp.concatenate([jnp.where(src == lane + float(o), w, 0.0) for o in range(0, TB, src.shape[1])],
                             axis=1).astype(BF16)
        acc_ref[...] += lax.dot_general(wm, yloc_ref[pl.ds(r0, KCC), :],
                                        (((0,), (0,)), ((), ())), preferred_element_type=F32)
        return carry

    lax.fori_loop(0, n_chunks, chunk, 0)

    h2 = _ln_rows(alpha * h1_ref[...] + acc_ref[...], g2_ref[...], b2_ref[...])
    gate = jax.nn.sigmoid(_dot(h2.astype(BF16), wpg_ref[...]) + bpg_ref[...])
    pe = _dot(p_ref[...].astype(BF16), wpe_ref[...])
    y = h2 + gate * pe

    @pl.when(b < n_pb)
    def _():
        yp_ref[...] = y

    @pl.when(b >= n_pb)
    def _():
        ysm_ref[...] = y


def _combine(meta, src_rep, w_rep, h1_all, h1b_all, p_all, wsg_b, wsu_b, wsd_b, g2, b2, wpg_b, bpg, wpe_b, ys,
             m_out_max, n_prompt, alpha):
    n, D = h1_all.shape
    nb = n // TB
    n_pb = n_prompt // TB
    d_pe = p_all.shape[1]
    cs = lambda shape: pl.BlockSpec(shape, lambda i, *_: (0,) * len(shape), pipeline_mode=pl.Buffered(1))
    grid_spec = pltpu.PrefetchScalarGridSpec(
        num_scalar_prefetch=3,
        grid=(nb,),
        in_specs=[
            pl.BlockSpec((1, m_out_max, 128), lambda i, *_: (i, 0, 0)),
            pl.BlockSpec((1, m_out_max, 128), lambda i, *_: (i, 0, 0)),
            pl.BlockSpec((TB, D), lambda i, *_: (i, 0)),
            pl.BlockSpec((TB, D), lambda i, *_: (i, 0)),
            pl.BlockSpec((TB, d_pe), lambda i, *_: (i, 0)),
            cs(wsg_b.shape), cs(wsu_b.shape), cs(wsd_b.shape), cs(g2.shape), cs(b2.shape),
            cs(wpg_b.shape), cs(bpg.shape), cs(wpe_b.shape),
            pl.BlockSpec(memory_space=pl.ANY),
        ],
        out_specs=[
            pl.BlockSpec((TB, D), lambda i, *_: (jnp.minimum(i, n_pb - 1), 0)),
            pl.BlockSpec((TB, D), lambda i, *_: (jnp.maximum(i - n_pb, 0), 0)),
        ],
        scratch_shapes=[
            pltpu.VMEM((m_out_max, D), BF16),
            pltpu.VMEM((TB, D), F32),
            pltpu.SemaphoreType.DMA((m_out_max // KCC,)),
        ],
    )
    return pl.pallas_call(
        functools.partial(_combine_body, alpha, n_pb),
        grid_spec=grid_spec,
        out_shape=[jax.ShapeDtypeStruct((n_prompt, D), F32), jax.ShapeDtypeStruct((n - n_prompt, D), F32)],
        compiler_params=pltpu.CompilerParams(
            dimension_semantics=("arbitrary",), vmem_limit_bytes=VMEM_LIMIT),
        name="combine",
    )(meta["gran_g"], meta["n_gran"], meta["nchc"], src_rep, w_rep, h1_all, h1b_all, p_all,
      wsg_b, wsu_b, wsd_b, g2, b2, wpg_b, bpg, wpe_b, ys)


def _sort_meta(cnt, n_tiles_max, m_out_max):
    segp = (cnt + (SEG - 1)) // SEG * SEG
    loc = jnp.cumsum(segp, axis=1) - segp
    before = jnp.cumsum(segp, axis=0) - segp
    total = jnp.sum(segp, axis=0)
    total_c = (total + (TRC - 1)) // TRC * TRC
    total_p = (total + (TR - 1)) // TR * TR
    ends = jnp.cumsum(total_p)
    base = ends - total_p
    glob = base[None, :] + before
    n_valid = ends[-1] // TR
    tiles = jnp.arange(n_tiles_max, dtype=I32)
    tile_src = jnp.minimum(tiles, n_valid - 1)
    tile_e = jnp.minimum(jnp.sum((ends[None, :] <= (tile_src * TR)[:, None]).astype(I32), axis=1), N_EXPERTS - 1)
    left = jnp.sum(jnp.where(tile_e[:, None] == jnp.arange(N_EXPERTS, dtype=I32)[None, :],
                             (base + total_c)[None, :], 0), axis=1) - tile_src * TR
    tile_rows = jnp.where(tiles < n_valid, jnp.clip(left, 0, TR), 0)
    experts = jnp.arange(N_EXPERTS, dtype=I32)
    active = total_p > 0
    later = active[None, :] & (experts[None, :] > experts[:, None])
    e_next = jnp.min(jnp.where(later, experts[None, :], N_EXPERTS), axis=1)
    gmax = m_out_max // SEG
    n16 = segp // SEG
    g_end = jnp.cumsum(n16, axis=1)
    g_start = g_end - n16
    j = jnp.arange(gmax, dtype=I32)[None, :, None]
    in_seg = (g_start[:, None, :] <= j) & (j < g_end[:, None, :])
    gran_g = jnp.sum(jnp.where(in_seg, (glob - loc)[:, None, :] + j * SEG, 0), axis=2)
    return {
        "gran_g": gran_g.reshape(-1).astype(I32),
        "n_gran": g_end[:, -1].astype(I32),
        "nch": ((jnp.sum(segp, axis=1) + (KC - 1)) // KC).astype(I32),
        "nchc": ((jnp.sum(segp, axis=1) + (KCC - 1)) // KCC).astype(I32),
        "tail_g": (base + total).astype(I32),
        "tail_n": ((total_c - total) // SEG).astype(I32),
        "tile_e": tile_e.astype(I32),
        "tile_src": tile_src.astype(I32),
        "tile_rows": tile_rows.astype(I32),
        "e_slot": ((jnp.cumsum(active.astype(I32)) - 1) % 2).astype(I32),
        "e_next": jnp.where(e_next < N_EXPERTS, e_next, -1).astype(I32),
    }


def _layer(xp, xs, st, pp, ps, w_in, w_pool, pool_scale, sgu_ln_g, sgu_ln_b, w_s, b_s, w_out, ln1_g, ln1_b,
           router_w, router_bias, w_gate, w_up, w_down, ws_gate, ws_up, ws_down, ln2_g, ln2_b, w_pe, w_pgate,
           b_pgate, alpha):
    B, T, D = xp.shape
    NS, TS, _ = xs.shape
    n_prompt, n_sample = B * T, NS * TS
    n = n_prompt + n_sample
    n_heads = w_s.shape[0]
    row = lambda v: v.reshape(1, -1)

    x_t = jnp.transpose(xs, (1, 0, 2))
    st_t = jnp.transpose(st, (1, 0, 2))
    bsb = jnp.broadcast_to(b_s[:, :, None], (n_heads, CHUNK, SGU_HEAD_DIM))
    wrow = jnp.repeat(jnp.transpose(w_s[:, :TS, :TS], (1, 2, 0)), SGU_HEAD_DIM, axis=2)
    brow = jnp.repeat(jnp.transpose(b_s[:, :TS], (1, 0)), SGU_HEAD_DIM, axis=1)
    h1_all, h1b_all, pool_p, a_s, vn_s, dest, wsel, cnt = _mixer(
        xp, x_t, st_t, w_in.astype(BF16), w_pool.astype(BF16), row(pool_scale), sgu_ln_g, sgu_ln_b, w_s, bsb,
        wrow, brow, w_out.astype(BF16), row(ln1_g), row(ln1_b), jnp.transpose(router_w),
        router_bias.reshape(-1, 1), alpha)
    nb = n // TB
    m_out_max = TB * TOP_K + N_EXPERTS * SEG
    rows_max = n * TOP_K + nb * N_EXPERTS * (SEG - 1) + N_EXPERTS * (TR - SEG)
    n_tiles_max = -(-rows_max // TR)
    meta = _sort_meta(cnt[:, :, 0], n_tiles_max, m_out_max)
    xsort, src_rep, w_rep = _dispatch(meta, dest, wsel, h1b_all, n_tiles_max * TR, m_out_max)
    ysort = _expert_ffn(meta, xsort, w_gate, w_up, w_down)

    nbs = TB // TS
    ps_t = jnp.transpose(ps.reshape(NS // nbs, nbs, TS, -1), (0, 2, 1, 3)).reshape(n_sample, -1)
    p_all = jnp.concatenate([pp.reshape(n_prompt, -1), ps_t], axis=0)
    y_p, y_s = _combine(meta, src_rep, w_rep, h1_all, h1b_all, p_all, ws_gate.astype(BF16), ws_up.astype(BF16),
                        ws_down.astype(BF16), row(ln2_g), row(ln2_b), w_pgate.astype(BF16), row(b_pgate),
                        w_pe.astype(BF16), ysort, m_out_max, n_prompt, alpha)

    yp = y_p.reshape(B, T, D)
    ys = jnp.transpose(y_s.reshape(NS // nbs, TS, nbs, D), (0, 2, 1, 3)).reshape(NS, TS, D)
    new_pool_p = pool_p[:, HALO - POOL_BUF:, :]
    new_pool_s = jnp.concatenate([st, jnp.transpose(a_s, (1, 0, 2))], axis=1)[:, -POOL_BUF:]
    vn = jnp.transpose(vn_s, (1, 0, 2)).reshape(NS, TS, n_heads, SGU_HEAD_DIM)
    return yp, ys, new_pool_p, new_pool_s, vn


def kernel(x_prompt, x_sample, state_pool, p_prompt, p_sample, w_in, w_pool, pool_scale, sgu_ln_g, sgu_ln_b, w_s, b_s, w_out, ln1_g, ln1_b, router_w, router_bias, w_gate, w_up, w_down, ws_gate, ws_up, ws_down, ln2_g, ln2_b, w_pe, w_pgate, b_pgate):
    depth = w_in.shape[0]
    alpha = (2.0 * depth) ** 0.25
    hp, hs = x_prompt, x_sample
    pool_p, pool_s, v_s = [], [], []
    for i in range(depth):
        hp, hs, bp, bs, vs = _layer(
            hp, hs, state_pool[i], p_prompt[i], p_sample[i], w_in[i], w_pool[i], pool_scale[i], sgu_ln_g[i],
            sgu_ln_b[i], w_s[i], b_s[i], w_out[i], ln1_g[i], ln1_b[i], router_w[i], router_bias[i], w_gate[i],
            w_up[i], w_down[i], ws_gate[i], ws_up[i], ws_down[i], ln2_g[i], ln2_b[i], w_pe[i], w_pgate[i],
            b_pgate[i], alpha)
        pool_p.append(bp)
        pool_s.append(bs)
        v_s.append(vs)
    return hp, hs, jnp.stack(pool_p), jnp.stack(pool_s), jnp.stack(v_s)
```

```python
import functools

import jax
import jax.numpy as jnp
import numpy as np
from jax import lax
from jax.experimental import pallas as pl
from jax.experimental.pallas import tpu as pltpu

F32 = jnp.float32
BF16 = jnp.bfloat16
I32 = jnp.int32

POOL_WINDOWS = (2, 4, 8, 16)
POOL_BUF = max(POOL_WINDOWS) - 1
CHUNK = 128
SGU_HEAD_DIM = 128
N_EXPERTS = 64
TOP_K = 8
N_EXPERT_GROUPS = 8
TOPK_GROUPS = 4
ROUTE_SCALE = 2.5
LN_EPS = 1e-5
PAST_LEN = 16384

TB = 256
HALO = 16
TR = 512
TRC = 256
SEG = 16
SEGG = 8
KC = 256
KCC = 512
VMEM_LIMIT = 56 * 1024 * 1024


def _const_spec(shape):
    n = len(shape)
    return pl.BlockSpec(shape, lambda *_: (0,) * n, pipeline_mode=pl.Buffered(1))


def _ln_rows(x, g, b):
    mu = jnp.mean(x, axis=-1, keepdims=True)
    xc = x - mu
    var = jnp.mean(xc * xc, axis=-1, keepdims=True)
    return xc * lax.rsqrt(var + LN_EPS) * g + b


def _dot(a, b):
    return jnp.dot(a, b, preferred_element_type=F32)


def _prompt_tile(j, n_tiles, x, proj, band_ref, wpool_ref, pscale_ref, sg_ref, sb_ref, ws_ref, bsb_ref,
                 pool_ref, zcat_ref, mixin_ref):
    d_pool = zcat_ref.shape[1]
    d_pool_g = d_pool // len(POOL_WINDOWS)
    d_sgu = mixin_ref.shape[1] - d_pool

    a = proj[:, :d_pool]
    zcat_ref[HALO:, :] = a.astype(BF16)

    pos1 = (j * TB + lax.broadcasted_iota(I32, (TB, 1), 0) + 1).astype(F32)
    for g, w in enumerate(POOL_WINDOWS):
        c0, c1 = g * d_pool_g, (g + 1) * d_pool_g
        s = _dot(band_ref[g], zcat_ref[:, c0:c1])
        cnt = jnp.minimum(pos1, float(w))
        d = s / cnt - a[:, c0:c1]
        y = _dot(d.astype(BF16), wpool_ref[g])
        mixin_ref[:, c0:c1] = (y * pscale_ref[:, c0:c1]).astype(BF16)
    zcat_ref[0:HALO, :] = jnp.where(j == n_tiles - 1, jnp.zeros((HALO, d_pool), BF16), zcat_ref[TB:TB + HALO, :])
    pool_ref[0] = a[TB - HALO:, :]

    u = jax.nn.gelu(proj[:, d_pool:d_pool + d_sgu])
    v = jax.nn.gelu(proj[:, d_pool + d_sgu:])
    r = lax.broadcasted_iota(I32, (CHUNK, CHUNK), 0)
    c = lax.broadcasted_iota(I32, (CHUNK, CHUNK), 1)
    tril = r >= c
    for h in range(d_sgu // SGU_HEAD_DIM):
        l0, l1 = h * SGU_HEAD_DIM, (h + 1) * SGU_HEAD_DIM
        vn = _ln_rows(v[:, l0:l1], sg_ref[h:h + 1, :], sb_ref[h:h + 1, :]).astype(BF16)
        wst = jnp.where(tril, ws_ref[h], 0.0).astype(BF16)
        n_ch = TB // CHUNK
        mixed_all = _dot(wst, jnp.concatenate([vn[ci * CHUNK:(ci + 1) * CHUNK] for ci in range(n_ch)], axis=1))
        for ci in range(n_ch):
            r0, r1 = ci * CHUNK, (ci + 1) * CHUNK
            mixed = mixed_all[:, ci * SGU_HEAD_DIM:(ci + 1) * SGU_HEAD_DIM] + bsb_ref[h]
            mixin_ref[r0:r1, d_pool + l0:d_pool + l1] = (u[r0:r1, l0:l1] * mixed).astype(BF16)


def _sample_tile(T, NB, proj, st_ref, wpool_ref, pscale_ref, sg_ref, sb_ref, wrow_ref, brow_ref,
                 a_ref, vn_ref, dbuf_ref, mixin_ref):
    d_pool = dbuf_ref.shape[1]
    d_pool_g = d_pool // len(POOL_WINDOWS)
    d_sgu = mixin_ref.shape[1] - d_pool

    a = proj[:, :d_pool]
    for t in range(T):
        a_ref[t] = a[t * NB:(t + 1) * NB]

    for g, w in enumerate(POOL_WINDOWS):
        c0, c1 = g * d_pool_g, (g + 1) * d_pool_g
        for t in range(T):
            cnt = float(min(PAST_LEN + t + 1, w))
            acc = None
            for s in range(POOL_BUF + t - w + 1, POOL_BUF + t + 1):
                if s < POOL_BUF:
                    term = st_ref[s, :, c0:c1]
                else:
                    term = a[(s - POOL_BUF) * NB:(s - POOL_BUF + 1) * NB, c0:c1]
                acc = term if acc is None else acc + term
            d = acc / cnt - a[t * NB:(t + 1) * NB, c0:c1]
            dbuf_ref[t * NB:(t + 1) * NB, c0:c1] = d.astype(BF16)
        y = _dot(dbuf_ref[0:T * NB, c0:c1], wpool_ref[g])
        mixin_ref[:, c0:c1] = (y * pscale_ref[:, c0:c1]).astype(BF16)

    u = jax.nn.gelu(proj[:, d_pool:d_pool + d_sgu])
    v = jax.nn.gelu(proj[:, d_pool + d_sgu:])
    for h in range(d_sgu // SGU_HEAD_DIM):
        l0, l1 = h * SGU_HEAD_DIM, (h + 1) * SGU_HEAD_DIM
        vn = _ln_rows(v[:, l0:l1], sg_ref[h:h + 1, :], sb_ref[h:h + 1, :])
        for t in range(T):
            vn_ref[t, :, l0:l1] = vn[t * NB:(t + 1) * NB]
    for t in range(T):
        mixed = brow_ref[t:t + 1, :]
        for s in range(t + 1):
            mixed = mixed + wrow_ref[t, s:s + 1, :] * vn_ref[s]
        mixin_ref[t * NB:(t + 1) * NB, d_pool:] = (u[t * NB:(t + 1) * NB] * mixed).astype(BF16)


def _mixer_body(alpha, n_pt, tiles_per_seq, xp_ref, xs_ref, st_ref, win_ref, band_ref, wpool_ref, pscale_ref,
                sg_ref, sb_ref, ws_ref, bsb_ref, wrow_ref, brow_ref, wout_ref, g1_ref, b1_ref,
                rwh_ref, rwl_ref, rb_ref, upper_ref, lstrict_ref,
                h1_ref, h1b_ref, pool_ref, a_ref, vn_ref, dest_ref, wsel_ref, cnt_ref,
                zcat_ref, mixin_ref, hprev_ref):
    i = pl.program_id(0)
    n_tiles = pl.num_programs(0) - 1

    def route_previous_tile():
        h = hprev_ref[...]
        _route_tile(h, h.astype(BF16), rwh_ref, rwl_ref, rb_ref, upper_ref, lstrict_ref,
                    dest_ref, wsel_ref, cnt_ref)

    @pl.when(i == 0)
    def _():
        zcat_ref[0:HALO, :] = jnp.zeros((HALO, zcat_ref.shape[1]), BF16)
        hprev_ref[...] = jnp.zeros(hprev_ref.shape, F32)

    @pl.when(i < n_pt)
    def _():
        x = xp_ref[0]
        proj = _dot(x.astype(BF16), win_ref[...])
        route_previous_tile()
        _prompt_tile(i % tiles_per_seq, tiles_per_seq, x, proj, band_ref, wpool_ref, pscale_ref, sg_ref, sb_ref,
                     ws_ref, bsb_ref, pool_ref, zcat_ref, mixin_ref)
        h1_ref[...] = alpha * x

    @pl.when(i >= n_pt)
    def _():
        route_previous_tile()

    @pl.when((i >= n_pt) & (i < n_tiles))
    def _():
        T, NB, D = xs_ref.shape
        x = xs_ref[...].reshape(T * NB, D)
        proj = _dot(x.astype(BF16), win_ref[...])
        _sample_tile(T, NB, proj, st_ref, wpool_ref, pscale_ref, sg_ref, sb_ref, wrow_ref, brow_ref,
                     a_ref, vn_ref, zcat_ref, mixin_ref)
        h1_ref[...] = alpha * x

    @pl.when(i < n_tiles)
    def _():
        mix = _dot(mixin_ref[...], wout_ref[...])
        h1 = _ln_rows(h1_ref[...] + mix, g1_ref[...], b1_ref[...])
        h1_ref[...] = h1
        h1b_ref[...] = h1.astype(BF16)
        hprev_ref[...] = h1


def _pool_band(tm):
    t = np.arange(tm)[:, None] + HALO
    s = np.arange(tm + HALO)[None, :]
    return jnp.asarray(np.stack([((s <= t) & (s >= t - w + 1)) for w in POOL_WINDOWS]).astype(np.float32), BF16)


def _mixer(xp, xs_t, st_t, win_b, wpool_b, pscale, sg, sb, ws, bsb, wrow, brow, wout_b, g1, b1, rwt, rb_col, alpha):
    B, T, D = xp.shape
    TS, NS, _ = xs_t.shape
    d_pool = st_t.shape[2]
    d_sgu = wrow.shape[2]
    tps = T // TB
    n_pt = B * tps
    NB = TB // TS
    n_st = NS // NB
    n = B * T + NS * TS
    nblk = n_pt + n_st
    pt = lambda i: jnp.minimum(i, n_pt - 1)
    stile = lambda i: jnp.clip(i - n_pt, 0, n_st - 1)
    blk = lambda i: jnp.minimum(i, nblk - 1)
    routed = lambda i: jnp.maximum(i - 1, 0)
    body = functools.partial(_mixer_body, alpha, n_pt, tps)
    rw_hi = rwt.astype(BF16)
    rw_lo = (rwt - rw_hi.astype(F32)).astype(BF16)
    upper = jnp.asarray(np.triu(np.ones((TB, TB), np.float32), 1), BF16)
    lstrict = jnp.asarray(np.tril(np.ones((N_EXPERTS, N_EXPERTS), np.float32), -1), BF16)
    consts = (win_b, _pool_band(TB), wpool_b, pscale, sg, sb, ws, bsb, wrow, brow, wout_b, g1, b1,
              rw_hi, rw_lo, rb_col, upper, lstrict)
    return pl.pallas_call(
        body,
        grid=(nblk + 1,),
        in_specs=[
            pl.BlockSpec((1, TB, D), lambda i: (pt(i) // tps, pt(i) % tps, 0)),
            pl.BlockSpec((TS, NB, D), lambda i: (0, stile(i), 0), pipeline_mode=pl.Buffered(1)),
            pl.BlockSpec((POOL_BUF, NB, d_pool), lambda i: (0, stile(i), 0), pipeline_mode=pl.Buffered(1)),
        ] + [_const_spec(c.shape) for c in consts],
        out_specs=[
            pl.BlockSpec((TB, D), lambda i: (blk(i), 0)),
            pl.BlockSpec((TB, D), lambda i: (blk(i), 0)),
            pl.BlockSpec((1, HALO, d_pool), lambda i: (pt(i) // tps, 0, 0)),
            pl.BlockSpec((TS, NB, d_pool), lambda i: (0, stile(i), 0)),
            pl.BlockSpec((TS, NB, d_sgu), lambda i: (0, stile(i), 0)),
            pl.BlockSpec((1, TOP_K, TB), lambda i: (routed(i), 0, 0)),
            pl.BlockSpec((1, TOP_K, TB), lambda i: (routed(i), 0, 0)),
            pl.BlockSpec((1, N_EXPERTS, 128), lambda i: (routed(i), 0, 0)),
        ],
        out_shape=[
            jax.ShapeDtypeStruct((n, D), F32),
            jax.ShapeDtypeStruct((n, D), BF16),
            jax.ShapeDtypeStruct((B, HALO, d_pool), F32),
            jax.ShapeDtypeStruct((TS, NS, d_pool), F32),
            jax.ShapeDtypeStruct((TS, NS, d_sgu), F32),
            jax.ShapeDtypeStruct((nblk, TOP_K, TB), I32),
            jax.ShapeDtypeStruct((nblk, TOP_K, TB), F32),
            jax.ShapeDtypeStruct((nblk, N_EXPERTS, 128), I32),
        ],
        scratch_shapes=[
            pltpu.VMEM((TB + HALO, d_pool), BF16),
            pltpu.VMEM((TB, D), BF16),
            pltpu.VMEM((TB, D), F32),
        ],
        compiler_params=pltpu.CompilerParams(
            dimension_semantics=("arbitrary",), vmem_limit_bytes=VMEM_LIMIT),
        name="mixer",
    )(xp, xs_t, st_t, *consts)


def _route_tile(h1, h1b, rwh_ref, rwl_ref, rb_ref, upper_ref, lstrict_ref, dest_ref, wsel_ref, cnt_ref):
    per_group = N_EXPERTS // N_EXPERT_GROUPS
    neg = -jnp.inf
    nt = (((1,), (1,)), ((), ()))
    h_lo = (h1 - h1b.astype(F32)).astype(BF16)
    rwh = rwh_ref[...]
    logits_t = (lax.dot_general(rwh, h1b, nt, preferred_element_type=F32)
                + lax.dot_general(rwh, h_lo, nt, preferred_element_type=F32)
                + lax.dot_general(rwl_ref[...], h1b, nt, preferred_element_type=F32))
    s_t = jax.nn.sigmoid(logits_t)
    b_t = s_t + rb_ref[...]

    io_g = lax.broadcasted_iota(I32, (per_group, TB), 0)
    gs = []
    for g in range(N_EXPERT_GROUPS):
        xg = b_t[g * per_group:(g + 1) * per_group, :]
        m1 = jnp.max(xg, axis=0, keepdims=True)
        i1 = jnp.min(jnp.where(xg == m1, io_g, per_group), axis=0, keepdims=True)
        m2 = jnp.max(jnp.where(io_g == i1, neg, xg), axis=0, keepdims=True)
        gs.append(m1 + m2)
    masked = []
    for g in range(N_EXPERT_GROUPS):
        rank = jnp.zeros((1, TB), F32)
        for g2 in range(N_EXPERT_GROUPS):
            if g2 != g:
                ahead = (gs[g2] >= gs[g]) if g2 < g else (gs[g2] > gs[g])
                rank = rank + jnp.where(ahead, 1.0, 0.0)
        keep = rank < float(TOPK_GROUPS)
        masked.append(jnp.where(keep, b_t[g * per_group:(g + 1) * per_group, :], neg))
    xm = jnp.concatenate(masked, axis=0)

    io_e = lax.broadcasted_iota(I32, (N_EXPERTS, TB), 0)
    onehots = []
    sel = jnp.zeros((N_EXPERTS, TB), F32)
    for _ in range(TOP_K):
        m = jnp.max(xm, axis=0, keepdims=True)
        idx = jnp.min(jnp.where(xm == m, io_e, N_EXPERTS), axis=0, keepdims=True)
        oh = io_e == idx
        onehots.append(oh)
        sel = jnp.where(oh, 1.0, sel)
        xm = jnp.where(oh, neg, xm)

    ssel = sel * s_t
    denom = jnp.sum(ssel, axis=0, keepdims=True)
    comb = ssel / denom * ROUTE_SCALE

    rank_t = _dot(sel.astype(BF16), upper_ref[...])
    cnt = jnp.sum(sel, axis=1, keepdims=True)
    cnt_i = cnt.astype(I32)
    seg16 = ((cnt_i + (SEG - 1)) // SEG).astype(F32)
    off16 = _dot(lstrict_ref[...], jnp.broadcast_to(seg16, (N_EXPERTS, 128)).astype(BF16))
    d_t = rank_t + off16[:, 0:1] * float(SEG)
    for k in range(TOP_K):
        dest_ref[0, k:k + 1, :] = jnp.sum(jnp.where(onehots[k], d_t, 0.0), axis=0, keepdims=True).astype(I32)
        wsel_ref[0, k:k + 1, :] = jnp.sum(jnp.where(onehots[k], comb, 0.0), axis=0, keepdims=True)
    cnt_ref[0] = jnp.broadcast_to(cnt_i, (N_EXPERTS, 128))


def _start_listed_copies(b, table_l, table_g, counts, rows, make_copy):
    width = table_l.shape[0] // counts.shape[0]

    def one(j, c):
        idx = b * width + j
        make_copy(pl.multiple_of(table_l[idx], rows), pl.multiple_of(table_g[idx], SEGG), rows).start()
        return c

    lax.fori_loop(0, counts[b], one, 0)


def _wait_row_units(count, max_count, make_wait_copy):
    for k in range(int(max_count).bit_length()):
        @pl.when(((count >> k) & 1) == 1)
        def _():
            make_wait_copy(SEGG << k).wait()


def _dispatch_body(big_l, big_g, n_big, small_l, small_g, n_small, units, nch, tail_g, tail_n,
                   dest_ref, wsel_ref, h_ref, tokid_ref, ones_ref,
                   xs_hbm, src_ref, wrow_ref, xloc_ref, zero_ref, sems, tail_sem):
    b = pl.program_id(0)
    nb = pl.num_programs(0)
    max_units = xloc_ref.shape[1] // SEGG
    slot = b % 2

    def seg_copy(sl):
        return lambda l, g, rows: pltpu.make_async_copy(
            xloc_ref.at[sl, pl.ds(l, rows), :], xs_hbm.at[pl.ds(g, rows), :], sems.at[sl])

    def wait_block(count, sl):
        _wait_row_units(count, max_units, lambda rows: pltpu.make_async_copy(
            xloc_ref.at[sl, pl.ds(0, rows), :], xs_hbm.at[pl.ds(0, rows), :], sems.at[sl]))

    @pl.when(b >= 2)
    def _():
        wait_block(units[jnp.maximum(b - 2, 0)], slot)

    dest = dest_ref[0]
    wsel = wsel_ref[0]
    h = h_ref[...]
    src_ref[...] = jnp.zeros(src_ref.shape, F32)
    wrow_ref[...] = jnp.zeros(wrow_ref.shape, F32)

    row_in_chunk = lax.broadcasted_iota(I32, (KC, TB), 0).astype(F32).astype(BF16)
    wsel_b = wsel.astype(BF16)
    one = jnp.ones((1, TB), BF16)

    def chunk(c):
        r0 = pl.multiple_of(c * KC, KC)
        rel = (dest - r0).astype(F32).astype(BF16)
        sb = jnp.zeros((KC, TB), BF16)
        wm = jnp.zeros((KC, TB), BF16)
        for k in range(TOP_K):
            hit = rel[k:k + 1, :] == row_in_chunk
            sb = jnp.where(hit, one, sb)
            wm = jnp.where(hit, wsel_b[k:k + 1, :], wm)
        xloc_ref[slot, pl.ds(r0, KC), :] = _dot(sb, h).astype(BF16)
        src_ref[0, pl.ds(r0, KC), :] = _dot(sb, tokid_ref[...])
        wrow_ref[0, pl.ds(r0, KC), :] = _dot(wm, ones_ref[...])

    def chunk_pair(cp, carry):
        chunk(2 * cp)
        chunk(2 * cp + 1)
        return carry

    lax.fori_loop(0, (nch[b] + 1) // 2, chunk_pair, 0)

    _start_listed_copies(b, big_l, big_g, n_big, SEG, seg_copy(slot))
    _start_listed_copies(b, small_l, small_g, n_small, SEGG, seg_copy(slot))

    def tail_copy(g):
        return pltpu.make_async_copy(zero_ref, xs_hbm.at[pl.ds(g, SEGG), :], tail_sem)

    def tails(wait):
        def per_expert(e, carry):
            def per_granule(j, c):
                cp = tail_copy(pl.multiple_of(tail_g[e] + j * SEGG, SEGG))
                if wait:
                    cp.wait()
                else:
                    cp.start()
                return c
            lax.fori_loop(0, tail_n[e], per_granule, 0)
            return carry
        lax.fori_loop(0, N_EXPERTS, per_expert, 0)

    @pl.when(b == nb - 1)
    def _():
        zero_ref[...] = jnp.zeros(zero_ref.shape, BF16)
        tails(False)

        @pl.when(b >= 1)
        def _():
            wait_block(units[jnp.maximum(b - 1, 0)], 1 - slot)

        wait_block(units[b], slot)
        tails(True)


def _dispatch(meta, dest, wsel, h1b_all, n_rows_sorted, m_out_max):
    n, D = h1b_all.shape
    nb = n // TB
    tokid = jnp.asarray(np.broadcast_to(np.arange(TB, dtype=np.float32)[:, None], (TB, 128)), BF16)
    ones = jnp.ones((TB, 128), BF16)
    cs = lambda shape: pl.BlockSpec(shape, lambda i, *_: (0,) * len(shape), pipeline_mode=pl.Buffered(1))
    grid_spec = pltpu.PrefetchScalarGridSpec(
        num_scalar_prefetch=10,
        grid=(nb,),
        in_specs=[
            pl.BlockSpec((1, TOP_K, TB), lambda i, *_: (i, 0, 0)),
            pl.BlockSpec((1, TOP_K, TB), lambda i, *_: (i, 0, 0)),
            pl.BlockSpec((TB, D), lambda i, *_: (i, 0)),
            cs(tokid.shape), cs(ones.shape),
        ],
        out_specs=[
            pl.BlockSpec(memory_space=pl.ANY),
            pl.BlockSpec((1, m_out_max, 128), lambda i, *_: (i, 0, 0)),
            pl.BlockSpec((1, m_out_max, 128), lambda i, *_: (i, 0, 0)),
        ],
        scratch_shapes=[
            pltpu.VMEM((2, m_out_max, D), BF16),
            pltpu.VMEM((SEGG, D), BF16),
            pltpu.SemaphoreType.DMA((2,)),
            pltpu.SemaphoreType.DMA(()),
        ],
    )
    return pl.pallas_call(
        _dispatch_body,
        grid_spec=grid_spec,
        out_shape=[
            jax.ShapeDtypeStruct((n_rows_sorted, D), BF16),
            jax.ShapeDtypeStruct((nb, m_out_max, 128), F32),
            jax.ShapeDtypeStruct((nb, m_out_max, 128), F32),
        ],
        compiler_params=pltpu.CompilerParams(
            dimension_semantics=("arbitrary",), vmem_limit_bytes=VMEM_LIMIT, has_side_effects=True),
        name="dispatch",
    )(meta["big_l"], meta["big_g"], meta["n_big"], meta["small_l"], meta["small_g"], meta["n_small"],
      meta["units"], meta["nch"], meta["tail_g"], meta["tail_n"], dest, wsel, h1b_all, tokid, ones)


def _ffn_body(tile_e, tile_src, tile_rows, e_slot, e_next, x_ref, wg_hbm, wu_hbm, wd_hbm, y_ref,
              wg_f, wu_f, wd_f, wgu_b, wd_b, sems):
    i = pl.program_id(0)
    f = wg_f.shape[2]
    e = tile_e[i]
    e_prev = tile_e[jnp.maximum(i - 1, 0)]

    def weight_copies(expert, slot):
        return (pltpu.make_async_copy(wg_hbm.at[expert], wg_f.at[slot], sems.at[slot, 0]),
                pltpu.make_async_copy(wu_hbm.at[expert], wu_f.at[slot], sems.at[slot, 1]),
                pltpu.make_async_copy(wd_hbm.at[expert], wd_f.at[slot], sems.at[slot, 2]))

    @pl.when((i == 0) | (e != e_prev))
    def _():
        slot = e_slot[e]

        @pl.when(i == 0)
        def _():
            for cp in weight_copies(e, slot):
                cp.start()

        for cp in weight_copies(e, slot):
            cp.wait()
        nxt = e_next[e]

        @pl.when(nxt >= 0)
        def _():
            for cp in weight_copies(nxt, 1 - slot):
                cp.start(priority=1)

        wgu_b[:, :f] = wg_f[slot].astype(BF16)
        wgu_b[:, f:] = wu_f[slot].astype(BF16)
        wd_b[...] = wd_f[slot].astype(BF16)

    def swiglu(x):
        gu = _dot(x, wgu_b[...])
        act = (jax.nn.silu(gu[:, :f]) * gu[:, f:]).astype(BF16)
        return _dot(act, wd_b[...]).astype(BF16)

    @pl.when(tile_rows[i] == TR)
    def _():
        y_ref[...] = swiglu(x_ref[...])

    @pl.when(tile_rows[i] == TRC)
    def _():
        y_ref[0:TRC, :] = swiglu(x_ref[0:TRC, :])
        y_ref[TRC:, :] = jnp.zeros((TR - TRC, y_ref.shape[1]), BF16)


def _expert_ffn(meta, xs, w_gate, w_up, w_down):
    R, D = xs.shape
    E, _, f = w_gate.shape
    nt = R // TR
    grid_spec = pltpu.PrefetchScalarGridSpec(
        num_scalar_prefetch=5,
        grid=(nt,),
        in_specs=[
            pl.BlockSpec((TR, D), lambda i, te, ts, *_: (ts[i], 0)),
            pl.BlockSpec(memory_space=pl.ANY),
            pl.BlockSpec(memory_space=pl.ANY),
            pl.BlockSpec(memory_space=pl.ANY),
        ],
        out_specs=pl.BlockSpec((TR, D), lambda i, te, ts, *_: (ts[i], 0)),
        scratch_shapes=[
            pltpu.VMEM((2, D, f), F32),
            pltpu.VMEM((2, D, f), F32),
            pltpu.VMEM((2, f, D), F32),
            pltpu.VMEM((D, 2 * f), BF16),
            pltpu.VMEM((f, D), BF16),
            pltpu.SemaphoreType.DMA((2, 3)),
        ],
    )
    return pl.pallas_call(
        _ffn_body,
        grid_spec=grid_spec,
        out_shape=jax.ShapeDtypeStruct((R, D), BF16),
        compiler_params=pltpu.CompilerParams(
            dimension_semantics=("arbitrary",), vmem_limit_bytes=VMEM_LIMIT),
        name="expert_ffn",
    )(meta["tile_e"], meta["tile_src"], meta["tile_rows"], meta["e_slot"], meta["e_next"],
      xs, w_gate, w_up, w_down)


def _combine_body(alpha, n_pb, big_l, big_g, n_big, small_l, small_g, n_small, chunk_units, nch,
                  src_ref, wrow_ref, h1_ref, h1b_ref, p_ref,
                  wsg_ref, wsu_ref, wsd_ref, g2_ref, b2_ref, wpg_ref, bpg_ref, wpe_ref, ys_hbm,
                  yp_ref, ysm_ref, yloc_ref, acc_ref, sems):
    b = pl.program_id(0)
    n_chunks = nch[b]
    max_chunks = sems.shape[0]

    @pl.when(b == 0)
    def _():
        yloc_ref[...] = jnp.zeros(yloc_ref.shape, BF16)

    def seg_copy(l, g, rows):
        return pltpu.make_async_copy(ys_hbm.at[pl.ds(g, rows), :], yloc_ref.at[pl.ds(l, rows), :],
                                     sems.at[l // KCC])

    _start_listed_copies(b, big_l, big_g, n_big, SEG, seg_copy)
    _start_listed_copies(b, small_l, small_g, n_small, SEGG, seg_copy)

    hb = h1b_ref[...]
    sh = (jax.nn.silu(_dot(hb, wsg_ref[...])) * _dot(hb, wsu_ref[...])).astype(BF16)
    acc_ref[...] = _dot(sh, wsd_ref[...])

    def chunk(c, carry):
        def wait_rows(rows):
            return pltpu.make_async_copy(ys_hbm.at[pl.ds(0, rows), :], yloc_ref.at[pl.ds(0, rows), :], sems.at[c])

        _wait_row_units(chunk_units[b * max_chunks + c], KCC // SEGG, wait_rows)

        r0 = pl.multiple_of(c * KCC, KCC)
        src = src_ref[0, pl.ds(r0, KCC), :]
        w = wrow_ref[0, pl.ds(r0, KCC), :]
        lane = lax.broadcasted_iota(I32, src.shape, 1).astype(F32)
        wm = jnp.concatenate([jnp.where(src == lane + float(o), w, 0.0) for o in range(0, TB, src.shape[1])],
                             axis=1).astype(BF16)
        acc_ref[...] += lax.dot_general(wm, yloc_ref[pl.ds(r0, KCC), :],
                                        (((0,), (0,)), ((), ())), preferred_element_type=F32)
        return carry

    lax.fori_loop(0, n_chunks, chunk, 0)

    h2 = _ln_rows(alpha * h1_ref[...] + acc_ref[...], g2_ref[...], b2_ref[...])
    gate = jax.nn.sigmoid(_dot(h2.astype(BF16), wpg_ref[...]) + bpg_ref[...])
    pe = _dot(p_ref[...].astype(BF16), wpe_ref[...])
    y = h2 + gate * pe

    @pl.when(b < n_pb)
    def _():
        yp_ref[...] = y

    @pl.when(b >= n_pb)
    def _():
        ysm_ref[...] = y


def _combine(meta, src_rep, w_rep, h1_all, h1b_all, p_all, wsg_b, wsu_b, wsd_b, g2, b2, wpg_b, bpg, wpe_b, ys,
             m_out_max, n_prompt, alpha):
    n, D = h1_all.shape
    nb = n // TB
    n_pb = n_prompt // TB
    d_pe = p_all.shape[1]
    cs = lambda shape: pl.BlockSpec(shape, lambda i, *_: (0,) * len(shape), pipeline_mode=pl.Buffered(1))
    grid_spec = pltpu.PrefetchScalarGridSpec(
        num_scalar_prefetch=8,
        grid=(nb,),
        in_specs=[
            pl.BlockSpec((1, m_out_max, 128), lambda i, *_: (i, 0, 0)),
            pl.BlockSpec((1, m_out_max, 128), lambda i, *_: (i, 0, 0)),
            pl.BlockSpec((TB, D), lambda i, *_: (i, 0)),
            pl.BlockSpec((TB, D), lambda i, *_: (i, 0)),
            pl.BlockSpec((TB, d_pe), lambda i, *_: (i, 0)),
            cs(wsg_b.shape), cs(wsu_b.shape), cs(wsd_b.shape), cs(g2.shape), cs(b2.shape),
            cs(wpg_b.shape), cs(bpg.shape), cs(wpe_b.shape),
            pl.BlockSpec(memory_space=pl.ANY),
        ],
        out_specs=[
            pl.BlockSpec((TB, D), lambda i, *_: (jnp.minimum(i, n_pb - 1), 0)),
            pl.BlockSpec((TB, D), lambda i, *_: (jnp.maximum(i - n_pb, 0), 0)),
        ],
        scratch_shapes=[
            pltpu.VMEM((m_out_max, D), BF16),
            pltpu.VMEM((TB, D), F32),
            pltpu.SemaphoreType.DMA((m_out_max // KCC,)),
        ],
    )
    return pl.pallas_call(
        functools.partial(_combine_body, alpha, n_pb),
        grid_spec=grid_spec,
        out_shape=[jax.ShapeDtypeStruct((n_prompt, D), F32), jax.ShapeDtypeStruct((n - n_prompt, D), F32)],
        compiler_params=pltpu.CompilerParams(
            dimension_semantics=("arbitrary",), vmem_limit_bytes=VMEM_LIMIT),
        name="combine",
    )(meta["big_l"], meta["big_g"], meta["n_big"], meta["small_l"], meta["small_g"], meta["n_small"],
      meta["chunk_units"], meta["nchc"], src_rep, w_rep, h1_all, h1b_all, p_all,
      wsg_b, wsu_b, wsd_b, g2, b2, wpg_b, bpg, wpe_b, ys)


def _sort_meta(cnt, n_tiles_max, m_out_max):
    segl = (cnt + (SEG - 1)) // SEG * SEG
    loc = jnp.cumsum(segl, axis=1) - segl
    n8 = (cnt + (SEGG - 1)) // SEGG
    segp = n8 * SEGG
    before = jnp.cumsum(segp, axis=0) - segp
    total = jnp.sum(segp, axis=0)
    total_c = (total + (TRC - 1)) // TRC * TRC
    total_p = (total + (TR - 1)) // TR * TR
    ends = jnp.cumsum(total_p)
    base = ends - total_p
    glob = base[None, :] + before
    n_valid = ends[-1] // TR
    tiles = jnp.arange(n_tiles_max, dtype=I32)
    tile_src = jnp.minimum(tiles, n_valid - 1)
    tile_e = jnp.minimum(jnp.sum((ends[None, :] <= (tile_src * TR)[:, None]).astype(I32), axis=1), N_EXPERTS - 1)
    left = jnp.sum(jnp.where(tile_e[:, None] == jnp.arange(N_EXPERTS, dtype=I32)[None, :],
                             (base + total_c)[None, :], 0), axis=1) - tile_src * TR
    tile_rows = jnp.where(tiles < n_valid, jnp.clip(left, 0, TR), 0)
    experts = jnp.arange(N_EXPERTS, dtype=I32)
    active = total_p > 0
    later = active[None, :] & (experts[None, :] > experts[:, None])
    e_next = jnp.min(jnp.where(later, experts[None, :], N_EXPERTS), axis=1)
    def copy_list(count, first_l, first_g, width):
        end = jnp.cumsum(count, axis=1)
        start = end - count
        j = jnp.arange(width, dtype=I32)[None, :, None]
        in_seg = (start[:, None, :] <= j) & (j < end[:, None, :])
        step = (j - start[:, None, :]) * SEG
        lst_l = jnp.sum(jnp.where(in_seg, first_l[:, None, :] + step, 0), axis=2)
        lst_g = jnp.sum(jnp.where(in_seg, first_g[:, None, :] + step, 0), axis=2)
        return lst_l, lst_g, end[:, -1], j[:, :, 0] < end[:, -1:]

    n_big, n_small = n8 // 2, n8 % 2
    big_l, big_g, big_n, big_ok = copy_list(n_big, loc, glob, m_out_max // SEG)
    small_l, small_g, small_n, small_ok = copy_list(n_small, loc + n_big * SEG, glob + n_big * SEG, N_EXPERTS)
    chunks = jnp.arange(m_out_max // KCC, dtype=I32)[None, :, None]
    chunk_units = (jnp.sum(jnp.where(big_ok[:, None, :] & (big_l[:, None, :] // KCC == chunks), SEG // SEGG, 0), axis=2)
                   + jnp.sum(jnp.where(small_ok[:, None, :] & (small_l[:, None, :] // KCC == chunks), 1, 0), axis=2))
    local_rows = jnp.sum(segl, axis=1)
    return {
        "big_l": big_l.reshape(-1).astype(I32),
        "big_g": big_g.reshape(-1).astype(I32),
        "n_big": big_n.astype(I32),
        "small_l": small_l.reshape(-1).astype(I32),
        "small_g": small_g.reshape(-1).astype(I32),
        "n_small": small_n.astype(I32),
        "units": jnp.sum(n8, axis=1).astype(I32),
        "chunk_units": chunk_units.reshape(-1).astype(I32),
        "nch": ((local_rows + (KC - 1)) // KC).astype(I32),
        "nchc": ((local_rows + (KCC - 1)) // KCC).astype(I32),
        "tail_g": (base + total).astype(I32),
        "tail_n": ((total_c - total) // SEGG).astype(I32),
        "tile_e": tile_e.astype(I32),
        "tile_src": tile_src.astype(I32),
        "tile_rows": tile_rows.astype(I32),
        "e_slot": ((jnp.cumsum(active.astype(I32)) - 1) % 2).astype(I32),
        "e_next": jnp.where(e_next < N_EXPERTS, e_next, -1).astype(I32),
    }


def _layer(xp, xs, st, pp, ps, w_in, w_pool, pool_scale, sgu_ln_g, sgu_ln_b, w_s, b_s, w_out, ln1_g, ln1_b,
           router_w, router_bias, w_gate, w_up, w_down, ws_gate, ws_up, ws_down, ln2_g, ln2_b, w_pe, w_pgate,
           b_pgate, alpha):
    B, T, D = xp.shape
    NS, TS, _ = xs.shape
    n_prompt, n_sample = B * T, NS * TS
    n = n_prompt + n_sample
    n_heads = w_s.shape[0]
    row = lambda v: v.reshape(1, -1)

    x_t = jnp.transpose(xs, (1, 0, 2))
    st_t = jnp.transpose(st, (1, 0, 2))
    bsb = jnp.broadcast_to(b_s[:, :, None], (n_heads, CHUNK, SGU_HEAD_DIM))
    wrow = jnp.repeat(jnp.transpose(w_s[:, :TS, :TS], (1, 2, 0)), SGU_HEAD_DIM, axis=2)
    brow = jnp.repeat(jnp.transpose(b_s[:, :TS], (1, 0)), SGU_HEAD_DIM, axis=1)
    h1_all, h1b_all, pool_p, a_s, vn_s, dest, wsel, cnt = _mixer(
        xp, x_t, st_t, w_in.astype(BF16), w_pool.astype(BF16), row(pool_scale), sgu_ln_g, sgu_ln_b, w_s, bsb,
        wrow, brow, w_out.astype(BF16), row(ln1_g), row(ln1_b), jnp.transpose(router_w),
        router_bias.reshape(-1, 1), alpha)
    nb = n // TB
    m_out_max = TB * TOP_K + N_EXPERTS * SEG
    rows_max = n * TOP_K + nb * N_EXPERTS * (SEGG - 1) + N_EXPERTS * (TR - SEGG)
    n_tiles_max = -(-rows_max // TR)
    meta = _sort_meta(cnt[:, :, 0], n_tiles_max, m_out_max)
    xsort, src_rep, w_rep = _dispatch(meta, dest, wsel, h1b_all, n_tiles_max * TR, m_out_max)
    ysort = _expert_ffn(meta, xsort, w_gate, w_up, w_down)

    nbs = TB // TS
    ps_t = jnp.transpose(ps.reshape(NS // nbs, nbs, TS, -1), (0, 2, 1, 3)).reshape(n_sample, -1)
    p_all = jnp.concatenate([pp.reshape(n_prompt, -1), ps_t], axis=0)
    y_p, y_s = _combine(meta, src_rep, w_rep, h1_all, h1b_all, p_all, ws_gate.astype(BF16), ws_up.astype(BF16),
                        ws_down.astype(BF16), row(ln2_g), row(ln2_b), w_pgate.astype(BF16), row(b_pgate),
                        w_pe.astype(BF16), ysort, m_out_max, n_prompt, alpha)

    yp = y_p.reshape(B, T, D)
    ys = jnp.transpose(y_s.reshape(NS // nbs, TS, nbs, D), (0, 2, 1, 3)).reshape(NS, TS, D)
    new_pool_p = pool_p[:, HALO - POOL_BUF:, :]
    new_pool_s = jnp.concatenate([st, jnp.transpose(a_s, (1, 0, 2))], axis=1)[:, -POOL_BUF:]
    vn = jnp.transpose(vn_s, (1, 0, 2)).reshape(NS, TS, n_heads, SGU_HEAD_DIM)
    return yp, ys, new_pool_p, new_pool_s, vn


def kernel(x_prompt, x_sample, state_pool, p_prompt, p_sample, w_in, w_pool, pool_scale, sgu_ln_g, sgu_ln_b, w_s, b_s, w_out, ln1_g, ln1_b, router_w, router_bias, w_gate, w_up, w_down, ws_gate, ws_up, ws_down, ln2_g, ln2_b, w_pe, w_pgate, b_pgate):
    depth = w_in.shape[0]
    alpha = (2.0 * depth) ** 0.25
    hp, hs = x_prompt, x_sample
    pool_p, pool_s, v_s = [], [], []
    for i in range(depth):
        hp, hs, bp, bs, vs = _layer(
            hp, hs, state_pool[i], p_prompt[i], p_sample[i], w_in[i], w_pool[i], pool_scale[i], sgu_ln_g[i],
            sgu_ln_b[i], w_s[i], b_s[i], w_out[i], ln1_g[i], ln1_b[i], router_w[i], router_bias[i], w_gate[i],
            w_up[i], w_down[i], ws_gate[i], ws_up[i], ws_down[i], ln2_g[i], ln2_b[i], w_pe[i], w_pgate[i],
            b_pgate[i], alpha)
        pool_p.append(bp)
        pool_s.append(bs)
        v_s.append(vs)
    return hp, hs, jnp.stack(pool_p), jnp.stack(pool_s), jnp.stack(v_s)
```

```python
import functools

import jax
import jax.numpy as jnp
import numpy as np
from jax import lax
from jax.experimental import pallas as pl
from jax.experimental.pallas import tpu as pltpu

F32 = jnp.float32
BF16 = jnp.bfloat16
I32 = jnp.int32

POOL_WINDOWS = (2, 4, 8, 16)
POOL_BUF = max(POOL_WINDOWS) - 1
CHUNK = 128
SGU_HEAD_DIM = 128
N_EXPERTS = 64
TOP_K = 8
N_EXPERT_GROUPS = 8
TOPK_GROUPS = 4
ROUTE_SCALE = 2.5
LN_EPS = 1e-5
PAST_LEN = 16384

TB = 256
HALO = 16
TR = 512
TRC = 256
SEG = 16
SEGG = 8
KC = 256
KCC = 512
VMEM_LIMIT = 56 * 1024 * 1024


def _const_spec(shape):
    n = len(shape)
    return pl.BlockSpec(shape, lambda *_: (0,) * n, pipeline_mode=pl.Buffered(1))


def _ln_rows(x, g, b):
    mu = jnp.mean(x, axis=-1, keepdims=True)
    xc = x - mu
    var = jnp.mean(xc * xc, axis=-1, keepdims=True)
    return xc * lax.rsqrt(var + LN_EPS) * g + b


def _dot(a, b):
    return jnp.dot(a, b, preferred_element_type=F32)


def _prompt_tile(j, n_tiles, x, proj, band_ref, wpool_ref, pscale_ref, sg_ref, sb_ref, ws_ref, bsb_ref,
                 pool_ref, zcat_ref, mixin_ref):
    d_pool = zcat_ref.shape[1]
    d_pool_g = d_pool // len(POOL_WINDOWS)
    d_sgu = mixin_ref.shape[1] - d_pool

    a = proj[:, :d_pool]
    zcat_ref[HALO:, :] = a.astype(BF16)

    pos1 = (j * TB + lax.broadcasted_iota(I32, (TB, 1), 0) + 1).astype(F32)
    for g, w in enumerate(POOL_WINDOWS):
        c0, c1 = g * d_pool_g, (g + 1) * d_pool_g
        s = _dot(band_ref[g], zcat_ref[:, c0:c1])
        cnt = jnp.minimum(pos1, float(w))
        d = s / cnt - a[:, c0:c1]
        y = _dot(d.astype(BF16), wpool_ref[g])
        mixin_ref[:, c0:c1] = (y * pscale_ref[:, c0:c1]).astype(BF16)
    zcat_ref[0:HALO, :] = jnp.where(j == n_tiles - 1, jnp.zeros((HALO, d_pool), BF16), zcat_ref[TB:TB + HALO, :])
    pool_ref[0] = a[TB - HALO:, :]

    u = jax.nn.gelu(proj[:, d_pool:d_pool + d_sgu])
    v = jax.nn.gelu(proj[:, d_pool + d_sgu:])
    r = lax.broadcasted_iota(I32, (CHUNK, CHUNK), 0)
    c = lax.broadcasted_iota(I32, (CHUNK, CHUNK), 1)
    tril = r >= c
    for h in range(d_sgu // SGU_HEAD_DIM):
        l0, l1 = h * SGU_HEAD_DIM, (h + 1) * SGU_HEAD_DIM
        vn = _ln_rows(v[:, l0:l1], sg_ref[h:h + 1, :], sb_ref[h:h + 1, :]).astype(BF16)
        wst = jnp.where(tril, ws_ref[h], 0.0).astype(BF16)
        n_ch = TB // CHUNK
        mixed_all = _dot(wst, jnp.concatenate([vn[ci * CHUNK:(ci + 1) * CHUNK] for ci in range(n_ch)], axis=1))
        for ci in range(n_ch):
            r0, r1 = ci * CHUNK, (ci + 1) * CHUNK
            mixed = mixed_all[:, ci * SGU_HEAD_DIM:(ci + 1) * SGU_HEAD_DIM] + bsb_ref[h]
            mixin_ref[r0:r1, d_pool + l0:d_pool + l1] = (u[r0:r1, l0:l1] * mixed).astype(BF16)


def _sample_tile(T, NB, proj, st_ref, wpool_ref, pscale_ref, sg_ref, sb_ref, wrow_ref, brow_ref,
                 a_ref, vn_ref, dbuf_ref, mixin_ref):
    d_pool = dbuf_ref.shape[1]
    d_pool_g = d_pool // len(POOL_WINDOWS)
    d_sgu = mixin_ref.shape[1] - d_pool

    a = proj[:, :d_pool]
    for t in range(T):
        a_ref[t] = a[t * NB:(t + 1) * NB]

    for g, w in enumerate(POOL_WINDOWS):
        c0, c1 = g * d_pool_g, (g + 1) * d_pool_g
        for t in range(T):
            cnt = float(min(PAST_LEN + t + 1, w))
            acc = None
            for s in range(POOL_BUF + t - w + 1, POOL_BUF + t + 1):
                if s < POOL_BUF:
                    term = st_ref[s, :, c0:c1]
                else:
                    term = a[(s - POOL_BUF) * NB:(s - POOL_BUF + 1) * NB, c0:c1]
                acc = term if acc is None else acc + term
            d = acc / cnt - a[t * NB:(t + 1) * NB, c0:c1]
            dbuf_ref[t * NB:(t + 1) * NB, c0:c1] = d.astype(BF16)
        y = _dot(dbuf_ref[0:T * NB, c0:c1], wpool_ref[g])
        mixin_ref[:, c0:c1] = (y * pscale_ref[:, c0:c1]).astype(BF16)

    u = jax.nn.gelu(proj[:, d_pool:d_pool + d_sgu])
    v = jax.nn.gelu(proj[:, d_pool + d_sgu:])
    for h in range(d_sgu // SGU_HEAD_DIM):
        l0, l1 = h * SGU_HEAD_DIM, (h + 1) * SGU_HEAD_DIM
        vn = _ln_rows(v[:, l0:l1], sg_ref[h:h + 1, :], sb_ref[h:h + 1, :])
        for t in range(T):
            vn_ref[t, :, l0:l1] = vn[t * NB:(t + 1) * NB]
    for t in range(T):
        mixed = brow_ref[t:t + 1, :]
        for s in range(t + 1):
            mixed = mixed + wrow_ref[t, s:s + 1, :] * vn_ref[s]
        mixin_ref[t * NB:(t + 1) * NB, d_pool:] = (u[t * NB:(t + 1) * NB] * mixed).astype(BF16)


def _mixer_body(alpha, n_pt, tiles_per_seq, xp_ref, xs_ref, st_ref, win_ref, band_ref, wpool_ref, pscale_ref,
                sg_ref, sb_ref, ws_ref, bsb_ref, wrow_ref, brow_ref, wout_ref, g1_ref, b1_ref,
                rwh_ref, rwl_ref, rb_ref, upper_ref, lstrict_ref,
                h1_ref, h1b_ref, pool_ref, a_ref, vn_ref, dest_ref, wsel_ref, cnt_ref,
                zcat_ref, mixin_ref, hprev_ref):
    i = pl.program_id(0)
    n_tiles = pl.num_programs(0) - 1

    def route_previous_tile():
        h = hprev_ref[...]
        _route_tile(h, h.astype(BF16), rwh_ref, rwl_ref, rb_ref, upper_ref, lstrict_ref,
                    dest_ref, wsel_ref, cnt_ref)

    @pl.when(i == 0)
    def _():
        zcat_ref[0:HALO, :] = jnp.zeros((HALO, zcat_ref.shape[1]), BF16)
        hprev_ref[...] = jnp.zeros(hprev_ref.shape, F32)

    @pl.when(i < n_pt)
    def _():
        x = xp_ref[0]
        proj = _dot(x.astype(BF16), win_ref[...])
        route_previous_tile()
        _prompt_tile(i % tiles_per_seq, tiles_per_seq, x, proj, band_ref, wpool_ref, pscale_ref, sg_ref, sb_ref,
                     ws_ref, bsb_ref, pool_ref, zcat_ref, mixin_ref)
        h1_ref[...] = alpha * x

    @pl.when(i >= n_pt)
    def _():
        route_previous_tile()

    @pl.when((i >= n_pt) & (i < n_tiles))
    def _():
        T, NB, D = xs_ref.shape
        x = xs_ref[...].reshape(T * NB, D)
        proj = _dot(x.astype(BF16), win_ref[...])
        _sample_tile(T, NB, proj, st_ref, wpool_ref, pscale_ref, sg_ref, sb_ref, wrow_ref, brow_ref,
                     a_ref, vn_ref, zcat_ref, mixin_ref)
        h1_ref[...] = alpha * x

    @pl.when(i < n_tiles)
    def _():
        mix = _dot(mixin_ref[...], wout_ref[...])
        h1 = _ln_rows(h1_ref[...] + mix, g1_ref[...], b1_ref[...])
        h1_ref[...] = h1
        h1b_ref[...] = h1.astype(BF16)
        hprev_ref[...] = h1


def _pool_band(tm):
    t = np.arange(tm)[:, None] + HALO
    s = np.arange(tm + HALO)[None, :]
    return jnp.asarray(np.stack([((s <= t) & (s >= t - w + 1)) for w in POOL_WINDOWS]).astype(np.float32), BF16)


def _mixer(xp, xs_t, st_t, win_b, wpool_b, pscale, sg, sb, ws, bsb, wrow, brow, wout_b, g1, b1, rwt, rb_col, alpha):
    B, T, D = xp.shape
    TS, NS, _ = xs_t.shape
    d_pool = st_t.shape[2]
    d_sgu = wrow.shape[2]
    tps = T // TB
    n_pt = B * tps
    NB = TB // TS
    n_st = NS // NB
    n = B * T + NS * TS
    nblk = n_pt + n_st
    pt = lambda i: jnp.minimum(i, n_pt - 1)
    stile = lambda i: jnp.clip(i - n_pt, 0, n_st - 1)
    blk = lambda i: jnp.minimum(i, nblk - 1)
    routed = lambda i: jnp.maximum(i - 1, 0)
    body = functools.partial(_mixer_body, alpha, n_pt, tps)
    rw_hi = rwt.astype(BF16)
    rw_lo = (rwt - rw_hi.astype(F32)).astype(BF16)
    upper = jnp.asarray(np.triu(np.ones((TB, TB), np.float32), 1), BF16)
    lstrict = jnp.asarray(np.tril(np.ones((N_EXPERTS, N_EXPERTS), np.float32), -1), BF16)
    consts = (win_b, _pool_band(TB), wpool_b, pscale, sg, sb, ws, bsb, wrow, brow, wout_b, g1, b1,
              rw_hi, rw_lo, rb_col, upper, lstrict)
    return pl.pallas_call(
        body,
        grid=(nblk + 1,),
        in_specs=[
            pl.BlockSpec((1, TB, D), lambda i: (pt(i) // tps, pt(i) % tps, 0)),
            pl.BlockSpec((TS, NB, D), lambda i: (0, stile(i), 0), pipeline_mode=pl.Buffered(1)),
            pl.BlockSpec((POOL_BUF, NB, d_pool), lambda i: (0, stile(i), 0), pipeline_mode=pl.Buffered(1)),
        ] + [_const_spec(c.shape) for c in consts],
        out_specs=[
            pl.BlockSpec((TB, D), lambda i: (blk(i), 0)),
            pl.BlockSpec((TB, D), lambda i: (blk(i), 0)),
            pl.BlockSpec((1, HALO, d_pool), lambda i: (pt(i) // tps, 0, 0)),
            pl.BlockSpec((TS, NB, d_pool), lambda i: (0, stile(i), 0)),
            pl.BlockSpec((TS, NB, d_sgu), lambda i: (0, stile(i), 0)),
            pl.BlockSpec((1, TOP_K, TB), lambda i: (routed(i), 0, 0)),
            pl.BlockSpec((1, TOP_K, TB), lambda i: (routed(i), 0, 0)),
            pl.BlockSpec((1, N_EXPERTS, 128), lambda i: (routed(i), 0, 0)),
        ],
        out_shape=[
            jax.ShapeDtypeStruct((n, D), F32),
            jax.ShapeDtypeStruct((n, D), BF16),
            jax.ShapeDtypeStruct((B, HALO, d_pool), F32),
            jax.ShapeDtypeStruct((TS, NS, d_pool), F32),
            jax.ShapeDtypeStruct((TS, NS, d_sgu), F32),
            jax.ShapeDtypeStruct((nblk, TOP_K, TB), I32),
            jax.ShapeDtypeStruct((nblk, TOP_K, TB), F32),
            jax.ShapeDtypeStruct((nblk, N_EXPERTS, 128), I32),
        ],
        scratch_shapes=[
            pltpu.VMEM((TB + HALO, d_pool), BF16),
            pltpu.VMEM((TB, D), BF16),
            pltpu.VMEM((TB, D), F32),
        ],
        compiler_params=pltpu.CompilerParams(
            dimension_semantics=("arbitrary",), vmem_limit_bytes=VMEM_LIMIT),
        name="mixer",
    )(xp, xs_t, st_t, *consts)


def _route_tile(h1, h1b, rwh_ref, rwl_ref, rb_ref, upper_ref, lstrict_ref, dest_ref, wsel_ref, cnt_ref):
    per_group = N_EXPERTS // N_EXPERT_GROUPS
    neg = -jnp.inf
    nt = (((1,), (1,)), ((), ()))
    h_lo = (h1 - h1b.astype(F32)).astype(BF16)
    rwh = rwh_ref[...]
    logits_t = (lax.dot_general(rwh, h1b, nt, preferred_element_type=F32)
                + lax.dot_general(rwh, h_lo, nt, preferred_element_type=F32)
                + lax.dot_general(rwl_ref[...], h1b, nt, preferred_element_type=F32))
    s_t = jax.nn.sigmoid(logits_t)
    b_t = s_t + rb_ref[...]

    io_g = lax.broadcasted_iota(I32, (per_group, TB), 0)
    gs = []
    for g in range(N_EXPERT_GROUPS):
        xg = b_t[g * per_group:(g + 1) * per_group, :]
        m1 = jnp.max(xg, axis=0, keepdims=True)
        i1 = jnp.min(jnp.where(xg == m1, io_g, per_group), axis=0, keepdims=True)
        m2 = jnp.max(jnp.where(io_g == i1, neg, xg), axis=0, keepdims=True)
        gs.append(m1 + m2)
    masked = []
    for g in range(N_EXPERT_GROUPS):
        rank = jnp.zeros((1, TB), F32)
        for g2 in range(N_EXPERT_GROUPS):
            if g2 != g:
                ahead = (gs[g2] >= gs[g]) if g2 < g else (gs[g2] > gs[g])
                rank = rank + jnp.where(ahead, 1.0, 0.0)
        keep = rank < float(TOPK_GROUPS)
        masked.append(jnp.where(keep, b_t[g * per_group:(g + 1) * per_group, :], neg))
    xm = jnp.concatenate(masked, axis=0)

    io_e = lax.broadcasted_iota(I32, (N_EXPERTS, TB), 0)
    onehots = []
    sel = jnp.zeros((N_EXPERTS, TB), F32)
    for _ in range(TOP_K):
        m = jnp.max(xm, axis=0, keepdims=True)
        idx = jnp.min(jnp.where(xm == m, io_e, N_EXPERTS), axis=0, keepdims=True)
        oh = io_e == idx
        onehots.append(oh)
        sel = jnp.where(oh, 1.0, sel)
        xm = jnp.where(oh, neg, xm)

    ssel = sel * s_t
    denom = jnp.sum(ssel, axis=0, keepdims=True)
    comb = ssel / denom * ROUTE_SCALE

    rank_t = _dot(sel.astype(BF16), upper_ref[...])
    cnt = jnp.sum(sel, axis=1, keepdims=True)
    cnt_i = cnt.astype(I32)
    seg16 = ((cnt_i + (SEG - 1)) // SEG).astype(F32)
    off16 = _dot(lstrict_ref[...], jnp.broadcast_to(seg16, (N_EXPERTS, 128)).astype(BF16))
    d_t = rank_t + off16[:, 0:1] * float(SEG)
    for k in range(TOP_K):
        dest_ref[0, k:k + 1, :] = jnp.sum(jnp.where(onehots[k], d_t, 0.0), axis=0, keepdims=True).astype(I32)
        wsel_ref[0, k:k + 1, :] = jnp.sum(jnp.where(onehots[k], comb, 0.0), axis=0, keepdims=True)
    cnt_ref[0] = jnp.broadcast_to(cnt_i, (N_EXPERTS, 128))


def _start_listed_copies(b, table_l, table_g, counts, rows, make_copy):
    width = table_l.shape[0] // counts.shape[0]

    def one(j, c):
        idx = b * width + j
        make_copy(pl.multiple_of(table_l[idx], rows), pl.multiple_of(table_g[idx], SEGG), rows).start()
        return c

    lax.fori_loop(0, counts[b], one, 0)


def _wait_row_units(count, max_count, make_wait_copy):
    for k in range(int(max_count).bit_length()):
        @pl.when(((count >> k) & 1) == 1)
        def _():
            make_wait_copy(SEGG << k).wait()


def _dispatch_body(big_l, big_g, n_big, small_l, small_g, n_small, units, nch, tail_g, tail_n,
                   dest_ref, wsel_ref, h_ref, tokid_ref, ones_ref,
                   xs_hbm, src_ref, wrow_ref, xloc_ref, zero_ref, sems, tail_sem):
    b = pl.program_id(0)
    nb = pl.num_programs(0)
    max_units = xloc_ref.shape[1] // SEGG
    slot = b % 2

    def seg_copy(sl):
        return lambda l, g, rows: pltpu.make_async_copy(
            xloc_ref.at[sl, pl.ds(l, rows), :], xs_hbm.at[pl.ds(g, rows), :], sems.at[sl])

    def wait_block(count, sl):
        _wait_row_units(count, max_units, lambda rows: pltpu.make_async_copy(
            xloc_ref.at[sl, pl.ds(0, rows), :], xs_hbm.at[pl.ds(0, rows), :], sems.at[sl]))

    @pl.when(b >= 2)
    def _():
        wait_block(units[jnp.maximum(b - 2, 0)], slot)

    dest = dest_ref[0]
    wsel = wsel_ref[0]
    h = h_ref[...]
    src_ref[...] = jnp.zeros(src_ref.shape, F32)
    wrow_ref[...] = jnp.zeros(wrow_ref.shape, F32)

    row_in_chunk = lax.broadcasted_iota(I32, (KC, TB), 0).astype(F32).astype(BF16)
    wsel_b = wsel.astype(BF16)
    one = jnp.ones((1, TB), BF16)

    def chunk(c):
        r0 = pl.multiple_of(c * KC, KC)
        rel = (dest - r0).astype(F32).astype(BF16)
        sb = jnp.zeros((KC, TB), BF16)
        wm = jnp.zeros((KC, TB), BF16)
        for k in range(TOP_K):
            hit = rel[k:k + 1, :] == row_in_chunk
            sb = jnp.where(hit, one, sb)
            wm = jnp.where(hit, wsel_b[k:k + 1, :], wm)
        xloc_ref[slot, pl.ds(r0, KC), :] = _dot(sb, h).astype(BF16)
        src_ref[0, pl.ds(r0, KC), :] = _dot(sb, tokid_ref[...])
        wrow_ref[0, pl.ds(r0, KC), :] = _dot(wm, ones_ref[...])

    def chunk_pair(cp, carry):
        chunk(2 * cp)
        chunk(2 * cp + 1)
        return carry

    lax.fori_loop(0, (nch[b] + 1) // 2, chunk_pair, 0)

    _start_listed_copies(b, big_l, big_g, n_big, SEG, seg_copy(slot))
    _start_listed_copies(b, small_l, small_g, n_small, SEGG, seg_copy(slot))

    def tail_copy(g):
        return pltpu.make_async_copy(zero_ref, xs_hbm.at[pl.ds(g, SEGG), :], tail_sem)

    def tails(wait):
        def per_expert(e, carry):
            def per_granule(j, c):
                cp = tail_copy(pl.multiple_of(tail_g[e] + j * SEGG, SEGG))
                if wait:
                    cp.wait()
                else:
                    cp.start()
                return c
            lax.fori_loop(0, tail_n[e], per_granule, 0)
            return carry
        lax.fori_loop(0, N_EXPERTS, per_expert, 0)

    @pl.when(b == nb - 1)
    def _():
        zero_ref[...] = jnp.zeros(zero_ref.shape, BF16)
        tails(False)

        @pl.when(b >= 1)
        def _():
            wait_block(units[jnp.maximum(b - 1, 0)], 1 - slot)

        wait_block(units[b], slot)
        tails(True)


def _dispatch(meta, dest, wsel, h1b_all, n_rows_sorted, m_out_max):
    n, D = h1b_all.shape
    nb = n // TB
    tokid = jnp.asarray(np.broadcast_to(np.arange(TB, dtype=np.float32)[:, None], (TB, 128)), BF16)
    ones = jnp.ones((TB, 128), BF16)
    cs = lambda shape: pl.BlockSpec(shape, lambda i, *_: (0,) * len(shape), pipeline_mode=pl.Buffered(1))
    grid_spec = pltpu.PrefetchScalarGridSpec(
        num_scalar_prefetch=10,
        grid=(nb,),
        in_specs=[
            pl.BlockSpec((1, TOP_K, TB), lambda i, *_: (i, 0, 0)),
            pl.BlockSpec((1, TOP_K, TB), lambda i, *_: (i, 0, 0)),
            pl.BlockSpec((TB, D), lambda i, *_: (i, 0)),
            cs(tokid.shape), cs(ones.shape),
        ],
        out_specs=[
            pl.BlockSpec(memory_space=pl.ANY),
            pl.BlockSpec((1, m_out_max, 128), lambda i, *_: (i, 0, 0)),
            pl.BlockSpec((1, m_out_max, 128), lambda i, *_: (i, 0, 0)),
        ],
        scratch_shapes=[
            pltpu.VMEM((2, m_out_max, D), BF16),
            pltpu.VMEM((SEGG, D), BF16),
            pltpu.SemaphoreType.DMA((2,)),
            pltpu.SemaphoreType.DMA(()),
        ],
    )
    return pl.pallas_call(
        _dispatch_body,
        grid_spec=grid_spec,
        out_shape=[
            jax.ShapeDtypeStruct((n_rows_sorted, D), BF16),
            jax.ShapeDtypeStruct((nb, m_out_max, 128), F32),
            jax.ShapeDtypeStruct((nb, m_out_max, 128), F32),
        ],
        compiler_params=pltpu.CompilerParams(
            dimension_semantics=("arbitrary",), vmem_limit_bytes=VMEM_LIMIT, has_side_effects=True),
        name="dispatch",
    )(meta["big_l"], meta["big_g"], meta["n_big"], meta["small_l"], meta["small_g"], meta["n_small"],
      meta["units"], meta["nch"], meta["tail_g"], meta["tail_n"], dest, wsel, h1b_all, tokid, ones)


def _ffn_body(tile_e, tile_src, tile_rows, e_slot, e_next, x_ref, wg_hbm, wu_hbm, wd_hbm, y_ref,
              wg_f, wu_f, wd_f, wgu_b, wd_b, sems):
    i = pl.program_id(0)
    f = wg_f.shape[2]
    e = tile_e[i]
    e_prev = tile_e[jnp.maximum(i - 1, 0)]

    def weight_copies(expert, slot):
        return (pltpu.make_async_copy(wg_hbm.at[expert], wg_f.at[slot], sems.at[slot, 0]),
                pltpu.make_async_copy(wu_hbm.at[expert], wu_f.at[slot], sems.at[slot, 1]),
                pltpu.make_async_copy(wd_hbm.at[expert], wd_f.at[slot], sems.at[slot, 2]))

    @pl.when((i == 0) | (e != e_prev))
    def _():
        slot = e_slot[e]

        @pl.when(i == 0)
        def _():
            for cp in weight_copies(e, slot):
                cp.start()

        for cp in weight_copies(e, slot):
            cp.wait()
        nxt = e_next[e]

        @pl.when(nxt >= 0)
        def _():
            for cp in weight_copies(nxt, 1 - slot):
                cp.start(priority=1)

        wgu_b[:, :f] = wg_f[slot].astype(BF16)
        wgu_b[:, f:] = wu_f[slot].astype(BF16)
        wd_b[...] = wd_f[slot].astype(BF16)

    def swiglu(x):
        gu = _dot(x, wgu_b[...])
        act = (jax.nn.silu(gu[:, :f]) * gu[:, f:]).astype(BF16)
        return _dot(act, wd_b[...]).astype(BF16)

    @pl.when(tile_rows[i] == TR)
    def _():
        y_ref[...] = swiglu(x_ref[...])

    @pl.when(tile_rows[i] == TRC)
    def _():
        y_ref[0:TRC, :] = swiglu(x_ref[0:TRC, :])
        y_ref[TRC:, :] = jnp.zeros((TR - TRC, y_ref.shape[1]), BF16)


def _expert_ffn(meta, xs, w_gate, w_up, w_down):
    R, D = xs.shape
    E, _, f = w_gate.shape
    nt = R // TR
    grid_spec = pltpu.PrefetchScalarGridSpec(
        num_scalar_prefetch=5,
        grid=(nt,),
        in_specs=[
            pl.BlockSpec((TR, D), lambda i, te, ts, *_: (ts[i], 0)),
            pl.BlockSpec(memory_space=pl.ANY),
            pl.BlockSpec(memory_space=pl.ANY),
            pl.BlockSpec(memory_space=pl.ANY),
        ],
        out_specs=pl.BlockSpec((TR, D), lambda i, te, ts, *_: (ts[i], 0)),
        scratch_shapes=[
            pltpu.VMEM((2, D, f), F32),
            pltpu.VMEM((2, D, f), F32),
            pltpu.VMEM((2, f, D), F32),
            pltpu.VMEM((D, 2 * f), BF16),
            pltpu.VMEM((f, D), BF16),
            pltpu.SemaphoreType.DMA((2, 3)),
        ],
    )
    return pl.pallas_call(
        _ffn_body,
        grid_spec=grid_spec,
        out_shape=jax.ShapeDtypeStruct((R, D), BF16),
        compiler_params=pltpu.CompilerParams(
            dimension_semantics=("arbitrary",), vmem_limit_bytes=VMEM_LIMIT),
        name="expert_ffn",
    )(meta["tile_e"], meta["tile_src"], meta["tile_rows"], meta["e_slot"], meta["e_next"],
      xs, w_gate, w_up, w_down)


def _combine_body(alpha, n_pb, big_l, big_g, n_big, small_l, small_g, n_small, chunk_units, nch,
                  src_ref, wrow_ref, h1_ref, h1b_ref, p_ref,
                  wsg_ref, wsu_ref, wsd_ref, g2_ref, b2_ref, wpg_ref, bpg_ref, wpe_ref, ys_hbm,
                  yp_ref, ysm_ref, yloc_ref, acc_ref, sems):
    b = pl.program_id(0)
    n_chunks = nch[b]
    max_chunks = sems.shape[0]

    def seg_copy(l, g, rows):
        return pltpu.make_async_copy(ys_hbm.at[pl.ds(g, rows), :], yloc_ref.at[pl.ds(l, rows), :],
                                     sems.at[l // KCC])

    def fetch_block(blk):
        _start_listed_copies(blk, big_l, big_g, n_big, SEG, seg_copy)
        _start_listed_copies(blk, small_l, small_g, n_small, SEGG, seg_copy)

    @pl.when(b == 0)
    def _():
        yloc_ref[...] = jnp.zeros(yloc_ref.shape, BF16)
        fetch_block(b)

    hb = h1b_ref[...]
    sh = (jax.nn.silu(_dot(hb, wsg_ref[...])) * _dot(hb, wsu_ref[...])).astype(BF16)
    acc_ref[...] = _dot(sh, wsd_ref[...])

    def chunk(c, carry):
        def wait_rows(rows):
            return pltpu.make_async_copy(ys_hbm.at[pl.ds(0, rows), :], yloc_ref.at[pl.ds(0, rows), :], sems.at[c])

        _wait_row_units(chunk_units[b * max_chunks + c], KCC // SEGG, wait_rows)

        r0 = pl.multiple_of(c * KCC, KCC)
        src = src_ref[0, pl.ds(r0, KCC), :]
        w = wrow_ref[0, pl.ds(r0, KCC), :]
        lane = lax.broadcasted_iota(I32, src.shape, 1).astype(F32)
        wm = jnp.concatenate([jnp.where(src == lane + float(o), w, 0.0) for o in range(0, TB, src.shape[1])],
                             axis=1).astype(BF16)
        acc_ref[...] += lax.dot_general(wm, yloc_ref[pl.ds(r0, KCC), :],
                                        (((0,), (0,)), ((), ())), preferred_element_type=F32)
        return carry

    lax.fori_loop(0, n_chunks, chunk, 0)

    @pl.when(b + 1 < pl.num_programs(0))
    def _():
        fetch_block(b + 1)

    h2 = _ln_rows(alpha * h1_ref[...] + acc_ref[...], g2_ref[...], b2_ref[...])
    gate = jax.nn.sigmoid(_dot(h2.astype(BF16), wpg_ref[...]) + bpg_ref[...])
    pe = _dot(p_ref[...].astype(BF16), wpe_ref[...])
    y = h2 + gate * pe

    @pl.when(b < n_pb)
    def _():
        yp_ref[...] = y

    @pl.when(b >= n_pb)
    def _():
        ysm_ref[...] = y


def _combine(meta, src_rep, w_rep, h1_all, h1b_all, p_all, wsg_b, wsu_b, wsd_b, g2, b2, wpg_b, bpg, wpe_b, ys,
             m_out_max, n_prompt, alpha):
    n, D = h1_all.shape
    nb = n // TB
    n_pb = n_prompt // TB
    d_pe = p_all.shape[1]
    cs = lambda shape: pl.BlockSpec(shape, lambda i, *_: (0,) * len(shape), pipeline_mode=pl.Buffered(1))
    grid_spec = pltpu.PrefetchScalarGridSpec(
        num_scalar_prefetch=8,
        grid=(nb,),
        in_specs=[
            pl.BlockSpec((1, m_out_max, 128), lambda i, *_: (i, 0, 0)),
            pl.BlockSpec((1, m_out_max, 128), lambda i, *_: (i, 0, 0)),
            pl.BlockSpec((TB, D), lambda i, *_: (i, 0)),
            pl.BlockSpec((TB, D), lambda i, *_: (i, 0)),
            pl.BlockSpec((TB, d_pe), lambda i, *_: (i, 0)),
            cs(wsg_b.shape), cs(wsu_b.shape), cs(wsd_b.shape), cs(g2.shape), cs(b2.shape),
            cs(wpg_b.shape), cs(bpg.shape), cs(wpe_b.shape),
            pl.BlockSpec(memory_space=pl.ANY),
        ],
        out_specs=[
            pl.BlockSpec((TB, D), lambda i, *_: (jnp.minimum(i, n_pb - 1), 0)),
            pl.BlockSpec((TB, D), lambda i, *_: (jnp.maximum(i - n_pb, 0), 0)),
        ],
        scratch_shapes=[
            pltpu.VMEM((m_out_max, D), BF16),
            pltpu.VMEM((TB, D), F32),
            pltpu.SemaphoreType.DMA((m_out_max // KCC,)),
        ],
    )
    return pl.pallas_call(
        functools.partial(_combine_body, alpha, n_pb),
        grid_spec=grid_spec,
        out_shape=[jax.ShapeDtypeStruct((n_prompt, D), F32), jax.ShapeDtypeStruct((n - n_prompt, D), F32)],
        compiler_params=pltpu.CompilerParams(
            dimension_semantics=("arbitrary",), vmem_limit_bytes=VMEM_LIMIT),
        name="combine",
    )(meta["big_l"], meta["big_g"], meta["n_big"], meta["small_l"], meta["small_g"], meta["n_small"],
      meta["chunk_units"], meta["nchc"], src_rep, w_rep, h1_all, h1b_all, p_all,
      wsg_b, wsu_b, wsd_b, g2, b2, wpg_b, bpg, wpe_b, ys)


def _sort_meta(cnt, n_tiles_max, m_out_max):
    segl = (cnt + (SEG - 1)) // SEG * SEG
    loc = jnp.cumsum(segl, axis=1) - segl
    n8 = (cnt + (SEGG - 1)) // SEGG
    segp = n8 * SEGG
    before = jnp.cumsum(segp, axis=0) - segp
    total = jnp.sum(segp, axis=0)
    total_c = (total + (TRC - 1)) // TRC * TRC
    total_p = (total + (TR - 1)) // TR * TR
    ends = jnp.cumsum(total_p)
    base = ends - total_p
    glob = base[None, :] + before
    n_valid = ends[-1] // TR
    tiles = jnp.arange(n_tiles_max, dtype=I32)
    tile_src = jnp.minimum(tiles, n_valid - 1)
    tile_e = jnp.minimum(jnp.sum((ends[None, :] <= (tile_src * TR)[:, None]).astype(I32), axis=1), N_EXPERTS - 1)
    left = jnp.sum(jnp.where(tile_e[:, None] == jnp.arange(N_EXPERTS, dtype=I32)[None, :],
                             (base + total_c)[None, :], 0), axis=1) - tile_src * TR
    tile_rows = jnp.where(tiles < n_valid, jnp.clip(left, 0, TR), 0)
    experts = jnp.arange(N_EXPERTS, dtype=I32)
    active = total_p > 0
    later = active[None, :] & (experts[None, :] > experts[:, None])
    e_next = jnp.min(jnp.where(later, experts[None, :], N_EXPERTS), axis=1)
    def copy_list(count, first_l, first_g, width):
        end = jnp.cumsum(count, axis=1)
        start = end - count
        j = jnp.arange(width, dtype=I32)[None, :, None]
        in_seg = (start[:, None, :] <= j) & (j < end[:, None, :])
        step = (j - start[:, None, :]) * SEG
        lst_l = jnp.sum(jnp.where(in_seg, first_l[:, None, :] + step, 0), axis=2)
        lst_g = jnp.sum(jnp.where(in_seg, first_g[:, None, :] + step, 0), axis=2)
        return lst_l, lst_g, end[:, -1], j[:, :, 0] < end[:, -1:]

    n_big, n_small = n8 // 2, n8 % 2
    big_l, big_g, big_n, big_ok = copy_list(n_big, loc, glob, m_out_max // SEG)
    small_l, small_g, small_n, small_ok = copy_list(n_small, loc + n_big * SEG, glob + n_big * SEG, N_EXPERTS)
    chunks = jnp.arange(m_out_max // KCC, dtype=I32)[None, :, None]
    chunk_units = (jnp.sum(jnp.where(big_ok[:, None, :] & (big_l[:, None, :] // KCC == chunks), SEG // SEGG, 0), axis=2)
                   + jnp.sum(jnp.where(small_ok[:, None, :] & (small_l[:, None, :] // KCC == chunks), 1, 0), axis=2))
    local_rows = jnp.sum(segl, axis=1)
    return {
        "big_l": big_l.reshape(-1).astype(I32),
        "big_g": big_g.reshape(-1).astype(I32),
        "n_big": big_n.astype(I32),
        "small_l": small_l.reshape(-1).astype(I32),
        "small_g": small_g.reshape(-1).astype(I32),
        "n_small": small_n.astype(I32),
        "units": jnp.sum(n8, axis=1).astype(I32),
        "chunk_units": chunk_units.reshape(-1).astype(I32),
        "nch": ((local_rows + (KC - 1)) // KC).astype(I32),
        "nchc": ((local_rows + (KCC - 1)) // KCC).astype(I32),
        "tail_g": (base + total).astype(I32),
        "tail_n": ((total_c - total) // SEGG).astype(I32),
        "tile_e": tile_e.astype(I32),
        "tile_src": tile_src.astype(I32),
        "tile_rows": tile_rows.astype(I32),
        "e_slot": ((jnp.cumsum(active.astype(I32)) - 1) % 2).astype(I32),
        "e_next": jnp.where(e_next < N_EXPERTS, e_next, -1).astype(I32),
    }


def _layer(xp, xs, st, pp, ps, w_in, w_pool, pool_scale, sgu_ln_g, sgu_ln_b, w_s, b_s, w_out, ln1_g, ln1_b,
           router_w, router_bias, w_gate, w_up, w_down, ws_gate, ws_up, ws_down, ln2_g, ln2_b, w_pe, w_pgate,
           b_pgate, alpha):
    B, T, D = xp.shape
    NS, TS, _ = xs.shape
    n_prompt, n_sample = B * T, NS * TS
    n = n_prompt + n_sample
    n_heads = w_s.shape[0]
    row = lambda v: v.reshape(1, -1)

    x_t = jnp.transpose(xs, (1, 0, 2))
    st_t = jnp.transpose(st, (1, 0, 2))
    bsb = jnp.broadcast_to(b_s[:, :, None], (n_heads, CHUNK, SGU_HEAD_DIM))
    wrow = jnp.repeat(jnp.transpose(w_s[:, :TS, :TS], (1, 2, 0)), SGU_HEAD_DIM, axis=2)
    brow = jnp.repeat(jnp.transpose(b_s[:, :TS], (1, 0)), SGU_HEAD_DIM, axis=1)
    h1_all, h1b_all, pool_p, a_s, vn_s, dest, wsel, cnt = _mixer(
        xp, x_t, st_t, w_in.astype(BF16), w_pool.astype(BF16), row(pool_scale), sgu_ln_g, sgu_ln_b, w_s, bsb,
        wrow, brow, w_out.astype(BF16), row(ln1_g), row(ln1_b), jnp.transpose(router_w),
        router_bias.reshape(-1, 1), alpha)
    nb = n // TB
    m_out_max = TB * TOP_K + N_EXPERTS * SEG
    rows_max = n * TOP_K + nb * N_EXPERTS * (SEGG - 1) + N_EXPERTS * (TR - SEGG)
    n_tiles_max = -(-rows_max // TR)
    meta = _sort_meta(cnt[:, :, 0], n_tiles_max, m_out_max)
    xsort, src_rep, w_rep = _dispatch(meta, dest, wsel, h1b_all, n_tiles_max * TR, m_out_max)
    ysort = _expert_ffn(meta, xsort, w_gate, w_up, w_down)

    nbs = TB // TS
    ps_t = jnp.transpose(ps.reshape(NS // nbs, nbs, TS, -1), (0, 2, 1, 3)).reshape(n_sample, -1)
    p_all = jnp.concatenate([pp.reshape(n_prompt, -1), ps_t], axis=0)
    y_p, y_s = _combine(meta, src_rep, w_rep, h1_all, h1b_all, p_all, ws_gate.astype(BF16), ws_up.astype(BF16),
                        ws_down.astype(BF16), row(ln2_g), row(ln2_b), w_pgate.astype(BF16), row(b_pgate),
                        w_pe.astype(BF16), ysort, m_out_max, n_prompt, alpha)

    yp = y_p.reshape(B, T, D)
    ys = jnp.transpose(y_s.reshape(NS // nbs, TS, nbs, D), (0, 2, 1, 3)).reshape(NS, TS, D)
    new_pool_p = pool_p[:, HALO - POOL_BUF:, :]
    new_pool_s = jnp.concatenate([st, jnp.transpose(a_s, (1, 0, 2))], axis=1)[:, -POOL_BUF:]
    vn = jnp.transpose(vn_s, (1, 0, 2)).reshape(NS, TS, n_heads, SGU_HEAD_DIM)
    return yp, ys, new_pool_p, new_pool_s, vn


def kernel(x_prompt, x_sample, state_pool, p_prompt, p_sample, w_in, w_pool, pool_scale, sgu_ln_g, sgu_ln_b, w_s, b_s, w_out, ln1_g, ln1_b, router_w, router_bias, w_gate, w_up, w_down, ws_gate, ws_up, ws_down, ln2_g, ln2_b, w_pe, w_pgate, b_pgate):
    depth = w_in.shape[0]
    alpha = (2.0 * depth) ** 0.25
    hp, hs = x_prompt, x_sample
    pool_p, pool_s, v_s = [], [], []
    for i in range(depth):
        hp, hs, bp, bs, vs = _layer(
            hp, hs, state_pool[i], p_prompt[i], p_sample[i], w_in[i], w_pool[i], pool_scale[i], sgu_ln_g[i],
            sgu_ln_b[i], w_s[i], b_s[i], w_out[i], ln1_g[i], ln1_b[i], router_w[i], router_bias[i], w_gate[i],
            w_up[i], w_down[i], ws_gate[i], ws_up[i], ws_down[i], ln2_g[i], ln2_b[i], w_pe[i], w_pgate[i],
            b_pgate[i], alpha)
        pool_p.append(bp)
        pool_s.append(bs)
        v_s.append(vs)
    return hp, hs, jnp.stack(pool_p), jnp.stack(pool_s), jnp.stack(v_s)
```

```python
import functools

import jax
import jax.numpy as jnp
import numpy as np
from jax import lax
from jax.experimental import pallas as pl
from jax.experimental.pallas import tpu as pltpu

F32 = jnp.float32
BF16 = jnp.bfloat16
I32 = jnp.int32

POOL_WINDOWS = (2, 4, 8, 16)
POOL_BUF = max(POOL_WINDOWS) - 1
CHUNK = 128
SGU_HEAD_DIM = 128
N_EXPERTS = 64
TOP_K = 8
N_EXPERT_GROUPS = 8
TOPK_GROUPS = 4
ROUTE_SCALE = 2.5
LN_EPS = 1e-5
PAST_LEN = 16384

TB = 256
HALO = 16
TR = 512
TRC = 256
SEG = 16
SEGG = 8
KC = 256
KCC = 512
VMEM_LIMIT = 56 * 1024 * 1024


def _const_spec(shape):
    n = len(shape)
    return pl.BlockSpec(shape, lambda *_: (0,) * n, pipeline_mode=pl.Buffered(1))


def _ln_rows(x, g, b):
    mu = jnp.mean(x, axis=-1, keepdims=True)
    xc = x - mu
    var = jnp.mean(xc * xc, axis=-1, keepdims=True)
    return xc * lax.rsqrt(var + LN_EPS) * g + b


def _dot(a, b):
    return jnp.dot(a, b, preferred_element_type=F32)


def _prompt_tile(j, n_tiles, x, proj, band_ref, wpool_ref, pscale_ref, sg_ref, sb_ref, ws_ref, bsb_ref,
                 pool_ref, zcat_ref, mixin_ref):
    d_pool = zcat_ref.shape[1]
    d_pool_g = d_pool // len(POOL_WINDOWS)
    d_sgu = mixin_ref.shape[1] - d_pool

    a = proj[:, :d_pool]
    zcat_ref[HALO:, :] = a.astype(BF16)

    pos1 = (j * TB + lax.broadcasted_iota(I32, (TB, 1), 0) + 1).astype(F32)
    for g, w in enumerate(POOL_WINDOWS):
        c0, c1 = g * d_pool_g, (g + 1) * d_pool_g
        s = _dot(band_ref[g], zcat_ref[:, c0:c1])
        cnt = jnp.minimum(pos1, float(w))
        d = s / cnt - a[:, c0:c1]
        y = _dot(d.astype(BF16), wpool_ref[g])
        mixin_ref[:, c0:c1] = (y * pscale_ref[:, c0:c1]).astype(BF16)
    zcat_ref[0:HALO, :] = jnp.where(j == n_tiles - 1, jnp.zeros((HALO, d_pool), BF16), zcat_ref[TB:TB + HALO, :])
    pool_ref[0] = a[TB - HALO:, :]

    u = jax.nn.gelu(proj[:, d_pool:d_pool + d_sgu])
    v = jax.nn.gelu(proj[:, d_pool + d_sgu:])
    r = lax.broadcasted_iota(I32, (CHUNK, CHUNK), 0)
    c = lax.broadcasted_iota(I32, (CHUNK, CHUNK), 1)
    tril = r >= c
    for h in range(d_sgu // SGU_HEAD_DIM):
        l0, l1 = h * SGU_HEAD_DIM, (h + 1) * SGU_HEAD_DIM
        vn = _ln_rows(v[:, l0:l1], sg_ref[h:h + 1, :], sb_ref[h:h + 1, :]).astype(BF16)
        wst = jnp.where(tril, ws_ref[h], 0.0).astype(BF16)
        n_ch = TB // CHUNK
        mixed_all = _dot(wst, jnp.concatenate([vn[ci * CHUNK:(ci + 1) * CHUNK] for ci in range(n_ch)], axis=1))
        for ci in range(n_ch):
            r0, r1 = ci * CHUNK, (ci + 1) * CHUNK
            mixed = mixed_all[:, ci * SGU_HEAD_DIM:(ci + 1) * SGU_HEAD_DIM] + bsb_ref[h]
            mixin_ref[r0:r1, d_pool + l0:d_pool + l1] = (u[r0:r1, l0:l1] * mixed).astype(BF16)


def _sample_tile(T, NB, proj, st_ref, wpool_ref, pscale_ref, sg_ref, sb_ref, wrow_ref, brow_ref,
                 a_ref, vn_ref, dbuf_ref, mixin_ref):
    d_pool = dbuf_ref.shape[1]
    d_pool_g = d_pool // len(POOL_WINDOWS)
    d_sgu = mixin_ref.shape[1] - d_pool

    a = proj[:, :d_pool]
    for t in range(T):
        a_ref[t] = a[t * NB:(t + 1) * NB]

    for g, w in enumerate(POOL_WINDOWS):
        c0, c1 = g * d_pool_g, (g + 1) * d_pool_g
        for t in range(T):
            cnt = float(min(PAST_LEN + t + 1, w))
            acc = None
            for s in range(POOL_BUF + t - w + 1, POOL_BUF + t + 1):
                if s < POOL_BUF:
                    term = st_ref[s, :, c0:c1]
                else:
                    term = a[(s - POOL_BUF) * NB:(s - POOL_BUF + 1) * NB, c0:c1]
                acc = term if acc is None else acc + term
            d = acc / cnt - a[t * NB:(t + 1) * NB, c0:c1]
            dbuf_ref[t * NB:(t + 1) * NB, c0:c1] = d.astype(BF16)
        y = _dot(dbuf_ref[0:T * NB, c0:c1], wpool_ref[g])
        mixin_ref[:, c0:c1] = (y * pscale_ref[:, c0:c1]).astype(BF16)

    u = jax.nn.gelu(proj[:, d_pool:d_pool + d_sgu])
    v = jax.nn.gelu(proj[:, d_pool + d_sgu:])
    for h in range(d_sgu // SGU_HEAD_DIM):
        l0, l1 = h * SGU_HEAD_DIM, (h + 1) * SGU_HEAD_DIM
        vn = _ln_rows(v[:, l0:l1], sg_ref[h:h + 1, :], sb_ref[h:h + 1, :])
        for t in range(T):
            vn_ref[t, :, l0:l1] = vn[t * NB:(t + 1) * NB]
    for t in range(T):
        mixed = brow_ref[t:t + 1, :]
        for s in range(t + 1):
            mixed = mixed + wrow_ref[t, s:s + 1, :] * vn_ref[s]
        mixin_ref[t * NB:(t + 1) * NB, d_pool:] = (u[t * NB:(t + 1) * NB] * mixed).astype(BF16)


def _mixer_body(alpha, n_pt, tiles_per_seq, xp_ref, xs_ref, st_ref, win_ref, band_ref, wpool_ref, pscale_ref,
                sg_ref, sb_ref, ws_ref, bsb_ref, wrow_ref, brow_ref, wout_ref, g1_ref, b1_ref,
                rwh_ref, rwl_ref, rb_ref, upper_ref, lstrict_ref,
                h1_ref, h1b_ref, pool_ref, a_ref, vn_ref, dest_ref, wsel_ref, cnt_ref,
                zcat_ref, mixin_ref, hprev_ref):
    i = pl.program_id(0)
    n_tiles = pl.num_programs(0) - 1

    def route_previous_tile():
        h = hprev_ref[...]
        _route_tile(h, h.astype(BF16), rwh_ref, rwl_ref, rb_ref, upper_ref, lstrict_ref,
                    dest_ref, wsel_ref, cnt_ref)

    @pl.when(i == 0)
    def _():
        zcat_ref[0:HALO, :] = jnp.zeros((HALO, zcat_ref.shape[1]), BF16)
        hprev_ref[...] = jnp.zeros(hprev_ref.shape, F32)

    @pl.when(i < n_pt)
    def _():
        x = xp_ref[0]
        proj = _dot(x.astype(BF16), win_ref[...])
        route_previous_tile()
        _prompt_tile(i % tiles_per_seq, tiles_per_seq, x, proj, band_ref, wpool_ref, pscale_ref, sg_ref, sb_ref,
                     ws_ref, bsb_ref, pool_ref, zcat_ref, mixin_ref)
        h1_ref[...] = alpha * x

    @pl.when(i >= n_pt)
    def _():
        route_previous_tile()

    @pl.when((i >= n_pt) & (i < n_tiles))
    def _():
        T, NB, D = xs_ref.shape
        x = xs_ref[...].reshape(T * NB, D)
        proj = _dot(x.astype(BF16), win_ref[...])
        _sample_tile(T, NB, proj, st_ref, wpool_ref, pscale_ref, sg_ref, sb_ref, wrow_ref, brow_ref,
                     a_ref, vn_ref, zcat_ref, mixin_ref)
        h1_ref[...] = alpha * x

    @pl.when(i < n_tiles)
    def _():
        mix = _dot(mixin_ref[...], wout_ref[...])
        h1 = _ln_rows(h1_ref[...] + mix, g1_ref[...], b1_ref[...])
        h1_ref[...] = h1
        h1b_ref[...] = h1.astype(BF16)
        hprev_ref[...] = h1


def _pool_band(tm):
    t = np.arange(tm)[:, None] + HALO
    s = np.arange(tm + HALO)[None, :]
    return jnp.asarray(np.stack([((s <= t) & (s >= t - w + 1)) for w in POOL_WINDOWS]).astype(np.float32), BF16)


def _mixer(xp, xs_t, st_t, win_b, wpool_b, pscale, sg, sb, ws, bsb, wrow, brow, wout_b, g1, b1, rwt, rb_col, alpha):
    B, T, D = xp.shape
    TS, NS, _ = xs_t.shape
    d_pool = st_t.shape[2]
    d_sgu = wrow.shape[2]
    tps = T // TB
    n_pt = B * tps
    NB = TB // TS
    n_st = NS // NB
    n = B * T + NS * TS
    nblk = n_pt + n_st
    pt = lambda i: jnp.minimum(i, n_pt - 1)
    stile = lambda i: jnp.clip(i - n_pt, 0, n_st - 1)
    blk = lambda i: jnp.minimum(i, nblk - 1)
    routed = lambda i: jnp.maximum(i - 1, 0)
    body = functools.partial(_mixer_body, alpha, n_pt, tps)
    rw_hi = rwt.astype(BF16)
    rw_lo = (rwt - rw_hi.astype(F32)).astype(BF16)
    upper = jnp.asarray(np.triu(np.ones((TB, TB), np.float32), 1), BF16)
    lstrict = jnp.asarray(np.tril(np.ones((N_EXPERTS, N_EXPERTS), np.float32), -1), BF16)
    consts = (win_b, _pool_band(TB), wpool_b, pscale, sg, sb, ws, bsb, wrow, brow, wout_b, g1, b1,
              rw_hi, rw_lo, rb_col, upper, lstrict)
    return pl.pallas_call(
        body,
        grid=(nblk + 1,),
        in_specs=[
            pl.BlockSpec((1, TB, D), lambda i: (pt(i) // tps, pt(i) % tps, 0)),
            pl.BlockSpec((TS, NB, D), lambda i: (0, stile(i), 0), pipeline_mode=pl.Buffered(1)),
            pl.BlockSpec((POOL_BUF, NB, d_pool), lambda i: (0, stile(i), 0), pipeline_mode=pl.Buffered(1)),
        ] + [_const_spec(c.shape) for c in consts],
        out_specs=[
            pl.BlockSpec((TB, D), lambda i: (blk(i), 0)),
            pl.BlockSpec((TB, D), lambda i: (blk(i), 0)),
            pl.BlockSpec((1, HALO, d_pool), lambda i: (pt(i) // tps, 0, 0)),
            pl.BlockSpec((TS, NB, d_pool), lambda i: (0, stile(i), 0)),
            pl.BlockSpec((TS, NB, d_sgu), lambda i: (0, stile(i), 0)),
            pl.BlockSpec((1, TOP_K, TB), lambda i: (routed(i), 0, 0)),
            pl.BlockSpec((1, TOP_K, TB), lambda i: (routed(i), 0, 0)),
            pl.BlockSpec((1, N_EXPERTS, 128), lambda i: (routed(i), 0, 0)),
        ],
        out_shape=[
            jax.ShapeDtypeStruct((n, D), F32),
            jax.ShapeDtypeStruct((n, D), BF16),
            jax.ShapeDtypeStruct((B, HALO, d_pool), F32),
            jax.ShapeDtypeStruct((TS, NS, d_pool), F32),
            jax.ShapeDtypeStruct((TS, NS, d_sgu), F32),
            jax.ShapeDtypeStruct((nblk, TOP_K, TB), I32),
            jax.ShapeDtypeStruct((nblk, TOP_K, TB), F32),
            jax.ShapeDtypeStruct((nblk, N_EXPERTS, 128), I32),
        ],
        scratch_shapes=[
            pltpu.VMEM((TB + HALO, d_pool), BF16),
            pltpu.VMEM((TB, D), BF16),
            pltpu.VMEM((TB, D), F32),
        ],
        compiler_params=pltpu.CompilerParams(
            dimension_semantics=("arbitrary",), vmem_limit_bytes=VMEM_LIMIT),
        name="mixer",
    )(xp, xs_t, st_t, *consts)


def _route_tile(h1, h1b, rwh_ref, rwl_ref, rb_ref, upper_ref, lstrict_ref, dest_ref, wsel_ref, cnt_ref):
    per_group = N_EXPERTS // N_EXPERT_GROUPS
    neg = -jnp.inf
    nt = (((1,), (1,)), ((), ()))
    h_lo = (h1 - h1b.astype(F32)).astype(BF16)
    rwh = rwh_ref[...]
    logits_t = (lax.dot_general(rwh, h1b, nt, preferred_element_type=F32)
                + lax.dot_general(rwh, h_lo, nt, preferred_element_type=F32)
                + lax.dot_general(rwl_ref[...], h1b, nt, preferred_element_type=F32))
    s_t = jax.nn.sigmoid(logits_t)
    b_t = s_t + rb_ref[...]

    io_g = lax.broadcasted_iota(I32, (per_group, TB), 0)
    gs = []
    for g in range(N_EXPERT_GROUPS):
        xg = b_t[g * per_group:(g + 1) * per_group, :]
        m1 = jnp.max(xg, axis=0, keepdims=True)
        i1 = jnp.min(jnp.where(xg == m1, io_g, per_group), axis=0, keepdims=True)
        m2 = jnp.max(jnp.where(io_g == i1, neg, xg), axis=0, keepdims=True)
        gs.append(m1 + m2)
    masked = []
    for g in range(N_EXPERT_GROUPS):
        rank = jnp.zeros((1, TB), F32)
        for g2 in range(N_EXPERT_GROUPS):
            if g2 != g:
                ahead = (gs[g2] >= gs[g]) if g2 < g else (gs[g2] > gs[g])
                rank = rank + jnp.where(ahead, 1.0, 0.0)
        keep = rank < float(TOPK_GROUPS)
        masked.append(jnp.where(keep, b_t[g * per_group:(g + 1) * per_group, :], neg))
    xm = jnp.concatenate(masked, axis=0)

    io_e = lax.broadcasted_iota(I32, (N_EXPERTS, TB), 0)
    onehots = []
    sel = jnp.zeros((N_EXPERTS, TB), F32)
    for _ in range(TOP_K):
        m = jnp.max(xm, axis=0, keepdims=True)
        idx = jnp.min(jnp.where(xm == m, io_e, N_EXPERTS), axis=0, keepdims=True)
        oh = io_e == idx
        onehots.append(oh)
        sel = jnp.where(oh, 1.0, sel)
        xm = jnp.where(oh, neg, xm)

    ssel = sel * s_t
    denom = jnp.sum(ssel, axis=0, keepdims=True)
    comb = ssel / denom * ROUTE_SCALE

    rank_t = _dot(sel.astype(BF16), upper_ref[...])
    cnt = jnp.sum(sel, axis=1, keepdims=True)
    cnt_i = cnt.astype(I32)
    seg16 = ((cnt_i + (SEG - 1)) // SEG).astype(F32)
    off16 = _dot(lstrict_ref[...], jnp.broadcast_to(seg16, (N_EXPERTS, 128)).astype(BF16))
    d_t = rank_t + off16[:, 0:1] * float(SEG)
    for k in range(TOP_K):
        dest_ref[0, k:k + 1, :] = jnp.sum(jnp.where(onehots[k], d_t, 0.0), axis=0, keepdims=True).astype(I32)
        wsel_ref[0, k:k + 1, :] = jnp.sum(jnp.where(onehots[k], comb, 0.0), axis=0, keepdims=True)
    cnt_ref[0] = jnp.broadcast_to(cnt_i, (N_EXPERTS, 128))


def _start_listed_copies(b, table_l, table_g, counts, rows, make_copy):
    width = table_l.shape[0] // counts.shape[0]
    n = counts[b]

    def start(j):
        idx = b * width + j
        make_copy(pl.multiple_of(table_l[idx], rows), pl.multiple_of(table_g[idx], SEGG), rows).start()

    def four(q, c):
        for u in range(4):
            start(4 * q + u)
        return c

    def one(j, c):
        start(j)
        return c

    lax.fori_loop(0, n // 4, four, 0)
    lax.fori_loop(n // 4 * 4, n, one, 0)


def _wait_row_units(count, max_count, make_wait_copy):
    for k in range(int(max_count).bit_length()):
        @pl.when(((count >> k) & 1) == 1)
        def _():
            make_wait_copy(SEGG << k).wait()


def _dispatch_body(big_l, big_g, n_big, small_l, small_g, n_small, units, nch, tail_g, tail_n,
                   dest_ref, wsel_ref, h_ref, tokid_ref, ones_ref,
                   xs_hbm, src_ref, wrow_ref, xloc_ref, zero_ref, sems, tail_sem):
    b = pl.program_id(0)
    nb = pl.num_programs(0)
    max_units = xloc_ref.shape[1] // SEGG
    slot = b % 2

    def seg_copy(sl):
        return lambda l, g, rows: pltpu.make_async_copy(
            xloc_ref.at[sl, pl.ds(l, rows), :], xs_hbm.at[pl.ds(g, rows), :], sems.at[sl])

    def wait_block(count, sl):
        _wait_row_units(count, max_units, lambda rows: pltpu.make_async_copy(
            xloc_ref.at[sl, pl.ds(0, rows), :], xs_hbm.at[pl.ds(0, rows), :], sems.at[sl]))

    @pl.when(b >= 2)
    def _():
        wait_block(units[jnp.maximum(b - 2, 0)], slot)

    dest = dest_ref[0]
    wsel = wsel_ref[0]
    h = h_ref[...]
    src_ref[...] = jnp.zeros(src_ref.shape, F32)
    wrow_ref[...] = jnp.zeros(wrow_ref.shape, F32)

    row_in_chunk = lax.broadcasted_iota(I32, (KC, TB), 0).astype(F32).astype(BF16)
    wsel_b = wsel.astype(BF16)
    one = jnp.ones((1, TB), BF16)

    def chunk(c):
        r0 = pl.multiple_of(c * KC, KC)
        rel = (dest - r0).astype(F32).astype(BF16)
        sb = jnp.zeros((KC, TB), BF16)
        wm = jnp.zeros((KC, TB), BF16)
        for k in range(TOP_K):
            hit = rel[k:k + 1, :] == row_in_chunk
            sb = jnp.where(hit, one, sb)
            wm = jnp.where(hit, wsel_b[k:k + 1, :], wm)
        xloc_ref[slot, pl.ds(r0, KC), :] = _dot(sb, h).astype(BF16)
        src_ref[0, pl.ds(r0, KC), :] = _dot(sb, tokid_ref[...])
        wrow_ref[0, pl.ds(r0, KC), :] = _dot(wm, ones_ref[...])

    def chunk_pair(cp, carry):
        chunk(2 * cp)
        chunk(2 * cp + 1)
        return carry

    lax.fori_loop(0, (nch[b] + 1) // 2, chunk_pair, 0)

    _start_listed_copies(b, big_l, big_g, n_big, SEG, seg_copy(slot))
    _start_listed_copies(b, small_l, small_g, n_small, SEGG, seg_copy(slot))

    def tail_copy(g):
        return pltpu.make_async_copy(zero_ref, xs_hbm.at[pl.ds(g, SEGG), :], tail_sem)

    def tails(wait):
        def per_expert(e, carry):
            def per_granule(j, c):
                cp = tail_copy(pl.multiple_of(tail_g[e] + j * SEGG, SEGG))
                if wait:
                    cp.wait()
                else:
                    cp.start()
                return c
            lax.fori_loop(0, tail_n[e], per_granule, 0)
            return carry
        lax.fori_loop(0, N_EXPERTS, per_expert, 0)

    @pl.when(b == nb - 1)
    def _():
        zero_ref[...] = jnp.zeros(zero_ref.shape, BF16)
        tails(False)

        @pl.when(b >= 1)
        def _():
            wait_block(units[jnp.maximum(b - 1, 0)], 1 - slot)

        wait_block(units[b], slot)
        tails(True)


def _dispatch(meta, dest, wsel, h1b_all, n_rows_sorted, m_out_max):
    n, D = h1b_all.shape
    nb = n // TB
    tokid = jnp.asarray(np.broadcast_to(np.arange(TB, dtype=np.float32)[:, None], (TB, 128)), BF16)
    ones = jnp.ones((TB, 128), BF16)
    cs = lambda shape: pl.BlockSpec(shape, lambda i, *_: (0,) * len(shape), pipeline_mode=pl.Buffered(1))
    grid_spec = pltpu.PrefetchScalarGridSpec(
        num_scalar_prefetch=10,
        grid=(nb,),
        in_specs=[
            pl.BlockSpec((1, TOP_K, TB), lambda i, *_: (i, 0, 0)),
            pl.BlockSpec((1, TOP_K, TB), lambda i, *_: (i, 0, 0)),
            pl.BlockSpec((TB, D), lambda i, *_: (i, 0)),
            cs(tokid.shape), cs(ones.shape),
        ],
        out_specs=[
            pl.BlockSpec(memory_space=pl.ANY),
            pl.BlockSpec((1, m_out_max, 128), lambda i, *_: (i, 0, 0)),
            pl.BlockSpec((1, m_out_max, 128), lambda i, *_: (i, 0, 0)),
        ],
        scratch_shapes=[
            pltpu.VMEM((2, m_out_max, D), BF16),
            pltpu.VMEM((SEGG, D), BF16),
            pltpu.SemaphoreType.DMA((2,)),
            pltpu.SemaphoreType.DMA(()),
        ],
    )
    return pl.pallas_call(
        _dispatch_body,
        grid_spec=grid_spec,
        out_shape=[
            jax.ShapeDtypeStruct((n_rows_sorted, D), BF16),
            jax.ShapeDtypeStruct((nb, m_out_max, 128), F32),
            jax.ShapeDtypeStruct((nb, m_out_max, 128), F32),
        ],
        compiler_params=pltpu.CompilerParams(
            dimension_semantics=("arbitrary",), vmem_limit_bytes=VMEM_LIMIT, has_side_effects=True),
        name="dispatch",
    )(meta["big_l"], meta["big_g"], meta["n_big"], meta["small_l"], meta["small_g"], meta["n_small"],
      meta["units"], meta["nch"], meta["tail_g"], meta["tail_n"], dest, wsel, h1b_all, tokid, ones)


def _ffn_body(tile_e, tile_src, tile_rows, e_slot, e_next, x_ref, wg_hbm, wu_hbm, wd_hbm, y_ref,
              wg_f, wu_f, wd_f, wgu_b, wd_b, sems):
    i = pl.program_id(0)
    f = wg_f.shape[2]
    e = tile_e[i]
    e_prev = tile_e[jnp.maximum(i - 1, 0)]

    def weight_copies(expert, slot):
        return (pltpu.make_async_copy(wg_hbm.at[expert], wg_f.at[slot], sems.at[slot, 0]),
                pltpu.make_async_copy(wu_hbm.at[expert], wu_f.at[slot], sems.at[slot, 1]),
                pltpu.make_async_copy(wd_hbm.at[expert], wd_f.at[slot], sems.at[slot, 2]))

    @pl.when((i == 0) | (e != e_prev))
    def _():
        slot = e_slot[e]

        @pl.when(i == 0)
        def _():
            for cp in weight_copies(e, slot):
                cp.start()

        for cp in weight_copies(e, slot):
            cp.wait()
        nxt = e_next[e]

        @pl.when(nxt >= 0)
        def _():
            for cp in weight_copies(nxt, 1 - slot):
                cp.start(priority=1)

        wgu_b[:, :f] = wg_f[slot].astype(BF16)
        wgu_b[:, f:] = wu_f[slot].astype(BF16)
        wd_b[...] = wd_f[slot].astype(BF16)

    def swiglu(x):
        gu = _dot(x, wgu_b[...])
        act = (jax.nn.silu(gu[:, :f]) * gu[:, f:]).astype(BF16)
        return _dot(act, wd_b[...]).astype(BF16)

    @pl.when(tile_rows[i] == TR)
    def _():
        y_ref[...] = swiglu(x_ref[...])

    @pl.when(tile_rows[i] == TRC)
    def _():
        y_ref[0:TRC, :] = swiglu(x_ref[0:TRC, :])
        y_ref[TRC:, :] = jnp.zeros((TR - TRC, y_ref.shape[1]), BF16)


def _expert_ffn(meta, xs, w_gate, w_up, w_down):
    R, D = xs.shape
    E, _, f = w_gate.shape
    nt = R // TR
    grid_spec = pltpu.PrefetchScalarGridSpec(
        num_scalar_prefetch=5,
        grid=(nt,),
        in_specs=[
            pl.BlockSpec((TR, D), lambda i, te, ts, *_: (ts[i], 0)),
            pl.BlockSpec(memory_space=pl.ANY),
            pl.BlockSpec(memory_space=pl.ANY),
            pl.BlockSpec(memory_space=pl.ANY),
        ],
        out_specs=pl.BlockSpec((TR, D), lambda i, te, ts, *_: (ts[i], 0)),
        scratch_shapes=[
            pltpu.VMEM((2, D, f), F32),
            pltpu.VMEM((2, D, f), F32),
            pltpu.VMEM((2, f, D), F32),
            pltpu.VMEM((D, 2 * f), BF16),
            pltpu.VMEM((f, D), BF16),
            pltpu.SemaphoreType.DMA((2, 3)),
        ],
    )
    return pl.pallas_call(
        _ffn_body,
        grid_spec=grid_spec,
        out_shape=jax.ShapeDtypeStruct((R, D), BF16),
        compiler_params=pltpu.CompilerParams(
            dimension_semantics=("arbitrary",), vmem_limit_bytes=VMEM_LIMIT),
        name="expert_ffn",
    )(meta["tile_e"], meta["tile_src"], meta["tile_rows"], meta["e_slot"], meta["e_next"],
      xs, w_gate, w_up, w_down)


def _combine_body(alpha, n_pb, big_l, big_g, n_big, small_l, small_g, n_small, chunk_units, nch,
                  src_ref, wrow_ref, h1_ref, h1b_ref, pp_ref, ps_ref,
                  wsg_ref, wsu_ref, wsd_ref, g2_ref, b2_ref, wpg_ref, bpg_ref, wpe_ref, ys_hbm,
                  yp_ref, ysm_ref, yloc_ref, acc_ref, sems):
    b = pl.program_id(0)
    n_chunks = nch[b]
    max_chunks = sems.shape[0]

    def seg_copy(l, g, rows):
        return pltpu.make_async_copy(ys_hbm.at[pl.ds(g, rows), :], yloc_ref.at[pl.ds(l, rows), :],
                                     sems.at[l // KCC])

    def fetch_block(blk):
        _start_listed_copies(blk, big_l, big_g, n_big, SEG, seg_copy)
        _start_listed_copies(blk, small_l, small_g, n_small, SEGG, seg_copy)

    @pl.when(b == 0)
    def _():
        yloc_ref[...] = jnp.zeros(yloc_ref.shape, BF16)
        fetch_block(b)

    hb = h1b_ref[...]
    sh = (jax.nn.silu(_dot(hb, wsg_ref[...])) * _dot(hb, wsu_ref[...])).astype(BF16)
    acc_ref[...] = _dot(sh, wsd_ref[...])

    def chunk(c, carry):
        def wait_rows(rows):
            return pltpu.make_async_copy(ys_hbm.at[pl.ds(0, rows), :], yloc_ref.at[pl.ds(0, rows), :], sems.at[c])

        _wait_row_units(chunk_units[b * max_chunks + c], KCC // SEGG, wait_rows)

        r0 = pl.multiple_of(c * KCC, KCC)
        src = src_ref[0, pl.ds(r0, KCC), :]
        w = wrow_ref[0, pl.ds(r0, KCC), :]
        lane = lax.broadcasted_iota(I32, src.shape, 1).astype(F32)
        wm = jnp.concatenate([jnp.where(src == lane + float(o), w, 0.0) for o in range(0, TB, src.shape[1])],
                             axis=1).astype(BF16)
        acc_ref[...] += lax.dot_general(wm, yloc_ref[pl.ds(r0, KCC), :],
                                        (((0,), (0,)), ((), ())), preferred_element_type=F32)
        return carry

    lax.fori_loop(0, n_chunks, chunk, 0)

    @pl.when(b + 1 < pl.num_programs(0))
    def _():
        fetch_block(b + 1)

    h2 = _ln_rows(alpha * h1_ref[...] + acc_ref[...], g2_ref[...], b2_ref[...])
    gate = jax.nn.sigmoid(_dot(h2.astype(BF16), wpg_ref[...]) + bpg_ref[...])
    p = jnp.where(b < n_pb, pp_ref[...], ps_ref[...])
    pe = _dot(p.astype(BF16), wpe_ref[...])
    y = h2 + gate * pe

    @pl.when(b < n_pb)
    def _():
        yp_ref[...] = y

    @pl.when(b >= n_pb)
    def _():
        ysm_ref[...] = y


def _combine(meta, src_rep, w_rep, h1_all, h1b_all, p_prompt, p_sample, wsg_b, wsu_b, wsd_b, g2, b2, wpg_b, bpg,
             wpe_b, ys, m_out_max, n_prompt, alpha):
    n, D = h1_all.shape
    nb = n // TB
    n_pb = n_prompt // TB
    d_pe = p_prompt.shape[1]
    cs = lambda shape: pl.BlockSpec(shape, lambda i, *_: (0,) * len(shape), pipeline_mode=pl.Buffered(1))
    grid_spec = pltpu.PrefetchScalarGridSpec(
        num_scalar_prefetch=8,
        grid=(nb,),
        in_specs=[
            pl.BlockSpec((1, m_out_max, 128), lambda i, *_: (i, 0, 0)),
            pl.BlockSpec((1, m_out_max, 128), lambda i, *_: (i, 0, 0)),
            pl.BlockSpec((TB, D), lambda i, *_: (i, 0)),
            pl.BlockSpec((TB, D), lambda i, *_: (i, 0)),
            pl.BlockSpec((TB, d_pe), lambda i, *_: (jnp.minimum(i, n_pb - 1), 0)),
            pl.BlockSpec((TB, d_pe), lambda i, *_: (jnp.maximum(i - n_pb, 0), 0)),
            cs(wsg_b.shape), cs(wsu_b.shape), cs(wsd_b.shape), cs(g2.shape), cs(b2.shape),
            cs(wpg_b.shape), cs(bpg.shape), cs(wpe_b.shape),
            pl.BlockSpec(memory_space=pl.ANY),
        ],
        out_specs=[
            pl.BlockSpec((TB, D), lambda i, *_: (jnp.minimum(i, n_pb - 1), 0)),
            pl.BlockSpec((TB, D), lambda i, *_: (jnp.maximum(i - n_pb, 0), 0)),
        ],
        scratch_shapes=[
            pltpu.VMEM((m_out_max, D), BF16),
            pltpu.VMEM((TB, D), F32),
            pltpu.SemaphoreType.DMA((m_out_max // KCC,)),
        ],
    )
    return pl.pallas_call(
        functools.partial(_combine_body, alpha, n_pb),
        grid_spec=grid_spec,
        out_shape=[jax.ShapeDtypeStruct((n_prompt, D), F32), jax.ShapeDtypeStruct((n - n_prompt, D), F32)],
        compiler_params=pltpu.CompilerParams(
            dimension_semantics=("arbitrary",), vmem_limit_bytes=VMEM_LIMIT),
        name="combine",
    )(meta["big_l"], meta["big_g"], meta["n_big"], meta["small_l"], meta["small_g"], meta["n_small"],
      meta["chunk_units"], meta["nchc"], src_rep, w_rep, h1_all, h1b_all, p_prompt, p_sample,
      wsg_b, wsu_b, wsd_b, g2, b2, wpg_b, bpg, wpe_b, ys)


def _sort_meta(cnt, n_tiles_max, m_out_max):
    segl = (cnt + (SEG - 1)) // SEG * SEG
    loc = jnp.cumsum(segl, axis=1) - segl
    n8 = (cnt + (SEGG - 1)) // SEGG
    segp = n8 * SEGG
    before = jnp.cumsum(segp, axis=0) - segp
    total = jnp.sum(segp, axis=0)
    total_c = (total + (TRC - 1)) // TRC * TRC
    total_p = (total + (TR - 1)) // TR * TR
    ends = jnp.cumsum(total_p)
    base = ends - total_p
    glob = base[None, :] + before
    n_valid = ends[-1] // TR
    tiles = jnp.arange(n_tiles_max, dtype=I32)
    tile_src = jnp.minimum(tiles, n_valid - 1)
    tile_e = jnp.minimum(jnp.sum((ends[None, :] <= (tile_src * TR)[:, None]).astype(I32), axis=1), N_EXPERTS - 1)
    left = jnp.sum(jnp.where(tile_e[:, None] == jnp.arange(N_EXPERTS, dtype=I32)[None, :],
                             (base + total_c)[None, :], 0), axis=1) - tile_src * TR
    tile_rows = jnp.where(tiles < n_valid, jnp.clip(left, 0, TR), 0)
    experts = jnp.arange(N_EXPERTS, dtype=I32)
    active = total_p > 0
    later = active[None, :] & (experts[None, :] > experts[:, None])
    e_next = jnp.min(jnp.where(later, experts[None, :], N_EXPERTS), axis=1)
    def copy_list(count, first_l, first_g, width):
        end = jnp.cumsum(count, axis=1)
        start = end - count
        j = jnp.arange(width, dtype=I32)[None, :, None]
        in_seg = (start[:, None, :] <= j) & (j < end[:, None, :])
        step = (j - start[:, None, :]) * SEG
        lst_l = jnp.sum(jnp.where(in_seg, first_l[:, None, :] + step, 0), axis=2)
        lst_g = jnp.sum(jnp.where(in_seg, first_g[:, None, :] + step, 0), axis=2)
        return lst_l, lst_g, end[:, -1], j[:, :, 0] < end[:, -1:]

    n_big, n_small = n8 // 2, n8 % 2
    big_l, big_g, big_n, big_ok = copy_list(n_big, loc, glob, m_out_max // SEG)
    small_l, small_g, small_n, small_ok = copy_list(n_small, loc + n_big * SEG, glob + n_big * SEG, N_EXPERTS)
    chunks = jnp.arange(m_out_max // KCC, dtype=I32)[None, :, None]
    chunk_units = (jnp.sum(jnp.where(big_ok[:, None, :] & (big_l[:, None, :] // KCC == chunks), SEG // SEGG, 0), axis=2)
                   + jnp.sum(jnp.where(small_ok[:, None, :] & (small_l[:, None, :] // KCC == chunks), 1, 0), axis=2))
    local_rows = jnp.sum(segl, axis=1)
    return {
        "big_l": big_l.reshape(-1).astype(I32),
        "big_g": big_g.reshape(-1).astype(I32),
        "n_big": big_n.astype(I32),
        "small_l": small_l.reshape(-1).astype(I32),
        "small_g": small_g.reshape(-1).astype(I32),
        "n_small": small_n.astype(I32),
        "units": jnp.sum(n8, axis=1).astype(I32),
        "chunk_units": chunk_units.reshape(-1).astype(I32),
        "nch": ((local_rows + (KC - 1)) // KC).astype(I32),
        "nchc": ((local_rows + (KCC - 1)) // KCC).astype(I32),
        "tail_g": (base + total).astype(I32),
        "tail_n": ((total_c - total) // SEGG).astype(I32),
        "tile_e": tile_e.astype(I32),
        "tile_src": tile_src.astype(I32),
        "tile_rows": tile_rows.astype(I32),
        "e_slot": ((jnp.cumsum(active.astype(I32)) - 1) % 2).astype(I32),
        "e_next": jnp.where(e_next < N_EXPERTS, e_next, -1).astype(I32),
    }


def _layer(xp, xs, st, pp, ps, w_in, w_pool, pool_scale, sgu_ln_g, sgu_ln_b, w_s, b_s, w_out, ln1_g, ln1_b,
           router_w, router_bias, w_gate, w_up, w_down, ws_gate, ws_up, ws_down, ln2_g, ln2_b, w_pe, w_pgate,
           b_pgate, alpha):
    B, T, D = xp.shape
    NS, TS, _ = xs.shape
    n_prompt, n_sample = B * T, NS * TS
    n = n_prompt + n_sample
    n_heads = w_s.shape[0]
    row = lambda v: v.reshape(1, -1)

    x_t = jnp.transpose(xs, (1, 0, 2))
    st_t = jnp.transpose(st, (1, 0, 2))
    bsb = jnp.broadcast_to(b_s[:, :, None], (n_heads, CHUNK, SGU_HEAD_DIM))
    wrow = jnp.repeat(jnp.transpose(w_s[:, :TS, :TS], (1, 2, 0)), SGU_HEAD_DIM, axis=2)
    brow = jnp.repeat(jnp.transpose(b_s[:, :TS], (1, 0)), SGU_HEAD_DIM, axis=1)
    h1_all, h1b_all, pool_p, a_s, vn_s, dest, wsel, cnt = _mixer(
        xp, x_t, st_t, w_in.astype(BF16), w_pool.astype(BF16), row(pool_scale), sgu_ln_g, sgu_ln_b, w_s, bsb,
        wrow, brow, w_out.astype(BF16), row(ln1_g), row(ln1_b), jnp.transpose(router_w),
        router_bias.reshape(-1, 1), alpha)
    nb = n // TB
    m_out_max = TB * TOP_K + N_EXPERTS * SEG
    rows_max = n * TOP_K + nb * N_EXPERTS * (SEGG - 1) + N_EXPERTS * (TR - SEGG)
    n_tiles_max = -(-rows_max // TR)
    meta = _sort_meta(cnt[:, :, 0], n_tiles_max, m_out_max)
    xsort, src_rep, w_rep = _dispatch(meta, dest, wsel, h1b_all, n_tiles_max * TR, m_out_max)
    ysort = _expert_ffn(meta, xsort, w_gate, w_up, w_down)

    nbs = TB // TS
    ps_t = jnp.transpose(ps.reshape(NS // nbs, nbs, TS, -1), (0, 2, 1, 3)).reshape(n_sample, -1)
    y_p, y_s = _combine(meta, src_rep, w_rep, h1_all, h1b_all, pp.reshape(n_prompt, -1), ps_t,
                        ws_gate.astype(BF16), ws_up.astype(BF16),
                        ws_down.astype(BF16), row(ln2_g), row(ln2_b), w_pgate.astype(BF16), row(b_pgate),
                        w_pe.astype(BF16), ysort, m_out_max, n_prompt, alpha)

    yp = y_p.reshape(B, T, D)
    ys = jnp.transpose(y_s.reshape(NS // nbs, TS, nbs, D), (0, 2, 1, 3)).reshape(NS, TS, D)
    new_pool_p = pool_p[:, HALO - POOL_BUF:, :]
    new_pool_s = jnp.concatenate([st, jnp.transpose(a_s, (1, 0, 2))], axis=1)[:, -POOL_BUF:]
    vn = jnp.transpose(vn_s, (1, 0, 2)).reshape(NS, TS, n_heads, SGU_HEAD_DIM)
    return yp, ys, new_pool_p, new_pool_s, vn


def kernel(x_prompt, x_sample, state_pool, p_prompt, p_sample, w_in, w_pool, pool_scale, sgu_ln_g, sgu_ln_b, w_s, b_s, w_out, ln1_g, ln1_b, router_w, router_bias, w_gate, w_up, w_down, ws_gate, ws_up, ws_down, ln2_g, ln2_b, w_pe, w_pgate, b_pgate):
    depth = w_in.shape[0]
    alpha = (2.0 * depth) ** 0.25
    hp, hs = x_prompt, x_sample
    pool_p, pool_s, v_s = [], [], []
    for i in range(depth):
        hp, hs, bp, bs, vs = _layer(
            hp, hs, state_pool[i], p_prompt[i], p_sample[i], w_in[i], w_pool[i], pool_scale[i], sgu_ln_g[i],
            sgu_ln_b[i], w_s[i], b_s[i], w_out[i], ln1_g[i], ln1_b[i], router_w[i], router_bias[i], w_gate[i],
            w_up[i], w_down[i], ws_gate[i], ws_up[i], ws_down[i], ln2_g[i], ln2_b[i], w_pe[i], w_pgate[i],
            b_pgate[i], alpha)
        pool_p.append(bp)
        pool_s.append(bs)
        v_s.append(vs)
    return hp, hs, jnp.stack(pool_p), jnp.stack(pool_s), jnp.stack(v_s)
```

```python
import functools

import jax
import jax.numpy as jnp
import numpy as np
from jax import lax
from jax.experimental import pallas as pl
from jax.experimental.pallas import tpu as pltpu

F32 = jnp.float32
BF16 = jnp.bfloat16
I32 = jnp.int32

POOL_WINDOWS = (2, 4, 8, 16)
POOL_BUF = max(POOL_WINDOWS) - 1
CHUNK = 128
SGU_HEAD_DIM = 128
N_EXPERTS = 64
TOP_K = 8
N_EXPERT_GROUPS = 8
TOPK_GROUPS = 4
ROUTE_SCALE = 2.5
LN_EPS = 1e-5
PAST_LEN = 16384

TB = 256
HALO = 16
TR = 512
TRC = 128
SEG = 16
SEGG = 8
KC = 256
KCC = 512
VMEM_LIMIT = 56 * 1024 * 1024


def _const_spec(shape):
    n = len(shape)
    return pl.BlockSpec(shape, lambda *_: (0,) * n, pipeline_mode=pl.Buffered(1))


def _ln_rows(x, g, b):
    mu = jnp.mean(x, axis=-1, keepdims=True)
    xc = x - mu
    var = jnp.mean(xc * xc, axis=-1, keepdims=True)
    return xc * lax.rsqrt(var + LN_EPS) * g + b


def _dot(a, b):
    return jnp.dot(a, b, preferred_element_type=F32)


def _prompt_tile(j, n_tiles, x, proj, band_ref, wpool_ref, pscale_ref, sg_ref, sb_ref, ws_ref, bsb_ref,
                 pool_ref, zcat_ref, mixin_ref):
    d_pool = zcat_ref.shape[1]
    d_pool_g = d_pool // len(POOL_WINDOWS)
    d_sgu = mixin_ref.shape[1] - d_pool

    a = proj[:, :d_pool]
    zcat_ref[HALO:, :] = a.astype(BF16)

    pos1 = (j * TB + lax.broadcasted_iota(I32, (TB, 1), 0) + 1).astype(F32)
    for g, w in enumerate(POOL_WINDOWS):
        c0, c1 = g * d_pool_g, (g + 1) * d_pool_g
        s = _dot(band_ref[g], zcat_ref[:, c0:c1])
        cnt = jnp.minimum(pos1, float(w))
        d = s / cnt - a[:, c0:c1]
        y = _dot(d.astype(BF16), wpool_ref[g])
        mixin_ref[:, c0:c1] = (y * pscale_ref[:, c0:c1]).astype(BF16)
    zcat_ref[0:HALO, :] = jnp.where(j == n_tiles - 1, jnp.zeros((HALO, d_pool), BF16), zcat_ref[TB:TB + HALO, :])
    pool_ref[0] = a[TB - HALO:, :]

    u = jax.nn.gelu(proj[:, d_pool:d_pool + d_sgu])
    v = jax.nn.gelu(proj[:, d_pool + d_sgu:])
    r = lax.broadcasted_iota(I32, (CHUNK, CHUNK), 0)
    c = lax.broadcasted_iota(I32, (CHUNK, CHUNK), 1)
    tril = r >= c
    for h in range(d_sgu // SGU_HEAD_DIM):
        l0, l1 = h * SGU_HEAD_DIM, (h + 1) * SGU_HEAD_DIM
        vn = _ln_rows(v[:, l0:l1], sg_ref[h:h + 1, :], sb_ref[h:h + 1, :]).astype(BF16)
        wst = jnp.where(tril, ws_ref[h], 0.0).astype(BF16)
        n_ch = TB // CHUNK
        mixed_all = _dot(wst, jnp.concatenate([vn[ci * CHUNK:(ci + 1) * CHUNK] for ci in range(n_ch)], axis=1))
        for ci in range(n_ch):
            r0, r1 = ci * CHUNK, (ci + 1) * CHUNK
            mixed = mixed_all[:, ci * SGU_HEAD_DIM:(ci + 1) * SGU_HEAD_DIM] + bsb_ref[h]
            mixin_ref[r0:r1, d_pool + l0:d_pool + l1] = (u[r0:r1, l0:l1] * mixed).astype(BF16)


def _sample_tile(T, NB, proj, st_ref, wpool_ref, pscale_ref, sg_ref, sb_ref, wrow_ref, brow_ref,
                 a_ref, vn_ref, dbuf_ref, mixin_ref):
    d_pool = dbuf_ref.shape[1]
    d_pool_g = d_pool // len(POOL_WINDOWS)
    d_sgu = mixin_ref.shape[1] - d_pool

    a = proj[:, :d_pool]
    for t in range(T):
        a_ref[t] = a[t * NB:(t + 1) * NB]

    for g, w in enumerate(POOL_WINDOWS):
        c0, c1 = g * d_pool_g, (g + 1) * d_pool_g
        for t in range(T):
            cnt = float(min(PAST_LEN + t + 1, w))
            acc = None
            for s in range(POOL_BUF + t - w + 1, POOL_BUF + t + 1):
                if s < POOL_BUF:
                    term = st_ref[s, :, c0:c1]
                else:
                    term = a[(s - POOL_BUF) * NB:(s - POOL_BUF + 1) * NB, c0:c1]
                acc = term if acc is None else acc + term
            d = acc / cnt - a[t * NB:(t + 1) * NB, c0:c1]
            dbuf_ref[t * NB:(t + 1) * NB, c0:c1] = d.astype(BF16)
        y = _dot(dbuf_ref[0:T * NB, c0:c1], wpool_ref[g])
        mixin_ref[:, c0:c1] = (y * pscale_ref[:, c0:c1]).astype(BF16)

    u = jax.nn.gelu(proj[:, d_pool:d_pool + d_sgu])
    v = jax.nn.gelu(proj[:, d_pool + d_sgu:])
    for h in range(d_sgu // SGU_HEAD_DIM):
        l0, l1 = h * SGU_HEAD_DIM, (h + 1) * SGU_HEAD_DIM
        vn = _ln_rows(v[:, l0:l1], sg_ref[h:h + 1, :], sb_ref[h:h + 1, :])
        for t in range(T):
            vn_ref[t, :, l0:l1] = vn[t * NB:(t + 1) * NB]
    for t in range(T):
        mixed = brow_ref[t:t + 1, :]
        for s in range(t + 1):
            mixed = mixed + wrow_ref[t, s:s + 1, :] * vn_ref[s]
        mixin_ref[t * NB:(t + 1) * NB, d_pool:] = (u[t * NB:(t + 1) * NB] * mixed).astype(BF16)


def _mixer_body(alpha, n_pt, tiles_per_seq, xp_ref, xs_ref, st_ref, win_ref, band_ref, wpool_ref, pscale_ref,
                sg_ref, sb_ref, ws_ref, bsb_ref, wrow_ref, brow_ref, wout_ref, g1_ref, b1_ref,
                rwh_ref, rwl_ref, rb_ref, upper_ref, lstrict_ref,
                h1_ref, h1b_ref, pool_ref, a_ref, vn_ref, dest_ref, wsel_ref, cnt_ref,
                zcat_ref, mixin_ref, hprev_ref):
    i = pl.program_id(0)
    n_tiles = pl.num_programs(0) - 1

    def route_previous_tile():
        h = hprev_ref[...]
        _route_tile(h, h.astype(BF16), rwh_ref, rwl_ref, rb_ref, upper_ref, lstrict_ref,
                    dest_ref, wsel_ref, cnt_ref)

    @pl.when(i == 0)
    def _():
        zcat_ref[0:HALO, :] = jnp.zeros((HALO, zcat_ref.shape[1]), BF16)
        hprev_ref[...] = jnp.zeros(hprev_ref.shape, F32)

    @pl.when(i < n_pt)
    def _():
        x = xp_ref[0]
        proj = _dot(x.astype(BF16), win_ref[...])
        route_previous_tile()
        _prompt_tile(i % tiles_per_seq, tiles_per_seq, x, proj, band_ref, wpool_ref, pscale_ref, sg_ref, sb_ref,
                     ws_ref, bsb_ref, pool_ref, zcat_ref, mixin_ref)
        h1_ref[...] = alpha * x

    @pl.when(i >= n_pt)
    def _():
        route_previous_tile()

    @pl.when((i >= n_pt) & (i < n_tiles))
    def _():
        T, NB, D = xs_ref.shape
        x = xs_ref[...].reshape(T * NB, D)
        proj = _dot(x.astype(BF16), win_ref[...])
        _sample_tile(T, NB, proj, st_ref, wpool_ref, pscale_ref, sg_ref, sb_ref, wrow_ref, brow_ref,
                     a_ref, vn_ref, zcat_ref, mixin_ref)
        h1_ref[...] = alpha * x

    @pl.when(i < n_tiles)
    def _():
        mix = _dot(mixin_ref[...], wout_ref[...])
        h1 = _ln_rows(h1_ref[...] + mix, g1_ref[...], b1_ref[...])
        h1_ref[...] = h1
        h1b_ref[...] = h1.astype(BF16)
        hprev_ref[...] = h1


def _pool_band(tm):
    t = np.arange(tm)[:, None] + HALO
    s = np.arange(tm + HALO)[None, :]
    return jnp.asarray(np.stack([((s <= t) & (s >= t - w + 1)) for w in POOL_WINDOWS]).astype(np.float32), BF16)


def _mixer(xp, xs_t, st_t, win_b, wpool_b, pscale, sg, sb, ws, bsb, wrow, brow, wout_b, g1, b1, rwt, rb_col, alpha):
    B, T, D = xp.shape
    TS, NS, _ = xs_t.shape
    d_pool = st_t.shape[2]
    d_sgu = wrow.shape[2]
    tps = T // TB
    n_pt = B * tps
    NB = TB // TS
    n_st = NS // NB
    n = B * T + NS * TS
    nblk = n_pt + n_st
    pt = lambda i: jnp.minimum(i, n_pt - 1)
    stile = lambda i: jnp.clip(i - n_pt, 0, n_st - 1)
    blk = lambda i: jnp.minimum(i, nblk - 1)
    routed = lambda i: jnp.maximum(i - 1, 0)
    body = functools.partial(_mixer_body, alpha, n_pt, tps)
    rw_hi = rwt.astype(BF16)
    rw_lo = (rwt - rw_hi.astype(F32)).astype(BF16)
    upper = jnp.asarray(np.triu(np.ones((TB, TB), np.float32), 1), BF16)
    lstrict = jnp.asarray(np.tril(np.ones((N_EXPERTS, N_EXPERTS), np.float32), -1), BF16)
    consts = (win_b, _pool_band(TB), wpool_b, pscale, sg, sb, ws, bsb, wrow, brow, wout_b, g1, b1,
              rw_hi, rw_lo, rb_col, upper, lstrict)
    return pl.pallas_call(
        body,
        grid=(nblk + 1,),
        in_specs=[
            pl.BlockSpec((1, TB, D), lambda i: (pt(i) // tps, pt(i) % tps, 0)),
            pl.BlockSpec((TS, NB, D), lambda i: (0, stile(i), 0), pipeline_mode=pl.Buffered(1)),
            pl.BlockSpec((POOL_BUF, NB, d_pool), lambda i: (0, stile(i), 0), pipeline_mode=pl.Buffered(1)),
        ] + [_const_spec(c.shape) for c in consts],
        out_specs=[
            pl.BlockSpec((TB, D), lambda i: (blk(i), 0)),
            pl.BlockSpec((TB, D), lambda i: (blk(i), 0)),
            pl.BlockSpec((1, HALO, d_pool), lambda i: (pt(i) // tps, 0, 0)),
            pl.BlockSpec((TS, NB, d_pool), lambda i: (0, stile(i), 0)),
            pl.BlockSpec((TS, NB, d_sgu), lambda i: (0, stile(i), 0)),
            pl.BlockSpec((1, TOP_K, TB), lambda i: (routed(i), 0, 0)),
            pl.BlockSpec((1, TOP_K, TB), lambda i: (routed(i), 0, 0)),
            pl.BlockSpec((1, N_EXPERTS, 128), lambda i: (routed(i), 0, 0)),
        ],
        out_shape=[
            jax.ShapeDtypeStruct((n, D), F32),
            jax.ShapeDtypeStruct((n, D), BF16),
            jax.ShapeDtypeStruct((B, HALO, d_pool), F32),
            jax.ShapeDtypeStruct((TS, NS, d_pool), F32),
            jax.ShapeDtypeStruct((TS, NS, d_sgu), F32),
            jax.ShapeDtypeStruct((nblk, TOP_K, TB), I32),
            jax.ShapeDtypeStruct((nblk, TOP_K, TB), F32),
            jax.ShapeDtypeStruct((nblk, N_EXPERTS, 128), I32),
        ],
        scratch_shapes=[
            pltpu.VMEM((TB + HALO, d_pool), BF16),
            pltpu.VMEM((TB, D), BF16),
            pltpu.VMEM((TB, D), F32),
        ],
        compiler_params=pltpu.CompilerParams(
            dimension_semantics=("arbitrary",), vmem_limit_bytes=VMEM_LIMIT),
        name="mixer",
    )(xp, xs_t, st_t, *consts)


def _route_tile(h1, h1b, rwh_ref, rwl_ref, rb_ref, upper_ref, lstrict_ref, dest_ref, wsel_ref, cnt_ref):
    per_group = N_EXPERTS // N_EXPERT_GROUPS
    neg = -jnp.inf
    nt = (((1,), (1,)), ((), ()))
    h_lo = (h1 - h1b.astype(F32)).astype(BF16)
    rwh = rwh_ref[...]
    logits_t = (lax.dot_general(rwh, h1b, nt, preferred_element_type=F32)
                + lax.dot_general(rwh, h_lo, nt, preferred_element_type=F32)
                + lax.dot_general(rwl_ref[...], h1b, nt, preferred_element_type=F32))
    s_t = jax.nn.sigmoid(logits_t)
    b_t = s_t + rb_ref[...]

    io_g = lax.broadcasted_iota(I32, (per_group, TB), 0)
    gs = []
    for g in range(N_EXPERT_GROUPS):
        xg = b_t[g * per_group:(g + 1) * per_group, :]
        m1 = jnp.max(xg, axis=0, keepdims=True)
        i1 = jnp.min(jnp.where(xg == m1, io_g, per_group), axis=0, keepdims=True)
        m2 = jnp.max(jnp.where(io_g == i1, neg, xg), axis=0, keepdims=True)
        gs.append(m1 + m2)
    masked = []
    for g in range(N_EXPERT_GROUPS):
        rank = jnp.zeros((1, TB), F32)
        for g2 in range(N_EXPERT_GROUPS):
            if g2 != g:
                ahead = (gs[g2] >= gs[g]) if g2 < g else (gs[g2] > gs[g])
                rank = rank + jnp.where(ahead, 1.0, 0.0)
        keep = rank < float(TOPK_GROUPS)
        masked.append(jnp.where(keep, b_t[g * per_group:(g + 1) * per_group, :], neg))
    xm = jnp.concatenate(masked, axis=0)

    io_e = lax.broadcasted_iota(I32, (N_EXPERTS, TB), 0)
    onehots = []
    sel = jnp.zeros((N_EXPERTS, TB), F32)
    for _ in range(TOP_K):
        m = jnp.max(xm, axis=0, keepdims=True)
        idx = jnp.min(jnp.where(xm == m, io_e, N_EXPERTS), axis=0, keepdims=True)
        oh = io_e == idx
        onehots.append(oh)
        sel = jnp.where(oh, 1.0, sel)
        xm = jnp.where(oh, neg, xm)

    ssel = sel * s_t
    denom = jnp.sum(ssel, axis=0, keepdims=True)
    comb = ssel / denom * ROUTE_SCALE

    rank_t = _dot(sel.astype(BF16), upper_ref[...])
    cnt = jnp.sum(sel, axis=1, keepdims=True)
    cnt_i = cnt.astype(I32)
    seg16 = ((cnt_i + (SEG - 1)) // SEG).astype(F32)
    off16 = _dot(lstrict_ref[...], jnp.broadcast_to(seg16, (N_EXPERTS, 128)).astype(BF16))
    d_t = rank_t + off16[:, 0:1] * float(SEG)
    for k in range(TOP_K):
        dest_ref[0, k:k + 1, :] = jnp.sum(jnp.where(onehots[k], d_t, 0.0), axis=0, keepdims=True).astype(I32)
        wsel_ref[0, k:k + 1, :] = jnp.sum(jnp.where(onehots[k], comb, 0.0), axis=0, keepdims=True)
    cnt_ref[0] = jnp.broadcast_to(cnt_i, (N_EXPERTS, 128))


def _start_listed_copies(b, table_l, table_g, counts, rows, make_copy):
    width = table_l.shape[0] // counts.shape[0]
    n = counts[b]

    def start(j):
        idx = b * width + j
        make_copy(pl.multiple_of(table_l[idx], rows), pl.multiple_of(table_g[idx], SEGG), rows).start()

    def eight(q, c):
        for u in range(8):
            start(8 * q + u)
        return c

    def one(j, c):
        start(j)
        return c

    lax.fori_loop(0, n // 8, eight, 0)
    lax.fori_loop(n // 8 * 8, n, one, 0)


def _wait_row_units(count, max_count, make_wait_copy):
    for k in range(int(max_count).bit_length()):
        @pl.when(((count >> k) & 1) == 1)
        def _():
            make_wait_copy(SEGG << k).wait()


def _dispatch_body(big_l, big_g, n_big, small_l, small_g, n_small, units, nch, tail_g, tail_n,
                   dest_ref, wsel_ref, h_ref, tokid_ref, ones_ref,
                   xs_hbm, src_ref, wrow_ref, xloc_ref, zero_ref, sems, tail_sem):
    b = pl.program_id(0)
    nb = pl.num_programs(0)
    max_units = xloc_ref.shape[1] // SEGG
    slot = b % 2

    def seg_copy(sl):
        return lambda l, g, rows: pltpu.make_async_copy(
            xloc_ref.at[sl, pl.ds(l, rows), :], xs_hbm.at[pl.ds(g, rows), :], sems.at[sl])

    def wait_block(count, sl):
        _wait_row_units(count, max_units, lambda rows: pltpu.make_async_copy(
            xloc_ref.at[sl, pl.ds(0, rows), :], xs_hbm.at[pl.ds(0, rows), :], sems.at[sl]))

    @pl.when(b >= 2)
    def _():
        wait_block(units[jnp.maximum(b - 2, 0)], slot)

    dest = dest_ref[0]
    wsel = wsel_ref[0]
    h = h_ref[...]
    src_ref[...] = jnp.zeros(src_ref.shape, F32)
    wrow_ref[...] = jnp.zeros(wrow_ref.shape, F32)

    row_in_chunk = lax.broadcasted_iota(I32, (KC, TB), 0).astype(F32).astype(BF16)
    wsel_b = wsel.astype(BF16)
    one = jnp.ones((1, TB), BF16)

    def chunk(c):
        r0 = pl.multiple_of(c * KC, KC)
        rel = (dest - r0).astype(F32).astype(BF16)
        sb = jnp.zeros((KC, TB), BF16)
        wm = jnp.zeros((KC, TB), BF16)
        for k in range(TOP_K):
            hit = rel[k:k + 1, :] == row_in_chunk
            sb = jnp.where(hit, one, sb)
            wm = jnp.where(hit, wsel_b[k:k + 1, :], wm)
        xloc_ref[slot, pl.ds(r0, KC), :] = _dot(sb, h).astype(BF16)
        src_ref[0, pl.ds(r0, KC), :] = _dot(sb, tokid_ref[...])
        wrow_ref[0, pl.ds(r0, KC), :] = _dot(wm, ones_ref[...])

    def chunk_pair(cp, carry):
        chunk(2 * cp)
        chunk(2 * cp + 1)
        return carry

    lax.fori_loop(0, nch[b] // 2, chunk_pair, 0)

    @pl.when(nch[b] % 2 == 1)
    def _():
        chunk(nch[b] - 1)

    _start_listed_copies(b, big_l, big_g, n_big, SEG, seg_copy(slot))
    _start_listed_copies(b, small_l, small_g, n_small, SEGG, seg_copy(slot))

    def tail_copy(g):
        return pltpu.make_async_copy(zero_ref, xs_hbm.at[pl.ds(g, SEGG), :], tail_sem)

    def tails(wait):
        def per_expert(e, carry):
            def per_granule(j, c):
                cp = tail_copy(pl.multiple_of(tail_g[e] + j * SEGG, SEGG))
                if wait:
                    cp.wait()
                else:
                    cp.start()
                return c
            lax.fori_loop(0, tail_n[e], per_granule, 0)
            return carry
        lax.fori_loop(0, N_EXPERTS, per_expert, 0)

    @pl.when(b == nb - 1)
    def _():
        zero_ref[...] = jnp.zeros(zero_ref.shape, BF16)
        tails(False)

        @pl.when(b >= 1)
        def _():
            wait_block(units[jnp.maximum(b - 1, 0)], 1 - slot)

        wait_block(units[b], slot)
        tails(True)


def _dispatch(meta, dest, wsel, h1b_all, n_rows_sorted, m_out_max):
    n, D = h1b_all.shape
    nb = n // TB
    tokid = jnp.asarray(np.broadcast_to(np.arange(TB, dtype=np.float32)[:, None], (TB, 128)), BF16)
    ones = jnp.ones((TB, 128), BF16)
    cs = lambda shape: pl.BlockSpec(shape, lambda i, *_: (0,) * len(shape), pipeline_mode=pl.Buffered(1))
    grid_spec = pltpu.PrefetchScalarGridSpec(
        num_scalar_prefetch=10,
        grid=(nb,),
        in_specs=[
            pl.BlockSpec((1, TOP_K, TB), lambda i, *_: (i, 0, 0)),
            pl.BlockSpec((1, TOP_K, TB), lambda i, *_: (i, 0, 0)),
            pl.BlockSpec((TB, D), lambda i, *_: (i, 0)),
            cs(tokid.shape), cs(ones.shape),
        ],
        out_specs=[
            pl.BlockSpec(memory_space=pl.ANY),
            pl.BlockSpec((1, m_out_max, 128), lambda i, *_: (i, 0, 0)),
            pl.BlockSpec((1, m_out_max, 128), lambda i, *_: (i, 0, 0)),
        ],
        scratch_shapes=[
            pltpu.VMEM((2, m_out_max, D), BF16),
            pltpu.VMEM((SEGG, D), BF16),
            pltpu.SemaphoreType.DMA((2,)),
            pltpu.SemaphoreType.DMA(()),
        ],
    )
    return pl.pallas_call(
        _dispatch_body,
        grid_spec=grid_spec,
        out_shape=[
            jax.ShapeDtypeStruct((n_rows_sorted, D), BF16),
            jax.ShapeDtypeStruct((nb, m_out_max, 128), F32),
            jax.ShapeDtypeStruct((nb, m_out_max, 128), F32),
        ],
        compiler_params=pltpu.CompilerParams(
            dimension_semantics=("arbitrary",), vmem_limit_bytes=VMEM_LIMIT, has_side_effects=True),
        name="dispatch",
    )(meta["big_l"], meta["big_g"], meta["n_big"], meta["small_l"], meta["small_g"], meta["n_small"],
      meta["units"], meta["nch"], meta["tail_g"], meta["tail_n"], dest, wsel, h1b_all, tokid, ones)


def _ffn_body(tile_e, tile_src, tile_rows, e_slot, e_next, x_ref, wg_hbm, wu_hbm, wd_hbm, y_ref,
              wg_f, wu_f, wd_f, wgu_b, wd_b, sems):
    i = pl.program_id(0)
    f = wg_f.shape[2]
    e = tile_e[i]
    e_prev = tile_e[jnp.maximum(i - 1, 0)]

    def weight_copies(expert, slot):
        return (pltpu.make_async_copy(wg_hbm.at[expert], wg_f.at[slot], sems.at[slot, 0]),
                pltpu.make_async_copy(wu_hbm.at[expert], wu_f.at[slot], sems.at[slot, 1]),
                pltpu.make_async_copy(wd_hbm.at[expert], wd_f.at[slot], sems.at[slot, 2]))

    @pl.when((i == 0) | (e != e_prev))
    def _():
        slot = e_slot[e]

        @pl.when(i == 0)
        def _():
            for cp in weight_copies(e, slot):
                cp.start()

        for cp in weight_copies(e, slot):
            cp.wait()
        nxt = e_next[e]

        @pl.when(nxt >= 0)
        def _():
            for cp in weight_copies(nxt, 1 - slot):
                cp.start(priority=1)

        wgu_b[:, :f] = wg_f[slot].astype(BF16)
        wgu_b[:, f:] = wu_f[slot].astype(BF16)
        wd_b[...] = wd_f[slot].astype(BF16)

    def swiglu(x):
        gu = _dot(x, wgu_b[...])
        act = (jax.nn.silu(gu[:, :f]) * gu[:, f:]).astype(BF16)
        return _dot(act, wd_b[...]).astype(BF16)

    for m in range(TRC, TR + 1, TRC):
        @pl.when(tile_rows[i] == m)
        def _(m=m):
            y_ref[0:m, :] = swiglu(x_ref[0:m, :])
            if m < TR:
                y_ref[m:, :] = jnp.zeros((TR - m, y_ref.shape[1]), BF16)


def _expert_ffn(meta, xs, w_gate, w_up, w_down):
    R, D = xs.shape
    E, _, f = w_gate.shape
    nt = R // TR
    grid_spec = pltpu.PrefetchScalarGridSpec(
        num_scalar_prefetch=5,
        grid=(nt,),
        in_specs=[
            pl.BlockSpec((TR, D), lambda i, te, ts, *_: (ts[i], 0)),
            pl.BlockSpec(memory_space=pl.ANY),
            pl.BlockSpec(memory_space=pl.ANY),
            pl.BlockSpec(memory_space=pl.ANY),
        ],
        out_specs=pl.BlockSpec((TR, D), lambda i, te, ts, *_: (ts[i], 0)),
        scratch_shapes=[
            pltpu.VMEM((2, D, f), F32),
            pltpu.VMEM((2, D, f), F32),
            pltpu.VMEM((2, f, D), F32),
            pltpu.VMEM((D, 2 * f), BF16),
            pltpu.VMEM((f, D), BF16),
            pltpu.SemaphoreType.DMA((2, 3)),
        ],
    )
    return pl.pallas_call(
        _ffn_body,
        grid_spec=grid_spec,
        out_shape=jax.ShapeDtypeStruct((R, D), BF16),
        compiler_params=pltpu.CompilerParams(
            dimension_semantics=("arbitrary",), vmem_limit_bytes=VMEM_LIMIT),
        name="expert_ffn",
    )(meta["tile_e"], meta["tile_src"], meta["tile_rows"], meta["e_slot"], meta["e_next"],
      xs, w_gate, w_up, w_down)


def _combine_body(alpha, n_pb, big_l, big_g, n_big, small_l, small_g, n_small, chunk_units, nch,
                  src_ref, wrow_ref, h1_ref, h1b_ref, pp_ref, ps_ref,
                  wsg_ref, wsu_ref, wsd_ref, g2_ref, b2_ref, wpg_ref, bpg_ref, wpe_ref, ys_hbm,
                  yp_ref, ysm_ref, yloc_ref, acc_ref, sems):
    b = pl.program_id(0)
    n_chunks = nch[b]
    max_chunks = sems.shape[0]

    def seg_copy(l, g, rows):
        return pltpu.make_async_copy(ys_hbm.at[pl.ds(g, rows), :], yloc_ref.at[pl.ds(l, rows), :],
                                     sems.at[l // KCC])

    def fetch_block(blk):
        _start_listed_copies(blk, big_l, big_g, n_big, SEG, seg_copy)
        _start_listed_copies(blk, small_l, small_g, n_small, SEGG, seg_copy)

    @pl.when(b == 0)
    def _():
        yloc_ref[...] = jnp.zeros(yloc_ref.shape, BF16)
        fetch_block(b)

    hb = h1b_ref[...]
    sh = (jax.nn.silu(_dot(hb, wsg_ref[...])) * _dot(hb, wsu_ref[...])).astype(BF16)
    acc_ref[...] = _dot(sh, wsd_ref[...])

    def chunk(c, carry):
        def wait_rows(rows):
            return pltpu.make_async_copy(ys_hbm.at[pl.ds(0, rows), :], yloc_ref.at[pl.ds(0, rows), :], sems.at[c])

        _wait_row_units(chunk_units[b * max_chunks + c], KCC // SEGG, wait_rows)

        r0 = pl.multiple_of(c * KCC, KCC)
        src = src_ref[0, pl.ds(r0, KCC), :]
        w = wrow_ref[0, pl.ds(r0, KCC), :]
        lane = lax.broadcasted_iota(I32, src.shape, 1).astype(F32)
        wm = jnp.concatenate([jnp.where(src == lane + float(o), w, 0.0) for o in range(0, TB, src.shape[1])],
                             axis=1).astype(BF16)
        acc_ref[...] += lax.dot_general(wm, yloc_ref[pl.ds(r0, KCC), :],
                                        (((0,), (0,)), ((), ())), preferred_element_type=F32)
        return carry

    lax.fori_loop(0, n_chunks, chunk, 0)

    @pl.when(b + 1 < pl.num_programs(0))
    def _():
        fetch_block(b + 1)

    h2 = _ln_rows(alpha * h1_ref[...] + acc_ref[...], g2_ref[...], b2_ref[...])
    gate = jax.nn.sigmoid(_dot(h2.astype(BF16), wpg_ref[...]) + bpg_ref[...])
    p = jnp.where(b < n_pb, pp_ref[...], ps_ref[...])
    pe = _dot(p.astype(BF16), wpe_ref[...])
    y = h2 + gate * pe

    @pl.when(b < n_pb)
    def _():
        yp_ref[...] = y

    @pl.when(b >= n_pb)
    def _():
        ysm_ref[...] = y


def _combine(meta, src_rep, w_rep, h1_all, h1b_all, p_prompt, p_sample, wsg_b, wsu_b, wsd_b, g2, b2, wpg_b, bpg,
             wpe_b, ys, m_out_max, n_prompt, alpha):
    n, D = h1_all.shape
    nb = n // TB
    n_pb = n_prompt // TB
    d_pe = p_prompt.shape[1]
    cs = lambda shape: pl.BlockSpec(shape, lambda i, *_: (0,) * len(shape), pipeline_mode=pl.Buffered(1))
    grid_spec = pltpu.PrefetchScalarGridSpec(
        num_scalar_prefetch=8,
        grid=(nb,),
        in_specs=[
            pl.BlockSpec((1, m_out_max, 128), lambda i, *_: (i, 0, 0)),
            pl.BlockSpec((1, m_out_max, 128), lambda i, *_: (i, 0, 0)),
            pl.BlockSpec((TB, D), lambda i, *_: (i, 0)),
            pl.BlockSpec((TB, D), lambda i, *_: (i, 0)),
            pl.BlockSpec((TB, d_pe), lambda i, *_: (jnp.minimum(i, n_pb - 1), 0)),
            pl.BlockSpec((TB, d_pe), lambda i, *_: (jnp.maximum(i - n_pb, 0), 0)),
            cs(wsg_b.shape), cs(wsu_b.shape), cs(wsd_b.shape), cs(g2.shape), cs(b2.shape),
            cs(wpg_b.shape), cs(bpg.shape), cs(wpe_b.shape),
            pl.BlockSpec(memory_space=pl.ANY),
        ],
        out_specs=[
            pl.BlockSpec((TB, D), lambda i, *_: (jnp.minimum(i, n_pb - 1), 0)),
            pl.BlockSpec((TB, D), lambda i, *_: (jnp.maximum(i - n_pb, 0), 0)),
        ],
        scratch_shapes=[
            pltpu.VMEM((m_out_max, D), BF16),
            pltpu.VMEM((TB, D), F32),
            pltpu.SemaphoreType.DMA((m_out_max // KCC,)),
        ],
    )
    return pl.pallas_call(
        functools.partial(_combine_body, alpha, n_pb),
        grid_spec=grid_spec,
        out_shape=[jax.ShapeDtypeStruct((n_prompt, D), F32), jax.ShapeDtypeStruct((n - n_prompt, D), F32)],
        compiler_params=pltpu.CompilerParams(
            dimension_semantics=("arbitrary",), vmem_limit_bytes=VMEM_LIMIT),
        name="combine",
    )(meta["big_l"], meta["big_g"], meta["n_big"], meta["small_l"], meta["small_g"], meta["n_small"],
      meta["chunk_units"], meta["nchc"], src_rep, w_rep, h1_all, h1b_all, p_prompt, p_sample,
      wsg_b, wsu_b, wsd_b, g2, b2, wpg_b, bpg, wpe_b, ys)


def _sort_meta(cnt, n_tiles_max, m_out_max):
    segl = (cnt + (SEG - 1)) // SEG * SEG
    loc = jnp.cumsum(segl, axis=1) - segl
    n8 = (cnt + (SEGG - 1)) // SEGG
    segp = n8 * SEGG
    before = jnp.cumsum(segp, axis=0) - segp
    total = jnp.sum(segp, axis=0)
    total_c = (total + (TRC - 1)) // TRC * TRC
    total_p = (total + (TR - 1)) // TR * TR
    ends = jnp.cumsum(total_p)
    base = ends - total_p
    glob = base[None, :] + before
    n_valid = ends[-1] // TR
    tiles = jnp.arange(n_tiles_max, dtype=I32)
    tile_src = jnp.minimum(tiles, n_valid - 1)
    tile_e = jnp.minimum(jnp.sum((ends[None, :] <= (tile_src * TR)[:, None]).astype(I32), axis=1), N_EXPERTS - 1)
    left = jnp.sum(jnp.where(tile_e[:, None] == jnp.arange(N_EXPERTS, dtype=I32)[None, :],
                             (base + total_c)[None, :], 0), axis=1) - tile_src * TR
    tile_rows = jnp.where(tiles < n_valid, jnp.clip(left, 0, TR), 0)
    experts = jnp.arange(N_EXPERTS, dtype=I32)
    active = total_p > 0
    later = active[None, :] & (experts[None, :] > experts[:, None])
    e_next = jnp.min(jnp.where(later, experts[None, :], N_EXPERTS), axis=1)
    def copy_list(count, first_l, first_g, width):
        end = jnp.cumsum(count, axis=1)
        start = end - count
        j = jnp.arange(width, dtype=I32)[None, :, None]
        in_seg = (start[:, None, :] <= j) & (j < end[:, None, :])
        step = (j - start[:, None, :]) * SEG
        lst_l = jnp.sum(jnp.where(in_seg, first_l[:, None, :] + step, 0), axis=2)
        lst_g = jnp.sum(jnp.where(in_seg, first_g[:, None, :] + step, 0), axis=2)
        return lst_l, lst_g, end[:, -1], j[:, :, 0] < end[:, -1:]

    n_big, n_small = n8 // 2, n8 % 2
    big_l, big_g, big_n, big_ok = copy_list(n_big, loc, glob, m_out_max // SEG)
    small_l, small_g, small_n, small_ok = copy_list(n_small, loc + n_big * SEG, glob + n_big * SEG, N_EXPERTS)
    chunks = jnp.arange(m_out_max // KCC, dtype=I32)[None, :, None]
    chunk_units = (jnp.sum(jnp.where(big_ok[:, None, :] & (big_l[:, None, :] // KCC == chunks), SEG // SEGG, 0), axis=2)
                   + jnp.sum(jnp.where(small_ok[:, None, :] & (small_l[:, None, :] // KCC == chunks), 1, 0), axis=2))
    local_rows = jnp.sum(segl, axis=1)
    return {
        "big_l": big_l.reshape(-1).astype(I32),
        "big_g": big_g.reshape(-1).astype(I32),
        "n_big": big_n.astype(I32),
        "small_l": small_l.reshape(-1).astype(I32),
        "small_g": small_g.reshape(-1).astype(I32),
        "n_small": small_n.astype(I32),
        "units": jnp.sum(n8, axis=1).astype(I32),
        "chunk_units": chunk_units.reshape(-1).astype(I32),
        "nch": ((local_rows + (KC - 1)) // KC).astype(I32),
        "nchc": ((local_rows + (KCC - 1)) // KCC).astype(I32),
        "tail_g": (base + total).astype(I32),
        "tail_n": ((total_c - total) // SEGG).astype(I32),
        "tile_e": tile_e.astype(I32),
        "tile_src": tile_src.astype(I32),
        "tile_rows": tile_rows.astype(I32),
        "e_slot": ((jnp.cumsum(active.astype(I32)) - 1) % 2).astype(I32),
        "e_next": jnp.where(e_next < N_EXPERTS, e_next, -1).astype(I32),
    }


def _layer(xp, xs, st, pp, ps, w_in, w_pool, pool_scale, sgu_ln_g, sgu_ln_b, w_s, b_s, w_out, ln1_g, ln1_b,
           router_w, router_bias, w_gate, w_up, w_down, ws_gate, ws_up, ws_down, ln2_g, ln2_b, w_pe, w_pgate,
           b_pgate, alpha):
    B, T, D = xp.shape
    NS, TS, _ = xs.shape
    n_prompt, n_sample = B * T, NS * TS
    n = n_prompt + n_sample
    n_heads = w_s.shape[0]
    row = lambda v: v.reshape(1, -1)

    x_t = jnp.transpose(xs, (1, 0, 2))
    st_t = jnp.transpose(st, (1, 0, 2))
    bsb = jnp.broadcast_to(b_s[:, :, None], (n_heads, CHUNK, SGU_HEAD_DIM))
    wrow = jnp.repeat(jnp.transpose(w_s[:, :TS, :TS], (1, 2, 0)), SGU_HEAD_DIM, axis=2)
    brow = jnp.repeat(jnp.transpose(b_s[:, :TS], (1, 0)), SGU_HEAD_DIM, axis=1)
    h1_all, h1b_all, pool_p, a_s, vn_s, dest, wsel, cnt = _mixer(
        xp, x_t, st_t, w_in.astype(BF16), w_pool.astype(BF16), row(pool_scale), sgu_ln_g, sgu_ln_b, w_s, bsb,
        wrow, brow, w_out.astype(BF16), row(ln1_g), row(ln1_b), jnp.transpose(router_w),
        router_bias.reshape(-1, 1), alpha)
    nb = n // TB
    m_out_max = TB * TOP_K + N_EXPERTS * SEG
    rows_max = n * TOP_K + nb * N_EXPERTS * (SEGG - 1) + N_EXPERTS * (TR - SEGG)
    n_tiles_max = -(-rows_max // TR)
    meta = _sort_meta(cnt[:, :, 0], n_tiles_max, m_out_max)
    xsort, src_rep, w_rep = _dispatch(meta, dest, wsel, h1b_all, n_tiles_max * TR, m_out_max)
    ysort = _expert_ffn(meta, xsort, w_gate, w_up, w_down)

    nbs = TB // TS
    ps_t = jnp.transpose(ps.reshape(NS // nbs, nbs, TS, -1), (0, 2, 1, 3)).reshape(n_sample, -1)
    y_p, y_s = _combine(meta, src_rep, w_rep, h1_all, h1b_all, pp.reshape(n_prompt, -1), ps_t,
                        ws_gate.astype(BF16), ws_up.astype(BF16),
                        ws_down.astype(BF16), row(ln2_g), row(ln2_b), w_pgate.astype(BF16), row(b_pgate),
                        w_pe.astype(BF16), ysort, m_out_max, n_prompt, alpha)

    yp = y_p.reshape(B, T, D)
    ys = jnp.transpose(y_s.reshape(NS // nbs, TS, nbs, D), (0, 2, 1, 3)).reshape(NS, TS, D)
    new_pool_p = pool_p[:, HALO - POOL_BUF:, :]
    new_pool_s = jnp.concatenate([st, jnp.transpose(a_s, (1, 0, 2))], axis=1)[:, -POOL_BUF:]
    vn = jnp.transpose(vn_s, (1, 0, 2)).reshape(NS, TS, n_heads, SGU_HEAD_DIM)
    return yp, ys, new_pool_p, new_pool_s, vn


def kernel(x_prompt, x_sample, state_pool, p_prompt, p_sample, w_in, w_pool, pool_scale, sgu_ln_g, sgu_ln_b, w_s, b_s, w_out, ln1_g, ln1_b, router_w, router_bias, w_gate, w_up, w_down, ws_gate, ws_up, ws_down, ln2_g, ln2_b, w_pe, w_pgate, b_pgate):
    depth = w_in.shape[0]
    alpha = (2.0 * depth) ** 0.25
    hp, hs = x_prompt, x_sample
    pool_p, pool_s, v_s = [], [], []
    for i in range(depth):
        hp, hs, bp, bs, vs = _layer(
            hp, hs, state_pool[i], p_prompt[i], p_sample[i], w_in[i], w_pool[i], pool_scale[i], sgu_ln_g[i],
            sgu_ln_b[i], w_s[i], b_s[i], w_out[i], ln1_g[i], ln1_b[i], router_w[i], router_bias[i], w_gate[i],
            w_up[i], w_down[i], ws_gate[i], ws_up[i], ws_down[i], ln2_g[i], ln2_b[i], w_pe[i], w_pgate[i],
            b_pgate[i], alpha)
        pool_p.append(bp)
        pool_s.append(bs)
        v_s.append(vs)
    return hp, hs, jnp.stack(pool_p), jnp.stack(pool_s), jnp.stack(v_s)
```

```python
import functools

import jax
import jax.numpy as jnp
import numpy as np
from jax import lax
from jax.experimental import pallas as pl
from jax.experimental.pallas import tpu as pltpu

F32 = jnp.float32
BF16 = jnp.bfloat16
I32 = jnp.int32

POOL_WINDOWS = (2, 4, 8, 16)
POOL_BUF = max(POOL_WINDOWS) - 1
CHUNK = 128
SGU_HEAD_DIM = 128
N_EXPERTS = 64
TOP_K = 8
N_EXPERT_GROUPS = 8
TOPK_GROUPS = 4
ROUTE_SCALE = 2.5
LN_EPS = 1e-5
PAST_LEN = 16384

TB = 256
HALO = 16
TR = 512
TRC = 128
SEG = 16
SEGG = 8
KC = 256
KCC = 512
VMEM_LIMIT = 56 * 1024 * 1024


def _const_spec(shape):
    n = len(shape)
    return pl.BlockSpec(shape, lambda *_: (0,) * n, pipeline_mode=pl.Buffered(1))


def _ln_rows(x, g, b):
    mu = jnp.mean(x, axis=-1, keepdims=True)
    xc = x - mu
    var = jnp.mean(xc * xc, axis=-1, keepdims=True)
    return xc * lax.rsqrt(var + LN_EPS) * g + b


def _dot(a, b):
    return jnp.dot(a, b, preferred_element_type=F32)


def _prompt_tile(j, n_tiles, x, proj, band_ref, wpool_ref, pscale_ref, sg_ref, sb_ref, ws_ref, bsb_ref,
                 pool_ref, zcat_ref, mixin_ref):
    d_pool = zcat_ref.shape[1]
    d_pool_g = d_pool // len(POOL_WINDOWS)
    d_sgu = mixin_ref.shape[1] - d_pool

    a = proj[:, :d_pool]
    zcat_ref[HALO:, :] = a.astype(BF16)

    pos1 = (j * TB + lax.broadcasted_iota(I32, (TB, 1), 0) + 1).astype(F32)
    for g, w in enumerate(POOL_WINDOWS):
        c0, c1 = g * d_pool_g, (g + 1) * d_pool_g
        s = _dot(band_ref[g], zcat_ref[:, c0:c1])
        cnt = jnp.minimum(pos1, float(w))
        d = s / cnt - a[:, c0:c1]
        y = _dot(d.astype(BF16), wpool_ref[g])
        mixin_ref[:, c0:c1] = (y * pscale_ref[:, c0:c1]).astype(BF16)
    zcat_ref[0:HALO, :] = jnp.where(j == n_tiles - 1, jnp.zeros((HALO, d_pool), BF16), zcat_ref[TB:TB + HALO, :])
    pool_ref[0] = a[TB - HALO:, :]

    u = jax.nn.gelu(proj[:, d_pool:d_pool + d_sgu])
    v = jax.nn.gelu(proj[:, d_pool + d_sgu:])
    r = lax.broadcasted_iota(I32, (CHUNK, CHUNK), 0)
    c = lax.broadcasted_iota(I32, (CHUNK, CHUNK), 1)
    tril = r >= c
    for h in range(d_sgu // SGU_HEAD_DIM):
        l0, l1 = h * SGU_HEAD_DIM, (h + 1) * SGU_HEAD_DIM
        vn = _ln_rows(v[:, l0:l1], sg_ref[h:h + 1, :], sb_ref[h:h + 1, :]).astype(BF16)
        wst = jnp.where(tril, ws_ref[h], 0.0).astype(BF16)
        n_ch = TB // CHUNK
        mixed_all = _dot(wst, jnp.concatenate([vn[ci * CHUNK:(ci + 1) * CHUNK] for ci in range(n_ch)], axis=1))
        for ci in range(n_ch):
            r0, r1 = ci * CHUNK, (ci + 1) * CHUNK
            mixed = mixed_all[:, ci * SGU_HEAD_DIM:(ci + 1) * SGU_HEAD_DIM] + bsb_ref[h]
            mixin_ref[r0:r1, d_pool + l0:d_pool + l1] = (u[r0:r1, l0:l1] * mixed).astype(BF16)


def _sample_tile(T, NB, proj, st_ref, wpool_ref, pscale_ref, sg_ref, sb_ref, wrow_ref, brow_ref,
                 a_ref, vn_ref, dbuf_ref, mixin_ref):
    d_pool = dbuf_ref.shape[1]
    d_pool_g = d_pool // len(POOL_WINDOWS)
    d_sgu = mixin_ref.shape[1] - d_pool

    a = proj[:, :d_pool]
    for t in range(T):
        a_ref[t] = a[t * NB:(t + 1) * NB]

    for g, w in enumerate(POOL_WINDOWS):
        c0, c1 = g * d_pool_g, (g + 1) * d_pool_g
        for t in range(T):
            cnt = float(min(PAST_LEN + t + 1, w))
            acc = None
            for s in range(POOL_BUF + t - w + 1, POOL_BUF + t + 1):
                if s < POOL_BUF:
                    term = st_ref[s, :, c0:c1]
                else:
                    term = a[(s - POOL_BUF) * NB:(s - POOL_BUF + 1) * NB, c0:c1]
                acc = term if acc is None else acc + term
            d = acc / cnt - a[t * NB:(t + 1) * NB, c0:c1]
            dbuf_ref[t * NB:(t + 1) * NB, c0:c1] = d.astype(BF16)
        y = _dot(dbuf_ref[0:T * NB, c0:c1], wpool_ref[g])
        mixin_ref[:, c0:c1] = (y * pscale_ref[:, c0:c1]).astype(BF16)

    u = jax.nn.gelu(proj[:, d_pool:d_pool + d_sgu])
    v = jax.nn.gelu(proj[:, d_pool + d_sgu:])
    for h in range(d_sgu // SGU_HEAD_DIM):
        l0, l1 = h * SGU_HEAD_DIM, (h + 1) * SGU_HEAD_DIM
        vn = _ln_rows(v[:, l0:l1], sg_ref[h:h + 1, :], sb_ref[h:h + 1, :])
        for t in range(T):
            vn_ref[t, :, l0:l1] = vn[t * NB:(t + 1) * NB]
    for t in range(T):
        mixed = brow_ref[t:t + 1, :]
        for s in range(t + 1):
            mixed = mixed + wrow_ref[t, s:s + 1, :] * vn_ref[s]
        mixin_ref[t * NB:(t + 1) * NB, d_pool:] = (u[t * NB:(t + 1) * NB] * mixed).astype(BF16)


def _mixer_body(alpha, n_pt, tiles_per_seq, xp_ref, xs_ref, st_ref, win_ref, band_ref, wpool_ref, pscale_ref,
                sg_ref, sb_ref, ws_ref, bsb_ref, wrow_ref, brow_ref, wout_ref, g1_ref, b1_ref,
                rw_ref, rb_ref, upper_ref, lstrict_ref,
                h1_ref, h1b_ref, pool_ref, a_ref, vn_ref, dest_ref, wsel_ref, cnt_ref,
                zcat_ref, mixin_ref, hprev_ref):
    i = pl.program_id(0)
    n_tiles = pl.num_programs(0) - 1

    def route_previous_tile():
        h = hprev_ref[...]
        _route_tile(h, h.astype(BF16), rw_ref, rb_ref, upper_ref, lstrict_ref,
                    dest_ref, wsel_ref, cnt_ref)

    @pl.when(i == 0)
    def _():
        zcat_ref[0:HALO, :] = jnp.zeros((HALO, zcat_ref.shape[1]), BF16)
        hprev_ref[...] = jnp.zeros(hprev_ref.shape, F32)

    @pl.when(i < n_pt)
    def _():
        x = xp_ref[0]
        proj = _dot(x.astype(BF16), win_ref[...])
        route_previous_tile()
        _prompt_tile(i % tiles_per_seq, tiles_per_seq, x, proj, band_ref, wpool_ref, pscale_ref, sg_ref, sb_ref,
                     ws_ref, bsb_ref, pool_ref, zcat_ref, mixin_ref)
        h1_ref[...] = alpha * x

    @pl.when(i >= n_pt)
    def _():
        route_previous_tile()

    @pl.when((i >= n_pt) & (i < n_tiles))
    def _():
        T, NB, D = xs_ref.shape
        x = xs_ref[...].reshape(T * NB, D)
        proj = _dot(x.astype(BF16), win_ref[...])
        _sample_tile(T, NB, proj, st_ref, wpool_ref, pscale_ref, sg_ref, sb_ref, wrow_ref, brow_ref,
                     a_ref, vn_ref, zcat_ref, mixin_ref)
        h1_ref[...] = alpha * x

    @pl.when(i < n_tiles)
    def _():
        mix = _dot(mixin_ref[...], wout_ref[...])
        h1 = _ln_rows(h1_ref[...] + mix, g1_ref[...], b1_ref[...])
        h1_ref[...] = h1
        h1b_ref[...] = h1.astype(BF16)
        hprev_ref[...] = h1


def _pool_band(tm):
    t = np.arange(tm)[:, None] + HALO
    s = np.arange(tm + HALO)[None, :]
    return jnp.asarray(np.stack([((s <= t) & (s >= t - w + 1)) for w in POOL_WINDOWS]).astype(np.float32), BF16)


def _mixer(xp, xs_t, st_t, win_b, wpool_b, pscale, sg, sb, ws, bsb, wrow, brow, wout_b, g1, b1, rwt, rb_col, alpha):
    B, T, D = xp.shape
    TS, NS, _ = xs_t.shape
    d_pool = st_t.shape[2]
    d_sgu = wrow.shape[2]
    tps = T // TB
    n_pt = B * tps
    NB = TB // TS
    n_st = NS // NB
    n = B * T + NS * TS
    nblk = n_pt + n_st
    pt = lambda i: jnp.minimum(i, n_pt - 1)
    stile = lambda i: jnp.clip(i - n_pt, 0, n_st - 1)
    blk = lambda i: jnp.minimum(i, nblk - 1)
    routed = lambda i: jnp.maximum(i - 1, 0)
    body = functools.partial(_mixer_body, alpha, n_pt, tps)
    rw_hi = rwt.astype(BF16)
    rw_lo = (rwt - rw_hi.astype(F32)).astype(BF16)
    upper = jnp.asarray(np.triu(np.ones((TB, TB), np.float32), 1), BF16)
    lstrict = jnp.asarray(np.tril(np.ones((N_EXPERTS, N_EXPERTS), np.float32), -1), BF16)
    consts = (win_b, _pool_band(TB), wpool_b, pscale, sg, sb, ws, bsb, wrow, brow, wout_b, g1, b1,
              jnp.concatenate([rw_hi, rw_lo], axis=0), rb_col, upper, lstrict)
    return pl.pallas_call(
        body,
        grid=(nblk + 1,),
        in_specs=[
            pl.BlockSpec((1, TB, D), lambda i: (pt(i) // tps, pt(i) % tps, 0)),
            pl.BlockSpec((TS, NB, D), lambda i: (0, stile(i), 0), pipeline_mode=pl.Buffered(1)),
            pl.BlockSpec((POOL_BUF, NB, d_pool), lambda i: (0, stile(i), 0), pipeline_mode=pl.Buffered(1)),
        ] + [_const_spec(c.shape) for c in consts],
        out_specs=[
            pl.BlockSpec((TB, D), lambda i: (blk(i), 0)),
            pl.BlockSpec((TB, D), lambda i: (blk(i), 0)),
            pl.BlockSpec((1, HALO, d_pool), lambda i: (pt(i) // tps, 0, 0)),
            pl.BlockSpec((TS, NB, d_pool), lambda i: (0, stile(i), 0)),
            pl.BlockSpec((TS, NB, d_sgu), lambda i: (0, stile(i), 0)),
            pl.BlockSpec((1, TOP_K, TB), lambda i: (routed(i), 0, 0)),
            pl.BlockSpec((1, TOP_K, TB), lambda i: (routed(i), 0, 0)),
            pl.BlockSpec((1, N_EXPERTS, 128), lambda i: (routed(i), 0, 0)),
        ],
        out_shape=[
            jax.ShapeDtypeStruct((n, D), F32),
            jax.ShapeDtypeStruct((n, D), BF16),
            jax.ShapeDtypeStruct((B, HALO, d_pool), F32),
            jax.ShapeDtypeStruct((TS, NS, d_pool), F32),
            jax.ShapeDtypeStruct((TS, NS, d_sgu), F32),
            jax.ShapeDtypeStruct((nblk, TOP_K, TB), I32),
            jax.ShapeDtypeStruct((nblk, TOP_K, TB), F32),
            jax.ShapeDtypeStruct((nblk, N_EXPERTS, 128), I32),
        ],
        scratch_shapes=[
            pltpu.VMEM((TB + HALO, d_pool), BF16),
            pltpu.VMEM((TB, D), BF16),
            pltpu.VMEM((TB, D), F32),
        ],
        compiler_params=pltpu.CompilerParams(
            dimension_semantics=("arbitrary",), vmem_limit_bytes=VMEM_LIMIT),
        name="mixer",
    )(xp, xs_t, st_t, *consts)


def _route_tile(h1, h1b, rw_ref, rb_ref, upper_ref, lstrict_ref, dest_ref, wsel_ref, cnt_ref):
    per_group = N_EXPERTS // N_EXPERT_GROUPS
    neg = -jnp.inf
    nt = (((1,), (1,)), ((), ()))
    h_lo = (h1 - h1b.astype(F32)).astype(BF16)
    by_h_hi = lax.dot_general(rw_ref[...], h1b, nt, preferred_element_type=F32)
    logits_t = (by_h_hi[:N_EXPERTS] + by_h_hi[N_EXPERTS:]
                + lax.dot_general(rw_ref[0:N_EXPERTS, :], h_lo, nt, preferred_element_type=F32))
    s_t = jax.nn.sigmoid(logits_t)
    b_t = s_t + rb_ref[...]

    io_g = lax.broadcasted_iota(I32, (per_group, TB), 0)
    gs = []
    for g in range(N_EXPERT_GROUPS):
        xg = b_t[g * per_group:(g + 1) * per_group, :]
        m1 = jnp.max(xg, axis=0, keepdims=True)
        i1 = jnp.min(jnp.where(xg == m1, io_g, per_group), axis=0, keepdims=True)
        m2 = jnp.max(jnp.where(io_g == i1, neg, xg), axis=0, keepdims=True)
        gs.append(m1 + m2)
    masked = []
    for g in range(N_EXPERT_GROUPS):
        rank = jnp.zeros((1, TB), F32)
        for g2 in range(N_EXPERT_GROUPS):
            if g2 != g:
                ahead = (gs[g2] >= gs[g]) if g2 < g else (gs[g2] > gs[g])
                rank = rank + jnp.where(ahead, 1.0, 0.0)
        keep = rank < float(TOPK_GROUPS)
        masked.append(jnp.where(keep, b_t[g * per_group:(g + 1) * per_group, :], neg))
    xm = jnp.concatenate(masked, axis=0)

    io_e = lax.broadcasted_iota(I32, (N_EXPERTS, TB), 0)
    onehots = []
    sel = jnp.zeros((N_EXPERTS, TB), F32)
    for _ in range(TOP_K):
        m = jnp.max(xm, axis=0, keepdims=True)
        idx = jnp.min(jnp.where(xm == m, io_e, N_EXPERTS), axis=0, keepdims=True)
        oh = io_e == idx
        onehots.append(oh)
        sel = jnp.where(oh, 1.0, sel)
        xm = jnp.where(oh, neg, xm)

    ssel = sel * s_t
    denom = jnp.sum(ssel, axis=0, keepdims=True)
    comb = ssel / denom * ROUTE_SCALE

    rank_t = _dot(sel.astype(BF16), upper_ref[...])
    cnt = jnp.sum(sel, axis=1, keepdims=True)
    cnt_i = cnt.astype(I32)
    seg16 = ((cnt_i + (SEG - 1)) // SEG).astype(F32)
    off16 = _dot(lstrict_ref[...], jnp.broadcast_to(seg16, (N_EXPERTS, 128)).astype(BF16))
    d_t = rank_t + off16[:, 0:1] * float(SEG)
    for k in range(TOP_K):
        dest_ref[0, k:k + 1, :] = jnp.sum(jnp.where(onehots[k], d_t, 0.0), axis=0, keepdims=True).astype(I32)
        wsel_ref[0, k:k + 1, :] = jnp.sum(jnp.where(onehots[k], comb, 0.0), axis=0, keepdims=True)
    cnt_ref[0] = jnp.broadcast_to(cnt_i, (N_EXPERTS, 128))


def _start_listed_copies(b, table_l, table_g, counts, rows, make_copy):
    width = table_l.shape[0] // counts.shape[0]
    n = counts[b]

    def start(j):
        idx = b * width + j
        make_copy(pl.multiple_of(table_l[idx], rows), pl.multiple_of(table_g[idx], SEGG), rows).start()

    def eight(q, c):
        for u in range(8):
            start(8 * q + u)
        return c

    def one(j, c):
        start(j)
        return c

    lax.fori_loop(0, n // 8, eight, 0)
    lax.fori_loop(n // 8 * 8, n, one, 0)


def _wait_row_units(count, max_count, make_wait_copy):
    for k in range(int(max_count).bit_length()):
        @pl.when(((count >> k) & 1) == 1)
        def _():
            make_wait_copy(SEGG << k).wait()


def _dispatch_body(big_l, big_g, n_big, small_l, small_g, n_small, units, nch, tail_g, tail_n,
                   dest_ref, wsel_ref, h_ref, tokid_ref, ones_ref,
                   xs_hbm, src_ref, wrow_ref, xloc_ref, zero_ref, sems, tail_sem):
    b = pl.program_id(0)
    nb = pl.num_programs(0)
    max_units = xloc_ref.shape[1] // SEGG
    slot = b % 2

    def seg_copy(sl):
        return lambda l, g, rows: pltpu.make_async_copy(
            xloc_ref.at[sl, pl.ds(l, rows), :], xs_hbm.at[pl.ds(g, rows), :], sems.at[sl])

    def wait_block(count, sl):
        _wait_row_units(count, max_units, lambda rows: pltpu.make_async_copy(
            xloc_ref.at[sl, pl.ds(0, rows), :], xs_hbm.at[pl.ds(0, rows), :], sems.at[sl]))

    @pl.when(b >= 2)
    def _():
        wait_block(units[jnp.maximum(b - 2, 0)], slot)

    dest = dest_ref[0]
    wsel = wsel_ref[0]
    h = h_ref[...]
    src_ref[...] = jnp.zeros(src_ref.shape, F32)
    wrow_ref[...] = jnp.zeros(wrow_ref.shape, F32)

    row_in_chunk = lax.broadcasted_iota(I32, (KC, TB), 0).astype(F32).astype(BF16)
    wsel_b = wsel.astype(BF16)
    one = jnp.ones((1, TB), BF16)

    def chunk(c):
        r0 = pl.multiple_of(c * KC, KC)
        rel = (dest - r0).astype(F32).astype(BF16)
        sb = jnp.zeros((KC, TB), BF16)
        wm = jnp.zeros((KC, TB), BF16)
        for k in range(TOP_K):
            hit = rel[k:k + 1, :] == row_in_chunk
            sb = jnp.where(hit, one, sb)
            wm = jnp.where(hit, wsel_b[k:k + 1, :], wm)
        xloc_ref[slot, pl.ds(r0, KC), :] = _dot(sb, h).astype(BF16)
        src_ref[0, pl.ds(r0, KC), :] = _dot(sb, tokid_ref[...])
        wrow_ref[0, pl.ds(r0, KC), :] = _dot(wm, ones_ref[...])

    def chunk_pair(cp, carry):
        chunk(2 * cp)
        chunk(2 * cp + 1)
        return carry

    lax.fori_loop(0, nch[b] // 2, chunk_pair, 0)

    @pl.when(nch[b] % 2 == 1)
    def _():
        chunk(nch[b] - 1)

    _start_listed_copies(b, big_l, big_g, n_big, SEG, seg_copy(slot))
    _start_listed_copies(b, small_l, small_g, n_small, SEGG, seg_copy(slot))

    def tail_copy(g):
        return pltpu.make_async_copy(zero_ref, xs_hbm.at[pl.ds(g, SEGG), :], tail_sem)

    def tails(wait):
        def per_expert(e, carry):
            def per_granule(j, c):
                cp = tail_copy(pl.multiple_of(tail_g[e] + j * SEGG, SEGG))
                if wait:
                    cp.wait()
                else:
                    cp.start()
                return c
            lax.fori_loop(0, tail_n[e], per_granule, 0)
            return carry
        lax.fori_loop(0, N_EXPERTS, per_expert, 0)

    @pl.when(b == nb - 1)
    def _():
        zero_ref[...] = jnp.zeros(zero_ref.shape, BF16)
        tails(False)

        @pl.when(b >= 1)
        def _():
            wait_block(units[jnp.maximum(b - 1, 0)], 1 - slot)

        wait_block(units[b], slot)
        tails(True)


def _dispatch(meta, dest, wsel, h1b_all, n_rows_sorted, m_out_max):
    n, D = h1b_all.shape
    nb = n // TB
    tokid = jnp.asarray(np.broadcast_to(np.arange(TB, dtype=np.float32)[:, None], (TB, 128)), BF16)
    ones = jnp.ones((TB, 128), BF16)
    cs = lambda shape: pl.BlockSpec(shape, lambda i, *_: (0,) * len(shape), pipeline_mode=pl.Buffered(1))
    grid_spec = pltpu.PrefetchScalarGridSpec(
        num_scalar_prefetch=10,
        grid=(nb,),
        in_specs=[
            pl.BlockSpec((1, TOP_K, TB), lambda i, *_: (i, 0, 0)),
            pl.BlockSpec((1, TOP_K, TB), lambda i, *_: (i, 0, 0)),
            pl.BlockSpec((TB, D), lambda i, *_: (i, 0)),
            cs(tokid.shape), cs(ones.shape),
        ],
        out_specs=[
            pl.BlockSpec(memory_space=pl.ANY),
            pl.BlockSpec((1, m_out_max, 128), lambda i, *_: (i, 0, 0)),
            pl.BlockSpec((1, m_out_max, 128), lambda i, *_: (i, 0, 0)),
        ],
        scratch_shapes=[
            pltpu.VMEM((2, m_out_max, D), BF16),
            pltpu.VMEM((SEGG, D), BF16),
            pltpu.SemaphoreType.DMA((2,)),
            pltpu.SemaphoreType.DMA(()),
        ],
    )
    return pl.pallas_call(
        _dispatch_body,
        grid_spec=grid_spec,
        out_shape=[
            jax.ShapeDtypeStruct((n_rows_sorted, D), BF16),
            jax.ShapeDtypeStruct((nb, m_out_max, 128), F32),
            jax.ShapeDtypeStruct((nb, m_out_max, 128), F32),
        ],
        compiler_params=pltpu.CompilerParams(
            dimension_semantics=("arbitrary",), vmem_limit_bytes=VMEM_LIMIT, has_side_effects=True),
        name="dispatch",
    )(meta["big_l"], meta["big_g"], meta["n_big"], meta["small_l"], meta["small_g"], meta["n_small"],
      meta["units"], meta["nch"], meta["tail_g"], meta["tail_n"], dest, wsel, h1b_all, tokid, ones)


def _ffn_body(tile_e, tile_src, tile_rows, e_slot, e_next, x_ref, wg_hbm, wu_hbm, wd_hbm, y_ref,
              wg_f, wu_f, wd_f, wgu_b, wd_b, sems):
    i = pl.program_id(0)
    f = wg_f.shape[2]
    e = tile_e[i]
    e_prev = tile_e[jnp.maximum(i - 1, 0)]

    def weight_copies(expert, slot):
        return (pltpu.make_async_copy(wg_hbm.at[expert], wg_f.at[slot], sems.at[slot, 0]),
                pltpu.make_async_copy(wu_hbm.at[expert], wu_f.at[slot], sems.at[slot, 1]),
                pltpu.make_async_copy(wd_hbm.at[expert], wd_f.at[slot], sems.at[slot, 2]))

    @pl.when((i == 0) | (e != e_prev))
    def _():
        slot = e_slot[e]

        @pl.when(i == 0)
        def _():
            for cp in weight_copies(e, slot):
                cp.start()

        for cp in weight_copies(e, slot):
            cp.wait()
        nxt = e_next[e]

        @pl.when(nxt >= 0)
        def _():
            for cp in weight_copies(nxt, 1 - slot):
                cp.start(priority=1)

        wgu_b[:, :f] = wg_f[slot].astype(BF16)
        wgu_b[:, f:] = wu_f[slot].astype(BF16)
        wd_b[...] = wd_f[slot].astype(BF16)

    def swiglu(x):
        gu = _dot(x, wgu_b[...])
        act = (jax.nn.silu(gu[:, :f]) * gu[:, f:]).astype(BF16)
        return _dot(act, wd_b[...]).astype(BF16)

    for m in range(TRC, TR + 1, TRC):
        @pl.when(tile_rows[i] == m)
        def _(m=m):
            y_ref[0:m, :] = swiglu(x_ref[0:m, :])
            if m < TR:
                y_ref[m:, :] = jnp.zeros((TR - m, y_ref.shape[1]), BF16)


def _expert_ffn(meta, xs, w_gate, w_up, w_down):
    R, D = xs.shape
    E, _, f = w_gate.shape
    nt = R // TR
    grid_spec = pltpu.PrefetchScalarGridSpec(
        num_scalar_prefetch=5,
        grid=(nt,),
        in_specs=[
            pl.BlockSpec((TR, D), lambda i, te, ts, *_: (ts[i], 0)),
            pl.BlockSpec(memory_space=pl.ANY),
            pl.BlockSpec(memory_space=pl.ANY),
            pl.BlockSpec(memory_space=pl.ANY),
        ],
        out_specs=pl.BlockSpec((TR, D), lambda i, te, ts, *_: (ts[i], 0)),
        scratch_shapes=[
            pltpu.VMEM((2, D, f), F32),
            pltpu.VMEM((2, D, f), F32),
            pltpu.VMEM((2, f, D), F32),
            pltpu.VMEM((D, 2 * f), BF16),
            pltpu.VMEM((f, D), BF16),
            pltpu.SemaphoreType.DMA((2, 3)),
        ],
    )
    return pl.pallas_call(
        _ffn_body,
        grid_spec=grid_spec,
        out_shape=jax.ShapeDtypeStruct((R, D), BF16),
        compiler_params=pltpu.CompilerParams(
            dimension_semantics=("arbitrary",), vmem_limit_bytes=VMEM_LIMIT),
        name="expert_ffn",
    )(meta["tile_e"], meta["tile_src"], meta["tile_rows"], meta["e_slot"], meta["e_next"],
      xs, w_gate, w_up, w_down)


def _combine_body(alpha, n_pb, big_l, big_g, n_big, small_l, small_g, n_small, chunk_units, nch,
                  src_ref, wrow_ref, h1_ref, h1b_ref, pp_ref, ps_ref,
                  wsg_ref, wsu_ref, wsd_ref, g2_ref, b2_ref, wpg_ref, bpg_ref, wpe_ref, ys_hbm,
                  yp_ref, ysm_ref, yloc_ref, acc_ref, sems):
    b = pl.program_id(0)
    n_chunks = nch[b]
    max_chunks = sems.shape[0]

    def seg_copy(l, g, rows):
        return pltpu.make_async_copy(ys_hbm.at[pl.ds(g, rows), :], yloc_ref.at[pl.ds(l, rows), :],
                                     sems.at[l // KCC])

    def fetch_block(blk):
        _start_listed_copies(blk, big_l, big_g, n_big, SEG, seg_copy)
        _start_listed_copies(blk, small_l, small_g, n_small, SEGG, seg_copy)

    @pl.when(b == 0)
    def _():
        yloc_ref[...] = jnp.zeros(yloc_ref.shape, BF16)
        fetch_block(b)

    hb = h1b_ref[...]
    sh = (jax.nn.silu(_dot(hb, wsg_ref[...])) * _dot(hb, wsu_ref[...])).astype(BF16)
    acc_ref[...] = _dot(sh, wsd_ref[...])

    def chunk(c, carry):
        def wait_rows(rows):
            return pltpu.make_async_copy(ys_hbm.at[pl.ds(0, rows), :], yloc_ref.at[pl.ds(0, rows), :], sems.at[c])

        _wait_row_units(chunk_units[b * max_chunks + c], KCC // SEGG, wait_rows)

        r0 = pl.multiple_of(c * KCC, KCC)
        src = src_ref[0, pl.ds(r0, KCC), :]
        w = wrow_ref[0, pl.ds(r0, KCC), :]
        lane = lax.broadcasted_iota(I32, src.shape, 1).astype(F32)
        wm = jnp.concatenate([jnp.where(src == lane + float(o), w, 0.0) for o in range(0, TB, src.shape[1])],
                             axis=1).astype(BF16)
        acc_ref[...] += lax.dot_general(wm, yloc_ref[pl.ds(r0, KCC), :],
                                        (((0,), (0,)), ((), ())), preferred_element_type=F32)
        return carry

    lax.fori_loop(0, n_chunks, chunk, 0)

    @pl.when(b + 1 < pl.num_programs(0))
    def _():
        fetch_block(b + 1)

    h2 = _ln_rows(alpha * h1_ref[...] + acc_ref[...], g2_ref[...], b2_ref[...])
    gate = jax.nn.sigmoid(_dot(h2.astype(BF16), wpg_ref[...]) + bpg_ref[...])
    p = jnp.where(b < n_pb, pp_ref[...], ps_ref[...])
    pe = _dot(p.astype(BF16), wpe_ref[...])
    y = h2 + gate * pe

    @pl.when(b < n_pb)
    def _():
        yp_ref[...] = y

    @pl.when(b >= n_pb)
    def _():
        ysm_ref[...] = y


def _combine(meta, src_rep, w_rep, h1_all, h1b_all, p_prompt, p_sample, wsg_b, wsu_b, wsd_b, g2, b2, wpg_b, bpg,
             wpe_b, ys, m_out_max, n_prompt, alpha):
    n, D = h1_all.shape
    nb = n // TB
    n_pb = n_prompt // TB
    d_pe = p_prompt.shape[1]
    cs = lambda shape: pl.BlockSpec(shape, lambda i, *_: (0,) * len(shape), pipeline_mode=pl.Buffered(1))
    grid_spec = pltpu.PrefetchScalarGridSpec(
        num_scalar_prefetch=8,
        grid=(nb,),
        in_specs=[
            pl.BlockSpec((1, m_out_max, 128), lambda i, *_: (i, 0, 0)),
            pl.BlockSpec((1, m_out_max, 128), lambda i, *_: (i, 0, 0)),
            pl.BlockSpec((TB, D), lambda i, *_: (i, 0)),
            pl.BlockSpec((TB, D), lambda i, *_: (i, 0)),
            pl.BlockSpec((TB, d_pe), lambda i, *_: (jnp.minimum(i, n_pb - 1), 0)),
            pl.BlockSpec((TB, d_pe), lambda i, *_: (jnp.maximum(i - n_pb, 0), 0)),
            cs(wsg_b.shape), cs(wsu_b.shape), cs(wsd_b.shape), cs(g2.shape), cs(b2.shape),
            cs(wpg_b.shape), cs(bpg.shape), cs(wpe_b.shape),
            pl.BlockSpec(memory_space=pl.ANY),
        ],
        out_specs=[
            pl.BlockSpec((TB, D), lambda i, *_: (jnp.minimum(i, n_pb - 1), 0)),
            pl.BlockSpec((TB, D), lambda i, *_: (jnp.maximum(i - n_pb, 0), 0)),
        ],
        scratch_shapes=[
            pltpu.VMEM((m_out_max, D), BF16),
            pltpu.VMEM((TB, D), F32),
            pltpu.SemaphoreType.DMA((m_out_max // KCC,)),
        ],
    )
    return pl.pallas_call(
        functools.partial(_combine_body, alpha, n_pb),
        grid_spec=grid_spec,
        out_shape=[jax.ShapeDtypeStruct((n_prompt, D), F32), jax.ShapeDtypeStruct((n - n_prompt, D), F32)],
        compiler_params=pltpu.CompilerParams(
            dimension_semantics=("arbitrary",), vmem_limit_bytes=VMEM_LIMIT),
        name="combine",
    )(meta["big_l"], meta["big_g"], meta["n_big"], meta["small_l"], meta["small_g"], meta["n_small"],
      meta["chunk_units"], meta["nchc"], src_rep, w_rep, h1_all, h1b_all, p_prompt, p_sample,
      wsg_b, wsu_b, wsd_b, g2, b2, wpg_b, bpg, wpe_b, ys)


def _sort_meta(cnt, n_tiles_max, m_out_max):
    segl = (cnt + (SEG - 1)) // SEG * SEG
    loc = jnp.cumsum(segl, axis=1) - segl
    n8 = (cnt + (SEGG - 1)) // SEGG
    segp = n8 * SEGG
    before = jnp.cumsum(segp, axis=0) - segp
    total = jnp.sum(segp, axis=0)
    total_c = (total + (TRC - 1)) // TRC * TRC
    total_p = (total + (TR - 1)) // TR * TR
    ends = jnp.cumsum(total_p)
    base = ends - total_p
    glob = base[None, :] + before
    n_valid = ends[-1] // TR
    tiles = jnp.arange(n_tiles_max, dtype=I32)
    tile_src = jnp.minimum(tiles, n_valid - 1)
    tile_e = jnp.minimum(jnp.sum((ends[None, :] <= (tile_src * TR)[:, None]).astype(I32), axis=1), N_EXPERTS - 1)
    left = jnp.sum(jnp.where(tile_e[:, None] == jnp.arange(N_EXPERTS, dtype=I32)[None, :],
                             (base + total_c)[None, :], 0), axis=1) - tile_src * TR
    tile_rows = jnp.where(tiles < n_valid, jnp.clip(left, 0, TR), 0)
    experts = jnp.arange(N_EXPERTS, dtype=I32)
    active = total_p > 0
    later = active[None, :] & (experts[None, :] > experts[:, None])
    e_next = jnp.min(jnp.where(later, experts[None, :], N_EXPERTS), axis=1)
    def copy_list(count, first_l, first_g, width):
        end = jnp.cumsum(count, axis=1)
        start = end - count
        j = jnp.arange(width, dtype=I32)[None, :, None]
        in_seg = (start[:, None, :] <= j) & (j < end[:, None, :])
        step = (j - start[:, None, :]) * SEG
        lst_l = jnp.sum(jnp.where(in_seg, first_l[:, None, :] + step, 0), axis=2)
        lst_g = jnp.sum(jnp.where(in_seg, first_g[:, None, :] + step, 0), axis=2)
        return lst_l, lst_g, end[:, -1], j[:, :, 0] < end[:, -1:]

    n_big, n_small = n8 // 2, n8 % 2
    big_l, big_g, big_n, big_ok = copy_list(n_big, loc, glob, m_out_max // SEG)
    small_l, small_g, small_n, small_ok = copy_list(n_small, loc + n_big * SEG, glob + n_big * SEG, N_EXPERTS)
    chunks = jnp.arange(m_out_max // KCC, dtype=I32)[None, :, None]
    chunk_units = (jnp.sum(jnp.where(big_ok[:, None, :] & (big_l[:, None, :] // KCC == chunks), SEG // SEGG, 0), axis=2)
                   + jnp.sum(jnp.where(small_ok[:, None, :] & (small_l[:, None, :] // KCC == chunks), 1, 0), axis=2))
    local_rows = jnp.sum(segl, axis=1)
    return {
        "big_l": big_l.reshape(-1).astype(I32),
        "big_g": big_g.reshape(-1).astype(I32),
        "n_big": big_n.astype(I32),
        "small_l": small_l.reshape(-1).astype(I32),
        "small_g": small_g.reshape(-1).astype(I32),
        "n_small": small_n.astype(I32),
        "units": jnp.sum(n8, axis=1).astype(I32),
        "chunk_units": chunk_units.reshape(-1).astype(I32),
        "nch": ((local_rows + (KC - 1)) // KC).astype(I32),
        "nchc": ((local_rows + (KCC - 1)) // KCC).astype(I32),
        "tail_g": (base + total).astype(I32),
        "tail_n": ((total_c - total) // SEGG).astype(I32),
        "tile_e": tile_e.astype(I32),
        "tile_src": tile_src.astype(I32),
        "tile_rows": tile_rows.astype(I32),
        "e_slot": ((jnp.cumsum(active.astype(I32)) - 1) % 2).astype(I32),
        "e_next": jnp.where(e_next < N_EXPERTS, e_next, -1).astype(I32),
    }


def _layer(xp, xs, st, pp, ps, w_in, w_pool, pool_scale, sgu_ln_g, sgu_ln_b, w_s, b_s, w_out, ln1_g, ln1_b,
           router_w, router_bias, w_gate, w_up, w_down, ws_gate, ws_up, ws_down, ln2_g, ln2_b, w_pe, w_pgate,
           b_pgate, alpha):
    B, T, D = xp.shape
    NS, TS, _ = xs.shape
    n_prompt, n_sample = B * T, NS * TS
    n = n_prompt + n_sample
    n_heads = w_s.shape[0]
    row = lambda v: v.reshape(1, -1)

    x_t = jnp.transpose(xs, (1, 0, 2))
    st_t = jnp.transpose(st, (1, 0, 2))
    bsb = jnp.broadcast_to(b_s[:, :, None], (n_heads, CHUNK, SGU_HEAD_DIM))
    wrow = jnp.repeat(jnp.transpose(w_s[:, :TS, :TS], (1, 2, 0)), SGU_HEAD_DIM, axis=2)
    brow = jnp.repeat(jnp.transpose(b_s[:, :TS], (1, 0)), SGU_HEAD_DIM, axis=1)
    h1_all, h1b_all, pool_p, a_s, vn_s, dest, wsel, cnt = _mixer(
        xp, x_t, st_t, w_in.astype(BF16), w_pool.astype(BF16), row(pool_scale), sgu_ln_g, sgu_ln_b, w_s, bsb,
        wrow, brow, w_out.astype(BF16), row(ln1_g), row(ln1_b), jnp.transpose(router_w),
        router_bias.reshape(-1, 1), alpha)
    nb = n // TB
    m_out_max = TB * TOP_K + N_EXPERTS * SEG
    rows_max = n * TOP_K + nb * N_EXPERTS * (SEGG - 1) + N_EXPERTS * (TR - SEGG)
    n_tiles_max = -(-rows_max // TR)
    meta = _sort_meta(cnt[:, :, 0], n_tiles_max, m_out_max)
    xsort, src_rep, w_rep = _dispatch(meta, dest, wsel, h1b_all, n_tiles_max * TR, m_out_max)
    ysort = _expert_ffn(meta, xsort, w_gate, w_up, w_down)

    nbs = TB // TS
    ps_t = jnp.transpose(ps.reshape(NS // nbs, nbs, TS, -1), (0, 2, 1, 3)).reshape(n_sample, -1)
    y_p, y_s = _combine(meta, src_rep, w_rep, h1_all, h1b_all, pp.reshape(n_prompt, -1), ps_t,
                        ws_gate.astype(BF16), ws_up.astype(BF16),
                        ws_down.astype(BF16), row(ln2_g), row(ln2_b), w_pgate.astype(BF16), row(b_pgate),
                        w_pe.astype(BF16), ysort, m_out_max, n_prompt, alpha)

    yp = y_p.reshape(B, T, D)
    ys = jnp.transpose(y_s.reshape(NS // nbs, TS, nbs, D), (0, 2, 1, 3)).reshape(NS, TS, D)
    new_pool_p = pool_p[:, HALO - POOL_BUF:, :]
    new_pool_s = jnp.concatenate([st, jnp.transpose(a_s, (1, 0, 2))], axis=1)[:, -POOL_BUF:]
    vn = jnp.transpose(vn_s, (1, 0, 2)).reshape(NS, TS, n_heads, SGU_HEAD_DIM)
    return yp, ys, new_pool_p, new_pool_s, vn


def kernel(x_prompt, x_sample, state_pool, p_prompt, p_sample, w_in, w_pool, pool_scale, sgu_ln_g, sgu_ln_b, w_s, b_s, w_out, ln1_g, ln1_b, router_w, router_bias, w_gate, w_up, w_down, ws_gate, ws_up, ws_down, ln2_g, ln2_b, w_pe, w_pgate, b_pgate):
    depth = w_in.shape[0]
    alpha = (2.0 * depth) ** 0.25
    hp, hs = x_prompt, x_sample
    pool_p, pool_s, v_s = [], [], []
    for i in range(depth):
        hp, hs, bp, bs, vs = _layer(
            hp, hs, state_pool[i], p_prompt[i], p_sample[i], w_in[i], w_pool[i], pool_scale[i], sgu_ln_g[i],
            sgu_ln_b[i], w_s[i], b_s[i], w_out[i], ln1_g[i], ln1_b[i], router_w[i], router_bias[i], w_gate[i],
            w_up[i], w_down[i], ws_gate[i], ws_up[i], ws_down[i], ln2_g[i], ln2_b[i], w_pe[i], w_pgate[i],
            b_pgate[i], alpha)
        pool_p.append(bp)
        pool_s.append(bs)
        v_s.append(vs)
    return hp, hs, jnp.stack(pool_p), jnp.stack(pool_s), jnp.stack(v_s)
```

```python
import functools

import jax
import jax.numpy as jnp
import numpy as np
from jax import lax
from jax.experimental import pallas as pl
from jax.experimental.pallas import tpu as pltpu

F32 = jnp.float32
BF16 = jnp.bfloat16
I32 = jnp.int32

POOL_WINDOWS = (2, 4, 8, 16)
POOL_BUF = max(POOL_WINDOWS) - 1
CHUNK = 128
SGU_HEAD_DIM = 128
N_EXPERTS = 64
TOP_K = 8
N_EXPERT_GROUPS = 8
TOPK_GROUPS = 4
ROUTE_SCALE = 2.5
LN_EPS = 1e-5
PAST_LEN = 16384

TB = 256
HALO = 16
TR = 512
TRC = 128
SEG = 16
SEGG = 8
COPY_ROWS = (32, 16, 8)
KC = 256
KCC = 512
VMEM_LIMIT = 56 * 1024 * 1024


def _const_spec(shape):
    n = len(shape)
    return pl.BlockSpec(shape, lambda *_: (0,) * n, pipeline_mode=pl.Buffered(1))


def _ln_rows(x, g, b):
    mu = jnp.mean(x, axis=-1, keepdims=True)
    xc = x - mu
    var = jnp.mean(xc * xc, axis=-1, keepdims=True)
    return xc * lax.rsqrt(var + LN_EPS) * g + b


def _dot(a, b):
    return jnp.dot(a, b, preferred_element_type=F32)


def _prompt_tile(j, n_tiles, x, proj, band_ref, wpool_ref, pscale_ref, sg_ref, sb_ref, ws_ref, bsb_ref,
                 pool_ref, zcat_ref, mixin_ref):
    d_pool = zcat_ref.shape[1]
    d_pool_g = d_pool // len(POOL_WINDOWS)
    d_sgu = mixin_ref.shape[1] - d_pool

    a = proj[:, :d_pool]
    zcat_ref[HALO:, :] = a.astype(BF16)

    pos1 = (j * TB + lax.broadcasted_iota(I32, (TB, 1), 0) + 1).astype(F32)
    for g, w in enumerate(POOL_WINDOWS):
        c0, c1 = g * d_pool_g, (g + 1) * d_pool_g
        s = _dot(band_ref[g], zcat_ref[:, c0:c1])
        cnt = jnp.minimum(pos1, float(w))
        d = s / cnt - a[:, c0:c1]
        y = _dot(d.astype(BF16), wpool_ref[g])
        mixin_ref[:, c0:c1] = (y * pscale_ref[:, c0:c1]).astype(BF16)
    zcat_ref[0:HALO, :] = jnp.where(j == n_tiles - 1, jnp.zeros((HALO, d_pool), BF16), zcat_ref[TB:TB + HALO, :])
    pool_ref[0] = a[TB - HALO:, :]

    u = jax.nn.gelu(proj[:, d_pool:d_pool + d_sgu])
    v = jax.nn.gelu(proj[:, d_pool + d_sgu:])
    r = lax.broadcasted_iota(I32, (CHUNK, CHUNK), 0)
    c = lax.broadcasted_iota(I32, (CHUNK, CHUNK), 1)
    tril = r >= c
    for h in range(d_sgu // SGU_HEAD_DIM):
        l0, l1 = h * SGU_HEAD_DIM, (h + 1) * SGU_HEAD_DIM
        vn = _ln_rows(v[:, l0:l1], sg_ref[h:h + 1, :], sb_ref[h:h + 1, :]).astype(BF16)
        wst = jnp.where(tril, ws_ref[h], 0.0).astype(BF16)
        n_ch = TB // CHUNK
        mixed_all = _dot(wst, jnp.concatenate([vn[ci * CHUNK:(ci + 1) * CHUNK] for ci in range(n_ch)], axis=1))
        for ci in range(n_ch):
            r0, r1 = ci * CHUNK, (ci + 1) * CHUNK
            mixed = mixed_all[:, ci * SGU_HEAD_DIM:(ci + 1) * SGU_HEAD_DIM] + bsb_ref[h]
            mixin_ref[r0:r1, d_pool + l0:d_pool + l1] = (u[r0:r1, l0:l1] * mixed).astype(BF16)


def _sample_tile(T, NB, proj, st_ref, wpool_ref, pscale_ref, sg_ref, sb_ref, wrow_ref, brow_ref,
                 a_ref, vn_ref, dbuf_ref, mixin_ref):
    d_pool = dbuf_ref.shape[1]
    d_pool_g = d_pool // len(POOL_WINDOWS)
    d_sgu = mixin_ref.shape[1] - d_pool

    a = proj[:, :d_pool]
    for t in range(T):
        a_ref[t] = a[t * NB:(t + 1) * NB]

    for g, w in enumerate(POOL_WINDOWS):
        c0, c1 = g * d_pool_g, (g + 1) * d_pool_g
        for t in range(T):
            cnt = float(min(PAST_LEN + t + 1, w))
            acc = None
            for s in range(POOL_BUF + t - w + 1, POOL_BUF + t + 1):
                if s < POOL_BUF:
                    term = st_ref[s, :, c0:c1]
                else:
                    term = a[(s - POOL_BUF) * NB:(s - POOL_BUF + 1) * NB, c0:c1]
                acc = term if acc is None else acc + term
            d = acc / cnt - a[t * NB:(t + 1) * NB, c0:c1]
            dbuf_ref[t * NB:(t + 1) * NB, c0:c1] = d.astype(BF16)
        y = _dot(dbuf_ref[0:T * NB, c0:c1], wpool_ref[g])
        mixin_ref[:, c0:c1] = (y * pscale_ref[:, c0:c1]).astype(BF16)

    u = jax.nn.gelu(proj[:, d_pool:d_pool + d_sgu])
    v = jax.nn.gelu(proj[:, d_pool + d_sgu:])
    for h in range(d_sgu // SGU_HEAD_DIM):
        l0, l1 = h * SGU_HEAD_DIM, (h + 1) * SGU_HEAD_DIM
        vn = _ln_rows(v[:, l0:l1], sg_ref[h:h + 1, :], sb_ref[h:h + 1, :])
        for t in range(T):
            vn_ref[t, :, l0:l1] = vn[t * NB:(t + 1) * NB]
    for t in range(T):
        mixed = brow_ref[t:t + 1, :]
        for s in range(t + 1):
            mixed = mixed + wrow_ref[t, s:s + 1, :] * vn_ref[s]
        mixin_ref[t * NB:(t + 1) * NB, d_pool:] = (u[t * NB:(t + 1) * NB] * mixed).astype(BF16)


def _mixer_body(alpha, n_pt, tiles_per_seq, xp_ref, xs_ref, st_ref, win_ref, band_ref, wpool_ref, pscale_ref,
                sg_ref, sb_ref, ws_ref, bsb_ref, wrow_ref, brow_ref, wout_ref, g1_ref, b1_ref,
                rw_ref, rb_ref, upper_ref, lstrict_ref,
                h1_ref, h1b_ref, pool_ref, a_ref, vn_ref, dest_ref, wsel_ref, cnt_ref,
                zcat_ref, mixin_ref, hprev_ref):
    i = pl.program_id(0)
    n_tiles = pl.num_programs(0) - 1

    def route_previous_tile():
        h = hprev_ref[...]
        _route_tile(h, h.astype(BF16), rw_ref, rb_ref, upper_ref, lstrict_ref,
                    dest_ref, wsel_ref, cnt_ref)

    @pl.when(i == 0)
    def _():
        zcat_ref[0:HALO, :] = jnp.zeros((HALO, zcat_ref.shape[1]), BF16)
        hprev_ref[...] = jnp.zeros(hprev_ref.shape, F32)

    @pl.when(i < n_pt)
    def _():
        x = xp_ref[0]
        proj = _dot(x.astype(BF16), win_ref[...])
        route_previous_tile()
        _prompt_tile(i % tiles_per_seq, tiles_per_seq, x, proj, band_ref, wpool_ref, pscale_ref, sg_ref, sb_ref,
                     ws_ref, bsb_ref, pool_ref, zcat_ref, mixin_ref)
        h1_ref[...] = alpha * x

    @pl.when(i >= n_pt)
    def _():
        route_previous_tile()

    @pl.when((i >= n_pt) & (i < n_tiles))
    def _():
        T, NB, D = xs_ref.shape
        x = xs_ref[...].reshape(T * NB, D)
        proj = _dot(x.astype(BF16), win_ref[...])
        _sample_tile(T, NB, proj, st_ref, wpool_ref, pscale_ref, sg_ref, sb_ref, wrow_ref, brow_ref,
                     a_ref, vn_ref, zcat_ref, mixin_ref)
        h1_ref[...] = alpha * x

    @pl.when(i < n_tiles)
    def _():
        mix = _dot(mixin_ref[...], wout_ref[...])
        h1 = _ln_rows(h1_ref[...] + mix, g1_ref[...], b1_ref[...])
        h1_ref[...] = h1
        h1b_ref[...] = h1.astype(BF16)
        hprev_ref[...] = h1


def _pool_band(tm):
    t = np.arange(tm)[:, None] + HALO
    s = np.arange(tm + HALO)[None, :]
    return jnp.asarray(np.stack([((s <= t) & (s >= t - w + 1)) for w in POOL_WINDOWS]).astype(np.float32), BF16)


def _mixer(xp, xs_t, st_t, win_b, wpool_b, pscale, sg, sb, ws, bsb, wrow, brow, wout_b, g1, b1, rwt, rb_col, alpha):
    B, T, D = xp.shape
    TS, NS, _ = xs_t.shape
    d_pool = st_t.shape[2]
    d_sgu = wrow.shape[2]
    tps = T // TB
    n_pt = B * tps
    NB = TB // TS
    n_st = NS // NB
    n = B * T + NS * TS
    nblk = n_pt + n_st
    pt = lambda i: jnp.minimum(i, n_pt - 1)
    stile = lambda i: jnp.clip(i - n_pt, 0, n_st - 1)
    blk = lambda i: jnp.minimum(i, nblk - 1)
    routed = lambda i: jnp.maximum(i - 1, 0)
    body = functools.partial(_mixer_body, alpha, n_pt, tps)
    rw_hi = rwt.astype(BF16)
    rw_lo = (rwt - rw_hi.astype(F32)).astype(BF16)
    upper = jnp.asarray(np.triu(np.ones((TB, TB), np.float32), 1), BF16)
    lstrict = jnp.asarray(np.tril(np.ones((N_EXPERTS, N_EXPERTS), np.float32), -1), BF16)
    consts = (win_b, _pool_band(TB), wpool_b, pscale, sg, sb, ws, bsb, wrow, brow, wout_b, g1, b1,
              jnp.concatenate([rw_hi, rw_lo], axis=0), rb_col, upper, lstrict)
    return pl.pallas_call(
        body,
        grid=(nblk + 1,),
        in_specs=[
            pl.BlockSpec((1, TB, D), lambda i: (pt(i) // tps, pt(i) % tps, 0)),
            pl.BlockSpec((TS, NB, D), lambda i: (0, stile(i), 0), pipeline_mode=pl.Buffered(1)),
            pl.BlockSpec((POOL_BUF, NB, d_pool), lambda i: (0, stile(i), 0), pipeline_mode=pl.Buffered(1)),
        ] + [_const_spec(c.shape) for c in consts],
        out_specs=[
            pl.BlockSpec((TB, D), lambda i: (blk(i), 0)),
            pl.BlockSpec((TB, D), lambda i: (blk(i), 0)),
            pl.BlockSpec((1, HALO, d_pool), lambda i: (pt(i) // tps, 0, 0)),
            pl.BlockSpec((TS, NB, d_pool), lambda i: (0, stile(i), 0)),
            pl.BlockSpec((TS, NB, d_sgu), lambda i: (0, stile(i), 0)),
            pl.BlockSpec((1, TOP_K, TB), lambda i: (routed(i), 0, 0)),
            pl.BlockSpec((1, TOP_K, TB), lambda i: (routed(i), 0, 0)),
            pl.BlockSpec((1, N_EXPERTS, 128), lambda i: (routed(i), 0, 0)),
        ],
        out_shape=[
            jax.ShapeDtypeStruct((n, D), F32),
            jax.ShapeDtypeStruct((n, D), BF16),
            jax.ShapeDtypeStruct((B, HALO, d_pool), F32),
            jax.ShapeDtypeStruct((TS, NS, d_pool), F32),
            jax.ShapeDtypeStruct((TS, NS, d_sgu), F32),
            jax.ShapeDtypeStruct((nblk, TOP_K, TB), I32),
            jax.ShapeDtypeStruct((nblk, TOP_K, TB), F32),
            jax.ShapeDtypeStruct((nblk, N_EXPERTS, 128), I32),
        ],
        scratch_shapes=[
            pltpu.VMEM((TB + HALO, d_pool), BF16),
            pltpu.VMEM((TB, D), BF16),
            pltpu.VMEM((TB, D), F32),
        ],
        compiler_params=pltpu.CompilerParams(
            dimension_semantics=("arbitrary",), vmem_limit_bytes=VMEM_LIMIT),
        name="mixer",
    )(xp, xs_t, st_t, *consts)


def _route_tile(h1, h1b, rw_ref, rb_ref, upper_ref, lstrict_ref, dest_ref, wsel_ref, cnt_ref):
    per_group = N_EXPERTS // N_EXPERT_GROUPS
    neg = -jnp.inf
    nt = (((1,), (1,)), ((), ()))
    h_lo = (h1 - h1b.astype(F32)).astype(BF16)
    by_h_hi = lax.dot_general(rw_ref[...], h1b, nt, preferred_element_type=F32)
    logits_t = (by_h_hi[:N_EXPERTS] + by_h_hi[N_EXPERTS:]
                + lax.dot_general(rw_ref[0:N_EXPERTS, :], h_lo, nt, preferred_element_type=F32))
    s_t = jax.nn.sigmoid(logits_t)
    b_t = s_t + rb_ref[...]

    io_g = lax.broadcasted_iota(I32, (per_group, TB), 0)
    gs = []
    for g in range(N_EXPERT_GROUPS):
        xg = b_t[g * per_group:(g + 1) * per_group, :]
        m1 = jnp.max(xg, axis=0, keepdims=True)
        i1 = jnp.min(jnp.where(xg == m1, io_g, per_group), axis=0, keepdims=True)
        m2 = jnp.max(jnp.where(io_g == i1, neg, xg), axis=0, keepdims=True)
        gs.append(m1 + m2)
    masked = []
    for g in range(N_EXPERT_GROUPS):
        rank = jnp.zeros((1, TB), F32)
        for g2 in range(N_EXPERT_GROUPS):
            if g2 != g:
                ahead = (gs[g2] >= gs[g]) if g2 < g else (gs[g2] > gs[g])
                rank = rank + jnp.where(ahead, 1.0, 0.0)
        keep = rank < float(TOPK_GROUPS)
        masked.append(jnp.where(keep, b_t[g * per_group:(g + 1) * per_group, :], neg))
    xm = jnp.concatenate(masked, axis=0)

    io_e = lax.broadcasted_iota(I32, (N_EXPERTS, TB), 0)
    onehots = []
    sel = jnp.zeros((N_EXPERTS, TB), F32)
    for _ in range(TOP_K):
        m = jnp.max(xm, axis=0, keepdims=True)
        idx = jnp.min(jnp.where(xm == m, io_e, N_EXPERTS), axis=0, keepdims=True)
        oh = io_e == idx
        onehots.append(oh)
        sel = jnp.where(oh, 1.0, sel)
        xm = jnp.where(oh, neg, xm)

    ssel = sel * s_t
    denom = jnp.sum(ssel, axis=0, keepdims=True)
    comb = ssel / denom * ROUTE_SCALE

    rank_t = _dot(sel.astype(BF16), upper_ref[...])
    cnt = jnp.sum(sel, axis=1, keepdims=True)
    cnt_i = cnt.astype(I32)
    seg16 = ((cnt_i + (SEG - 1)) // SEG).astype(F32)
    off16 = _dot(lstrict_ref[...], jnp.broadcast_to(seg16, (N_EXPERTS, 128)).astype(BF16))
    d_t = rank_t + off16[:, 0:1] * float(SEG)
    for k in range(TOP_K):
        dest_ref[0, k:k + 1, :] = jnp.sum(jnp.where(onehots[k], d_t, 0.0), axis=0, keepdims=True).astype(I32)
        wsel_ref[0, k:k + 1, :] = jnp.sum(jnp.where(onehots[k], comb, 0.0), axis=0, keepdims=True)
    cnt_ref[0] = jnp.broadcast_to(cnt_i, (N_EXPERTS, 128))


N_LIST_REFS = 3 * len(COPY_ROWS)


def _start_block_copies(b, copy_lists, make_copy):
    for k, rows in enumerate(COPY_ROWS):
        table_l, table_g, counts = copy_lists[3 * k:3 * k + 3]
        width = table_l.shape[0] // counts.shape[0]
        n = counts[b]

        def start(j, table_l=table_l, table_g=table_g, width=width, rows=rows):
            idx = b * width + j
            make_copy(pl.multiple_of(table_l[idx], min(rows, SEG)), pl.multiple_of(table_g[idx], SEGG), rows).start()

        def eight(q, c, start=start):
            for u in range(8):
                start(8 * q + u)
            return c

        def one(j, c, start=start):
            start(j)
            return c

        lax.fori_loop(0, n // 8, eight, 0)
        lax.fori_loop(n // 8 * 8, n, one, 0)


def _wait_row_units(count, max_count, make_wait_copy):
    for k in range(int(max_count).bit_length()):
        @pl.when(((count >> k) & 1) == 1)
        def _():
            make_wait_copy(SEGG << k).wait()


def _dispatch_body(*refs):
    copy_lists = refs[:N_LIST_REFS]
    (units, nch, tail_g, tail_n, dest_ref, wsel_ref, h_ref, tokid_ref, ones_ref,
     xs_hbm, src_ref, wrow_ref, xloc_ref, zero_ref, sems, tail_sem) = refs[N_LIST_REFS:]
    b = pl.program_id(0)
    nb = pl.num_programs(0)
    max_units = xloc_ref.shape[1] // SEGG
    slot = b % 2

    def seg_copy(sl):
        return lambda l, g, rows: pltpu.make_async_copy(
            xloc_ref.at[sl, pl.ds(l, rows), :], xs_hbm.at[pl.ds(g, rows), :], sems.at[sl])

    def wait_block(count, sl):
        _wait_row_units(count, max_units, lambda rows: pltpu.make_async_copy(
            xloc_ref.at[sl, pl.ds(0, rows), :], xs_hbm.at[pl.ds(0, rows), :], sems.at[sl]))

    @pl.when(b >= 2)
    def _():
        wait_block(units[jnp.maximum(b - 2, 0)], slot)

    dest = dest_ref[0]
    wsel = wsel_ref[0]
    h = h_ref[...]
    src_ref[...] = jnp.zeros(src_ref.shape, F32)
    wrow_ref[...] = jnp.zeros(wrow_ref.shape, F32)

    row_in_chunk = lax.broadcasted_iota(I32, (KC, TB), 0).astype(F32).astype(BF16)
    wsel_b = wsel.astype(BF16)
    one = jnp.ones((1, TB), BF16)

    def chunk(c):
        r0 = pl.multiple_of(c * KC, KC)
        rel = (dest - r0).astype(F32).astype(BF16)
        sb = jnp.zeros((KC, TB), BF16)
        wm = jnp.zeros((KC, TB), BF16)
        for k in range(TOP_K):
            hit = rel[k:k + 1, :] == row_in_chunk
            sb = jnp.where(hit, one, sb)
            wm = jnp.where(hit, wsel_b[k:k + 1, :], wm)
        xloc_ref[slot, pl.ds(r0, KC), :] = _dot(sb, h).astype(BF16)
        src_ref[0, pl.ds(r0, KC), :] = _dot(sb, tokid_ref[...])
        wrow_ref[0, pl.ds(r0, KC), :] = _dot(wm, ones_ref[...])

    def chunk_pair(cp, carry):
        chunk(2 * cp)
        chunk(2 * cp + 1)
        return carry

    lax.fori_loop(0, nch[b] // 2, chunk_pair, 0)

    @pl.when(nch[b] % 2 == 1)
    def _():
        chunk(nch[b] - 1)

    _start_block_copies(b, copy_lists, seg_copy(slot))

    def tail_copy(g):
        return pltpu.make_async_copy(zero_ref, xs_hbm.at[pl.ds(g, SEGG), :], tail_sem)

    def tails(wait):
        def per_expert(e, carry):
            def per_granule(j, c):
                cp = tail_copy(pl.multiple_of(tail_g[e] + j * SEGG, SEGG))
                if wait:
                    cp.wait()
                else:
                    cp.start()
                return c
            lax.fori_loop(0, tail_n[e], per_granule, 0)
            return carry
        lax.fori_loop(0, N_EXPERTS, per_expert, 0)

    @pl.when(b == nb - 1)
    def _():
        zero_ref[...] = jnp.zeros(zero_ref.shape, BF16)
        tails(False)

        @pl.when(b >= 1)
        def _():
            wait_block(units[jnp.maximum(b - 1, 0)], 1 - slot)

        wait_block(units[b], slot)
        tails(True)


def _dispatch(meta, dest, wsel, h1b_all, n_rows_sorted, m_out_max):
    n, D = h1b_all.shape
    nb = n // TB
    tokid = jnp.asarray(np.broadcast_to(np.arange(TB, dtype=np.float32)[:, None], (TB, 128)), BF16)
    ones = jnp.ones((TB, 128), BF16)
    cs = lambda shape: pl.BlockSpec(shape, lambda i, *_: (0,) * len(shape), pipeline_mode=pl.Buffered(1))
    grid_spec = pltpu.PrefetchScalarGridSpec(
        num_scalar_prefetch=N_LIST_REFS + 4,
        grid=(nb,),
        in_specs=[
            pl.BlockSpec((1, TOP_K, TB), lambda i, *_: (i, 0, 0)),
            pl.BlockSpec((1, TOP_K, TB), lambda i, *_: (i, 0, 0)),
            pl.BlockSpec((TB, D), lambda i, *_: (i, 0)),
            cs(tokid.shape), cs(ones.shape),
        ],
        out_specs=[
            pl.BlockSpec(memory_space=pl.ANY),
            pl.BlockSpec((1, m_out_max, 128), lambda i, *_: (i, 0, 0)),
            pl.BlockSpec((1, m_out_max, 128), lambda i, *_: (i, 0, 0)),
        ],
        scratch_shapes=[
            pltpu.VMEM((2, m_out_max, D), BF16),
            pltpu.VMEM((SEGG, D), BF16),
            pltpu.SemaphoreType.DMA((2,)),
            pltpu.SemaphoreType.DMA(()),
        ],
    )
    return pl.pallas_call(
        _dispatch_body,
        grid_spec=grid_spec,
        out_shape=[
            jax.ShapeDtypeStruct((n_rows_sorted, D), BF16),
            jax.ShapeDtypeStruct((nb, m_out_max, 128), F32),
            jax.ShapeDtypeStruct((nb, m_out_max, 128), F32),
        ],
        compiler_params=pltpu.CompilerParams(
            dimension_semantics=("arbitrary",), vmem_limit_bytes=VMEM_LIMIT, has_side_effects=True),
        name="dispatch",
    )(*meta["copy_lists"], meta["units"], meta["nch"], meta["tail_g"], meta["tail_n"], dest, wsel, h1b_all, tokid,
      ones)


def _ffn_body(tile_e, tile_src, tile_rows, e_slot, e_next, x_ref, wg_hbm, wu_hbm, wd_hbm, y_ref,
              wg_f, wu_f, wd_f, wgu_b, wd_b, sems):
    i = pl.program_id(0)
    f = wg_f.shape[2]
    e = tile_e[i]
    e_prev = tile_e[jnp.maximum(i - 1, 0)]

    def weight_copies(expert, slot):
        return (pltpu.make_async_copy(wg_hbm.at[expert], wg_f.at[slot], sems.at[slot, 0]),
                pltpu.make_async_copy(wu_hbm.at[expert], wu_f.at[slot], sems.at[slot, 1]),
                pltpu.make_async_copy(wd_hbm.at[expert], wd_f.at[slot], sems.at[slot, 2]))

    @pl.when((i == 0) | (e != e_prev))
    def _():
        slot = e_slot[e]

        @pl.when(i == 0)
        def _():
            for cp in weight_copies(e, slot):
                cp.start()

        for cp in weight_copies(e, slot):
            cp.wait()
        nxt = e_next[e]

        @pl.when(nxt >= 0)
        def _():
            for cp in weight_copies(nxt, 1 - slot):
                cp.start(priority=1)

        wgu_b[:, :f] = wg_f[slot].astype(BF16)
        wgu_b[:, f:] = wu_f[slot].astype(BF16)
        wd_b[...] = wd_f[slot].astype(BF16)

    def swiglu(x):
        gu = _dot(x, wgu_b[...])
        act = (jax.nn.silu(gu[:, :f]) * gu[:, f:]).astype(BF16)
        return _dot(act, wd_b[...]).astype(BF16)

    for m in range(TRC, TR + 1, TRC):
        @pl.when(tile_rows[i] == m)
        def _(m=m):
            y_ref[0:m, :] = swiglu(x_ref[0:m, :])
            if m < TR:
                y_ref[m:, :] = jnp.zeros((TR - m, y_ref.shape[1]), BF16)


def _expert_ffn(meta, xs, w_gate, w_up, w_down):
    R, D = xs.shape
    E, _, f = w_gate.shape
    nt = R // TR
    grid_spec = pltpu.PrefetchScalarGridSpec(
        num_scalar_prefetch=5,
        grid=(nt,),
        in_specs=[
            pl.BlockSpec((TR, D), lambda i, te, ts, *_: (ts[i], 0)),
            pl.BlockSpec(memory_space=pl.ANY),
            pl.BlockSpec(memory_space=pl.ANY),
            pl.BlockSpec(memory_space=pl.ANY),
        ],
        out_specs=pl.BlockSpec((TR, D), lambda i, te, ts, *_: (ts[i], 0)),
        scratch_shapes=[
            pltpu.VMEM((2, D, f), F32),
            pltpu.VMEM((2, D, f), F32),
            pltpu.VMEM((2, f, D), F32),
            pltpu.VMEM((D, 2 * f), BF16),
            pltpu.VMEM((f, D), BF16),
            pltpu.SemaphoreType.DMA((2, 3)),
        ],
    )
    return pl.pallas_call(
        _ffn_body,
        grid_spec=grid_spec,
        out_shape=jax.ShapeDtypeStruct((R, D), BF16),
        compiler_params=pltpu.CompilerParams(
            dimension_semantics=("arbitrary",), vmem_limit_bytes=VMEM_LIMIT),
        name="expert_ffn",
    )(meta["tile_e"], meta["tile_src"], meta["tile_rows"], meta["e_slot"], meta["e_next"],
      xs, w_gate, w_up, w_down)


def _combine_body(alpha, n_pb, *refs):
    copy_lists = refs[:N_LIST_REFS]
    (chunk_units, nch, src_ref, wrow_ref, h1_ref, h1b_ref, pp_ref, ps_ref,
     wsg_ref, wsu_ref, wsd_ref, g2_ref, b2_ref, wpg_ref, bpg_ref, wpe_ref, ys_hbm,
     yp_ref, ysm_ref, yloc_ref, acc_ref, sems) = refs[N_LIST_REFS:]
    b = pl.program_id(0)
    n_chunks = nch[b]
    max_chunks = sems.shape[0]

    def seg_copy(l, g, rows):
        return pltpu.make_async_copy(ys_hbm.at[pl.ds(g, rows), :], yloc_ref.at[pl.ds(l, rows), :],
                                     sems.at[l // KCC])

    def fetch_block(blk):
        _start_block_copies(blk, copy_lists, seg_copy)

    @pl.when(b == 0)
    def _():
        yloc_ref[...] = jnp.zeros(yloc_ref.shape, BF16)
        fetch_block(b)

    hb = h1b_ref[...]
    sh = (jax.nn.silu(_dot(hb, wsg_ref[...])) * _dot(hb, wsu_ref[...])).astype(BF16)
    acc_ref[...] = _dot(sh, wsd_ref[...])

    def chunk(c, carry):
        def wait_rows(rows):
            return pltpu.make_async_copy(ys_hbm.at[pl.ds(0, rows), :], yloc_ref.at[pl.ds(0, rows), :], sems.at[c])

        _wait_row_units(chunk_units[b * max_chunks + c], KCC // SEGG, wait_rows)

        r0 = pl.multiple_of(c * KCC, KCC)
        src = src_ref[0, pl.ds(r0, KCC), :]
        w = wrow_ref[0, pl.ds(r0, KCC), :]
        lane = lax.broadcasted_iota(I32, src.shape, 1).astype(F32)
        wm = jnp.concatenate([jnp.where(src == lane + float(o), w, 0.0) for o in range(0, TB, src.shape[1])],
                             axis=1).astype(BF16)
        acc_ref[...] += lax.dot_general(wm, yloc_ref[pl.ds(r0, KCC), :],
                                        (((0,), (0,)), ((), ())), preferred_element_type=F32)
        return carry

    lax.fori_loop(0, n_chunks, chunk, 0)

    @pl.when(b + 1 < pl.num_programs(0))
    def _():
        fetch_block(b + 1)

    h2 = _ln_rows(alpha * h1_ref[...] + acc_ref[...], g2_ref[...], b2_ref[...])
    gate = jax.nn.sigmoid(_dot(h2.astype(BF16), wpg_ref[...]) + bpg_ref[...])
    p = jnp.where(b < n_pb, pp_ref[...], ps_ref[...])
    pe = _dot(p.astype(BF16), wpe_ref[...])
    y = h2 + gate * pe

    @pl.when(b < n_pb)
    def _():
        yp_ref[...] = y

    @pl.when(b >= n_pb)
    def _():
        ysm_ref[...] = y


def _combine(meta, src_rep, w_rep, h1_all, h1b_all, p_prompt, p_sample, wsg_b, wsu_b, wsd_b, g2, b2, wpg_b, bpg,
             wpe_b, ys, m_out_max, n_prompt, alpha):
    n, D = h1_all.shape
    nb = n // TB
    n_pb = n_prompt // TB
    d_pe = p_prompt.shape[1]
    cs = lambda shape: pl.BlockSpec(shape, lambda i, *_: (0,) * len(shape), pipeline_mode=pl.Buffered(1))
    grid_spec = pltpu.PrefetchScalarGridSpec(
        num_scalar_prefetch=N_LIST_REFS + 2,
        grid=(nb,),
        in_specs=[
            pl.BlockSpec((1, m_out_max, 128), lambda i, *_: (i, 0, 0)),
            pl.BlockSpec((1, m_out_max, 128), lambda i, *_: (i, 0, 0)),
            pl.BlockSpec((TB, D), lambda i, *_: (i, 0)),
            pl.BlockSpec((TB, D), lambda i, *_: (i, 0)),
            pl.BlockSpec((TB, d_pe), lambda i, *_: (jnp.minimum(i, n_pb - 1), 0)),
            pl.BlockSpec((TB, d_pe), lambda i, *_: (jnp.maximum(i - n_pb, 0), 0)),
            cs(wsg_b.shape), cs(wsu_b.shape), cs(wsd_b.shape), cs(g2.shape), cs(b2.shape),
            cs(wpg_b.shape), cs(bpg.shape), cs(wpe_b.shape),
            pl.BlockSpec(memory_space=pl.ANY),
        ],
        out_specs=[
            pl.BlockSpec((TB, D), lambda i, *_: (jnp.minimum(i, n_pb - 1), 0)),
            pl.BlockSpec((TB, D), lambda i, *_: (jnp.maximum(i - n_pb, 0), 0)),
        ],
        scratch_shapes=[
            pltpu.VMEM((m_out_max, D), BF16),
            pltpu.VMEM((TB, D), F32),
            pltpu.SemaphoreType.DMA((m_out_max // KCC,)),
        ],
    )
    return pl.pallas_call(
        functools.partial(_combine_body, alpha, n_pb),
        grid_spec=grid_spec,
        out_shape=[jax.ShapeDtypeStruct((n_prompt, D), F32), jax.ShapeDtypeStruct((n - n_prompt, D), F32)],
        compiler_params=pltpu.CompilerParams(
            dimension_semantics=("arbitrary",), vmem_limit_bytes=VMEM_LIMIT),
        name="combine",
    )(*meta["copy_lists"], meta["chunk_units"], meta["nchc"], src_rep, w_rep, h1_all, h1b_all, p_prompt, p_sample,
      wsg_b, wsu_b, wsd_b, g2, b2, wpg_b, bpg, wpe_b, ys)


def _sort_meta(cnt, n_tiles_max, m_out_max):
    segl = (cnt + (SEG - 1)) // SEG * SEG
    loc = jnp.cumsum(segl, axis=1) - segl
    n8 = (cnt + (SEGG - 1)) // SEGG
    segp = n8 * SEGG
    before = jnp.cumsum(segp, axis=0) - segp
    total = jnp.sum(segp, axis=0)
    total_c = (total + (TRC - 1)) // TRC * TRC
    total_p = (total + (TR - 1)) // TR * TR
    ends = jnp.cumsum(total_p)
    base = ends - total_p
    glob = base[None, :] + before
    n_valid = ends[-1] // TR
    tiles = jnp.arange(n_tiles_max, dtype=I32)
    tile_src = jnp.minimum(tiles, n_valid - 1)
    tile_e = jnp.minimum(jnp.sum((ends[None, :] <= (tile_src * TR)[:, None]).astype(I32), axis=1), N_EXPERTS - 1)
    left = jnp.sum(jnp.where(tile_e[:, None] == jnp.arange(N_EXPERTS, dtype=I32)[None, :],
                             (base + total_c)[None, :], 0), axis=1) - tile_src * TR
    tile_rows = jnp.where(tiles < n_valid, jnp.clip(left, 0, TR), 0)
    experts = jnp.arange(N_EXPERTS, dtype=I32)
    active = total_p > 0
    later = active[None, :] & (experts[None, :] > experts[:, None])
    e_next = jnp.min(jnp.where(later, experts[None, :], N_EXPERTS), axis=1)
    def copy_list(count, first_l, first_g, rows, width):
        end = jnp.cumsum(count, axis=1)
        start = end - count
        j = jnp.arange(width, dtype=I32)[None, :, None]
        in_seg = (start[:, None, :] <= j) & (j < end[:, None, :])
        step = (j - start[:, None, :]) * rows
        lst_l = jnp.sum(jnp.where(in_seg, first_l[:, None, :] + step, 0), axis=2)
        lst_g = jnp.sum(jnp.where(in_seg, first_g[:, None, :] + step, 0), axis=2)
        return lst_l, lst_g, end[:, -1], j[:, :, 0] < end[:, -1:]

    chunks = jnp.arange(m_out_max // KCC, dtype=I32)[None, :, None]
    chunk_units = 0
    lists = {}
    left_units, sent_rows, larger = n8, jnp.zeros_like(n8), None
    for rows in COPY_ROWS:
        per = rows // SEGG
        count = left_units // per
        width = m_out_max // rows if larger is None else N_EXPERTS * (larger // rows - 1)
        larger = rows
        lst_l, lst_g, lst_n, ok = copy_list(count, loc + sent_rows, glob + sent_rows, rows, width)
        chunk_units = chunk_units + jnp.sum(jnp.where(ok[:, None, :] & (lst_l[:, None, :] // KCC == chunks), per, 0), axis=2)
        lists[rows] = (lst_l.reshape(-1).astype(I32), lst_g.reshape(-1).astype(I32), lst_n.astype(I32))
        left_units, sent_rows = left_units - count * per, sent_rows + count * rows
    local_rows = jnp.sum(segl, axis=1)
    return {
        "copy_lists": tuple(a for rows in COPY_ROWS for a in lists[rows]),
        "units": jnp.sum(n8, axis=1).astype(I32),
        "chunk_units": chunk_units.reshape(-1).astype(I32),
        "nch": ((local_rows + (KC - 1)) // KC).astype(I32),
        "nchc": ((local_rows + (KCC - 1)) // KCC).astype(I32),
        "tail_g": (base + total).astype(I32),
        "tail_n": ((total_c - total) // SEGG).astype(I32),
        "tile_e": tile_e.astype(I32),
        "tile_src": tile_src.astype(I32),
        "tile_rows": tile_rows.astype(I32),
        "e_slot": ((jnp.cumsum(active.astype(I32)) - 1) % 2).astype(I32),
        "e_next": jnp.where(e_next < N_EXPERTS, e_next, -1).astype(I32),
    }


def _layer(xp, xs, st, pp, ps, w_in, w_pool, pool_scale, sgu_ln_g, sgu_ln_b, w_s, b_s, w_out, ln1_g, ln1_b,
           router_w, router_bias, w_gate, w_up, w_down, ws_gate, ws_up, ws_down, ln2_g, ln2_b, w_pe, w_pgate,
           b_pgate, alpha):
    B, T, D = xp.shape
    NS, TS, _ = xs.shape
    n_prompt, n_sample = B * T, NS * TS
    n = n_prompt + n_sample
    n_heads = w_s.shape[0]
    row = lambda v: v.reshape(1, -1)

    x_t = jnp.transpose(xs, (1, 0, 2))
    st_t = jnp.transpose(st, (1, 0, 2))
    bsb = jnp.broadcast_to(b_s[:, :, None], (n_heads, CHUNK, SGU_HEAD_DIM))
    wrow = jnp.repeat(jnp.transpose(w_s[:, :TS, :TS], (1, 2, 0)), SGU_HEAD_DIM, axis=2)
    brow = jnp.repeat(jnp.transpose(b_s[:, :TS], (1, 0)), SGU_HEAD_DIM, axis=1)
    h1_all, h1b_all, pool_p, a_s, vn_s, dest, wsel, cnt = _mixer(
        xp, x_t, st_t, w_in.astype(BF16), w_pool.astype(BF16), row(pool_scale), sgu_ln_g, sgu_ln_b, w_s, bsb,
        wrow, brow, w_out.astype(BF16), row(ln1_g), row(ln1_b), jnp.transpose(router_w),
        router_bias.reshape(-1, 1), alpha)
    nb = n // TB
    m_out_max = TB * TOP_K + N_EXPERTS * SEG
    rows_max = n * TOP_K + nb * N_EXPERTS * (SEGG - 1) + N_EXPERTS * (TR - SEGG)
    n_tiles_max = -(-rows_max // TR)
    meta = _sort_meta(cnt[:, :, 0], n_tiles_max, m_out_max)
    xsort, src_rep, w_rep = _dispatch(meta, dest, wsel, h1b_all, n_tiles_max * TR, m_out_max)
    ysort = _expert_ffn(meta, xsort, w_gate, w_up, w_down)

    nbs = TB // TS
    ps_t = jnp.transpose(ps.reshape(NS // nbs, nbs, TS, -1), (0, 2, 1, 3)).reshape(n_sample, -1)
    y_p, y_s = _combine(meta, src_rep, w_rep, h1_all, h1b_all, pp.reshape(n_prompt, -1), ps_t,
                        ws_gate.astype(BF16), ws_up.astype(BF16),
                        ws_down.astype(BF16), row(ln2_g), row(ln2_b), w_pgate.astype(BF16), row(b_pgate),
                        w_pe.astype(BF16), ysort, m_out_max, n_prompt, alpha)

    yp = y_p.reshape(B, T, D)
    ys = jnp.transpose(y_s.reshape(NS // nbs, TS, nbs, D), (0, 2, 1, 3)).reshape(NS, TS, D)
    new_pool_p = pool_p[:, HALO - POOL_BUF:, :]
    new_pool_s = jnp.concatenate([st, jnp.transpose(a_s, (1, 0, 2))], axis=1)[:, -POOL_BUF:]
    vn = jnp.transpose(vn_s, (1, 0, 2)).reshape(NS, TS, n_heads, SGU_HEAD_DIM)
    return yp, ys, new_pool_p, new_pool_s, vn


def kernel(x_prompt, x_sample, state_pool, p_prompt, p_sample, w_in, w_pool, pool_scale, sgu_ln_g, sgu_ln_b, w_s, b_s, w_out, ln1_g, ln1_b, router_w, router_bias, w_gate, w_up, w_down, ws_gate, ws_up, ws_down, ln2_g, ln2_b, w_pe, w_pgate, b_pgate):
    depth = w_in.shape[0]
    alpha = (2.0 * depth) ** 0.25
    hp, hs = x_prompt, x_sample
    pool_p, pool_s, v_s = [], [], []
    for i in range(depth):
        hp, hs, bp, bs, vs = _layer(
            hp, hs, state_pool[i], p_prompt[i], p_sample[i], w_in[i], w_pool[i], pool_scale[i], sgu_ln_g[i],
            sgu_ln_b[i], w_s[i], b_s[i], w_out[i], ln1_g[i], ln1_b[i], router_w[i], router_bias[i], w_gate[i],
            w_up[i], w_down[i], ws_gate[i], ws_up[i], ws_down[i], ln2_g[i], ln2_b[i], w_pe[i], w_pgate[i],
            b_pgate[i], alpha)
        pool_p.append(bp)
        pool_s.append(bs)
        v_s.append(vs)
    return hp, hs, jnp.stack(pool_p), jnp.stack(pool_s), jnp.stack(v_s)
```

```python
import functools

import jax
import jax.numpy as jnp
import numpy as np
from jax import lax
from jax.experimental import pallas as pl
from jax.experimental.pallas import tpu as pltpu

F32 = jnp.float32
BF16 = jnp.bfloat16
I32 = jnp.int32

POOL_WINDOWS = (2, 4, 8, 16)
POOL_BUF = max(POOL_WINDOWS) - 1
CHUNK = 128
SGU_HEAD_DIM = 128
N_EXPERTS = 64
TOP_K = 8
N_EXPERT_GROUPS = 8
TOPK_GROUPS = 4
ROUTE_SCALE = 2.5
LN_EPS = 1e-5
PAST_LEN = 16384

TB = 256
HALO = 16
TR = 512
TRC = 128
SEG = 16
SEGG = 8
COPY_ROWS = (32, 16, 8)
KC = 256
KCC = 512
VMEM_LIMIT = 56 * 1024 * 1024


def _const_spec(shape):
    n = len(shape)
    return pl.BlockSpec(shape, lambda *_: (0,) * n, pipeline_mode=pl.Buffered(1))


def _ln_rows(x, g, b):
    mu = jnp.mean(x, axis=-1, keepdims=True)
    xc = x - mu
    var = jnp.mean(xc * xc, axis=-1, keepdims=True)
    return xc * lax.rsqrt(var + LN_EPS) * g + b


def _dot(a, b):
    return jnp.dot(a, b, preferred_element_type=F32)


def _prompt_tile(j, n_tiles, x, proj, band_ref, wpool_ref, pscale_ref, sg_ref, sb_ref, ws_ref, bsb_ref,
                 pool_ref, zcat_ref, mixin_ref):
    d_pool = zcat_ref.shape[1]
    d_pool_g = d_pool // len(POOL_WINDOWS)
    d_sgu = mixin_ref.shape[1] - d_pool

    a = proj[:, :d_pool]
    zcat_ref[HALO:, :] = a.astype(BF16)

    pos1 = (j * TB + lax.broadcasted_iota(I32, (TB, 1), 0) + 1).astype(F32)
    for g, w in enumerate(POOL_WINDOWS):
        c0, c1 = g * d_pool_g, (g + 1) * d_pool_g
        s = _dot(band_ref[g], zcat_ref[:, c0:c1])
        cnt = jnp.minimum(pos1, float(w))
        d = s / cnt - a[:, c0:c1]
        y = _dot(d.astype(BF16), wpool_ref[g])
        mixin_ref[:, c0:c1] = (y * pscale_ref[:, c0:c1]).astype(BF16)
    zcat_ref[0:HALO, :] = jnp.where(j == n_tiles - 1, jnp.zeros((HALO, d_pool), BF16), zcat_ref[TB:TB + HALO, :])
    pool_ref[0] = a[TB - HALO:, :]

    u = jax.nn.gelu(proj[:, d_pool:d_pool + d_sgu])
    v = jax.nn.gelu(proj[:, d_pool + d_sgu:])
    r = lax.broadcasted_iota(I32, (CHUNK, CHUNK), 0)
    c = lax.broadcasted_iota(I32, (CHUNK, CHUNK), 1)
    tril = r >= c
    for h in range(d_sgu // SGU_HEAD_DIM):
        l0, l1 = h * SGU_HEAD_DIM, (h + 1) * SGU_HEAD_DIM
        vn = _ln_rows(v[:, l0:l1], sg_ref[h:h + 1, :], sb_ref[h:h + 1, :]).astype(BF16)
        wst = jnp.where(tril, ws_ref[h], 0.0).astype(BF16)
        n_ch = TB // CHUNK
        mixed_all = _dot(wst, jnp.concatenate([vn[ci * CHUNK:(ci + 1) * CHUNK] for ci in range(n_ch)], axis=1))
        for ci in range(n_ch):
            r0, r1 = ci * CHUNK, (ci + 1) * CHUNK
            mixed = mixed_all[:, ci * SGU_HEAD_DIM:(ci + 1) * SGU_HEAD_DIM] + bsb_ref[h]
            mixin_ref[r0:r1, d_pool + l0:d_pool + l1] = (u[r0:r1, l0:l1] * mixed).astype(BF16)


def _sample_tile(T, NB, proj, st_ref, wpool_ref, pscale_ref, sg_ref, sb_ref, wrow_ref, brow_ref,
                 a_ref, vn_ref, dbuf_ref, mixin_ref):
    d_pool = dbuf_ref.shape[1]
    d_pool_g = d_pool // len(POOL_WINDOWS)
    d_sgu = mixin_ref.shape[1] - d_pool

    a = proj[:, :d_pool]
    for t in range(T):
        a_ref[t] = a[t * NB:(t + 1) * NB]

    for g, w in enumerate(POOL_WINDOWS):
        c0, c1 = g * d_pool_g, (g + 1) * d_pool_g
        for t in range(T):
            cnt = float(min(PAST_LEN + t + 1, w))
            acc = None
            for s in range(POOL_BUF + t - w + 1, POOL_BUF + t + 1):
                if s < POOL_BUF:
                    term = st_ref[s, :, c0:c1]
                else:
                    term = a[(s - POOL_BUF) * NB:(s - POOL_BUF + 1) * NB, c0:c1]
                acc = term if acc is None else acc + term
            d = acc / cnt - a[t * NB:(t + 1) * NB, c0:c1]
            dbuf_ref[t * NB:(t + 1) * NB, c0:c1] = d.astype(BF16)
        y = _dot(dbuf_ref[0:T * NB, c0:c1], wpool_ref[g])
        mixin_ref[:, c0:c1] = (y * pscale_ref[:, c0:c1]).astype(BF16)

    u = jax.nn.gelu(proj[:, d_pool:d_pool + d_sgu])
    v = jax.nn.gelu(proj[:, d_pool + d_sgu:])
    for h in range(d_sgu // SGU_HEAD_DIM):
        l0, l1 = h * SGU_HEAD_DIM, (h + 1) * SGU_HEAD_DIM
        vn = _ln_rows(v[:, l0:l1], sg_ref[h:h + 1, :], sb_ref[h:h + 1, :])
        for t in range(T):
            vn_ref[t, :, l0:l1] = vn[t * NB:(t + 1) * NB]
    for t in range(T):
        mixed = brow_ref[t:t + 1, :]
        for s in range(t + 1):
            mixed = mixed + wrow_ref[t, s:s + 1, :] * vn_ref[s]
        mixin_ref[t * NB:(t + 1) * NB, d_pool:] = (u[t * NB:(t + 1) * NB] * mixed).astype(BF16)


def _mixer_body(alpha, n_pt, tiles_per_seq, xp_ref, xs_ref, st_ref, win_ref, band_ref, wpool_ref, pscale_ref,
                sg_ref, sb_ref, ws_ref, bsb_ref, wrow_ref, brow_ref, wout_ref, g1_ref, b1_ref,
                rw_ref, rb_ref, upper_ref, lstrict_ref,
                h1_ref, h1b_ref, pool_ref, a_ref, vn_ref, dest_ref, wsel_ref, cnt_ref,
                zcat_ref, mixin_ref, hprev_ref):
    i = pl.program_id(0)
    n_tiles = pl.num_programs(0) - 1

    def route_previous_tile():
        h = hprev_ref[...]
        _route_tile(h, h.astype(BF16), rw_ref, rb_ref, upper_ref, lstrict_ref,
                    dest_ref, wsel_ref, cnt_ref)

    @pl.when(i == 0)
    def _():
        zcat_ref[0:HALO, :] = jnp.zeros((HALO, zcat_ref.shape[1]), BF16)
        hprev_ref[...] = jnp.zeros(hprev_ref.shape, F32)

    @pl.when(i < n_pt)
    def _():
        x = xp_ref[0]
        proj = _dot(x.astype(BF16), win_ref[...])
        route_previous_tile()
        _prompt_tile(i % tiles_per_seq, tiles_per_seq, x, proj, band_ref, wpool_ref, pscale_ref, sg_ref, sb_ref,
                     ws_ref, bsb_ref, pool_ref, zcat_ref, mixin_ref)
        h1_ref[...] = alpha * x

    @pl.when(i >= n_pt)
    def _():
        route_previous_tile()

    @pl.when((i >= n_pt) & (i < n_tiles))
    def _():
        T, NB, D = xs_ref.shape
        x = xs_ref[...].reshape(T * NB, D)
        proj = _dot(x.astype(BF16), win_ref[...])
        _sample_tile(T, NB, proj, st_ref, wpool_ref, pscale_ref, sg_ref, sb_ref, wrow_ref, brow_ref,
                     a_ref, vn_ref, zcat_ref, mixin_ref)
        h1_ref[...] = alpha * x

    @pl.when(i < n_tiles)
    def _():
        mix = _dot(mixin_ref[...], wout_ref[...])
        h1 = _ln_rows(h1_ref[...] + mix, g1_ref[...], b1_ref[...])
        h1_ref[...] = h1
        h1b_ref[...] = h1.astype(BF16)
        hprev_ref[...] = h1


def _pool_band(tm):
    t = np.arange(tm)[:, None] + HALO
    s = np.arange(tm + HALO)[None, :]
    return jnp.asarray(np.stack([((s <= t) & (s >= t - w + 1)) for w in POOL_WINDOWS]).astype(np.float32), BF16)


def _mixer(xp, xs_t, st_t, win_b, wpool_b, pscale, sg, sb, ws, bsb, wrow, brow, wout_b, g1, b1, rwt, rb_col, alpha):
    B, T, D = xp.shape
    TS, NS, _ = xs_t.shape
    d_pool = st_t.shape[2]
    d_sgu = wrow.shape[2]
    tps = T // TB
    n_pt = B * tps
    NB = TB // TS
    n_st = NS // NB
    n = B * T + NS * TS
    nblk = n_pt + n_st
    pt = lambda i: jnp.minimum(i, n_pt - 1)
    stile = lambda i: jnp.clip(i - n_pt, 0, n_st - 1)
    blk = lambda i: jnp.minimum(i, nblk - 1)
    routed = lambda i: jnp.maximum(i - 1, 0)
    body = functools.partial(_mixer_body, alpha, n_pt, tps)
    rw_hi = rwt.astype(BF16)
    rw_lo = (rwt - rw_hi.astype(F32)).astype(BF16)
    upper = jnp.asarray(np.triu(np.ones((TB, TB), np.float32), 1), BF16)
    lstrict = jnp.asarray(np.tril(np.ones((N_EXPERTS, N_EXPERTS), np.float32), -1), BF16)
    consts = (win_b, _pool_band(TB), wpool_b, pscale, sg, sb, ws, bsb, wrow, brow, wout_b, g1, b1,
              jnp.concatenate([rw_hi, rw_lo], axis=0), rb_col, upper, lstrict)
    return pl.pallas_call(
        body,
        grid=(nblk + 1,),
        in_specs=[
            pl.BlockSpec((1, TB, D), lambda i: (pt(i) // tps, pt(i) % tps, 0)),
            pl.BlockSpec((TS, NB, D), lambda i: (0, stile(i), 0), pipeline_mode=pl.Buffered(1)),
            pl.BlockSpec((POOL_BUF, NB, d_pool), lambda i: (0, stile(i), 0), pipeline_mode=pl.Buffered(1)),
        ] + [_const_spec(c.shape) for c in consts],
        out_specs=[
            pl.BlockSpec((TB, D), lambda i: (blk(i), 0)),
            pl.BlockSpec((TB, D), lambda i: (blk(i), 0)),
            pl.BlockSpec((1, HALO, d_pool), lambda i: (pt(i) // tps, 0, 0)),
            pl.BlockSpec((TS, NB, d_pool), lambda i: (0, stile(i), 0)),
            pl.BlockSpec((TS, NB, d_sgu), lambda i: (0, stile(i), 0)),
            pl.BlockSpec((1, TOP_K, TB), lambda i: (routed(i), 0, 0)),
            pl.BlockSpec((1, TOP_K, TB), lambda i: (routed(i), 0, 0)),
            pl.BlockSpec((1, N_EXPERTS, 128), lambda i: (routed(i), 0, 0)),
        ],
        out_shape=[
            jax.ShapeDtypeStruct((n, D), F32),
            jax.ShapeDtypeStruct((n, D), BF16),
            jax.ShapeDtypeStruct((B, HALO, d_pool), F32),
            jax.ShapeDtypeStruct((TS, NS, d_pool), F32),
            jax.ShapeDtypeStruct((TS, NS, d_sgu), F32),
            jax.ShapeDtypeStruct((nblk, TOP_K, TB), I32),
            jax.ShapeDtypeStruct((nblk, TOP_K, TB), F32),
            jax.ShapeDtypeStruct((nblk, N_EXPERTS, 128), I32),
        ],
        scratch_shapes=[
            pltpu.VMEM((TB + HALO, d_pool), BF16),
            pltpu.VMEM((TB, D), BF16),
            pltpu.VMEM((TB, D), F32),
        ],
        compiler_params=pltpu.CompilerParams(
            dimension_semantics=("arbitrary",), vmem_limit_bytes=VMEM_LIMIT),
        name="mixer",
    )(xp, xs_t, st_t, *consts)


def _route_tile(h1, h1b, rw_ref, rb_ref, upper_ref, lstrict_ref, dest_ref, wsel_ref, cnt_ref):
    per_group = N_EXPERTS // N_EXPERT_GROUPS
    neg = -jnp.inf
    nt = (((1,), (1,)), ((), ()))
    h_lo = (h1 - h1b.astype(F32)).astype(BF16)
    by_h_hi = lax.dot_general(rw_ref[...], h1b, nt, preferred_element_type=F32)
    logits_t = (by_h_hi[:N_EXPERTS] + by_h_hi[N_EXPERTS:]
                + lax.dot_general(rw_ref[0:N_EXPERTS, :], h_lo, nt, preferred_element_type=F32))
    s_t = jax.nn.sigmoid(logits_t)
    b_t = s_t + rb_ref[...]

    io_g = lax.broadcasted_iota(I32, (per_group, TB), 0)
    gs = []
    for g in range(N_EXPERT_GROUPS):
        xg = b_t[g * per_group:(g + 1) * per_group, :]
        m1 = jnp.max(xg, axis=0, keepdims=True)
        i1 = jnp.min(jnp.where(xg == m1, io_g, per_group), axis=0, keepdims=True)
        m2 = jnp.max(jnp.where(io_g == i1, neg, xg), axis=0, keepdims=True)
        gs.append(m1 + m2)
    masked = []
    for g in range(N_EXPERT_GROUPS):
        rank = jnp.zeros((1, TB), F32)
        for g2 in range(N_EXPERT_GROUPS):
            if g2 != g:
                ahead = (gs[g2] >= gs[g]) if g2 < g else (gs[g2] > gs[g])
                rank = rank + jnp.where(ahead, 1.0, 0.0)
        keep = rank < float(TOPK_GROUPS)
        masked.append(jnp.where(keep, b_t[g * per_group:(g + 1) * per_group, :], neg))
    xm = jnp.concatenate(masked, axis=0)

    io_e = lax.broadcasted_iota(I32, (N_EXPERTS, TB), 0)
    onehots = []
    sel = jnp.zeros((N_EXPERTS, TB), F32)
    for _ in range(TOP_K):
        m = jnp.max(xm, axis=0, keepdims=True)
        idx = jnp.min(jnp.where(xm == m, io_e, N_EXPERTS), axis=0, keepdims=True)
        oh = io_e == idx
        onehots.append(oh)
        sel = jnp.where(oh, 1.0, sel)
        xm = jnp.where(oh, neg, xm)

    ssel = sel * s_t
    denom = jnp.sum(ssel, axis=0, keepdims=True)
    comb = ssel / denom * ROUTE_SCALE

    rank_t = _dot(sel.astype(BF16), upper_ref[...])
    cnt = jnp.sum(sel, axis=1, keepdims=True)
    cnt_i = cnt.astype(I32)
    seg16 = ((cnt_i + (SEG - 1)) // SEG).astype(F32)
    off16 = _dot(lstrict_ref[...], jnp.broadcast_to(seg16, (N_EXPERTS, 128)).astype(BF16))
    d_t = rank_t + off16[:, 0:1] * float(SEG)
    for k in range(TOP_K):
        dest_ref[0, k:k + 1, :] = jnp.sum(jnp.where(onehots[k], d_t, 0.0), axis=0, keepdims=True).astype(I32)
        wsel_ref[0, k:k + 1, :] = jnp.sum(jnp.where(onehots[k], comb, 0.0), axis=0, keepdims=True)
    cnt_ref[0] = jnp.broadcast_to(cnt_i, (N_EXPERTS, 128))


N_LIST_REFS = 3 * len(COPY_ROWS)


def _start_block_copies(b, copy_lists, make_copy):
    for k, rows in enumerate(COPY_ROWS):
        table_l, table_g, counts = copy_lists[3 * k:3 * k + 3]
        width = table_l.shape[0] // counts.shape[0]
        n = counts[b]

        def start(j, table_l=table_l, table_g=table_g, width=width, rows=rows):
            idx = b * width + j
            make_copy(pl.multiple_of(table_l[idx], min(rows, SEG)), pl.multiple_of(table_g[idx], SEGG), rows).start()

        def eight(q, c, start=start):
            for u in range(8):
                start(8 * q + u)
            return c

        def one(j, c, start=start):
            start(j)
            return c

        lax.fori_loop(0, n // 8, eight, 0)
        lax.fori_loop(n // 8 * 8, n, one, 0)


def _wait_row_units(count, max_count, make_wait_copy):
    for k in range(int(max_count).bit_length()):
        @pl.when(((count >> k) & 1) == 1)
        def _():
            make_wait_copy(SEGG << k).wait()


def _dispatch_body(*refs):
    copy_lists = refs[:N_LIST_REFS]
    (units, nch, tail_g, tail_n, dest_ref, wsel_ref, h_ref, tokid_ref, ones_ref,
     xs_hbm, src_ref, wrow_ref, xloc_ref, zero_ref, sems, tail_sem) = refs[N_LIST_REFS:]
    b = pl.program_id(0)
    nb = pl.num_programs(0)
    max_units = xloc_ref.shape[1] // SEGG
    slot = b % 2

    def seg_copy(sl):
        return lambda l, g, rows: pltpu.make_async_copy(
            xloc_ref.at[sl, pl.ds(l, rows), :], xs_hbm.at[pl.ds(g, rows), :], sems.at[sl])

    def wait_block(count, sl):
        _wait_row_units(count, max_units, lambda rows: pltpu.make_async_copy(
            xloc_ref.at[sl, pl.ds(0, rows), :], xs_hbm.at[pl.ds(0, rows), :], sems.at[sl]))

    @pl.when(b >= 2)
    def _():
        wait_block(units[jnp.maximum(b - 2, 0)], slot)

    dest = dest_ref[0]
    wsel = wsel_ref[0]
    h = h_ref[...]
    src_ref[...] = jnp.zeros(src_ref.shape, F32)
    wrow_ref[...] = jnp.zeros(wrow_ref.shape, F32)

    row_in_chunk = lax.broadcasted_iota(I32, (KC, TB), 0).astype(F32).astype(BF16)
    wsel_b = wsel.astype(BF16)
    one = jnp.ones((1, TB), BF16)

    def chunk(c):
        r0 = pl.multiple_of(c * KC, KC)
        rel = (dest - r0).astype(F32).astype(BF16)
        sb = jnp.zeros((KC, TB), BF16)
        wm = jnp.zeros((KC, TB), BF16)
        for k in range(TOP_K):
            hit = rel[k:k + 1, :] == row_in_chunk
            sb = jnp.where(hit, one, sb)
            wm = jnp.where(hit, wsel_b[k:k + 1, :], wm)
        xloc_ref[slot, pl.ds(r0, KC), :] = _dot(sb, h).astype(BF16)
        src_ref[0, pl.ds(r0, KC), :] = _dot(sb, tokid_ref[...])
        wrow_ref[0, pl.ds(r0, KC), :] = _dot(wm, ones_ref[...])

    def chunk_pair(cp, carry):
        chunk(2 * cp)
        chunk(2 * cp + 1)
        return carry

    lax.fori_loop(0, nch[b] // 2, chunk_pair, 0)

    @pl.when(nch[b] % 2 == 1)
    def _():
        chunk(nch[b] - 1)

    _start_block_copies(b, copy_lists, seg_copy(slot))

    def tail_copy(g):
        return pltpu.make_async_copy(zero_ref, xs_hbm.at[pl.ds(g, SEGG), :], tail_sem)

    def tails(wait):
        def per_expert(e, carry):
            def per_granule(j, c):
                cp = tail_copy(pl.multiple_of(tail_g[e] + j * SEGG, SEGG))
                if wait:
                    cp.wait()
                else:
                    cp.start()
                return c
            lax.fori_loop(0, tail_n[e], per_granule, 0)
            return carry
        lax.fori_loop(0, N_EXPERTS, per_expert, 0)

    @pl.when(b == nb - 1)
    def _():
        zero_ref[...] = jnp.zeros(zero_ref.shape, BF16)
        tails(False)

        @pl.when(b >= 1)
        def _():
            wait_block(units[jnp.maximum(b - 1, 0)], 1 - slot)

        wait_block(units[b], slot)
        tails(True)


def _dispatch(meta, dest, wsel, h1b_all, n_rows_sorted, m_out_max):
    n, D = h1b_all.shape
    nb = n // TB
    tokid = jnp.asarray(np.broadcast_to(np.arange(TB, dtype=np.float32)[:, None], (TB, 128)), BF16)
    ones = jnp.ones((TB, 128), BF16)
    cs = lambda shape: pl.BlockSpec(shape, lambda i, *_: (0,) * len(shape), pipeline_mode=pl.Buffered(1))
    grid_spec = pltpu.PrefetchScalarGridSpec(
        num_scalar_prefetch=N_LIST_REFS + 4,
        grid=(nb,),
        in_specs=[
            pl.BlockSpec((1, TOP_K, TB), lambda i, *_: (i, 0, 0)),
            pl.BlockSpec((1, TOP_K, TB), lambda i, *_: (i, 0, 0)),
            pl.BlockSpec((TB, D), lambda i, *_: (i, 0)),
            cs(tokid.shape), cs(ones.shape),
        ],
        out_specs=[
            pl.BlockSpec(memory_space=pl.ANY),
            pl.BlockSpec((1, m_out_max, 128), lambda i, *_: (i, 0, 0)),
            pl.BlockSpec((1, m_out_max, 128), lambda i, *_: (i, 0, 0)),
        ],
        scratch_shapes=[
            pltpu.VMEM((2, m_out_max, D), BF16),
            pltpu.VMEM((SEGG, D), BF16),
            pltpu.SemaphoreType.DMA((2,)),
            pltpu.SemaphoreType.DMA(()),
        ],
    )
    return pl.pallas_call(
        _dispatch_body,
        grid_spec=grid_spec,
        out_shape=[
            jax.ShapeDtypeStruct((n_rows_sorted, D), BF16),
            jax.ShapeDtypeStruct((nb, m_out_max, 128), F32),
            jax.ShapeDtypeStruct((nb, m_out_max, 128), F32),
        ],
        compiler_params=pltpu.CompilerParams(
            dimension_semantics=("arbitrary",), vmem_limit_bytes=VMEM_LIMIT, has_side_effects=True),
        name="dispatch",
    )(*meta["copy_lists"], meta["units"], meta["nch"], meta["tail_g"], meta["tail_n"], dest, wsel, h1b_all, tokid,
      ones)


def _ffn_body(tile_e, tile_src, tile_rows, e_slot, e_next, x_ref, wg_hbm, wu_hbm, wd_hbm, y_ref,
              wg_f, wu_f, wd_f, wgu_b, wd_b, sems):
    i = pl.program_id(0)
    f = wg_f.shape[2]
    e = tile_e[i]
    e_prev = tile_e[jnp.maximum(i - 1, 0)]

    def weight_copies(expert, slot):
        return (pltpu.make_async_copy(wg_hbm.at[expert], wg_f.at[slot], sems.at[slot, 0]),
                pltpu.make_async_copy(wu_hbm.at[expert], wu_f.at[slot], sems.at[slot, 1]),
                pltpu.make_async_copy(wd_hbm.at[expert], wd_f.at[slot], sems.at[slot, 2]))

    @pl.when((i == 0) | (e != e_prev))
    def _():
        slot = e_slot[e]

        @pl.when(i == 0)
        def _():
            for cp in weight_copies(e, slot):
                cp.start()

        for cp in weight_copies(e, slot):
            cp.wait()
        nxt = e_next[e]

        @pl.when(nxt >= 0)
        def _():
            for cp in weight_copies(nxt, 1 - slot):
                cp.start(priority=1)

        wgu_b[:, :f] = wg_f[slot].astype(BF16)
        wgu_b[:, f:] = wu_f[slot].astype(BF16)
        wd_b[...] = wd_f[slot].astype(BF16)

    def swiglu(x):
        gu = _dot(x, wgu_b[...])
        act = (jax.nn.silu(gu[:, :f]) * gu[:, f:]).astype(BF16)
        return _dot(act, wd_b[...]).astype(BF16)

    for m in range(TRC, TR + 1, TRC):
        @pl.when(tile_rows[i] == m)
        def _(m=m):
            y_ref[0:m, :] = swiglu(x_ref[0:m, :])
            if m < TR:
                y_ref[m:, :] = jnp.zeros((TR - m, y_ref.shape[1]), BF16)


def _expert_ffn(meta, xs, w_gate, w_up, w_down):
    R, D = xs.shape
    E, _, f = w_gate.shape
    nt = R // TR
    grid_spec = pltpu.PrefetchScalarGridSpec(
        num_scalar_prefetch=5,
        grid=(nt,),
        in_specs=[
            pl.BlockSpec((TR, D), lambda i, te, ts, *_: (ts[i], 0)),
            pl.BlockSpec(memory_space=pl.ANY),
            pl.BlockSpec(memory_space=pl.ANY),
            pl.BlockSpec(memory_space=pl.ANY),
        ],
        out_specs=pl.BlockSpec((TR, D), lambda i, te, ts, *_: (ts[i], 0)),
        scratch_shapes=[
            pltpu.VMEM((2, D, f), F32),
            pltpu.VMEM((2, D, f), F32),
            pltpu.VMEM((2, f, D), F32),
            pltpu.VMEM((D, 2 * f), BF16),
            pltpu.VMEM((f, D), BF16),
            pltpu.SemaphoreType.DMA((2, 3)),
        ],
    )
    return pl.pallas_call(
        _ffn_body,
        grid_spec=grid_spec,
        out_shape=jax.ShapeDtypeStruct((R, D), BF16),
        compiler_params=pltpu.CompilerParams(
            dimension_semantics=("arbitrary",), vmem_limit_bytes=VMEM_LIMIT),
        name="expert_ffn",
    )(meta["tile_e"], meta["tile_src"], meta["tile_rows"], meta["e_slot"], meta["e_next"],
      xs, w_gate, w_up, w_down)


def _combine_body(alpha, n_pb, *refs):
    copy_lists = refs[:N_LIST_REFS]
    (chunk_units, nch, half_tail, src_ref, wrow_ref, h1_ref, h1b_ref, pp_ref, ps_ref,
     wsg_ref, wsu_ref, wsd_ref, g2_ref, b2_ref, wpg_ref, bpg_ref, wpe_ref, ys_hbm,
     yp_ref, ysm_ref, yloc_ref, acc_ref, sems) = refs[N_LIST_REFS:]
    b = pl.program_id(0)
    n_chunks = nch[b]
    max_chunks = sems.shape[0]

    def seg_copy(l, g, rows):
        return pltpu.make_async_copy(ys_hbm.at[pl.ds(g, rows), :], yloc_ref.at[pl.ds(l, rows), :],
                                     sems.at[l // KCC])

    def fetch_block(blk):
        _start_block_copies(blk, copy_lists, seg_copy)

    @pl.when(b == 0)
    def _():
        yloc_ref[...] = jnp.zeros(yloc_ref.shape, BF16)
        fetch_block(b)

    hb = h1b_ref[...]
    sh = (jax.nn.silu(_dot(hb, wsg_ref[...])) * _dot(hb, wsu_ref[...])).astype(BF16)
    acc_ref[...] = _dot(sh, wsd_ref[...])

    def chunk(c, rows):
        def wait_rows(n):
            return pltpu.make_async_copy(ys_hbm.at[pl.ds(0, n), :], yloc_ref.at[pl.ds(0, n), :], sems.at[c])

        _wait_row_units(chunk_units[b * max_chunks + c], KCC // SEGG, wait_rows)

        r0 = pl.multiple_of(c * KCC, KCC)
        src = src_ref[0, pl.ds(r0, rows), :]
        w = wrow_ref[0, pl.ds(r0, rows), :]
        lane = lax.broadcasted_iota(I32, src.shape, 1).astype(F32)
        wm = jnp.concatenate([jnp.where(src == lane + float(o), w, 0.0) for o in range(0, TB, src.shape[1])],
                             axis=1).astype(BF16)
        acc_ref[...] += lax.dot_general(wm, yloc_ref[pl.ds(r0, rows), :],
                                        (((0,), (0,)), ((), ())), preferred_element_type=F32)

    def full_chunk(c, carry):
        chunk(c, KCC)
        return carry

    lax.fori_loop(0, n_chunks - half_tail[b], full_chunk, 0)

    @pl.when(half_tail[b] == 1)
    def _():
        chunk(n_chunks - 1, KCC // 2)

    @pl.when(b + 1 < pl.num_programs(0))
    def _():
        fetch_block(b + 1)

    h2 = _ln_rows(alpha * h1_ref[...] + acc_ref[...], g2_ref[...], b2_ref[...])
    gate = jax.nn.sigmoid(_dot(h2.astype(BF16), wpg_ref[...]) + bpg_ref[...])
    p = jnp.where(b < n_pb, pp_ref[...], ps_ref[...])
    pe = _dot(p.astype(BF16), wpe_ref[...])
    y = h2 + gate * pe

    @pl.when(b < n_pb)
    def _():
        yp_ref[...] = y

    @pl.when(b >= n_pb)
    def _():
        ysm_ref[...] = y


def _combine(meta, src_rep, w_rep, h1_all, h1b_all, p_prompt, p_sample, wsg_b, wsu_b, wsd_b, g2, b2, wpg_b, bpg,
             wpe_b, ys, m_out_max, n_prompt, alpha):
    n, D = h1_all.shape
    nb = n // TB
    n_pb = n_prompt // TB
    d_pe = p_prompt.shape[1]
    cs = lambda shape: pl.BlockSpec(shape, lambda i, *_: (0,) * len(shape), pipeline_mode=pl.Buffered(1))
    grid_spec = pltpu.PrefetchScalarGridSpec(
        num_scalar_prefetch=N_LIST_REFS + 3,
        grid=(nb,),
        in_specs=[
            pl.BlockSpec((1, m_out_max, 128), lambda i, *_: (i, 0, 0)),
            pl.BlockSpec((1, m_out_max, 128), lambda i, *_: (i, 0, 0)),
            pl.BlockSpec((TB, D), lambda i, *_: (i, 0)),
            pl.BlockSpec((TB, D), lambda i, *_: (i, 0)),
            pl.BlockSpec((TB, d_pe), lambda i, *_: (jnp.minimum(i, n_pb - 1), 0)),
            pl.BlockSpec((TB, d_pe), lambda i, *_: (jnp.maximum(i - n_pb, 0), 0)),
            cs(wsg_b.shape), cs(wsu_b.shape), cs(wsd_b.shape), cs(g2.shape), cs(b2.shape),
            cs(wpg_b.shape), cs(bpg.shape), cs(wpe_b.shape),
            pl.BlockSpec(memory_space=pl.ANY),
        ],
        out_specs=[
            pl.BlockSpec((TB, D), lambda i, *_: (jnp.minimum(i, n_pb - 1), 0)),
            pl.BlockSpec((TB, D), lambda i, *_: (jnp.maximum(i - n_pb, 0), 0)),
        ],
        scratch_shapes=[
            pltpu.VMEM((m_out_max, D), BF16),
            pltpu.VMEM((TB, D), F32),
            pltpu.SemaphoreType.DMA((m_out_max // KCC,)),
        ],
    )
    return pl.pallas_call(
        functools.partial(_combine_body, alpha, n_pb),
        grid_spec=grid_spec,
        out_shape=[jax.ShapeDtypeStruct((n_prompt, D), F32), jax.ShapeDtypeStruct((n - n_prompt, D), F32)],
        compiler_params=pltpu.CompilerParams(
            dimension_semantics=("arbitrary",), vmem_limit_bytes=VMEM_LIMIT),
        name="combine",
    )(*meta["copy_lists"], meta["chunk_units"], meta["nchc"], meta["half_tail"], src_rep, w_rep, h1_all, h1b_all,
      p_prompt, p_sample,
      wsg_b, wsu_b, wsd_b, g2, b2, wpg_b, bpg, wpe_b, ys)


def _sort_meta(cnt, n_tiles_max, m_out_max):
    segl = (cnt + (SEG - 1)) // SEG * SEG
    loc = jnp.cumsum(segl, axis=1) - segl
    n8 = (cnt + (SEGG - 1)) // SEGG
    segp = n8 * SEGG
    before = jnp.cumsum(segp, axis=0) - segp
    total = jnp.sum(segp, axis=0)
    total_c = (total + (TRC - 1)) // TRC * TRC
    total_p = (total + (TR - 1)) // TR * TR
    ends = jnp.cumsum(total_p)
    base = ends - total_p
    glob = base[None, :] + before
    n_valid = ends[-1] // TR
    tiles = jnp.arange(n_tiles_max, dtype=I32)
    tile_src = jnp.minimum(tiles, n_valid - 1)
    tile_e = jnp.minimum(jnp.sum((ends[None, :] <= (tile_src * TR)[:, None]).astype(I32), axis=1), N_EXPERTS - 1)
    left = jnp.sum(jnp.where(tile_e[:, None] == jnp.arange(N_EXPERTS, dtype=I32)[None, :],
                             (base + total_c)[None, :], 0), axis=1) - tile_src * TR
    tile_rows = jnp.where(tiles < n_valid, jnp.clip(left, 0, TR), 0)
    experts = jnp.arange(N_EXPERTS, dtype=I32)
    active = total_p > 0
    later = active[None, :] & (experts[None, :] > experts[:, None])
    e_next = jnp.min(jnp.where(later, experts[None, :], N_EXPERTS), axis=1)
    def copy_list(count, first_l, first_g, rows, width):
        end = jnp.cumsum(count, axis=1)
        start = end - count
        j = jnp.arange(width, dtype=I32)[None, :, None]
        in_seg = (start[:, None, :] <= j) & (j < end[:, None, :])
        step = (j - start[:, None, :]) * rows
        lst_l = jnp.sum(jnp.where(in_seg, first_l[:, None, :] + step, 0), axis=2)
        lst_g = jnp.sum(jnp.where(in_seg, first_g[:, None, :] + step, 0), axis=2)
        return lst_l, lst_g, end[:, -1], j[:, :, 0] < end[:, -1:]

    chunks = jnp.arange(m_out_max // KCC, dtype=I32)[None, :, None]
    chunk_units = 0
    lists = {}
    left_units, sent_rows, larger = n8, jnp.zeros_like(n8), None
    for rows in COPY_ROWS:
        per = rows // SEGG
        count = left_units // per
        width = m_out_max // rows if larger is None else N_EXPERTS * (larger // rows - 1)
        larger = rows
        lst_l, lst_g, lst_n, ok = copy_list(count, loc + sent_rows, glob + sent_rows, rows, width)
        chunk_units = chunk_units + jnp.sum(jnp.where(ok[:, None, :] & (lst_l[:, None, :] // KCC == chunks), per, 0), axis=2)
        lists[rows] = (lst_l.reshape(-1).astype(I32), lst_g.reshape(-1).astype(I32), lst_n.astype(I32))
        left_units, sent_rows = left_units - count * per, sent_rows + count * rows
    local_rows = jnp.sum(segl, axis=1)
    return {
        "copy_lists": tuple(a for rows in COPY_ROWS for a in lists[rows]),
        "units": jnp.sum(n8, axis=1).astype(I32),
        "chunk_units": chunk_units.reshape(-1).astype(I32),
        "nch": ((local_rows + (KC - 1)) // KC).astype(I32),
        "nchc": ((local_rows + (KCC - 1)) // KCC).astype(I32),
        "half_tail": (((local_rows - 1) % KCC) < KCC // 2).astype(I32),
        "tail_g": (base + total).astype(I32),
        "tail_n": ((total_c - total) // SEGG).astype(I32),
        "tile_e": tile_e.astype(I32),
        "tile_src": tile_src.astype(I32),
        "tile_rows": tile_rows.astype(I32),
        "e_slot": ((jnp.cumsum(active.astype(I32)) - 1) % 2).astype(I32),
        "e_next": jnp.where(e_next < N_EXPERTS, e_next, -1).astype(I32),
    }


def _layer(xp, xs, st, pp, ps, w_in, w_pool, pool_scale, sgu_ln_g, sgu_ln_b, w_s, b_s, w_out, ln1_g, ln1_b,
           router_w, router_bias, w_gate, w_up, w_down, ws_gate, ws_up, ws_down, ln2_g, ln2_b, w_pe, w_pgate,
           b_pgate, alpha):
    B, T, D = xp.shape
    NS, TS, _ = xs.shape
    n_prompt, n_sample = B * T, NS * TS
    n = n_prompt + n_sample
    n_heads = w_s.shape[0]
    row = lambda v: v.reshape(1, -1)

    x_t = jnp.transpose(xs, (1, 0, 2))
    st_t = jnp.transpose(st, (1, 0, 2))
    bsb = jnp.broadcast_to(b_s[:, :, None], (n_heads, CHUNK, SGU_HEAD_DIM))
    wrow = jnp.repeat(jnp.transpose(w_s[:, :TS, :TS], (1, 2, 0)), SGU_HEAD_DIM, axis=2)
    brow = jnp.repeat(jnp.transpose(b_s[:, :TS], (1, 0)), SGU_HEAD_DIM, axis=1)
    h1_all, h1b_all, pool_p, a_s, vn_s, dest, wsel, cnt = _mixer(
        xp, x_t, st_t, w_in.astype(BF16), w_pool.astype(BF16), row(pool_scale), sgu_ln_g, sgu_ln_b, w_s, bsb,
        wrow, brow, w_out.astype(BF16), row(ln1_g), row(ln1_b), jnp.transpose(router_w),
        router_bias.reshape(-1, 1), alpha)
    nb = n // TB
    m_out_max = TB * TOP_K + N_EXPERTS * SEG
    rows_max = n * TOP_K + nb * N_EXPERTS * (SEGG - 1) + N_EXPERTS * (TR - SEGG)
    n_tiles_max = -(-rows_max // TR)
    meta = _sort_meta(cnt[:, :, 0], n_tiles_max, m_out_max)
    xsort, src_rep, w_rep = _dispatch(meta, dest, wsel, h1b_all, n_tiles_max * TR, m_out_max)
    ysort = _expert_ffn(meta, xsort, w_gate, w_up, w_down)

    nbs = TB // TS
    ps_t = jnp.transpose(ps.reshape(NS // nbs, nbs, TS, -1), (0, 2, 1, 3)).reshape(n_sample, -1)
    y_p, y_s = _combine(meta, src_rep, w_rep, h1_all, h1b_all, pp.reshape(n_prompt, -1), ps_t,
                        ws_gate.astype(BF16), ws_up.astype(BF16),
                        ws_down.astype(BF16), row(ln2_g), row(ln2_b), w_pgate.astype(BF16), row(b_pgate),
                        w_pe.astype(BF16), ysort, m_out_max, n_prompt, alpha)

    yp = y_p.reshape(B, T, D)
    ys = jnp.transpose(y_s.reshape(NS // nbs, TS, nbs, D), (0, 2, 1, 3)).reshape(NS, TS, D)
    new_pool_p = pool_p[:, HALO - POOL_BUF:, :]
    new_pool_s = jnp.concatenate([st, jnp.transpose(a_s, (1, 0, 2))], axis=1)[:, -POOL_BUF:]
    vn = jnp.transpose(vn_s, (1, 0, 2)).reshape(NS, TS, n_heads, SGU_HEAD_DIM)
    return yp, ys, new_pool_p, new_pool_s, vn


def kernel(x_prompt, x_sample, state_pool, p_prompt, p_sample, w_in, w_pool, pool_scale, sgu_ln_g, sgu_ln_b, w_s, b_s, w_out, ln1_g, ln1_b, router_w, router_bias, w_gate, w_up, w_down, ws_gate, ws_up, ws_down, ln2_g, ln2_b, w_pe, w_pgate, b_pgate):
    depth = w_in.shape[0]
    alpha = (2.0 * depth) ** 0.25
    hp, hs = x_prompt, x_sample
    pool_p, pool_s, v_s = [], [], []
    for i in range(depth):
        hp, hs, bp, bs, vs = _layer(
            hp, hs, state_pool[i], p_prompt[i], p_sample[i], w_in[i], w_pool[i], pool_scale[i], sgu_ln_g[i],
            sgu_ln_b[i], w_s[i], b_s[i], w_out[i], ln1_g[i], ln1_b[i], router_w[i], router_bias[i], w_gate[i],
            w_up[i], w_down[i], ws_gate[i], ws_up[i], ws_down[i], ln2_g[i], ln2_b[i], w_pe[i], w_pgate[i],
            b_pgate[i], alpha)
        pool_p.append(bp)
        pool_s.append(bs)
        v_s.append(vs)
    return hp, hs, jnp.stack(pool_p), jnp.stack(pool_s), jnp.stack(v_s)
```

```python
import functools

import jax
import jax.numpy as jnp
import numpy as np
from jax import lax
from jax.experimental import pallas as pl
from jax.experimental.pallas import tpu as pltpu

F32 = jnp.float32
BF16 = jnp.bfloat16
I32 = jnp.int32

POOL_WINDOWS = (2, 4, 8, 16)
POOL_BUF = max(POOL_WINDOWS) - 1
CHUNK = 128
SGU_HEAD_DIM = 128
N_EXPERTS = 64
TOP_K = 8
N_EXPERT_GROUPS = 8
TOPK_GROUPS = 4
ROUTE_SCALE = 2.5
LN_EPS = 1e-5
PAST_LEN = 16384

TB = 256
HALO = 16
TR = 1024
TRC = 128
SEG = 16
SEGG = 8
COPY_ROWS = (32, 16, 8)
KC = 256
KCC = 512
VMEM_LIMIT = 56 * 1024 * 1024


def _const_spec(shape):
    n = len(shape)
    return pl.BlockSpec(shape, lambda *_: (0,) * n, pipeline_mode=pl.Buffered(1))


def _ln_rows(x, g, b):
    mu = jnp.mean(x, axis=-1, keepdims=True)
    xc = x - mu
    var = jnp.mean(xc * xc, axis=-1, keepdims=True)
    return xc * lax.rsqrt(var + LN_EPS) * g + b


def _dot(a, b):
    return jnp.dot(a, b, preferred_element_type=F32)


def _prompt_tile(j, n_tiles, x, proj, band_ref, wpool_ref, pscale_ref, sg_ref, sb_ref, ws_ref, bsb_ref,
                 pool_ref, zcat_ref, mixin_ref):
    d_pool = zcat_ref.shape[1]
    d_pool_g = d_pool // len(POOL_WINDOWS)
    d_sgu = mixin_ref.shape[1] - d_pool

    a = proj[:, :d_pool]
    zcat_ref[HALO:, :] = a.astype(BF16)

    pos1 = (j * TB + lax.broadcasted_iota(I32, (TB, 1), 0) + 1).astype(F32)
    for g, w in enumerate(POOL_WINDOWS):
        c0, c1 = g * d_pool_g, (g + 1) * d_pool_g
        s = _dot(band_ref[g], zcat_ref[:, c0:c1])
        cnt = jnp.minimum(pos1, float(w))
        d = s / cnt - a[:, c0:c1]
        y = _dot(d.astype(BF16), wpool_ref[g])
        mixin_ref[:, c0:c1] = (y * pscale_ref[:, c0:c1]).astype(BF16)
    zcat_ref[0:HALO, :] = jnp.where(j == n_tiles - 1, jnp.zeros((HALO, d_pool), BF16), zcat_ref[TB:TB + HALO, :])
    pool_ref[0] = a[TB - HALO:, :]

    u = jax.nn.gelu(proj[:, d_pool:d_pool + d_sgu])
    v = jax.nn.gelu(proj[:, d_pool + d_sgu:])
    r = lax.broadcasted_iota(I32, (CHUNK, CHUNK), 0)
    c = lax.broadcasted_iota(I32, (CHUNK, CHUNK), 1)
    tril = r >= c
    for h in range(d_sgu // SGU_HEAD_DIM):
        l0, l1 = h * SGU_HEAD_DIM, (h + 1) * SGU_HEAD_DIM
        vn = _ln_rows(v[:, l0:l1], sg_ref[h:h + 1, :], sb_ref[h:h + 1, :]).astype(BF16)
        wst = jnp.where(tril, ws_ref[h], 0.0).astype(BF16)
        n_ch = TB // CHUNK
        mixed_all = _dot(wst, jnp.concatenate([vn[ci * CHUNK:(ci + 1) * CHUNK] for ci in range(n_ch)], axis=1))
        for ci in range(n_ch):
            r0, r1 = ci * CHUNK, (ci + 1) * CHUNK
            mixed = mixed_all[:, ci * SGU_HEAD_DIM:(ci + 1) * SGU_HEAD_DIM] + bsb_ref[h]
            mixin_ref[r0:r1, d_pool + l0:d_pool + l1] = (u[r0:r1, l0:l1] * mixed).astype(BF16)


def _sample_tile(T, NB, proj, st_ref, wpool_ref, pscale_ref, sg_ref, sb_ref, wrow_ref, brow_ref,
                 a_ref, vn_ref, dbuf_ref, mixin_ref):
    d_pool = dbuf_ref.shape[1]
    d_pool_g = d_pool // len(POOL_WINDOWS)
    d_sgu = mixin_ref.shape[1] - d_pool

    a = proj[:, :d_pool]
    for t in range(T):
        a_ref[t] = a[t * NB:(t + 1) * NB]

    for g, w in enumerate(POOL_WINDOWS):
        c0, c1 = g * d_pool_g, (g + 1) * d_pool_g
        for t in range(T):
            cnt = float(min(PAST_LEN + t + 1, w))
            acc = None
            for s in range(POOL_BUF + t - w + 1, POOL_BUF + t + 1):
                if s < POOL_BUF:
                    term = st_ref[s, :, c0:c1]
                else:
                    term = a[(s - POOL_BUF) * NB:(s - POOL_BUF + 1) * NB, c0:c1]
                acc = term if acc is None else acc + term
            d = acc / cnt - a[t * NB:(t + 1) * NB, c0:c1]
            dbuf_ref[t * NB:(t + 1) * NB, c0:c1] = d.astype(BF16)
        y = _dot(dbuf_ref[0:T * NB, c0:c1], wpool_ref[g])
        mixin_ref[:, c0:c1] = (y * pscale_ref[:, c0:c1]).astype(BF16)

    u = jax.nn.gelu(proj[:, d_pool:d_pool + d_sgu])
    v = jax.nn.gelu(proj[:, d_pool + d_sgu:])
    for h in range(d_sgu // SGU_HEAD_DIM):
        l0, l1 = h * SGU_HEAD_DIM, (h + 1) * SGU_HEAD_DIM
        vn = _ln_rows(v[:, l0:l1], sg_ref[h:h + 1, :], sb_ref[h:h + 1, :])
        for t in range(T):
            vn_ref[t, :, l0:l1] = vn[t * NB:(t + 1) * NB]
    for t in range(T):
        mixed = brow_ref[t:t + 1, :]
        for s in range(t + 1):
            mixed = mixed + wrow_ref[t, s:s + 1, :] * vn_ref[s]
        mixin_ref[t * NB:(t + 1) * NB, d_pool:] = (u[t * NB:(t + 1) * NB] * mixed).astype(BF16)


def _mixer_body(alpha, n_pt, tiles_per_seq, xp_ref, xs_ref, st_ref, win_ref, band_ref, wpool_ref, pscale_ref,
                sg_ref, sb_ref, ws_ref, bsb_ref, wrow_ref, brow_ref, wout_ref, g1_ref, b1_ref,
                rw_ref, rb_ref, upper_ref, lstrict_ref,
                h1_ref, h1b_ref, pool_ref, a_ref, vn_ref, dest_ref, wsel_ref, cnt_ref,
                zcat_ref, mixin_ref, hprev_ref):
    i = pl.program_id(0)
    n_tiles = pl.num_programs(0) - 1

    def route_previous_tile():
        h = hprev_ref[...]
        _route_tile(h, h.astype(BF16), rw_ref, rb_ref, upper_ref, lstrict_ref,
                    dest_ref, wsel_ref, cnt_ref)

    @pl.when(i == 0)
    def _():
        zcat_ref[0:HALO, :] = jnp.zeros((HALO, zcat_ref.shape[1]), BF16)
        hprev_ref[...] = jnp.zeros(hprev_ref.shape, F32)

    @pl.when(i < n_pt)
    def _():
        x = xp_ref[0]
        proj = _dot(x.astype(BF16), win_ref[...])
        route_previous_tile()
        _prompt_tile(i % tiles_per_seq, tiles_per_seq, x, proj, band_ref, wpool_ref, pscale_ref, sg_ref, sb_ref,
                     ws_ref, bsb_ref, pool_ref, zcat_ref, mixin_ref)
        h1_ref[...] = alpha * x

    @pl.when(i >= n_pt)
    def _():
        route_previous_tile()

    @pl.when((i >= n_pt) & (i < n_tiles))
    def _():
        T, NB, D = xs_ref.shape
        x = xs_ref[...].reshape(T * NB, D)
        proj = _dot(x.astype(BF16), win_ref[...])
        _sample_tile(T, NB, proj, st_ref, wpool_ref, pscale_ref, sg_ref, sb_ref, wrow_ref, brow_ref,
                     a_ref, vn_ref, zcat_ref, mixin_ref)
        h1_ref[...] = alpha * x

    @pl.when(i < n_tiles)
    def _():
        mix = _dot(mixin_ref[...], wout_ref[...])
        h1 = _ln_rows(h1_ref[...] + mix, g1_ref[...], b1_ref[...])
        h1_ref[...] = h1
        h1b_ref[...] = h1.astype(BF16)
        hprev_ref[...] = h1


def _pool_band(tm):
    t = np.arange(tm)[:, None] + HALO
    s = np.arange(tm + HALO)[None, :]
    return jnp.asarray(np.stack([((s <= t) & (s >= t - w + 1)) for w in POOL_WINDOWS]).astype(np.float32), BF16)


def _mixer(xp, xs_t, st_t, win_b, wpool_b, pscale, sg, sb, ws, bsb, wrow, brow, wout_b, g1, b1, rwt, rb_col, alpha):
    B, T, D = xp.shape
    TS, NS, _ = xs_t.shape
    d_pool = st_t.shape[2]
    d_sgu = wrow.shape[2]
    tps = T // TB
    n_pt = B * tps
    NB = TB // TS
    n_st = NS // NB
    n = B * T + NS * TS
    nblk = n_pt + n_st
    pt = lambda i: jnp.minimum(i, n_pt - 1)
    stile = lambda i: jnp.clip(i - n_pt, 0, n_st - 1)
    blk = lambda i: jnp.minimum(i, nblk - 1)
    routed = lambda i: jnp.maximum(i - 1, 0)
    body = functools.partial(_mixer_body, alpha, n_pt, tps)
    rw_hi = rwt.astype(BF16)
    rw_lo = (rwt - rw_hi.astype(F32)).astype(BF16)
    upper = jnp.asarray(np.triu(np.ones((TB, TB), np.float32), 1), BF16)
    lstrict = jnp.asarray(np.tril(np.ones((N_EXPERTS, N_EXPERTS), np.float32), -1), BF16)
    consts = (win_b, _pool_band(TB), wpool_b, pscale, sg, sb, ws, bsb, wrow, brow, wout_b, g1, b1,
              jnp.concatenate([rw_hi, rw_lo], axis=0), rb_col, upper, lstrict)
    return pl.pallas_call(
        body,
        grid=(nblk + 1,),
        in_specs=[
            pl.BlockSpec((1, TB, D), lambda i: (pt(i) // tps, pt(i) % tps, 0)),
            pl.BlockSpec((TS, NB, D), lambda i: (0, stile(i), 0), pipeline_mode=pl.Buffered(1)),
            pl.BlockSpec((POOL_BUF, NB, d_pool), lambda i: (0, stile(i), 0), pipeline_mode=pl.Buffered(1)),
        ] + [_const_spec(c.shape) for c in consts],
        out_specs=[
            pl.BlockSpec((TB, D), lambda i: (blk(i), 0)),
            pl.BlockSpec((TB, D), lambda i: (blk(i), 0)),
            pl.BlockSpec((1, HALO, d_pool), lambda i: (pt(i) // tps, 0, 0)),
            pl.BlockSpec((TS, NB, d_pool), lambda i: (0, stile(i), 0)),
            pl.BlockSpec((TS, NB, d_sgu), lambda i: (0, stile(i), 0)),
            pl.BlockSpec((1, TOP_K, TB), lambda i: (routed(i), 0, 0)),
            pl.BlockSpec((1, TOP_K, TB), lambda i: (routed(i), 0, 0)),
            pl.BlockSpec((1, N_EXPERTS, 128), lambda i: (routed(i), 0, 0)),
        ],
        out_shape=[
            jax.ShapeDtypeStruct((n, D), F32),
            jax.ShapeDtypeStruct((n, D), BF16),
            jax.ShapeDtypeStruct((B, HALO, d_pool), F32),
            jax.ShapeDtypeStruct((TS, NS, d_pool), F32),
            jax.ShapeDtypeStruct((TS, NS, d_sgu), F32),
            jax.ShapeDtypeStruct((nblk, TOP_K, TB), I32),
            jax.ShapeDtypeStruct((nblk, TOP_K, TB), F32),
            jax.ShapeDtypeStruct((nblk, N_EXPERTS, 128), I32),
        ],
        scratch_shapes=[
            pltpu.VMEM((TB + HALO, d_pool), BF16),
            pltpu.VMEM((TB, D), BF16),
            pltpu.VMEM((TB, D), F32),
        ],
        compiler_params=pltpu.CompilerParams(
            dimension_semantics=("arbitrary",), vmem_limit_bytes=VMEM_LIMIT),
        name="mixer",
    )(xp, xs_t, st_t, *consts)


def _route_tile(h1, h1b, rw_ref, rb_ref, upper_ref, lstrict_ref, dest_ref, wsel_ref, cnt_ref):
    per_group = N_EXPERTS // N_EXPERT_GROUPS
    neg = -jnp.inf
    nt = (((1,), (1,)), ((), ()))
    h_lo = (h1 - h1b.astype(F32)).astype(BF16)
    by_h_hi = lax.dot_general(rw_ref[...], h1b, nt, preferred_element_type=F32)
    logits_t = (by_h_hi[:N_EXPERTS] + by_h_hi[N_EXPERTS:]
                + lax.dot_general(rw_ref[0:N_EXPERTS, :], h_lo, nt, preferred_element_type=F32))
    s_t = jax.nn.sigmoid(logits_t)
    b_t = s_t + rb_ref[...]

    io_g = lax.broadcasted_iota(I32, (per_group, TB), 0)
    gs = []
    for g in range(N_EXPERT_GROUPS):
        xg = b_t[g * per_group:(g + 1) * per_group, :]
        m1 = jnp.max(xg, axis=0, keepdims=True)
        i1 = jnp.min(jnp.where(xg == m1, io_g, per_group), axis=0, keepdims=True)
        m2 = jnp.max(jnp.where(io_g == i1, neg, xg), axis=0, keepdims=True)
        gs.append(m1 + m2)
    masked = []
    for g in range(N_EXPERT_GROUPS):
        rank = jnp.zeros((1, TB), F32)
        for g2 in range(N_EXPERT_GROUPS):
            if g2 != g:
                ahead = (gs[g2] >= gs[g]) if g2 < g else (gs[g2] > gs[g])
                rank = rank + jnp.where(ahead, 1.0, 0.0)
        keep = rank < float(TOPK_GROUPS)
        masked.append(jnp.where(keep, b_t[g * per_group:(g + 1) * per_group, :], neg))
    xm = jnp.concatenate(masked, axis=0)

    io_e = lax.broadcasted_iota(I32, (N_EXPERTS, TB), 0)
    onehots = []
    sel = jnp.zeros((N_EXPERTS, TB), F32)
    for _ in range(TOP_K):
        m = jnp.max(xm, axis=0, keepdims=True)
        idx = jnp.min(jnp.where(xm == m, io_e, N_EXPERTS), axis=0, keepdims=True)
        oh = io_e == idx
        onehots.append(oh)
        sel = jnp.where(oh, 1.0, sel)
        xm = jnp.where(oh, neg, xm)

    ssel = sel * s_t
    denom = jnp.sum(ssel, axis=0, keepdims=True)
    comb = ssel / denom * ROUTE_SCALE

    rank_t = _dot(sel.astype(BF16), upper_ref[...])
    cnt = jnp.sum(sel, axis=1, keepdims=True)
    cnt_i = cnt.astype(I32)
    seg16 = ((cnt_i + (SEG - 1)) // SEG).astype(F32)
    off16 = _dot(lstrict_ref[...], jnp.broadcast_to(seg16, (N_EXPERTS, 128)).astype(BF16))
    d_t = rank_t + off16[:, 0:1] * float(SEG)
    for k in range(TOP_K):
        dest_ref[0, k:k + 1, :] = jnp.sum(jnp.where(onehots[k], d_t, 0.0), axis=0, keepdims=True).astype(I32)
        wsel_ref[0, k:k + 1, :] = jnp.sum(jnp.where(onehots[k], comb, 0.0), axis=0, keepdims=True)
    cnt_ref[0] = jnp.broadcast_to(cnt_i, (N_EXPERTS, 128))


N_LIST_REFS = 3 * len(COPY_ROWS)


def _start_block_copies(b, copy_lists, make_copy):
    for k, rows in enumerate(COPY_ROWS):
        table_l, table_g, counts = copy_lists[3 * k:3 * k + 3]
        width = table_l.shape[0] // counts.shape[0]
        n = counts[b]

        def start(j, table_l=table_l, table_g=table_g, width=width, rows=rows):
            idx = b * width + j
            make_copy(pl.multiple_of(table_l[idx], min(rows, SEG)), pl.multiple_of(table_g[idx], SEGG), rows).start()

        def eight(q, c, start=start):
            for u in range(8):
                start(8 * q + u)
            return c

        def one(j, c, start=start):
            start(j)
            return c

        lax.fori_loop(0, n // 8, eight, 0)
        lax.fori_loop(n // 8 * 8, n, one, 0)


def _wait_row_units(count, max_count, make_wait_copy):
    for k in range(int(max_count).bit_length()):
        @pl.when(((count >> k) & 1) == 1)
        def _():
            make_wait_copy(SEGG << k).wait()


def _dispatch_body(*refs):
    copy_lists = refs[:N_LIST_REFS]
    (units, nch, tail_g, tail_n, dest_ref, wsel_ref, h_ref, tokid_ref, ones_ref,
     xs_hbm, src_ref, wrow_ref, xloc_ref, zero_ref, sems, tail_sem) = refs[N_LIST_REFS:]
    b = pl.program_id(0)
    nb = pl.num_programs(0)
    max_units = xloc_ref.shape[1] // SEGG
    slot = b % 2

    def seg_copy(sl):
        return lambda l, g, rows: pltpu.make_async_copy(
            xloc_ref.at[sl, pl.ds(l, rows), :], xs_hbm.at[pl.ds(g, rows), :], sems.at[sl])

    def wait_block(count, sl):
        _wait_row_units(count, max_units, lambda rows: pltpu.make_async_copy(
            xloc_ref.at[sl, pl.ds(0, rows), :], xs_hbm.at[pl.ds(0, rows), :], sems.at[sl]))

    @pl.when(b >= 2)
    def _():
        wait_block(units[jnp.maximum(b - 2, 0)], slot)

    dest = dest_ref[0]
    wsel = wsel_ref[0]
    h = h_ref[...]
    src_ref[...] = jnp.zeros(src_ref.shape, F32)
    wrow_ref[...] = jnp.zeros(wrow_ref.shape, F32)

    row_in_chunk = lax.broadcasted_iota(I32, (KC, TB), 0).astype(F32).astype(BF16)
    wsel_b = wsel.astype(BF16)
    one = jnp.ones((1, TB), BF16)

    def chunk(c):
        r0 = pl.multiple_of(c * KC, KC)
        rel = (dest - r0).astype(F32).astype(BF16)
        sb = jnp.zeros((KC, TB), BF16)
        wm = jnp.zeros((KC, TB), BF16)
        for k in range(TOP_K):
            hit = rel[k:k + 1, :] == row_in_chunk
            sb = jnp.where(hit, one, sb)
            wm = jnp.where(hit, wsel_b[k:k + 1, :], wm)
        xloc_ref[slot, pl.ds(r0, KC), :] = _dot(sb, h).astype(BF16)
        src_ref[0, pl.ds(r0, KC), :] = _dot(sb, tokid_ref[...])
        wrow_ref[0, pl.ds(r0, KC), :] = _dot(wm, ones_ref[...])

    def chunk_pair(cp, carry):
        chunk(2 * cp)
        chunk(2 * cp + 1)
        return carry

    lax.fori_loop(0, nch[b] // 2, chunk_pair, 0)

    @pl.when(nch[b] % 2 == 1)
    def _():
        chunk(nch[b] - 1)

    _start_block_copies(b, copy_lists, seg_copy(slot))

    def tail_copy(g):
        return pltpu.make_async_copy(zero_ref, xs_hbm.at[pl.ds(g, SEGG), :], tail_sem)

    def tails(wait):
        def per_expert(e, carry):
            def per_granule(j, c):
                cp = tail_copy(pl.multiple_of(tail_g[e] + j * SEGG, SEGG))
                if wait:
                    cp.wait()
                else:
                    cp.start()
                return c
            lax.fori_loop(0, tail_n[e], per_granule, 0)
            return carry
        lax.fori_loop(0, N_EXPERTS, per_expert, 0)

    @pl.when(b == nb - 1)
    def _():
        zero_ref[...] = jnp.zeros(zero_ref.shape, BF16)
        tails(False)

        @pl.when(b >= 1)
        def _():
            wait_block(units[jnp.maximum(b - 1, 0)], 1 - slot)

        wait_block(units[b], slot)
        tails(True)


def _dispatch(meta, dest, wsel, h1b_all, n_rows_sorted, m_out_max):
    n, D = h1b_all.shape
    nb = n // TB
    tokid = jnp.asarray(np.broadcast_to(np.arange(TB, dtype=np.float32)[:, None], (TB, 128)), BF16)
    ones = jnp.ones((TB, 128), BF16)
    cs = lambda shape: pl.BlockSpec(shape, lambda i, *_: (0,) * len(shape), pipeline_mode=pl.Buffered(1))
    grid_spec = pltpu.PrefetchScalarGridSpec(
        num_scalar_prefetch=N_LIST_REFS + 4,
        grid=(nb,),
        in_specs=[
            pl.BlockSpec((1, TOP_K, TB), lambda i, *_: (i, 0, 0)),
            pl.BlockSpec((1, TOP_K, TB), lambda i, *_: (i, 0, 0)),
            pl.BlockSpec((TB, D), lambda i, *_: (i, 0)),
            cs(tokid.shape), cs(ones.shape),
        ],
        out_specs=[
            pl.BlockSpec(memory_space=pl.ANY),
            pl.BlockSpec((1, m_out_max, 128), lambda i, *_: (i, 0, 0)),
            pl.BlockSpec((1, m_out_max, 128), lambda i, *_: (i, 0, 0)),
        ],
        scratch_shapes=[
            pltpu.VMEM((2, m_out_max, D), BF16),
            pltpu.VMEM((SEGG, D), BF16),
            pltpu.SemaphoreType.DMA((2,)),
            pltpu.SemaphoreType.DMA(()),
        ],
    )
    return pl.pallas_call(
        _dispatch_body,
        grid_spec=grid_spec,
        out_shape=[
            jax.ShapeDtypeStruct((n_rows_sorted, D), BF16),
            jax.ShapeDtypeStruct((nb, m_out_max, 128), F32),
            jax.ShapeDtypeStruct((nb, m_out_max, 128), F32),
        ],
        compiler_params=pltpu.CompilerParams(
            dimension_semantics=("arbitrary",), vmem_limit_bytes=VMEM_LIMIT, has_side_effects=True),
        name="dispatch",
    )(*meta["copy_lists"], meta["units"], meta["nch"], meta["tail_g"], meta["tail_n"], dest, wsel, h1b_all, tokid,
      ones)


def _ffn_body(tile_e, tile_src, tile_rows, e_slot, e_next, x_ref, wg_hbm, wu_hbm, wd_hbm, y_ref,
              wg_f, wu_f, wd_f, wgu_b, wd_b, sems):
    i = pl.program_id(0)
    f = wg_f.shape[2]
    e = tile_e[i]
    e_prev = tile_e[jnp.maximum(i - 1, 0)]

    def weight_copies(expert, slot):
        return (pltpu.make_async_copy(wg_hbm.at[expert], wg_f.at[slot], sems.at[slot, 0]),
                pltpu.make_async_copy(wu_hbm.at[expert], wu_f.at[slot], sems.at[slot, 1]),
                pltpu.make_async_copy(wd_hbm.at[expert], wd_f.at[slot], sems.at[slot, 2]))

    @pl.when((i == 0) | (e != e_prev))
    def _():
        slot = e_slot[e]

        @pl.when(i == 0)
        def _():
            for cp in weight_copies(e, slot):
                cp.start()

        for cp in weight_copies(e, slot):
            cp.wait()
        nxt = e_next[e]

        @pl.when(nxt >= 0)
        def _():
            for cp in weight_copies(nxt, 1 - slot):
                cp.start(priority=1)

        wgu_b[:, :f] = wg_f[slot].astype(BF16)
        wgu_b[:, f:] = wu_f[slot].astype(BF16)
        wd_b[...] = wd_f[slot].astype(BF16)

    def swiglu(x):
        gu = _dot(x, wgu_b[...])
        act = (jax.nn.silu(gu[:, :f]) * gu[:, f:]).astype(BF16)
        return _dot(act, wd_b[...]).astype(BF16)

    for m in range(TRC, TR + 1, TRC):
        @pl.when(tile_rows[i] == m)
        def _(m=m):
            y_ref[0:m, :] = swiglu(x_ref[0:m, :])
            if m < TR:
                y_ref[m:, :] = jnp.zeros((TR - m, y_ref.shape[1]), BF16)


def _expert_ffn(meta, xs, w_gate, w_up, w_down):
    R, D = xs.shape
    E, _, f = w_gate.shape
    nt = R // TR
    grid_spec = pltpu.PrefetchScalarGridSpec(
        num_scalar_prefetch=5,
        grid=(nt,),
        in_specs=[
            pl.BlockSpec((TR, D), lambda i, te, ts, *_: (ts[i], 0)),
            pl.BlockSpec(memory_space=pl.ANY),
            pl.BlockSpec(memory_space=pl.ANY),
            pl.BlockSpec(memory_space=pl.ANY),
        ],
        out_specs=pl.BlockSpec((TR, D), lambda i, te, ts, *_: (ts[i], 0)),
        scratch_shapes=[
            pltpu.VMEM((2, D, f), F32),
            pltpu.VMEM((2, D, f), F32),
            pltpu.VMEM((2, f, D), F32),
            pltpu.VMEM((D, 2 * f), BF16),
            pltpu.VMEM((f, D), BF16),
            pltpu.SemaphoreType.DMA((2, 3)),
        ],
    )
    return pl.pallas_call(
        _ffn_body,
        grid_spec=grid_spec,
        out_shape=jax.ShapeDtypeStruct((R, D), BF16),
        compiler_params=pltpu.CompilerParams(
            dimension_semantics=("arbitrary",), vmem_limit_bytes=VMEM_LIMIT),
        name="expert_ffn",
    )(meta["tile_e"], meta["tile_src"], meta["tile_rows"], meta["e_slot"], meta["e_next"],
      xs, w_gate, w_up, w_down)


def _combine_body(alpha, n_pb, *refs):
    copy_lists = refs[:N_LIST_REFS]
    (chunk_units, nch, src_ref, wrow_ref, h1_ref, h1b_ref, pp_ref, ps_ref,
     wsg_ref, wsu_ref, wsd_ref, g2_ref, b2_ref, wpg_ref, bpg_ref, wpe_ref, ys_hbm,
     yp_ref, ysm_ref, yloc_ref, acc_ref, sems) = refs[N_LIST_REFS:]
    b = pl.program_id(0)
    n_chunks = nch[b]
    max_chunks = sems.shape[0]

    def seg_copy(l, g, rows):
        return pltpu.make_async_copy(ys_hbm.at[pl.ds(g, rows), :], yloc_ref.at[pl.ds(l, rows), :],
                                     sems.at[l // KCC])

    def fetch_block(blk):
        _start_block_copies(blk, copy_lists, seg_copy)

    @pl.when(b == 0)
    def _():
        yloc_ref[...] = jnp.zeros(yloc_ref.shape, BF16)
        fetch_block(b)

    hb = h1b_ref[...]
    sh = (jax.nn.silu(_dot(hb, wsg_ref[...])) * _dot(hb, wsu_ref[...])).astype(BF16)
    acc_ref[...] = _dot(sh, wsd_ref[...])

    def chunk(c, carry):
        def wait_rows(rows):
            return pltpu.make_async_copy(ys_hbm.at[pl.ds(0, rows), :], yloc_ref.at[pl.ds(0, rows), :], sems.at[c])

        _wait_row_units(chunk_units[b * max_chunks + c], KCC // SEGG, wait_rows)

        r0 = pl.multiple_of(c * KCC, KCC)
        src = src_ref[0, pl.ds(r0, KCC), :]
        w = wrow_ref[0, pl.ds(r0, KCC), :]
        lane = lax.broadcasted_iota(I32, src.shape, 1).astype(F32)
        wm = jnp.concatenate([jnp.where(src == lane + float(o), w, 0.0) for o in range(0, TB, src.shape[1])],
                             axis=1).astype(BF16)
        acc_ref[...] += lax.dot_general(wm, yloc_ref[pl.ds(r0, KCC), :],
                                        (((0,), (0,)), ((), ())), preferred_element_type=F32)
        return carry

    lax.fori_loop(0, n_chunks, chunk, 0)

    @pl.when(b + 1 < pl.num_programs(0))
    def _():
        fetch_block(b + 1)

    h2 = _ln_rows(alpha * h1_ref[...] + acc_ref[...], g2_ref[...], b2_ref[...])
    gate = jax.nn.sigmoid(_dot(h2.astype(BF16), wpg_ref[...]) + bpg_ref[...])
    p = jnp.where(b < n_pb, pp_ref[...], ps_ref[...])
    pe = _dot(p.astype(BF16), wpe_ref[...])
    y = h2 + gate * pe

    @pl.when(b < n_pb)
    def _():
        yp_ref[...] = y

    @pl.when(b >= n_pb)
    def _():
        ysm_ref[...] = y


def _combine(meta, src_rep, w_rep, h1_all, h1b_all, p_prompt, p_sample, wsg_b, wsu_b, wsd_b, g2, b2, wpg_b, bpg,
             wpe_b, ys, m_out_max, n_prompt, alpha):
    n, D = h1_all.shape
    nb = n // TB
    n_pb = n_prompt // TB
    d_pe = p_prompt.shape[1]
    cs = lambda shape: pl.BlockSpec(shape, lambda i, *_: (0,) * len(shape), pipeline_mode=pl.Buffered(1))
    grid_spec = pltpu.PrefetchScalarGridSpec(
        num_scalar_prefetch=N_LIST_REFS + 2,
        grid=(nb,),
        in_specs=[
            pl.BlockSpec((1, m_out_max, 128), lambda i, *_: (i, 0, 0)),
            pl.BlockSpec((1, m_out_max, 128), lambda i, *_: (i, 0, 0)),
            pl.BlockSpec((TB, D), lambda i, *_: (i, 0)),
            pl.BlockSpec((TB, D), lambda i, *_: (i, 0)),
            pl.BlockSpec((TB, d_pe), lambda i, *_: (jnp.minimum(i, n_pb - 1), 0)),
            pl.BlockSpec((TB, d_pe), lambda i, *_: (jnp.maximum(i - n_pb, 0), 0)),
            cs(wsg_b.shape), cs(wsu_b.shape), cs(wsd_b.shape), cs(g2.shape), cs(b2.shape),
            cs(wpg_b.shape), cs(bpg.shape), cs(wpe_b.shape),
            pl.BlockSpec(memory_space=pl.ANY),
        ],
        out_specs=[
            pl.BlockSpec((TB, D), lambda i, *_: (jnp.minimum(i, n_pb - 1), 0)),
            pl.BlockSpec((TB, D), lambda i, *_: (jnp.maximum(i - n_pb, 0), 0)),
        ],
        scratch_shapes=[
            pltpu.VMEM((m_out_max, D), BF16),
            pltpu.VMEM((TB, D), F32),
            pltpu.SemaphoreType.DMA((m_out_max // KCC,)),
        ],
    )
    return pl.pallas_call(
        functools.partial(_combine_body, alpha, n_pb),
        grid_spec=grid_spec,
        out_shape=[jax.ShapeDtypeStruct((n_prompt, D), F32), jax.ShapeDtypeStruct((n - n_prompt, D), F32)],
        compiler_params=pltpu.CompilerParams(
            dimension_semantics=("arbitrary",), vmem_limit_bytes=VMEM_LIMIT),
        name="combine",
    )(*meta["copy_lists"], meta["chunk_units"], meta["nchc"], src_rep, w_rep, h1_all, h1b_all, p_prompt, p_sample,
      wsg_b, wsu_b, wsd_b, g2, b2, wpg_b, bpg, wpe_b, ys)


def _sort_meta(cnt, n_tiles_max, m_out_max):
    segl = (cnt + (SEG - 1)) // SEG * SEG
    loc = jnp.cumsum(segl, axis=1) - segl
    n8 = (cnt + (SEGG - 1)) // SEGG
    segp = n8 * SEGG
    before = jnp.cumsum(segp, axis=0) - segp
    total = jnp.sum(segp, axis=0)
    total_c = (total + (TRC - 1)) // TRC * TRC
    total_p = (total + (TR - 1)) // TR * TR
    ends = jnp.cumsum(total_p)
    base = ends - total_p
    glob = base[None, :] + before
    n_valid = ends[-1] // TR
    tiles = jnp.arange(n_tiles_max, dtype=I32)
    tile_src = jnp.minimum(tiles, n_valid - 1)
    tile_e = jnp.minimum(jnp.sum((ends[None, :] <= (tile_src * TR)[:, None]).astype(I32), axis=1), N_EXPERTS - 1)
    left = jnp.sum(jnp.where(tile_e[:, None] == jnp.arange(N_EXPERTS, dtype=I32)[None, :],
                             (base + total_c)[None, :], 0), axis=1) - tile_src * TR
    tile_rows = jnp.where(tiles < n_valid, jnp.clip(left, 0, TR), 0)
    experts = jnp.arange(N_EXPERTS, dtype=I32)
    active = total_p > 0
    later = active[None, :] & (experts[None, :] > experts[:, None])
    e_next = jnp.min(jnp.where(later, experts[None, :], N_EXPERTS), axis=1)
    def copy_list(count, first_l, first_g, rows, width):
        end = jnp.cumsum(count, axis=1)
        start = end - count
        j = jnp.arange(width, dtype=I32)[None, :, None]
        in_seg = (start[:, None, :] <= j) & (j < end[:, None, :])
        step = (j - start[:, None, :]) * rows
        lst_l = jnp.sum(jnp.where(in_seg, first_l[:, None, :] + step, 0), axis=2)
        lst_g = jnp.sum(jnp.where(in_seg, first_g[:, None, :] + step, 0), axis=2)
        return lst_l, lst_g, end[:, -1], j[:, :, 0] < end[:, -1:]

    chunks = jnp.arange(m_out_max // KCC, dtype=I32)[None, :, None]
    chunk_units = 0
    lists = {}
    left_units, sent_rows, larger = n8, jnp.zeros_like(n8), None
    for rows in COPY_ROWS:
        per = rows // SEGG
        count = left_units // per
        width = m_out_max // rows if larger is None else N_EXPERTS * (larger // rows - 1)
        larger = rows
        lst_l, lst_g, lst_n, ok = copy_list(count, loc + sent_rows, glob + sent_rows, rows, width)
        chunk_units = chunk_units + jnp.sum(jnp.where(ok[:, None, :] & (lst_l[:, None, :] // KCC == chunks), per, 0), axis=2)
        lists[rows] = (lst_l.reshape(-1).astype(I32), lst_g.reshape(-1).astype(I32), lst_n.astype(I32))
        left_units, sent_rows = left_units - count * per, sent_rows + count * rows
    local_rows = jnp.sum(segl, axis=1)
    return {
        "copy_lists": tuple(a for rows in COPY_ROWS for a in lists[rows]),
        "units": jnp.sum(n8, axis=1).astype(I32),
        "chunk_units": chunk_units.reshape(-1).astype(I32),
        "nch": ((local_rows + (KC - 1)) // KC).astype(I32),
        "nchc": ((local_rows + (KCC - 1)) // KCC).astype(I32),
        "tail_g": (base + total).astype(I32),
        "tail_n": ((total_c - total) // SEGG).astype(I32),
        "tile_e": tile_e.astype(I32),
        "tile_src": tile_src.astype(I32),
        "tile_rows": tile_rows.astype(I32),
        "e_slot": ((jnp.cumsum(active.astype(I32)) - 1) % 2).astype(I32),
        "e_next": jnp.where(e_next < N_EXPERTS, e_next, -1).astype(I32),
    }


def _layer(xp, xs, st, pp, ps, w_in, w_pool, pool_scale, sgu_ln_g, sgu_ln_b, w_s, b_s, w_out, ln1_g, ln1_b,
           router_w, router_bias, w_gate, w_up, w_down, ws_gate, ws_up, ws_down, ln2_g, ln2_b, w_pe, w_pgate,
           b_pgate, alpha):
    B, T, D = xp.shape
    NS, TS, _ = xs.shape
    n_prompt, n_sample = B * T, NS * TS
    n = n_prompt + n_sample
    n_heads = w_s.shape[0]
    row = lambda v: v.reshape(1, -1)

    x_t = jnp.transpose(xs, (1, 0, 2))
    st_t = jnp.transpose(st, (1, 0, 2))
    bsb = jnp.broadcast_to(b_s[:, :, None], (n_heads, CHUNK, SGU_HEAD_DIM))
    wrow = jnp.repeat(jnp.transpose(w_s[:, :TS, :TS], (1, 2, 0)), SGU_HEAD_DIM, axis=2)
    brow = jnp.repeat(jnp.transpose(b_s[:, :TS], (1, 0)), SGU_HEAD_DIM, axis=1)
    h1_all, h1b_all, pool_p, a_s, vn_s, dest, wsel, cnt = _mixer(
        xp, x_t, st_t, w_in.astype(BF16), w_pool.astype(BF16), row(pool_scale), sgu_ln_g, sgu_ln_b, w_s, bsb,
        wrow, brow, w_out.astype(BF16), row(ln1_g), row(ln1_b), jnp.transpose(router_w),
        router_bias.reshape(-1, 1), alpha)
    nb = n // TB
    m_out_max = TB * TOP_K + N_EXPERTS * SEG
    rows_max = n * TOP_K + nb * N_EXPERTS * (SEGG - 1) + N_EXPERTS * (TR - SEGG)
    n_tiles_max = -(-rows_max // TR)
    meta = _sort_meta(cnt[:, :, 0], n_tiles_max, m_out_max)
    xsort, src_rep, w_rep = _dispatch(meta, dest, wsel, h1b_all, n_tiles_max * TR, m_out_max)
    ysort = _expert_ffn(meta, xsort, w_gate, w_up, w_down)

    nbs = TB // TS
    ps_t = jnp.transpose(ps.reshape(NS // nbs, nbs, TS, -1), (0, 2, 1, 3)).reshape(n_sample, -1)
    y_p, y_s = _combine(meta, src_rep, w_rep, h1_all, h1b_all, pp.reshape(n_prompt, -1), ps_t,
                        ws_gate.astype(BF16), ws_up.astype(BF16),
                        ws_down.astype(BF16), row(ln2_g), row(ln2_b), w_pgate.astype(BF16), row(b_pgate),
                        w_pe.astype(BF16), ysort, m_out_max, n_prompt, alpha)

    yp = y_p.reshape(B, T, D)
    ys = jnp.transpose(y_s.reshape(NS // nbs, TS, nbs, D), (0, 2, 1, 3)).reshape(NS, TS, D)
    new_pool_p = pool_p[:, HALO - POOL_BUF:, :]
    new_pool_s = jnp.concatenate([st, jnp.transpose(a_s, (1, 0, 2))], axis=1)[:, -POOL_BUF:]
    vn = jnp.transpose(vn_s, (1, 0, 2)).reshape(NS, TS, n_heads, SGU_HEAD_DIM)
    return yp, ys, new_pool_p, new_pool_s, vn


def kernel(x_prompt, x_sample, state_pool, p_prompt, p_sample, w_in, w_pool, pool_scale, sgu_ln_g, sgu_ln_b, w_s, b_s, w_out, ln1_g, ln1_b, router_w, router_bias, w_gate, w_up, w_down, ws_gate, ws_up, ws_down, ln2_g, ln2_b, w_pe, w_pgate, b_pgate):
    depth = w_in.shape[0]
    alpha = (2.0 * depth) ** 0.25
    hp, hs = x_prompt, x_sample
    pool_p, pool_s, v_s = [], [], []
    for i in range(depth):
        hp, hs, bp, bs, vs = _layer(
            hp, hs, state_pool[i], p_prompt[i], p_sample[i], w_in[i], w_pool[i], pool_scale[i], sgu_ln_g[i],
            sgu_ln_b[i], w_s[i], b_s[i], w_out[i], ln1_g[i], ln1_b[i], router_w[i], router_bias[i], w_gate[i],
            w_up[i], w_down[i], ws_gate[i], ws_up[i], ws_down[i], ln2_g[i], ln2_b[i], w_pe[i], w_pgate[i],
            b_pgate[i], alpha)
        pool_p.append(bp)
        pool_s.append(bs)
        v_s.append(vs)
    return hp, hs, jnp.stack(pool_p), jnp.stack(pool_s), jnp.stack(v_s)
```

```python
import functools

import jax
import jax.numpy as jnp
import numpy as np
from jax import lax
from jax.experimental import pallas as pl
from jax.experimental.pallas import tpu as pltpu

F32 = jnp.float32
BF16 = jnp.bfloat16
I32 = jnp.int32

POOL_WINDOWS = (2, 4, 8, 16)
POOL_BUF = max(POOL_WINDOWS) - 1
CHUNK = 128
SGU_HEAD_DIM = 128
N_EXPERTS = 64
TOP_K = 8
N_EXPERT_GROUPS = 8
TOPK_GROUPS = 4
ROUTE_SCALE = 2.5
LN_EPS = 1e-5
PAST_LEN = 16384

TB = 256
HALO = 16
TR = 512
TRC = 128
SEG = 16
SEGG = 8
COPY_ROWS = (32, 16, 8)
KC = 256
KCC = 512
VMEM_LIMIT = 56 * 1024 * 1024


def _const_spec(shape):
    n = len(shape)
    return pl.BlockSpec(shape, lambda *_: (0,) * n, pipeline_mode=pl.Buffered(1))


def _ln_rows(x, g, b):
    mu = jnp.mean(x, axis=-1, keepdims=True)
    xc = x - mu
    var = jnp.mean(xc * xc, axis=-1, keepdims=True)
    return xc * lax.rsqrt(var + LN_EPS) * g + b


def _dot(a, b):
    return jnp.dot(a, b, preferred_element_type=F32)


def _prompt_tile(j, n_tiles, x, proj, band_ref, wpool_ref, pscale_ref, sg_ref, sb_ref, ws_ref, bsb_ref,
                 pool_ref, zcat_ref, mixin_ref):
    d_pool = zcat_ref.shape[1]
    d_pool_g = d_pool // len(POOL_WINDOWS)
    d_sgu = mixin_ref.shape[1] - d_pool

    a = proj[:, :d_pool]
    zcat_ref[HALO:, :] = a.astype(BF16)

    pos1 = (j * TB + lax.broadcasted_iota(I32, (TB, 1), 0) + 1).astype(F32)
    for g, w in enumerate(POOL_WINDOWS):
        c0, c1 = g * d_pool_g, (g + 1) * d_pool_g
        s = _dot(band_ref[g], zcat_ref[:, c0:c1])
        cnt = jnp.minimum(pos1, float(w))
        d = s / cnt - a[:, c0:c1]
        y = _dot(d.astype(BF16), wpool_ref[g])
        mixin_ref[:, c0:c1] = (y * pscale_ref[:, c0:c1]).astype(BF16)
    zcat_ref[0:HALO, :] = jnp.where(j == n_tiles - 1, jnp.zeros((HALO, d_pool), BF16), zcat_ref[TB:TB + HALO, :])
    pool_ref[0] = a[TB - HALO:, :]

    u = jax.nn.gelu(proj[:, d_pool:d_pool + d_sgu])
    v = jax.nn.gelu(proj[:, d_pool + d_sgu:])
    r = lax.broadcasted_iota(I32, (CHUNK, CHUNK), 0)
    c = lax.broadcasted_iota(I32, (CHUNK, CHUNK), 1)
    tril = r >= c
    for h in range(d_sgu // SGU_HEAD_DIM):
        l0, l1 = h * SGU_HEAD_DIM, (h + 1) * SGU_HEAD_DIM
        vn = _ln_rows(v[:, l0:l1], sg_ref[h:h + 1, :], sb_ref[h:h + 1, :]).astype(BF16)
        wst = jnp.where(tril, ws_ref[h], 0.0).astype(BF16)
        n_ch = TB // CHUNK
        mixed_all = _dot(wst, jnp.concatenate([vn[ci * CHUNK:(ci + 1) * CHUNK] for ci in range(n_ch)], axis=1))
        for ci in range(n_ch):
            r0, r1 = ci * CHUNK, (ci + 1) * CHUNK
            mixed = mixed_all[:, ci * SGU_HEAD_DIM:(ci + 1) * SGU_HEAD_DIM] + bsb_ref[h]
            mixin_ref[r0:r1, d_pool + l0:d_pool + l1] = (u[r0:r1, l0:l1] * mixed).astype(BF16)


def _sample_tile(T, NB, proj, st_ref, wpool_ref, pscale_ref, sg_ref, sb_ref, wrow_ref, brow_ref,
                 a_ref, vn_ref, dbuf_ref, mixin_ref):
    d_pool = dbuf_ref.shape[1]
    d_pool_g = d_pool // len(POOL_WINDOWS)
    d_sgu = mixin_ref.shape[1] - d_pool

    a = proj[:, :d_pool]
    for t in range(T):
        a_ref[t] = a[t * NB:(t + 1) * NB]

    for g, w in enumerate(POOL_WINDOWS):
        c0, c1 = g * d_pool_g, (g + 1) * d_pool_g
        for t in range(T):
            cnt = float(min(PAST_LEN + t + 1, w))
            acc = None
            for s in range(POOL_BUF + t - w + 1, POOL_BUF + t + 1):
                if s < POOL_BUF:
                    term = st_ref[s, :, c0:c1]
                else:
                    term = a[(s - POOL_BUF) * NB:(s - POOL_BUF + 1) * NB, c0:c1]
                acc = term if acc is None else acc + term
            d = acc / cnt - a[t * NB:(t + 1) * NB, c0:c1]
            dbuf_ref[t * NB:(t + 1) * NB, c0:c1] = d.astype(BF16)
        y = _dot(dbuf_ref[0:T * NB, c0:c1], wpool_ref[g])
        mixin_ref[:, c0:c1] = (y * pscale_ref[:, c0:c1]).astype(BF16)

    u = jax.nn.gelu(proj[:, d_pool:d_pool + d_sgu])
    v = jax.nn.gelu(proj[:, d_pool + d_sgu:])
    for h in range(d_sgu // SGU_HEAD_DIM):
        l0, l1 = h * SGU_HEAD_DIM, (h + 1) * SGU_HEAD_DIM
        vn = _ln_rows(v[:, l0:l1], sg_ref[h:h + 1, :], sb_ref[h:h + 1, :])
        for t in range(T):
            vn_ref[t, :, l0:l1] = vn[t * NB:(t + 1) * NB]
    for t in range(T):
        mixed = brow_ref[t:t + 1, :]
        for s in range(t + 1):
            mixed = mixed + wrow_ref[t, s:s + 1, :] * vn_ref[s]
        mixin_ref[t * NB:(t + 1) * NB, d_pool:] = (u[t * NB:(t + 1) * NB] * mixed).astype(BF16)


def _mixer_body(alpha, n_pt, tiles_per_seq, xp_ref, xs_ref, st_ref, win_ref, band_ref, wpool_ref, pscale_ref,
                sg_ref, sb_ref, ws_ref, bsb_ref, wrow_ref, brow_ref, wout_ref, g1_ref, b1_ref,
                rw_ref, rb_ref, upper_ref, lstrict_ref,
                h1_ref, h1b_ref, pool_ref, a_ref, vn_ref, dest_ref, wsel_ref, cnt_ref,
                zcat_ref, mixin_ref, hprev_ref):
    i = pl.program_id(0)
    n_tiles = pl.num_programs(0) - 1

    def route_previous_tile():
        h = hprev_ref[...]
        _route_tile(h, h.astype(BF16), rw_ref, rb_ref, upper_ref, lstrict_ref,
                    dest_ref, wsel_ref, cnt_ref)

    @pl.when(i == 0)
    def _():
        zcat_ref[0:HALO, :] = jnp.zeros((HALO, zcat_ref.shape[1]), BF16)
        hprev_ref[...] = jnp.zeros(hprev_ref.shape, F32)

    @pl.when(i < n_pt)
    def _():
        x = xp_ref[0]
        proj = _dot(x.astype(BF16), win_ref[...])
        route_previous_tile()
        _prompt_tile(i % tiles_per_seq, tiles_per_seq, x, proj, band_ref, wpool_ref, pscale_ref, sg_ref, sb_ref,
                     ws_ref, bsb_ref, pool_ref, zcat_ref, mixin_ref)
        h1_ref[...] = alpha * x

    @pl.when(i >= n_pt)
    def _():
        route_previous_tile()

    @pl.when((i >= n_pt) & (i < n_tiles))
    def _():
        T, NB, D = xs_ref.shape
        x = xs_ref[...].reshape(T * NB, D)
        proj = _dot(x.astype(BF16), win_ref[...])
        _sample_tile(T, NB, proj, st_ref, wpool_ref, pscale_ref, sg_ref, sb_ref, wrow_ref, brow_ref,
                     a_ref, vn_ref, zcat_ref, mixin_ref)
        h1_ref[...] = alpha * x

    @pl.when(i < n_tiles)
    def _():
        mix = _dot(mixin_ref[...], wout_ref[...])
        h1 = _ln_rows(h1_ref[...] + mix, g1_ref[...], b1_ref[...])
        h1_ref[...] = h1
        h1b_ref[...] = h1.astype(BF16)
        hprev_ref[...] = h1


def _pool_band(tm):
    t = np.arange(tm)[:, None] + HALO
    s = np.arange(tm + HALO)[None, :]
    return jnp.asarray(np.stack([((s <= t) & (s >= t - w + 1)) for w in POOL_WINDOWS]).astype(np.float32), BF16)


def _mixer(xp, xs_t, st_t, win_b, wpool_b, pscale, sg, sb, ws, bsb, wrow, brow, wout_b, g1, b1, rwt, rb_col, alpha):
    B, T, D = xp.shape
    TS, NS, _ = xs_t.shape
    d_pool = st_t.shape[2]
    d_sgu = wrow.shape[2]
    tps = T // TB
    n_pt = B * tps
    NB = TB // TS
    n_st = NS // NB
    n = B * T + NS * TS
    nblk = n_pt + n_st
    pt = lambda i: jnp.minimum(i, n_pt - 1)
    stile = lambda i: jnp.clip(i - n_pt, 0, n_st - 1)
    blk = lambda i: jnp.minimum(i, nblk - 1)
    routed = lambda i: jnp.maximum(i - 1, 0)
    body = functools.partial(_mixer_body, alpha, n_pt, tps)
    rw_hi = rwt.astype(BF16)
    rw_lo = (rwt - rw_hi.astype(F32)).astype(BF16)
    upper = jnp.asarray(np.triu(np.ones((TB, TB), np.float32), 1), BF16)
    lstrict = jnp.asarray(np.tril(np.ones((N_EXPERTS, N_EXPERTS), np.float32), -1), BF16)
    consts = (win_b, _pool_band(TB), wpool_b, pscale, sg, sb, ws, bsb, wrow, brow, wout_b, g1, b1,
              jnp.concatenate([rw_hi, rw_lo], axis=0), rb_col, upper, lstrict)
    return pl.pallas_call(
        body,
        grid=(nblk + 1,),
        in_specs=[
            pl.BlockSpec((1, TB, D), lambda i: (pt(i) // tps, pt(i) % tps, 0)),
            pl.BlockSpec((TS, NB, D), lambda i: (0, stile(i), 0), pipeline_mode=pl.Buffered(1)),
            pl.BlockSpec((POOL_BUF, NB, d_pool), lambda i: (0, stile(i), 0), pipeline_mode=pl.Buffered(1)),
        ] + [_const_spec(c.shape) for c in consts],
        out_specs=[
            pl.BlockSpec((TB, D), lambda i: (blk(i), 0)),
            pl.BlockSpec((TB, D), lambda i: (blk(i), 0)),
            pl.BlockSpec((1, HALO, d_pool), lambda i: (pt(i) // tps, 0, 0)),
            pl.BlockSpec((TS, NB, d_pool), lambda i: (0, stile(i), 0)),
            pl.BlockSpec((TS, NB, d_sgu), lambda i: (0, stile(i), 0)),
            pl.BlockSpec((1, TOP_K, TB), lambda i: (routed(i), 0, 0)),
            pl.BlockSpec((1, TOP_K, TB), lambda i: (routed(i), 0, 0)),
            pl.BlockSpec((1, N_EXPERTS, 128), lambda i: (routed(i), 0, 0)),
        ],
        out_shape=[
            jax.ShapeDtypeStruct((n, D), F32),
            jax.ShapeDtypeStruct((n, D), BF16),
            jax.ShapeDtypeStruct((B, HALO, d_pool), F32),
            jax.ShapeDtypeStruct((TS, NS, d_pool), F32),
            jax.ShapeDtypeStruct((TS, NS, d_sgu), F32),
            jax.ShapeDtypeStruct((nblk, TOP_K, TB), I32),
            jax.ShapeDtypeStruct((nblk, TOP_K, TB), F32),
            jax.ShapeDtypeStruct((nblk, N_EXPERTS, 128), I32),
        ],
        scratch_shapes=[
            pltpu.VMEM((TB + HALO, d_pool), BF16),
            pltpu.VMEM((TB, D), BF16),
            pltpu.VMEM((TB, D), F32),
        ],
        compiler_params=pltpu.CompilerParams(
            dimension_semantics=("arbitrary",), vmem_limit_bytes=VMEM_LIMIT),
        name="mixer",
    )(xp, xs_t, st_t, *consts)


def _route_tile(h1, h1b, rw_ref, rb_ref, upper_ref, lstrict_ref, dest_ref, wsel_ref, cnt_ref):
    per_group = N_EXPERTS // N_EXPERT_GROUPS
    neg = -jnp.inf
    nt = (((1,), (1,)), ((), ()))
    h_lo = (h1 - h1b.astype(F32)).astype(BF16)
    by_h_hi = lax.dot_general(rw_ref[...], h1b, nt, preferred_element_type=F32)
    logits_t = (by_h_hi[:N_EXPERTS] + by_h_hi[N_EXPERTS:]
                + lax.dot_general(rw_ref[0:N_EXPERTS, :], h_lo, nt, preferred_element_type=F32))
    s_t = jax.nn.sigmoid(logits_t)
    b_t = s_t + rb_ref[...]

    io_g = lax.broadcasted_iota(I32, (per_group, TB), 0)
    gs = []
    for g in range(N_EXPERT_GROUPS):
        xg = b_t[g * per_group:(g + 1) * per_group, :]
        m1 = jnp.max(xg, axis=0, keepdims=True)
        i1 = jnp.min(jnp.where(xg == m1, io_g, per_group), axis=0, keepdims=True)
        m2 = jnp.max(jnp.where(io_g == i1, neg, xg), axis=0, keepdims=True)
        gs.append(m1 + m2)
    masked = []
    for g in range(N_EXPERT_GROUPS):
        rank = jnp.zeros((1, TB), F32)
        for g2 in range(N_EXPERT_GROUPS):
            if g2 != g:
                ahead = (gs[g2] >= gs[g]) if g2 < g else (gs[g2] > gs[g])
                rank = rank + jnp.where(ahead, 1.0, 0.0)
        keep = rank < float(TOPK_GROUPS)
        masked.append(jnp.where(keep, b_t[g * per_group:(g + 1) * per_group, :], neg))
    xm = jnp.concatenate(masked, axis=0)

    io_e = lax.broadcasted_iota(I32, (N_EXPERTS, TB), 0)
    onehots = []
    sel = jnp.zeros((N_EXPERTS, TB), F32)
    for _ in range(TOP_K):
        m = jnp.max(xm, axis=0, keepdims=True)
        idx = jnp.min(jnp.where(xm == m, io_e, N_EXPERTS), axis=0, keepdims=True)
        oh = io_e == idx
        onehots.append(oh)
        sel = jnp.where(oh, 1.0, sel)
        xm = jnp.where(oh, neg, xm)

    ssel = sel * s_t
    denom = jnp.sum(ssel, axis=0, keepdims=True)
    comb = ssel / denom * ROUTE_SCALE

    rank_t = _dot(sel.astype(BF16), upper_ref[...])
    cnt = jnp.sum(sel, axis=1, keepdims=True)
    cnt_i = cnt.astype(I32)
    seg16 = ((cnt_i + (SEG - 1)) // SEG).astype(F32)
    off16 = _dot(lstrict_ref[...], jnp.broadcast_to(seg16, (N_EXPERTS, 128)).astype(BF16))
    d_t = rank_t + off16[:, 0:1] * float(SEG)
    for k in range(TOP_K):
        dest_ref[0, k:k + 1, :] = jnp.sum(jnp.where(onehots[k], d_t, 0.0), axis=0, keepdims=True).astype(I32)
        wsel_ref[0, k:k + 1, :] = jnp.sum(jnp.where(onehots[k], comb, 0.0), axis=0, keepdims=True)
    cnt_ref[0] = jnp.broadcast_to(cnt_i, (N_EXPERTS, 128))


N_LIST_REFS = 3 * len(COPY_ROWS)


def _start_block_copies(b, copy_lists, make_copy):
    for k, rows in enumerate(COPY_ROWS):
        table_l, table_g, counts = copy_lists[3 * k:3 * k + 3]
        width = table_l.shape[0] // counts.shape[0]
        n = counts[b]

        def start(j, table_l=table_l, table_g=table_g, width=width, rows=rows):
            idx = b * width + j
            make_copy(pl.multiple_of(table_l[idx], min(rows, SEG)), pl.multiple_of(table_g[idx], SEGG), rows).start()

        def eight(q, c, start=start):
            for u in range(8):
                start(8 * q + u)
            return c

        def one(j, c, start=start):
            start(j)
            return c

        lax.fori_loop(0, n // 8, eight, 0)
        lax.fori_loop(n // 8 * 8, n, one, 0)


def _wait_row_units(count, max_count, make_wait_copy):
    for k in range(int(max_count).bit_length()):
        @pl.when(((count >> k) & 1) == 1)
        def _():
            make_wait_copy(SEGG << k).wait()


def _dispatch_body(*refs):
    copy_lists = refs[:N_LIST_REFS]
    (units, nch, tail_g, tail_n, dest_ref, wsel_ref, h_ref, tokid_ref, ones_ref,
     xs_hbm, src_ref, wrow_ref, xloc_ref, zero_ref, sems, tail_sem) = refs[N_LIST_REFS:]
    b = pl.program_id(0)
    nb = pl.num_programs(0)
    max_units = xloc_ref.shape[1] // SEGG
    slot = b % 2

    def seg_copy(sl):
        return lambda l, g, rows: pltpu.make_async_copy(
            xloc_ref.at[sl, pl.ds(l, rows), :], xs_hbm.at[pl.ds(g, rows), :], sems.at[sl])

    def wait_block(count, sl):
        _wait_row_units(count, max_units, lambda rows: pltpu.make_async_copy(
            xloc_ref.at[sl, pl.ds(0, rows), :], xs_hbm.at[pl.ds(0, rows), :], sems.at[sl]))

    @pl.when(b >= 2)
    def _():
        wait_block(units[jnp.maximum(b - 2, 0)], slot)

    dest = dest_ref[0]
    wsel = wsel_ref[0]
    h = h_ref[...]
    src_ref[...] = jnp.zeros(src_ref.shape, F32)
    wrow_ref[...] = jnp.zeros(wrow_ref.shape, F32)

    row_in_chunk = lax.broadcasted_iota(I32, (KC, TB), 0).astype(F32).astype(BF16)
    wsel_b = wsel.astype(BF16)
    one = jnp.ones((1, TB), BF16)

    def chunk(c):
        r0 = pl.multiple_of(c * KC, KC)
        rel = (dest - r0).astype(F32).astype(BF16)
        sb = jnp.zeros((KC, TB), BF16)
        wm = jnp.zeros((KC, TB), BF16)
        for k in range(TOP_K):
            hit = rel[k:k + 1, :] == row_in_chunk
            sb = jnp.where(hit, one, sb)
            wm = jnp.where(hit, wsel_b[k:k + 1, :], wm)
        xloc_ref[slot, pl.ds(r0, KC), :] = _dot(sb, h).astype(BF16)
        src_ref[0, pl.ds(r0, KC), :] = _dot(sb, tokid_ref[...])
        wrow_ref[0, pl.ds(r0, KC), :] = _dot(wm, ones_ref[...])

    def chunk_pair(cp, carry):
        chunk(2 * cp)
        chunk(2 * cp + 1)
        return carry

    lax.fori_loop(0, nch[b] // 2, chunk_pair, 0)

    @pl.when(nch[b] % 2 == 1)
    def _():
        chunk(nch[b] - 1)

    _start_block_copies(b, copy_lists, seg_copy(slot))

    def tail_copy(g):
        return pltpu.make_async_copy(zero_ref, xs_hbm.at[pl.ds(g, SEGG), :], tail_sem)

    def tails(wait):
        def per_expert(e, carry):
            def per_granule(j, c):
                cp = tail_copy(pl.multiple_of(tail_g[e] + j * SEGG, SEGG))
                if wait:
                    cp.wait()
                else:
                    cp.start()
                return c
            lax.fori_loop(0, tail_n[e], per_granule, 0)
            return carry
        lax.fori_loop(0, N_EXPERTS, per_expert, 0)

    @pl.when(b == nb - 1)
    def _():
        zero_ref[...] = jnp.zeros(zero_ref.shape, BF16)
        tails(False)

        @pl.when(b >= 1)
        def _():
            wait_block(units[jnp.maximum(b - 1, 0)], 1 - slot)

        wait_block(units[b], slot)
        tails(True)


def _dispatch(meta, dest, wsel, h1b_all, n_rows_sorted, m_out_max):
    n, D = h1b_all.shape
    nb = n // TB
    tokid = jnp.asarray(np.broadcast_to(np.arange(TB, dtype=np.float32)[:, None], (TB, 128)), BF16)
    ones = jnp.ones((TB, 128), BF16)
    cs = lambda shape: pl.BlockSpec(shape, lambda i, *_: (0,) * len(shape), pipeline_mode=pl.Buffered(1))
    grid_spec = pltpu.PrefetchScalarGridSpec(
        num_scalar_prefetch=N_LIST_REFS + 4,
        grid=(nb,),
        in_specs=[
            pl.BlockSpec((1, TOP_K, TB), lambda i, *_: (i, 0, 0)),
            pl.BlockSpec((1, TOP_K, TB), lambda i, *_: (i, 0, 0)),
            pl.BlockSpec((TB, D), lambda i, *_: (i, 0)),
            cs(tokid.shape), cs(ones.shape),
        ],
        out_specs=[
            pl.BlockSpec(memory_space=pl.ANY),
            pl.BlockSpec((1, m_out_max, 128), lambda i, *_: (i, 0, 0)),
            pl.BlockSpec((1, m_out_max, 128), lambda i, *_: (i, 0, 0)),
        ],
        scratch_shapes=[
            pltpu.VMEM((2, m_out_max, D), BF16),
            pltpu.VMEM((SEGG, D), BF16),
            pltpu.SemaphoreType.DMA((2,)),
            pltpu.SemaphoreType.DMA(()),
        ],
    )
    return pl.pallas_call(
        _dispatch_body,
        grid_spec=grid_spec,
        out_shape=[
            jax.ShapeDtypeStruct((n_rows_sorted, D), BF16),
            jax.ShapeDtypeStruct((nb, m_out_max, 128), F32),
            jax.ShapeDtypeStruct((nb, m_out_max, 128), F32),
        ],
        compiler_params=pltpu.CompilerParams(
            dimension_semantics=("arbitrary",), vmem_limit_bytes=VMEM_LIMIT, has_side_effects=True),
        name="dispatch",
    )(*meta["copy_lists"], meta["units"], meta["nch"], meta["tail_g"], meta["tail_n"], dest, wsel, h1b_all, tokid,
      ones)


def _ffn_body(tile_e, tile_src, tile_rows, e_slot, e_next, x_ref, wg_hbm, wu_hbm, wd_hbm, y_ref,
              wg_f, wu_f, wd_f, wgu_b, wd_b, sems):
    i = pl.program_id(0)
    f = wg_f.shape[2]
    e = tile_e[i]
    e_prev = tile_e[jnp.maximum(i - 1, 0)]

    def weight_copies(expert, slot):
        return (pltpu.make_async_copy(wg_hbm.at[expert], wg_f.at[slot], sems.at[slot, 0]),
                pltpu.make_async_copy(wu_hbm.at[expert], wu_f.at[slot], sems.at[slot, 1]),
                pltpu.make_async_copy(wd_hbm.at[expert], wd_f.at[slot], sems.at[slot, 2]))

    @pl.when((i == 0) | (e != e_prev))
    def _():
        slot = e_slot[e]

        @pl.when(i == 0)
        def _():
            for cp in weight_copies(e, slot):
                cp.start()

        for cp in weight_copies(e, slot):
            cp.wait()
        nxt = e_next[e]

        @pl.when(nxt >= 0)
        def _():
            for cp in weight_copies(nxt, 1 - slot):
                cp.start(priority=1)

        wgu_b[:, :f] = wg_f[slot].astype(BF16)
        wgu_b[:, f:] = wu_f[slot].astype(BF16)
        wd_b[...] = wd_f[slot].astype(BF16)

    def swiglu(x):
        gu = _dot(x, wgu_b[...])
        act = (jax.nn.silu(gu[:, :f]) * gu[:, f:]).astype(BF16)
        return _dot(act, wd_b[...]).astype(BF16)

    for m in range(TRC, TR + 1, TRC):
        @pl.when(tile_rows[i] == m)
        def _(m=m):
            y_ref[0:m, :] = swiglu(x_ref[0:m, :])
            if m < TR:
                y_ref[m:, :] = jnp.zeros((TR - m, y_ref.shape[1]), BF16)


def _expert_ffn(meta, xs, w_gate, w_up, w_down):
    R, D = xs.shape
    E, _, f = w_gate.shape
    nt = R // TR
    grid_spec = pltpu.PrefetchScalarGridSpec(
        num_scalar_prefetch=5,
        grid=(nt,),
        in_specs=[
            pl.BlockSpec((TR, D), lambda i, te, ts, *_: (ts[i], 0)),
            pl.BlockSpec(memory_space=pl.ANY),
            pl.BlockSpec(memory_space=pl.ANY),
            pl.BlockSpec(memory_space=pl.ANY),
        ],
        out_specs=pl.BlockSpec((TR, D), lambda i, te, ts, *_: (ts[i], 0)),
        scratch_shapes=[
            pltpu.VMEM((2, D, f), F32),
            pltpu.VMEM((2, D, f), F32),
            pltpu.VMEM((2, f, D), F32),
            pltpu.VMEM((D, 2 * f), BF16),
            pltpu.VMEM((f, D), BF16),
            pltpu.SemaphoreType.DMA((2, 3)),
        ],
    )
    return pl.pallas_call(
        _ffn_body,
        grid_spec=grid_spec,
        out_shape=jax.ShapeDtypeStruct((R, D), BF16),
        compiler_params=pltpu.CompilerParams(
            dimension_semantics=("arbitrary",), vmem_limit_bytes=VMEM_LIMIT),
        name="expert_ffn",
    )(meta["tile_e"], meta["tile_src"], meta["tile_rows"], meta["e_slot"], meta["e_next"],
      xs, w_gate, w_up, w_down)


def _combine_body(alpha, n_pb, *refs):
    copy_lists = refs[:N_LIST_REFS]
    (chunk_units, nch, src_ref, wrow_ref, h1_ref, h1b_ref, pp_ref, ps_ref,
     wsg_ref, wsu_ref, wsd_ref, g2_ref, b2_ref, wpg_ref, bpg_ref, wpe_ref, ys_hbm,
     yp_ref, ysm_ref, yloc_ref, acc_ref, sems) = refs[N_LIST_REFS:]
    b = pl.program_id(0)
    n_chunks = nch[b]
    max_chunks = sems.shape[0]

    def seg_copy(l, g, rows):
        return pltpu.make_async_copy(ys_hbm.at[pl.ds(g, rows), :], yloc_ref.at[pl.ds(l, rows), :],
                                     sems.at[l // KCC])

    def fetch_block(blk):
        _start_block_copies(blk, copy_lists, seg_copy)

    @pl.when(b == 0)
    def _():
        yloc_ref[...] = jnp.zeros(yloc_ref.shape, BF16)
        fetch_block(b)

    hb = h1b_ref[...]
    sh = (jax.nn.silu(_dot(hb, wsg_ref[...])) * _dot(hb, wsu_ref[...])).astype(BF16)
    acc_ref[...] = _dot(sh, wsd_ref[...])

    def chunk(c, carry):
        def wait_rows(rows):
            return pltpu.make_async_copy(ys_hbm.at[pl.ds(0, rows), :], yloc_ref.at[pl.ds(0, rows), :], sems.at[c])

        _wait_row_units(chunk_units[b * max_chunks + c], KCC // SEGG, wait_rows)

        r0 = pl.multiple_of(c * KCC, KCC)
        src = src_ref[0, pl.ds(r0, KCC), :]
        w = wrow_ref[0, pl.ds(r0, KCC), :]
        lane = lax.broadcasted_iota(I32, src.shape, 1).astype(F32)
        wm = jnp.concatenate([jnp.where(src == lane + float(o), w, 0.0) for o in range(0, TB, src.shape[1])],
                             axis=1).astype(BF16)
        acc_ref[...] += lax.dot_general(wm, yloc_ref[pl.ds(r0, KCC), :],
                                        (((0,), (0,)), ((), ())), preferred_element_type=F32)
        return carry

    lax.fori_loop(0, n_chunks, chunk, 0)

    @pl.when(b + 1 < pl.num_programs(0))
    def _():
        fetch_block(b + 1)

    h2 = _ln_rows(alpha * h1_ref[...] + acc_ref[...], g2_ref[...], b2_ref[...])
    gate = jax.nn.sigmoid(_dot(h2.astype(BF16), wpg_ref[...]) + bpg_ref[...])
    p = jnp.where(b < n_pb, pp_ref[...], ps_ref[...])
    pe = _dot(p.astype(BF16), wpe_ref[...])
    y = h2 + gate * pe

    @pl.when(b < n_pb)
    def _():
        yp_ref[...] = y

    @pl.when(b >= n_pb)
    def _():
        ysm_ref[...] = y


def _combine(meta, src_rep, w_rep, h1_all, h1b_all, p_prompt, p_sample, wsg_b, wsu_b, wsd_b, g2, b2, wpg_b, bpg,
             wpe_b, ys, m_out_max, n_prompt, alpha):
    n, D = h1_all.shape
    nb = n // TB
    n_pb = n_prompt // TB
    d_pe = p_prompt.shape[1]
    cs = lambda shape: pl.BlockSpec(shape, lambda i, *_: (0,) * len(shape), pipeline_mode=pl.Buffered(1))
    grid_spec = pltpu.PrefetchScalarGridSpec(
        num_scalar_prefetch=N_LIST_REFS + 2,
        grid=(nb,),
        in_specs=[
            pl.BlockSpec((1, m_out_max, 128), lambda i, *_: (i, 0, 0)),
            pl.BlockSpec((1, m_out_max, 128), lambda i, *_: (i, 0, 0)),
            pl.BlockSpec((TB, D), lambda i, *_: (i, 0)),
            pl.BlockSpec((TB, D), lambda i, *_: (i, 0)),
            pl.BlockSpec((TB, d_pe), lambda i, *_: (jnp.minimum(i, n_pb - 1), 0)),
            pl.BlockSpec((TB, d_pe), lambda i, *_: (jnp.maximum(i - n_pb, 0), 0)),
            cs(wsg_b.shape), cs(wsu_b.shape), cs(wsd_b.shape), cs(g2.shape), cs(b2.shape),
            cs(wpg_b.shape), cs(bpg.shape), cs(wpe_b.shape),
            pl.BlockSpec(memory_space=pl.ANY),
        ],
        out_specs=[
            pl.BlockSpec((TB, D), lambda i, *_: (jnp.minimum(i, n_pb - 1), 0)),
            pl.BlockSpec((TB, D), lambda i, *_: (jnp.maximum(i - n_pb, 0), 0)),
        ],
        scratch_shapes=[
            pltpu.VMEM((m_out_max, D), BF16),
            pltpu.VMEM((TB, D), F32),
            pltpu.SemaphoreType.DMA((m_out_max // KCC,)),
        ],
    )
    return pl.pallas_call(
        functools.partial(_combine_body, alpha, n_pb),
        grid_spec=grid_spec,
        out_shape=[jax.ShapeDtypeStruct((n_prompt, D), F32), jax.ShapeDtypeStruct((n - n_prompt, D), F32)],
        compiler_params=pltpu.CompilerParams(
            dimension_semantics=("arbitrary",), vmem_limit_bytes=VMEM_LIMIT),
        name="combine",
    )(*meta["copy_lists"], meta["chunk_units"], meta["nchc"], src_rep, w_rep, h1_all, h1b_all, p_prompt, p_sample,
      wsg_b, wsu_b, wsd_b, g2, b2, wpg_b, bpg, wpe_b, ys)


def _sort_meta(cnt, n_tiles_max, m_out_max):
    segl = (cnt + (SEG - 1)) // SEG * SEG
    loc = jnp.cumsum(segl, axis=1) - segl
    n8 = (cnt + (SEGG - 1)) // SEGG
    segp = n8 * SEGG
    before = jnp.cumsum(segp, axis=0) - segp
    total = jnp.sum(segp, axis=0)
    total_c = (total + (TRC - 1)) // TRC * TRC
    total_p = (total + (TR - 1)) // TR * TR
    ends = jnp.cumsum(total_p)
    base = ends - total_p
    glob = base[None, :] + before
    n_valid = ends[-1] // TR
    tiles = jnp.arange(n_tiles_max, dtype=I32)
    tile_src = jnp.minimum(tiles, n_valid - 1)
    tile_e = jnp.minimum(jnp.sum((ends[None, :] <= (tile_src * TR)[:, None]).astype(I32), axis=1), N_EXPERTS - 1)
    left = jnp.sum(jnp.where(tile_e[:, None] == jnp.arange(N_EXPERTS, dtype=I32)[None, :],
                             (base + total_c)[None, :], 0), axis=1) - tile_src * TR
    tile_rows = jnp.where(tiles < n_valid, jnp.clip(left, 0, TR), 0)
    experts = jnp.arange(N_EXPERTS, dtype=I32)
    active = total_p > 0
    later = active[None, :] & (experts[None, :] > experts[:, None])
    e_next = jnp.min(jnp.where(later, experts[None, :], N_EXPERTS), axis=1)
    def copy_list(count, first_l, first_g, rows, width):
        end = jnp.cumsum(count, axis=1)
        start = end - count
        j = jnp.arange(width, dtype=I32)[None, :, None]
        in_seg = (start[:, None, :] <= j) & (j < end[:, None, :])
        step = (j - start[:, None, :]) * rows
        lst_l = jnp.sum(jnp.where(in_seg, first_l[:, None, :] + step, 0), axis=2)
        lst_g = jnp.sum(jnp.where(in_seg, first_g[:, None, :] + step, 0), axis=2)
        return lst_l, lst_g, end[:, -1], j[:, :, 0] < end[:, -1:]

    chunks = jnp.arange(m_out_max // KCC, dtype=I32)[None, :, None]
    chunk_units = 0
    lists = {}
    left_units, sent_rows, larger = n8, jnp.zeros_like(n8), None
    for rows in COPY_ROWS:
        per = rows // SEGG
        count = left_units // per
        width = m_out_max // rows if larger is None else N_EXPERTS * (larger // rows - 1)
        larger = rows
        lst_l, lst_g, lst_n, ok = copy_list(count, loc + sent_rows, glob + sent_rows, rows, width)
        chunk_units = chunk_units + jnp.sum(jnp.where(ok[:, None, :] & (lst_l[:, None, :] // KCC == chunks), per, 0), axis=2)
        lists[rows] = (lst_l.reshape(-1).astype(I32), lst_g.reshape(-1).astype(I32), lst_n.astype(I32))
        left_units, sent_rows = left_units - count * per, sent_rows + count * rows
    local_rows = jnp.sum(segl, axis=1)
    return {
        "copy_lists": tuple(a for rows in COPY_ROWS for a in lists[rows]),
        "units": jnp.sum(n8, axis=1).astype(I32),
        "chunk_units": chunk_units.reshape(-1).astype(I32),
        "nch": ((local_rows + (KC - 1)) // KC).astype(I32),
        "nchc": ((local_rows + (KCC - 1)) // KCC).astype(I32),
        "tail_g": (base + total).astype(I32),
        "tail_n": ((total_c - total) // SEGG).astype(I32),
        "tile_e": tile_e.astype(I32),
        "tile_src": tile_src.astype(I32),
        "tile_rows": tile_rows.astype(I32),
        "e_slot": ((jnp.cumsum(active.astype(I32)) - 1) % 2).astype(I32),
        "e_next": jnp.where(e_next < N_EXPERTS, e_next, -1).astype(I32),
    }


def _layer(xp, xs, st, pp, ps, w_in, w_pool, pool_scale, sgu_ln_g, sgu_ln_b, w_s, b_s, w_out, ln1_g, ln1_b,
           router_w, router_bias, w_gate, w_up, w_down, ws_gate, ws_up, ws_down, ln2_g, ln2_b, w_pe, w_pgate,
           b_pgate, alpha):
    B, T, D = xp.shape
    NS, TS, _ = xs.shape
    n_prompt, n_sample = B * T, NS * TS
    n = n_prompt + n_sample
    n_heads = w_s.shape[0]
    row = lambda v: v.reshape(1, -1)

    x_t = jnp.transpose(xs, (1, 0, 2))
    st_t = jnp.transpose(st, (1, 0, 2))
    bsb = jnp.broadcast_to(b_s[:, :, None], (n_heads, CHUNK, SGU_HEAD_DIM))
    wrow = jnp.repeat(jnp.transpose(w_s[:, :TS, :TS], (1, 2, 0)), SGU_HEAD_DIM, axis=2)
    brow = jnp.repeat(jnp.transpose(b_s[:, :TS], (1, 0)), SGU_HEAD_DIM, axis=1)
    h1_all, h1b_all, pool_p, a_s, vn_s, dest, wsel, cnt = _mixer(
        xp, x_t, st_t, w_in.astype(BF16), w_pool.astype(BF16), row(pool_scale), sgu_ln_g, sgu_ln_b, w_s, bsb,
        wrow, brow, w_out.astype(BF16), row(ln1_g), row(ln1_b), jnp.transpose(router_w),
        router_bias.reshape(-1, 1), alpha)
    nb = n // TB
    m_out_max = TB * TOP_K + N_EXPERTS * SEG
    rows_max = n * TOP_K + nb * N_EXPERTS * (SEGG - 1) + N_EXPERTS * (TR - SEGG)
    n_tiles_max = -(-rows_max // TR)
    meta = _sort_meta(cnt[:, :, 0], n_tiles_max, m_out_max)
    xsort, src_rep, w_rep = _dispatch(meta, dest, wsel, h1b_all, n_tiles_max * TR, m_out_max)
    ysort = _expert_ffn(meta, xsort, w_gate, w_up, w_down)

    nbs = TB // TS
    ps_t = jnp.transpose(ps.reshape(NS // nbs, nbs, TS, -1), (0, 2, 1, 3)).reshape(n_sample, -1)
    y_p, y_s = _combine(meta, src_rep, w_rep, h1_all, h1b_all, pp.reshape(n_prompt, -1), ps_t,
                        ws_gate.astype(BF16), ws_up.astype(BF16),
                        ws_down.astype(BF16), row(ln2_g), row(ln2_b), w_pgate.astype(BF16), row(b_pgate),
                        w_pe.astype(BF16), ysort, m_out_max, n_prompt, alpha)

    yp = y_p.reshape(B, T, D)
    ys = jnp.transpose(y_s.reshape(NS // nbs, TS, nbs, D), (0, 2, 1, 3)).reshape(NS, TS, D)
    new_pool_p = pool_p[:, HALO - POOL_BUF:, :]
    new_pool_s = jnp.concatenate([st, jnp.transpose(a_s, (1, 0, 2))], axis=1)[:, -POOL_BUF:]
    vn = jnp.transpose(vn_s, (1, 0, 2)).reshape(NS, TS, n_heads, SGU_HEAD_DIM)
    return yp, ys, new_pool_p, new_pool_s, vn


def kernel(x_prompt, x_sample, state_pool, p_prompt, p_sample, w_in, w_pool, pool_scale, sgu_ln_g, sgu_ln_b, w_s, b_s, w_out, ln1_g, ln1_b, router_w, router_bias, w_gate, w_up, w_down, ws_gate, ws_up, ws_down, ln2_g, ln2_b, w_pe, w_pgate, b_pgate):
    depth = w_in.shape[0]
    alpha = (2.0 * depth) ** 0.25
    hp, hs = x_prompt, x_sample
    pool_p, pool_s, v_s = [], [], []
    for i in range(depth):
        hp, hs, bp, bs, vs = _layer(
            hp, hs, state_pool[i], p_prompt[i], p_sample[i], w_in[i], w_pool[i], pool_scale[i], sgu_ln_g[i],
            sgu_ln_b[i], w_s[i], b_s[i], w_out[i], ln1_g[i], ln1_b[i], router_w[i], router_bias[i], w_gate[i],
            w_up[i], w_down[i], ws_gate[i], ws_up[i], ws_down[i], ln2_g[i], ln2_b[i], w_pe[i], w_pgate[i],
            b_pgate[i], alpha)
        pool_p.append(bp)
        pool_s.append(bs)
        v_s.append(vs)
    return hp, hs, jnp.stack(pool_p), jnp.stack(pool_s), jnp.stack(v_s)
```

```python
import functools

import jax
import jax.numpy as jnp
import numpy as np
from jax import lax
from jax.experimental import pallas as pl
from jax.experimental.pallas import tpu as pltpu

F32 = jnp.float32
BF16 = jnp.bfloat16
I32 = jnp.int32

POOL_WINDOWS = (2, 4, 8, 16)
POOL_BUF = max(POOL_WINDOWS) - 1
CHUNK = 128
SGU_HEAD_DIM = 128
N_EXPERTS = 64
TOP_K = 8
N_EXPERT_GROUPS = 8
TOPK_GROUPS = 4
ROUTE_SCALE = 2.5
LN_EPS = 1e-5
PAST_LEN = 16384

TB = 256
HALO = 16
TR = 512
TRC = 128
SEG = 8
SEGG = 8
COPY_ROWS = (32, 16, 8)
KC = 256
KCC = 512
VMEM_LIMIT = 56 * 1024 * 1024


def _const_spec(shape):
    n = len(shape)
    return pl.BlockSpec(shape, lambda *_: (0,) * n, pipeline_mode=pl.Buffered(1))


def _ln_rows(x, g, b):
    mu = jnp.mean(x, axis=-1, keepdims=True)
    xc = x - mu
    var = jnp.mean(xc * xc, axis=-1, keepdims=True)
    return xc * lax.rsqrt(var + LN_EPS) * g + b


def _dot(a, b):
    return jnp.dot(a, b, preferred_element_type=F32)


def _prompt_tile(j, n_tiles, x, proj, band_ref, wpool_ref, pscale_ref, sg_ref, sb_ref, ws_ref, bsb_ref,
                 pool_ref, zcat_ref, mixin_ref):
    d_pool = zcat_ref.shape[1]
    d_pool_g = d_pool // len(POOL_WINDOWS)
    d_sgu = mixin_ref.shape[1] - d_pool

    a = proj[:, :d_pool]
    zcat_ref[HALO:, :] = a.astype(BF16)

    pos1 = (j * TB + lax.broadcasted_iota(I32, (TB, 1), 0) + 1).astype(F32)
    for g, w in enumerate(POOL_WINDOWS):
        c0, c1 = g * d_pool_g, (g + 1) * d_pool_g
        s = _dot(band_ref[g], zcat_ref[:, c0:c1])
        cnt = jnp.minimum(pos1, float(w))
        d = s / cnt - a[:, c0:c1]
        y = _dot(d.astype(BF16), wpool_ref[g])
        mixin_ref[:, c0:c1] = (y * pscale_ref[:, c0:c1]).astype(BF16)
    zcat_ref[0:HALO, :] = jnp.where(j == n_tiles - 1, jnp.zeros((HALO, d_pool), BF16), zcat_ref[TB:TB + HALO, :])
    pool_ref[0] = a[TB - HALO:, :]

    u = jax.nn.gelu(proj[:, d_pool:d_pool + d_sgu])
    v = jax.nn.gelu(proj[:, d_pool + d_sgu:])
    r = lax.broadcasted_iota(I32, (CHUNK, CHUNK), 0)
    c = lax.broadcasted_iota(I32, (CHUNK, CHUNK), 1)
    tril = r >= c
    for h in range(d_sgu // SGU_HEAD_DIM):
        l0, l1 = h * SGU_HEAD_DIM, (h + 1) * SGU_HEAD_DIM
        vn = _ln_rows(v[:, l0:l1], sg_ref[h:h + 1, :], sb_ref[h:h + 1, :]).astype(BF16)
        wst = jnp.where(tril, ws_ref[h], 0.0).astype(BF16)
        n_ch = TB // CHUNK
        mixed_all = _dot(wst, jnp.concatenate([vn[ci * CHUNK:(ci + 1) * CHUNK] for ci in range(n_ch)], axis=1))
        for ci in range(n_ch):
            r0, r1 = ci * CHUNK, (ci + 1) * CHUNK
            mixed = mixed_all[:, ci * SGU_HEAD_DIM:(ci + 1) * SGU_HEAD_DIM] + bsb_ref[h]
            mixin_ref[r0:r1, d_pool + l0:d_pool + l1] = (u[r0:r1, l0:l1] * mixed).astype(BF16)


def _sample_tile(T, NB, proj, st_ref, wpool_ref, pscale_ref, sg_ref, sb_ref, wrow_ref, brow_ref,
                 a_ref, vn_ref, dbuf_ref, mixin_ref):
    d_pool = dbuf_ref.shape[1]
    d_pool_g = d_pool // len(POOL_WINDOWS)
    d_sgu = mixin_ref.shape[1] - d_pool

    a = proj[:, :d_pool]
    for t in range(T):
        a_ref[t] = a[t * NB:(t + 1) * NB]

    for g, w in enumerate(POOL_WINDOWS):
        c0, c1 = g * d_pool_g, (g + 1) * d_pool_g
        for t in range(T):
            cnt = float(min(PAST_LEN + t + 1, w))
            acc = None
            for s in range(POOL_BUF + t - w + 1, POOL_BUF + t + 1):
                if s < POOL_BUF:
                    term = st_ref[s, :, c0:c1]
                else:
                    term = a[(s - POOL_BUF) * NB:(s - POOL_BUF + 1) * NB, c0:c1]
                acc = term if acc is None else acc + term
            d = acc / cnt - a[t * NB:(t + 1) * NB, c0:c1]
            dbuf_ref[t * NB:(t + 1) * NB, c0:c1] = d.astype(BF16)
        y = _dot(dbuf_ref[0:T * NB, c0:c1], wpool_ref[g])
        mixin_ref[:, c0:c1] = (y * pscale_ref[:, c0:c1]).astype(BF16)

    u = jax.nn.gelu(proj[:, d_pool:d_pool + d_sgu])
    v = jax.nn.gelu(proj[:, d_pool + d_sgu:])
    for h in range(d_sgu // SGU_HEAD_DIM):
        l0, l1 = h * SGU_HEAD_DIM, (h + 1) * SGU_HEAD_DIM
        vn = _ln_rows(v[:, l0:l1], sg_ref[h:h + 1, :], sb_ref[h:h + 1, :])
        for t in range(T):
            vn_ref[t, :, l0:l1] = vn[t * NB:(t + 1) * NB]
    for t in range(T):
        mixed = brow_ref[t:t + 1, :]
        for s in range(t + 1):
            mixed = mixed + wrow_ref[t, s:s + 1, :] * vn_ref[s]
        mixin_ref[t * NB:(t + 1) * NB, d_pool:] = (u[t * NB:(t + 1) * NB] * mixed).astype(BF16)


def _mixer_body(alpha, n_pt, tiles_per_seq, xp_ref, xs_ref, st_ref, win_ref, band_ref, wpool_ref, pscale_ref,
                sg_ref, sb_ref, ws_ref, bsb_ref, wrow_ref, brow_ref, wout_ref, g1_ref, b1_ref,
                rw_ref, rb_ref, upper_ref, lstrict_ref,
                h1_ref, h1b_ref, pool_ref, a_ref, vn_ref, dest_ref, wsel_ref, cnt_ref,
                zcat_ref, mixin_ref, hprev_ref):
    i = pl.program_id(0)
    n_tiles = pl.num_programs(0) - 1

    def route_previous_tile():
        h = hprev_ref[...]
        _route_tile(h, h.astype(BF16), rw_ref, rb_ref, upper_ref, lstrict_ref,
                    dest_ref, wsel_ref, cnt_ref)

    @pl.when(i == 0)
    def _():
        zcat_ref[0:HALO, :] = jnp.zeros((HALO, zcat_ref.shape[1]), BF16)
        hprev_ref[...] = jnp.zeros(hprev_ref.shape, F32)

    @pl.when(i < n_pt)
    def _():
        x = xp_ref[0]
        proj = _dot(x.astype(BF16), win_ref[...])
        route_previous_tile()
        _prompt_tile(i % tiles_per_seq, tiles_per_seq, x, proj, band_ref, wpool_ref, pscale_ref, sg_ref, sb_ref,
                     ws_ref, bsb_ref, pool_ref, zcat_ref, mixin_ref)
        h1_ref[...] = alpha * x

    @pl.when(i >= n_pt)
    def _():
        route_previous_tile()

    @pl.when((i >= n_pt) & (i < n_tiles))
    def _():
        T, NB, D = xs_ref.shape
        x = xs_ref[...].reshape(T * NB, D)
        proj = _dot(x.astype(BF16), win_ref[...])
        _sample_tile(T, NB, proj, st_ref, wpool_ref, pscale_ref, sg_ref, sb_ref, wrow_ref, brow_ref,
                     a_ref, vn_ref, zcat_ref, mixin_ref)
        h1_ref[...] = alpha * x

    @pl.when(i < n_tiles)
    def _():
        mix = _dot(mixin_ref[...], wout_ref[...])
        h1 = _ln_rows(h1_ref[...] + mix, g1_ref[...], b1_ref[...])
        h1_ref[...] = h1
        h1b_ref[...] = h1.astype(BF16)
        hprev_ref[...] = h1


def _pool_band(tm):
    t = np.arange(tm)[:, None] + HALO
    s = np.arange(tm + HALO)[None, :]
    return jnp.asarray(np.stack([((s <= t) & (s >= t - w + 1)) for w in POOL_WINDOWS]).astype(np.float32), BF16)


def _mixer(xp, xs_t, st_t, win_b, wpool_b, pscale, sg, sb, ws, bsb, wrow, brow, wout_b, g1, b1, rwt, rb_col, alpha):
    B, T, D = xp.shape
    TS, NS, _ = xs_t.shape
    d_pool = st_t.shape[2]
    d_sgu = wrow.shape[2]
    tps = T // TB
    n_pt = B * tps
    NB = TB // TS
    n_st = NS // NB
    n = B * T + NS * TS
    nblk = n_pt + n_st
    pt = lambda i: jnp.minimum(i, n_pt - 1)
    stile = lambda i: jnp.clip(i - n_pt, 0, n_st - 1)
    blk = lambda i: jnp.minimum(i, nblk - 1)
    routed = lambda i: jnp.maximum(i - 1, 0)
    body = functools.partial(_mixer_body, alpha, n_pt, tps)
    rw_hi = rwt.astype(BF16)
    rw_lo = (rwt - rw_hi.astype(F32)).astype(BF16)
    upper = jnp.asarray(np.triu(np.ones((TB, TB), np.float32), 1), BF16)
    lstrict = jnp.asarray(np.tril(np.ones((N_EXPERTS, N_EXPERTS), np.float32), -1), BF16)
    consts = (win_b, _pool_band(TB), wpool_b, pscale, sg, sb, ws, bsb, wrow, brow, wout_b, g1, b1,
              jnp.concatenate([rw_hi, rw_lo], axis=0), rb_col, upper, lstrict)
    return pl.pallas_call(
        body,
        grid=(nblk + 1,),
        in_specs=[
            pl.BlockSpec((1, TB, D), lambda i: (pt(i) // tps, pt(i) % tps, 0)),
            pl.BlockSpec((TS, NB, D), lambda i: (0, stile(i), 0), pipeline_mode=pl.Buffered(1)),
            pl.BlockSpec((POOL_BUF, NB, d_pool), lambda i: (0, stile(i), 0), pipeline_mode=pl.Buffered(1)),
        ] + [_const_spec(c.shape) for c in consts],
        out_specs=[
            pl.BlockSpec((TB, D), lambda i: (blk(i), 0)),
            pl.BlockSpec((TB, D), lambda i: (blk(i), 0)),
            pl.BlockSpec((1, HALO, d_pool), lambda i: (pt(i) // tps, 0, 0)),
            pl.BlockSpec((TS, NB, d_pool), lambda i: (0, stile(i), 0)),
            pl.BlockSpec((TS, NB, d_sgu), lambda i: (0, stile(i), 0)),
            pl.BlockSpec((1, TOP_K, TB), lambda i: (routed(i), 0, 0)),
            pl.BlockSpec((1, TOP_K, TB), lambda i: (routed(i), 0, 0)),
            pl.BlockSpec((1, N_EXPERTS, 128), lambda i: (routed(i), 0, 0)),
        ],
        out_shape=[
            jax.ShapeDtypeStruct((n, D), F32),
            jax.ShapeDtypeStruct((n, D), BF16),
            jax.ShapeDtypeStruct((B, HALO, d_pool), F32),
            jax.ShapeDtypeStruct((TS, NS, d_pool), F32),
            jax.ShapeDtypeStruct((TS, NS, d_sgu), F32),
            jax.ShapeDtypeStruct((nblk, TOP_K, TB), I32),
            jax.ShapeDtypeStruct((nblk, TOP_K, TB), F32),
            jax.ShapeDtypeStruct((nblk, N_EXPERTS, 128), I32),
        ],
        scratch_shapes=[
            pltpu.VMEM((TB + HALO, d_pool), BF16),
            pltpu.VMEM((TB, D), BF16),
            pltpu.VMEM((TB, D), F32),
        ],
        compiler_params=pltpu.CompilerParams(
            dimension_semantics=("arbitrary",), vmem_limit_bytes=VMEM_LIMIT),
        name="mixer",
    )(xp, xs_t, st_t, *consts)


def _route_tile(h1, h1b, rw_ref, rb_ref, upper_ref, lstrict_ref, dest_ref, wsel_ref, cnt_ref):
    per_group = N_EXPERTS // N_EXPERT_GROUPS
    neg = -jnp.inf
    nt = (((1,), (1,)), ((), ()))
    h_lo = (h1 - h1b.astype(F32)).astype(BF16)
    by_h_hi = lax.dot_general(rw_ref[...], h1b, nt, preferred_element_type=F32)
    logits_t = (by_h_hi[:N_EXPERTS] + by_h_hi[N_EXPERTS:]
                + lax.dot_general(rw_ref[0:N_EXPERTS, :], h_lo, nt, preferred_element_type=F32))
    s_t = jax.nn.sigmoid(logits_t)
    b_t = s_t + rb_ref[...]

    io_g = lax.broadcasted_iota(I32, (per_group, TB), 0)
    gs = []
    for g in range(N_EXPERT_GROUPS):
        xg = b_t[g * per_group:(g + 1) * per_group, :]
        m1 = jnp.max(xg, axis=0, keepdims=True)
        i1 = jnp.min(jnp.where(xg == m1, io_g, per_group), axis=0, keepdims=True)
        m2 = jnp.max(jnp.where(io_g == i1, neg, xg), axis=0, keepdims=True)
        gs.append(m1 + m2)
    masked = []
    for g in range(N_EXPERT_GROUPS):
        rank = jnp.zeros((1, TB), F32)
        for g2 in range(N_EXPERT_GROUPS):
            if g2 != g:
                ahead = (gs[g2] >= gs[g]) if g2 < g else (gs[g2] > gs[g])
                rank = rank + jnp.where(ahead, 1.0, 0.0)
        keep = rank < float(TOPK_GROUPS)
        masked.append(jnp.where(keep, b_t[g * per_group:(g + 1) * per_group, :], neg))
    xm = jnp.concatenate(masked, axis=0)

    io_e = lax.broadcasted_iota(I32, (N_EXPERTS, TB), 0)
    onehots = []
    sel = jnp.zeros((N_EXPERTS, TB), F32)
    for _ in range(TOP_K):
        m = jnp.max(xm, axis=0, keepdims=True)
        idx = jnp.min(jnp.where(xm == m, io_e, N_EXPERTS), axis=0, keepdims=True)
        oh = io_e == idx
        onehots.append(oh)
        sel = jnp.where(oh, 1.0, sel)
        xm = jnp.where(oh, neg, xm)

    ssel = sel * s_t
    denom = jnp.sum(ssel, axis=0, keepdims=True)
    comb = ssel / denom * ROUTE_SCALE

    rank_t = _dot(sel.astype(BF16), upper_ref[...])
    cnt = jnp.sum(sel, axis=1, keepdims=True)
    cnt_i = cnt.astype(I32)
    seg16 = ((cnt_i + (SEG - 1)) // SEG).astype(F32)
    off16 = _dot(lstrict_ref[...], jnp.broadcast_to(seg16, (N_EXPERTS, 128)).astype(BF16))
    d_t = rank_t + off16[:, 0:1] * float(SEG)
    for k in range(TOP_K):
        dest_ref[0, k:k + 1, :] = jnp.sum(jnp.where(onehots[k], d_t, 0.0), axis=0, keepdims=True).astype(I32)
        wsel_ref[0, k:k + 1, :] = jnp.sum(jnp.where(onehots[k], comb, 0.0), axis=0, keepdims=True)
    cnt_ref[0] = jnp.broadcast_to(cnt_i, (N_EXPERTS, 128))


N_LIST_REFS = 3 * len(COPY_ROWS)


def _start_block_copies(b, copy_lists, make_copy):
    for k, rows in enumerate(COPY_ROWS):
        table_l, table_g, counts = copy_lists[3 * k:3 * k + 3]
        width = table_l.shape[0] // counts.shape[0]
        n = counts[b]

        def start(j, table_l=table_l, table_g=table_g, width=width, rows=rows):
            idx = b * width + j
            make_copy(pl.multiple_of(table_l[idx], min(rows, SEG)), pl.multiple_of(table_g[idx], SEGG), rows).start()

        def eight(q, c, start=start):
            for u in range(8):
                start(8 * q + u)
            return c

        def one(j, c, start=start):
            start(j)
            return c

        lax.fori_loop(0, n // 8, eight, 0)
        lax.fori_loop(n // 8 * 8, n, one, 0)


def _wait_row_units(count, max_count, make_wait_copy):
    for k in range(int(max_count).bit_length()):
        @pl.when(((count >> k) & 1) == 1)
        def _():
            make_wait_copy(SEGG << k).wait()


def _dispatch_body(*refs):
    copy_lists = refs[:N_LIST_REFS]
    (units, nch, tail_g, tail_n, dest_ref, wsel_ref, h_ref, tokid_ref, ones_ref,
     xs_hbm, src_ref, wrow_ref, xloc_ref, zero_ref, sems, tail_sem) = refs[N_LIST_REFS:]
    b = pl.program_id(0)
    nb = pl.num_programs(0)
    max_units = xloc_ref.shape[1] // SEGG
    slot = b % 2

    def seg_copy(sl):
        return lambda l, g, rows: pltpu.make_async_copy(
            xloc_ref.at[sl, pl.ds(l, rows), :], xs_hbm.at[pl.ds(g, rows), :], sems.at[sl])

    def wait_block(count, sl):
        _wait_row_units(count, max_units, lambda rows: pltpu.make_async_copy(
            xloc_ref.at[sl, pl.ds(0, rows), :], xs_hbm.at[pl.ds(0, rows), :], sems.at[sl]))

    @pl.when(b >= 2)
    def _():
        wait_block(units[jnp.maximum(b - 2, 0)], slot)

    dest = dest_ref[0]
    wsel = wsel_ref[0]
    h = h_ref[...]
    src_ref[...] = jnp.zeros(src_ref.shape, F32)
    wrow_ref[...] = jnp.zeros(wrow_ref.shape, F32)

    row_in_chunk = lax.broadcasted_iota(I32, (KC, TB), 0).astype(F32).astype(BF16)
    wsel_b = wsel.astype(BF16)
    one = jnp.ones((1, TB), BF16)

    def chunk(c):
        r0 = pl.multiple_of(c * KC, KC)
        rel = (dest - r0).astype(F32).astype(BF16)
        sb = jnp.zeros((KC, TB), BF16)
        wm = jnp.zeros((KC, TB), BF16)
        for k in range(TOP_K):
            hit = rel[k:k + 1, :] == row_in_chunk
            sb = jnp.where(hit, one, sb)
            wm = jnp.where(hit, wsel_b[k:k + 1, :], wm)
        xloc_ref[slot, pl.ds(r0, KC), :] = _dot(sb, h).astype(BF16)
        src_ref[0, pl.ds(r0, KC), :] = _dot(sb, tokid_ref[...])
        wrow_ref[0, pl.ds(r0, KC), :] = _dot(wm, ones_ref[...])

    def chunk_pair(cp, carry):
        chunk(2 * cp)
        chunk(2 * cp + 1)
        return carry

    lax.fori_loop(0, nch[b] // 2, chunk_pair, 0)

    @pl.when(nch[b] % 2 == 1)
    def _():
        chunk(nch[b] - 1)

    _start_block_copies(b, copy_lists, seg_copy(slot))

    def tail_copy(g):
        return pltpu.make_async_copy(zero_ref, xs_hbm.at[pl.ds(g, SEGG), :], tail_sem)

    def tails(wait):
        def per_expert(e, carry):
            def per_granule(j, c):
                cp = tail_copy(pl.multiple_of(tail_g[e] + j * SEGG, SEGG))
                if wait:
                    cp.wait()
                else:
                    cp.start()
                return c
            lax.fori_loop(0, tail_n[e], per_granule, 0)
            return carry
        lax.fori_loop(0, N_EXPERTS, per_expert, 0)

    @pl.when(b == nb - 1)
    def _():
        zero_ref[...] = jnp.zeros(zero_ref.shape, BF16)
        tails(False)

        @pl.when(b >= 1)
        def _():
            wait_block(units[jnp.maximum(b - 1, 0)], 1 - slot)

        wait_block(units[b], slot)
        tails(True)


def _dispatch(meta, dest, wsel, h1b_all, n_rows_sorted, m_out_max):
    n, D = h1b_all.shape
    nb = n // TB
    tokid = jnp.asarray(np.broadcast_to(np.arange(TB, dtype=np.float32)[:, None], (TB, 128)), BF16)
    ones = jnp.ones((TB, 128), BF16)
    cs = lambda shape: pl.BlockSpec(shape, lambda i, *_: (0,) * len(shape), pipeline_mode=pl.Buffered(1))
    grid_spec = pltpu.PrefetchScalarGridSpec(
        num_scalar_prefetch=N_LIST_REFS + 4,
        grid=(nb,),
        in_specs=[
            pl.BlockSpec((1, TOP_K, TB), lambda i, *_: (i, 0, 0)),
            pl.BlockSpec((1, TOP_K, TB), lambda i, *_: (i, 0, 0)),
            pl.BlockSpec((TB, D), lambda i, *_: (i, 0)),
            cs(tokid.shape), cs(ones.shape),
        ],
        out_specs=[
            pl.BlockSpec(memory_space=pl.ANY),
            pl.BlockSpec((1, m_out_max, 128), lambda i, *_: (i, 0, 0)),
            pl.BlockSpec((1, m_out_max, 128), lambda i, *_: (i, 0, 0)),
        ],
        scratch_shapes=[
            pltpu.VMEM((2, m_out_max, D), BF16),
            pltpu.VMEM((SEGG, D), BF16),
            pltpu.SemaphoreType.DMA((2,)),
            pltpu.SemaphoreType.DMA(()),
        ],
    )
    return pl.pallas_call(
        _dispatch_body,
        grid_spec=grid_spec,
        out_shape=[
            jax.ShapeDtypeStruct((n_rows_sorted, D), BF16),
            jax.ShapeDtypeStruct((nb, m_out_max, 128), F32),
            jax.ShapeDtypeStruct((nb, m_out_max, 128), F32),
        ],
        compiler_params=pltpu.CompilerParams(
            dimension_semantics=("arbitrary",), vmem_limit_bytes=VMEM_LIMIT, has_side_effects=True),
        name="dispatch",
    )(*meta["copy_lists"], meta["units"], meta["nch"], meta["tail_g"], meta["tail_n"], dest, wsel, h1b_all, tokid,
      ones)


def _ffn_body(tile_e, tile_src, tile_rows, e_slot, e_next, x_ref, wg_hbm, wu_hbm, wd_hbm, y_ref,
              wg_f, wu_f, wd_f, wgu_b, wd_b, sems):
    i = pl.program_id(0)
    f = wg_f.shape[2]
    e = tile_e[i]
    e_prev = tile_e[jnp.maximum(i - 1, 0)]

    def weight_copies(expert, slot):
        return (pltpu.make_async_copy(wg_hbm.at[expert], wg_f.at[slot], sems.at[slot, 0]),
                pltpu.make_async_copy(wu_hbm.at[expert], wu_f.at[slot], sems.at[slot, 1]),
                pltpu.make_async_copy(wd_hbm.at[expert], wd_f.at[slot], sems.at[slot, 2]))

    @pl.when((i == 0) | (e != e_prev))
    def _():
        slot = e_slot[e]

        @pl.when(i == 0)
        def _():
            for cp in weight_copies(e, slot):
                cp.start()

        for cp in weight_copies(e, slot):
            cp.wait()
        nxt = e_next[e]

        @pl.when(nxt >= 0)
        def _():
            for cp in weight_copies(nxt, 1 - slot):
                cp.start(priority=1)

        wgu_b[:, :f] = wg_f[slot].astype(BF16)
        wgu_b[:, f:] = wu_f[slot].astype(BF16)
        wd_b[...] = wd_f[slot].astype(BF16)

    def swiglu(x):
        gu = _dot(x, wgu_b[...])
        act = (jax.nn.silu(gu[:, :f]) * gu[:, f:]).astype(BF16)
        return _dot(act, wd_b[...]).astype(BF16)

    for m in range(TRC, TR + 1, TRC):
        @pl.when(tile_rows[i] == m)
        def _(m=m):
            y_ref[0:m, :] = swiglu(x_ref[0:m, :])
            if m < TR:
                y_ref[m:, :] = jnp.zeros((TR - m, y_ref.shape[1]), BF16)


def _expert_ffn(meta, xs, w_gate, w_up, w_down):
    R, D = xs.shape
    E, _, f = w_gate.shape
    nt = R // TR
    grid_spec = pltpu.PrefetchScalarGridSpec(
        num_scalar_prefetch=5,
        grid=(nt,),
        in_specs=[
            pl.BlockSpec((TR, D), lambda i, te, ts, *_: (ts[i], 0)),
            pl.BlockSpec(memory_space=pl.ANY),
            pl.BlockSpec(memory_space=pl.ANY),
            pl.BlockSpec(memory_space=pl.ANY),
        ],
        out_specs=pl.BlockSpec((TR, D), lambda i, te, ts, *_: (ts[i], 0)),
        scratch_shapes=[
            pltpu.VMEM((2, D, f), F32),
            pltpu.VMEM((2, D, f), F32),
            pltpu.VMEM((2, f, D), F32),
            pltpu.VMEM((D, 2 * f), BF16),
            pltpu.VMEM((f, D), BF16),
            pltpu.SemaphoreType.DMA((2, 3)),
        ],
    )
    return pl.pallas_call(
        _ffn_body,
        grid_spec=grid_spec,
        out_shape=jax.ShapeDtypeStruct((R, D), BF16),
        compiler_params=pltpu.CompilerParams(
            dimension_semantics=("arbitrary",), vmem_limit_bytes=VMEM_LIMIT),
        name="expert_ffn",
    )(meta["tile_e"], meta["tile_src"], meta["tile_rows"], meta["e_slot"], meta["e_next"],
      xs, w_gate, w_up, w_down)


def _combine_body(alpha, n_pb, *refs):
    copy_lists = refs[:N_LIST_REFS]
    (chunk_units, nch, src_ref, wrow_ref, h1_ref, h1b_ref, pp_ref, ps_ref,
     wsg_ref, wsu_ref, wsd_ref, g2_ref, b2_ref, wpg_ref, bpg_ref, wpe_ref, ys_hbm,
     yp_ref, ysm_ref, yloc_ref, acc_ref, sems) = refs[N_LIST_REFS:]
    b = pl.program_id(0)
    n_chunks = nch[b]
    max_chunks = sems.shape[0]

    def seg_copy(l, g, rows):
        return pltpu.make_async_copy(ys_hbm.at[pl.ds(g, rows), :], yloc_ref.at[pl.ds(l, rows), :],
                                     sems.at[l // KCC])

    def fetch_block(blk):
        _start_block_copies(blk, copy_lists, seg_copy)

    @pl.when(b == 0)
    def _():
        yloc_ref[...] = jnp.zeros(yloc_ref.shape, BF16)
        fetch_block(b)

    hb = h1b_ref[...]
    sh = (jax.nn.silu(_dot(hb, wsg_ref[...])) * _dot(hb, wsu_ref[...])).astype(BF16)
    acc_ref[...] = _dot(sh, wsd_ref[...])

    def chunk(c, carry):
        def wait_rows(rows):
            return pltpu.make_async_copy(ys_hbm.at[pl.ds(0, rows), :], yloc_ref.at[pl.ds(0, rows), :], sems.at[c])

        _wait_row_units(chunk_units[b * max_chunks + c], KCC // SEGG, wait_rows)

        r0 = pl.multiple_of(c * KCC, KCC)
        src = src_ref[0, pl.ds(r0, KCC), :]
        w = wrow_ref[0, pl.ds(r0, KCC), :]
        lane = lax.broadcasted_iota(I32, src.shape, 1).astype(F32)
        wm = jnp.concatenate([jnp.where(src == lane + float(o), w, 0.0) for o in range(0, TB, src.shape[1])],
                             axis=1).astype(BF16)
        acc_ref[...] += lax.dot_general(wm, yloc_ref[pl.ds(r0, KCC), :],
                                        (((0,), (0,)), ((), ())), preferred_element_type=F32)
        return carry

    lax.fori_loop(0, n_chunks, chunk, 0)

    @pl.when(b + 1 < pl.num_programs(0))
    def _():
        fetch_block(b + 1)

    h2 = _ln_rows(alpha * h1_ref[...] + acc_ref[...], g2_ref[...], b2_ref[...])
    gate = jax.nn.sigmoid(_dot(h2.astype(BF16), wpg_ref[...]) + bpg_ref[...])
    p = jnp.where(b < n_pb, pp_ref[...], ps_ref[...])
    pe = _dot(p.astype(BF16), wpe_ref[...])
    y = h2 + gate * pe

    @pl.when(b < n_pb)
    def _():
        yp_ref[...] = y

    @pl.when(b >= n_pb)
    def _():
        ysm_ref[...] = y


def _combine(meta, src_rep, w_rep, h1_all, h1b_all, p_prompt, p_sample, wsg_b, wsu_b, wsd_b, g2, b2, wpg_b, bpg,
             wpe_b, ys, m_out_max, n_prompt, alpha):
    n, D = h1_all.shape
    nb = n // TB
    n_pb = n_prompt // TB
    d_pe = p_prompt.shape[1]
    cs = lambda shape: pl.BlockSpec(shape, lambda i, *_: (0,) * len(shape), pipeline_mode=pl.Buffered(1))
    grid_spec = pltpu.PrefetchScalarGridSpec(
        num_scalar_prefetch=N_LIST_REFS + 2,
        grid=(nb,),
        in_specs=[
            pl.BlockSpec((1, m_out_max, 128), lambda i, *_: (i, 0, 0)),
            pl.BlockSpec((1, m_out_max, 128), lambda i, *_: (i, 0, 0)),
            pl.BlockSpec((TB, D), lambda i, *_: (i, 0)),
            pl.BlockSpec((TB, D), lambda i, *_: (i, 0)),
            pl.BlockSpec((TB, d_pe), lambda i, *_: (jnp.minimum(i, n_pb - 1), 0)),
            pl.BlockSpec((TB, d_pe), lambda i, *_: (jnp.maximum(i - n_pb, 0), 0)),
            cs(wsg_b.shape), cs(wsu_b.shape), cs(wsd_b.shape), cs(g2.shape), cs(b2.shape),
            cs(wpg_b.shape), cs(bpg.shape), cs(wpe_b.shape),
            pl.BlockSpec(memory_space=pl.ANY),
        ],
        out_specs=[
            pl.BlockSpec((TB, D), lambda i, *_: (jnp.minimum(i, n_pb - 1), 0)),
            pl.BlockSpec((TB, D), lambda i, *_: (jnp.maximum(i - n_pb, 0), 0)),
        ],
        scratch_shapes=[
            pltpu.VMEM((m_out_max, D), BF16),
            pltpu.VMEM((TB, D), F32),
            pltpu.SemaphoreType.DMA((m_out_max // KCC,)),
        ],
    )
    return pl.pallas_call(
        functools.partial(_combine_body, alpha, n_pb),
        grid_spec=grid_spec,
        out_shape=[jax.ShapeDtypeStruct((n_prompt, D), F32), jax.ShapeDtypeStruct((n - n_prompt, D), F32)],
        compiler_params=pltpu.CompilerParams(
            dimension_semantics=("arbitrary",), vmem_limit_bytes=VMEM_LIMIT),
        name="combine",
    )(*meta["copy_lists"], meta["chunk_units"], meta["nchc"], src_rep, w_rep, h1_all, h1b_all, p_prompt, p_sample,
      wsg_b, wsu_b, wsd_b, g2, b2, wpg_b, bpg, wpe_b, ys)


def _sort_meta(cnt, n_tiles_max, m_out_max):
    segl = (cnt + (SEG - 1)) // SEG * SEG
    loc = jnp.cumsum(segl, axis=1) - segl
    n8 = (cnt + (SEGG - 1)) // SEGG
    segp = n8 * SEGG
    before = jnp.cumsum(segp, axis=0) - segp
    total = jnp.sum(segp, axis=0)
    total_c = (total + (TRC - 1)) // TRC * TRC
    total_p = (total + (TR - 1)) // TR * TR
    ends = jnp.cumsum(total_p)
    base = ends - total_p
    glob = base[None, :] + before
    n_valid = ends[-1] // TR
    tiles = jnp.arange(n_tiles_max, dtype=I32)
    tile_src = jnp.minimum(tiles, n_valid - 1)
    tile_e = jnp.minimum(jnp.sum((ends[None, :] <= (tile_src * TR)[:, None]).astype(I32), axis=1), N_EXPERTS - 1)
    left = jnp.sum(jnp.where(tile_e[:, None] == jnp.arange(N_EXPERTS, dtype=I32)[None, :],
                             (base + total_c)[None, :], 0), axis=1) - tile_src * TR
    tile_rows = jnp.where(tiles < n_valid, jnp.clip(left, 0, TR), 0)
    experts = jnp.arange(N_EXPERTS, dtype=I32)
    active = total_p > 0
    later = active[None, :] & (experts[None, :] > experts[:, None])
    e_next = jnp.min(jnp.where(later, experts[None, :], N_EXPERTS), axis=1)
    def copy_list(count, first_l, first_g, rows, width):
        end = jnp.cumsum(count, axis=1)
        start = end - count
        j = jnp.arange(width, dtype=I32)[None, :, None]
        in_seg = (start[:, None, :] <= j) & (j < end[:, None, :])
        step = (j - start[:, None, :]) * rows
        lst_l = jnp.sum(jnp.where(in_seg, first_l[:, None, :] + step, 0), axis=2)
        lst_g = jnp.sum(jnp.where(in_seg, first_g[:, None, :] + step, 0), axis=2)
        return lst_l, lst_g, end[:, -1], j[:, :, 0] < end[:, -1:]

    chunks = jnp.arange(m_out_max // KCC, dtype=I32)[None, :, None]
    chunk_units = 0
    lists = {}
    left_units, sent_rows, larger = n8, jnp.zeros_like(n8), None
    for rows in COPY_ROWS:
        per = rows // SEGG
        count = left_units // per
        width = m_out_max // rows if larger is None else N_EXPERTS * (larger // rows - 1)
        larger = rows
        lst_l, lst_g, lst_n, ok = copy_list(count, loc + sent_rows, glob + sent_rows, rows, width)
        chunk_units = chunk_units + jnp.sum(jnp.where(ok[:, None, :] & (lst_l[:, None, :] // KCC == chunks), per, 0), axis=2)
        lists[rows] = (lst_l.reshape(-1).astype(I32), lst_g.reshape(-1).astype(I32), lst_n.astype(I32))
        left_units, sent_rows = left_units - count * per, sent_rows + count * rows
    local_rows = jnp.sum(segl, axis=1)
    return {
        "copy_lists": tuple(a for rows in COPY_ROWS for a in lists[rows]),
        "units": jnp.sum(n8, axis=1).astype(I32),
        "chunk_units": chunk_units.reshape(-1).astype(I32),
        "nch": ((local_rows + (KC - 1)) // KC).astype(I32),
        "nchc": ((local_rows + (KCC - 1)) // KCC).astype(I32),
        "tail_g": (base + total).astype(I32),
        "tail_n": ((total_c - total) // SEGG).astype(I32),
        "tile_e": tile_e.astype(I32),
        "tile_src": tile_src.astype(I32),
        "tile_rows": tile_rows.astype(I32),
        "e_slot": ((jnp.cumsum(active.astype(I32)) - 1) % 2).astype(I32),
        "e_next": jnp.where(e_next < N_EXPERTS, e_next, -1).astype(I32),
    }


def _layer(xp, xs, st, pp, ps, w_in, w_pool, pool_scale, sgu_ln_g, sgu_ln_b, w_s, b_s, w_out, ln1_g, ln1_b,
           router_w, router_bias, w_gate, w_up, w_down, ws_gate, ws_up, ws_down, ln2_g, ln2_b, w_pe, w_pgate,
           b_pgate, alpha):
    B, T, D = xp.shape
    NS, TS, _ = xs.shape
    n_prompt, n_sample = B * T, NS * TS
    n = n_prompt + n_sample
    n_heads = w_s.shape[0]
    row = lambda v: v.reshape(1, -1)

    x_t = jnp.transpose(xs, (1, 0, 2))
    st_t = jnp.transpose(st, (1, 0, 2))
    bsb = jnp.broadcast_to(b_s[:, :, None], (n_heads, CHUNK, SGU_HEAD_DIM))
    wrow = jnp.repeat(jnp.transpose(w_s[:, :TS, :TS], (1, 2, 0)), SGU_HEAD_DIM, axis=2)
    brow = jnp.repeat(jnp.transpose(b_s[:, :TS], (1, 0)), SGU_HEAD_DIM, axis=1)
    h1_all, h1b_all, pool_p, a_s, vn_s, dest, wsel, cnt = _mixer(
        xp, x_t, st_t, w_in.astype(BF16), w_pool.astype(BF16), row(pool_scale), sgu_ln_g, sgu_ln_b, w_s, bsb,
        wrow, brow, w_out.astype(BF16), row(ln1_g), row(ln1_b), jnp.transpose(router_w),
        router_bias.reshape(-1, 1), alpha)
    nb = n // TB
    m_out_max = TB * TOP_K + N_EXPERTS * SEG
    rows_max = n * TOP_K + nb * N_EXPERTS * (SEGG - 1) + N_EXPERTS * (TR - SEGG)
    n_tiles_max = -(-rows_max // TR)
    meta = _sort_meta(cnt[:, :, 0], n_tiles_max, m_out_max)
    xsort, src_rep, w_rep = _dispatch(meta, dest, wsel, h1b_all, n_tiles_max * TR, m_out_max)
    ysort = _expert_ffn(meta, xsort, w_gate, w_up, w_down)

    nbs = TB // TS
    ps_t = jnp.transpose(ps.reshape(NS // nbs, nbs, TS, -1), (0, 2, 1, 3)).reshape(n_sample, -1)
    y_p, y_s = _combine(meta, src_rep, w_rep, h1_all, h1b_all, pp.reshape(n_prompt, -1), ps_t,
                        ws_gate.astype(BF16), ws_up.astype(BF16),
                        ws_down.astype(BF16), row(ln2_g), row(ln2_b), w_pgate.astype(BF16), row(b_pgate),
                        w_pe.astype(BF16), ysort, m_out_max, n_prompt, alpha)

    yp = y_p.reshape(B, T, D)
    ys = jnp.transpose(y_s.reshape(NS // nbs, TS, nbs, D), (0, 2, 1, 3)).reshape(NS, TS, D)
    new_pool_p = pool_p[:, HALO - POOL_BUF:, :]
    new_pool_s = jnp.concatenate([st, jnp.transpose(a_s, (1, 0, 2))], axis=1)[:, -POOL_BUF:]
    vn = jnp.transpose(vn_s, (1, 0, 2)).reshape(NS, TS, n_heads, SGU_HEAD_DIM)
    return yp, ys, new_pool_p, new_pool_s, vn


def kernel(x_prompt, x_sample, state_pool, p_prompt, p_sample, w_in, w_pool, pool_scale, sgu_ln_g, sgu_ln_b, w_s, b_s, w_out, ln1_g, ln1_b, router_w, router_bias, w_gate, w_up, w_down, ws_gate, ws_up, ws_down, ln2_g, ln2_b, w_pe, w_pgate, b_pgate):
    depth = w_in.shape[0]
    alpha = (2.0 * depth) ** 0.25
    hp, hs = x_prompt, x_sample
    pool_p, pool_s, v_s = [], [], []
    for i in range(depth):
        hp, hs, bp, bs, vs = _layer(
            hp, hs, state_pool[i], p_prompt[i], p_sample[i], w_in[i], w_pool[i], pool_scale[i], sgu_ln_g[i],
            sgu_ln_b[i], w_s[i], b_s[i], w_out[i], ln1_g[i], ln1_b[i], router_w[i], router_bias[i], w_gate[i],
            w_up[i], w_down[i], ws_gate[i], ws_up[i], ws_down[i], ln2_g[i], ln2_b[i], w_pe[i], w_pgate[i],
            b_pgate[i], alpha)
        pool_p.append(bp)
        pool_s.append(bs)
        v_s.append(vs)
    return hp, hs, jnp.stack(pool_p), jnp.stack(pool_s), jnp.stack(v_s)
```

```python
import functools

import jax
import jax.numpy as jnp
import numpy as np
from jax import lax
from jax.experimental import pallas as pl
from jax.experimental.pallas import tpu as pltpu

F32 = jnp.float32
BF16 = jnp.bfloat16
I32 = jnp.int32

POOL_WINDOWS = (2, 4, 8, 16)
POOL_BUF = max(POOL_WINDOWS) - 1
CHUNK = 128
SGU_HEAD_DIM = 128
N_EXPERTS = 64
TOP_K = 8
N_EXPERT_GROUPS = 8
TOPK_GROUPS = 4
ROUTE_SCALE = 2.5
LN_EPS = 1e-5
PAST_LEN = 16384

TB = 256
HALO = 16
TR = 512
TRC = 128
SEG = 16
SEGG = 8
COPY_ROWS = (32, 16, 8)
KC = 256
KCC = 512
VMEM_LIMIT = 56 * 1024 * 1024


def _const_spec(shape):
    n = len(shape)
    return pl.BlockSpec(shape, lambda *_: (0,) * n, pipeline_mode=pl.Buffered(1))


def _ln_rows(x, g, b):
    mu = jnp.mean(x, axis=-1, keepdims=True)
    xc = x - mu
    var = jnp.mean(xc * xc, axis=-1, keepdims=True)
    return xc * lax.rsqrt(var + LN_EPS) * g + b


def _dot(a, b):
    return jnp.dot(a, b, preferred_element_type=F32)


def _prompt_tile(j, n_tiles, x, proj, band_ref, wpool_ref, pscale_ref, sg_ref, sb_ref, ws_ref, bsb_ref,
                 pool_ref, zcat_ref, mixin_ref):
    d_pool = zcat_ref.shape[1]
    d_pool_g = d_pool // len(POOL_WINDOWS)
    d_sgu = mixin_ref.shape[1] - d_pool

    a = proj[:, :d_pool]
    zcat_ref[HALO:, :] = a.astype(BF16)

    pos1 = (j * TB + lax.broadcasted_iota(I32, (TB, 1), 0) + 1).astype(F32)
    for g, w in enumerate(POOL_WINDOWS):
        c0, c1 = g * d_pool_g, (g + 1) * d_pool_g
        s = _dot(band_ref[g], zcat_ref[:, c0:c1])
        cnt = jnp.minimum(pos1, float(w))
        d = s / cnt - a[:, c0:c1]
        y = _dot(d.astype(BF16), wpool_ref[g])
        mixin_ref[:, c0:c1] = (y * pscale_ref[:, c0:c1]).astype(BF16)
    zcat_ref[0:HALO, :] = jnp.where(j == n_tiles - 1, jnp.zeros((HALO, d_pool), BF16), zcat_ref[TB:TB + HALO, :])
    pool_ref[0] = a[TB - HALO:, :]

    u = jax.nn.gelu(proj[:, d_pool:d_pool + d_sgu])
    v = jax.nn.gelu(proj[:, d_pool + d_sgu:])
    r = lax.broadcasted_iota(I32, (CHUNK, CHUNK), 0)
    c = lax.broadcasted_iota(I32, (CHUNK, CHUNK), 1)
    tril = r >= c
    for h in range(d_sgu // SGU_HEAD_DIM):
        l0, l1 = h * SGU_HEAD_DIM, (h + 1) * SGU_HEAD_DIM
        vn = _ln_rows(v[:, l0:l1], sg_ref[h:h + 1, :], sb_ref[h:h + 1, :]).astype(BF16)
        wst = jnp.where(tril, ws_ref[h], 0.0).astype(BF16)
        n_ch = TB // CHUNK
        mixed_all = _dot(wst, jnp.concatenate([vn[ci * CHUNK:(ci + 1) * CHUNK] for ci in range(n_ch)], axis=1))
        for ci in range(n_ch):
            r0, r1 = ci * CHUNK, (ci + 1) * CHUNK
            mixed = mixed_all[:, ci * SGU_HEAD_DIM:(ci + 1) * SGU_HEAD_DIM] + bsb_ref[h]
            mixin_ref[r0:r1, d_pool + l0:d_pool + l1] = (u[r0:r1, l0:l1] * mixed).astype(BF16)


def _sample_tile(T, NB, proj, st_ref, wpool_ref, pscale_ref, sg_ref, sb_ref, wrow_ref, brow_ref,
                 a_ref, vn_ref, dbuf_ref, mixin_ref):
    d_pool = dbuf_ref.shape[1]
    d_pool_g = d_pool // len(POOL_WINDOWS)
    d_sgu = mixin_ref.shape[1] - d_pool

    a = proj[:, :d_pool]
    for t in range(T):
        a_ref[t] = a[t * NB:(t + 1) * NB]

    for g, w in enumerate(POOL_WINDOWS):
        c0, c1 = g * d_pool_g, (g + 1) * d_pool_g
        for t in range(T):
            cnt = float(min(PAST_LEN + t + 1, w))
            acc = None
            for s in range(POOL_BUF + t - w + 1, POOL_BUF + t + 1):
                if s < POOL_BUF:
                    term = st_ref[s, :, c0:c1]
                else:
                    term = a[(s - POOL_BUF) * NB:(s - POOL_BUF + 1) * NB, c0:c1]
                acc = term if acc is None else acc + term
            d = acc / cnt - a[t * NB:(t + 1) * NB, c0:c1]
            dbuf_ref[t * NB:(t + 1) * NB, c0:c1] = d.astype(BF16)
        y = _dot(dbuf_ref[0:T * NB, c0:c1], wpool_ref[g])
        mixin_ref[:, c0:c1] = (y * pscale_ref[:, c0:c1]).astype(BF16)

    u = jax.nn.gelu(proj[:, d_pool:d_pool + d_sgu])
    v = jax.nn.gelu(proj[:, d_pool + d_sgu:])
    for h in range(d_sgu // SGU_HEAD_DIM):
        l0, l1 = h * SGU_HEAD_DIM, (h + 1) * SGU_HEAD_DIM
        vn = _ln_rows(v[:, l0:l1], sg_ref[h:h + 1, :], sb_ref[h:h + 1, :])
        for t in range(T):
            vn_ref[t, :, l0:l1] = vn[t * NB:(t + 1) * NB]
    for t in range(T):
        mixed = brow_ref[t:t + 1, :]
        for s in range(t + 1):
            mixed = mixed + wrow_ref[t, s:s + 1, :] * vn_ref[s]
        mixin_ref[t * NB:(t + 1) * NB, d_pool:] = (u[t * NB:(t + 1) * NB] * mixed).astype(BF16)


def _mixer_body(alpha, n_pt, tiles_per_seq, xp_ref, xs_ref, st_ref, win_ref, band_ref, wpool_ref, pscale_ref,
                sg_ref, sb_ref, ws_ref, bsb_ref, wrow_ref, brow_ref, wout_ref, g1_ref, b1_ref,
                rw_ref, rb_ref, upper_ref, lstrict_ref,
                h1_ref, h1b_ref, pool_ref, a_ref, vn_ref, dest_ref, wsel_ref, cnt_ref,
                zcat_ref, mixin_ref, hprev_ref):
    i = pl.program_id(0)
    n_tiles = pl.num_programs(0) - 1

    def route_previous_tile():
        h = hprev_ref[...]
        _route_tile(h, h.astype(BF16), rw_ref, rb_ref, upper_ref, lstrict_ref,
                    dest_ref, wsel_ref, cnt_ref)

    @pl.when(i == 0)
    def _():
        zcat_ref[0:HALO, :] = jnp.zeros((HALO, zcat_ref.shape[1]), BF16)
        hprev_ref[...] = jnp.zeros(hprev_ref.shape, F32)

    @pl.when(i < n_pt)
    def _():
        x = xp_ref[0]
        proj = _dot(x.astype(BF16), win_ref[...])
        route_previous_tile()
        _prompt_tile(i % tiles_per_seq, tiles_per_seq, x, proj, band_ref, wpool_ref, pscale_ref, sg_ref, sb_ref,
                     ws_ref, bsb_ref, pool_ref, zcat_ref, mixin_ref)
        h1_ref[...] = alpha * x

    @pl.when(i >= n_pt)
    def _():
        route_previous_tile()

    @pl.when((i >= n_pt) & (i < n_tiles))
    def _():
        T, NB, D = xs_ref.shape
        x = xs_ref[...].reshape(T * NB, D)
        proj = _dot(x.astype(BF16), win_ref[...])
        _sample_tile(T, NB, proj, st_ref, wpool_ref, pscale_ref, sg_ref, sb_ref, wrow_ref, brow_ref,
                     a_ref, vn_ref, zcat_ref, mixin_ref)
        h1_ref[...] = alpha * x

    @pl.when(i < n_tiles)
    def _():
        mix = _dot(mixin_ref[...], wout_ref[...])
        h1 = _ln_rows(h1_ref[...] + mix, g1_ref[...], b1_ref[...])
        h1_ref[...] = h1
        h1b_ref[...] = h1.astype(BF16)
        hprev_ref[...] = h1


def _pool_band(tm):
    t = np.arange(tm)[:, None] + HALO
    s = np.arange(tm + HALO)[None, :]
    return jnp.asarray(np.stack([((s <= t) & (s >= t - w + 1)) for w in POOL_WINDOWS]).astype(np.float32), BF16)


def _mixer(xp, xs_t, st_t, win_b, wpool_b, pscale, sg, sb, ws, bsb, wrow, brow, wout_b, g1, b1, rwt, rb_col, alpha):
    B, T, D = xp.shape
    TS, NS, _ = xs_t.shape
    d_pool = st_t.shape[2]
    d_sgu = wrow.shape[2]
    tps = T // TB
    n_pt = B * tps
    NB = TB // TS
    n_st = NS // NB
    n = B * T + NS * TS
    nblk = n_pt + n_st
    pt = lambda i: jnp.minimum(i, n_pt - 1)
    stile = lambda i: jnp.clip(i - n_pt, 0, n_st - 1)
    blk = lambda i: jnp.minimum(i, nblk - 1)
    routed = lambda i: jnp.maximum(i - 1, 0)
    body = functools.partial(_mixer_body, alpha, n_pt, tps)
    rw_hi = rwt.astype(BF16)
    rw_lo = (rwt - rw_hi.astype(F32)).astype(BF16)
    upper = jnp.asarray(np.triu(np.ones((TB, TB), np.float32), 1), BF16)
    lstrict = jnp.asarray(np.tril(np.ones((N_EXPERTS, N_EXPERTS), np.float32), -1), BF16)
    consts = (win_b, _pool_band(TB), wpool_b, pscale, sg, sb, ws, bsb, wrow, brow, wout_b, g1, b1,
              jnp.concatenate([rw_hi, rw_lo], axis=0), rb_col, upper, lstrict)
    return pl.pallas_call(
        body,
        grid=(nblk + 1,),
        in_specs=[
            pl.BlockSpec((1, TB, D), lambda i: (pt(i) // tps, pt(i) % tps, 0)),
            pl.BlockSpec((TS, NB, D), lambda i: (0, stile(i), 0), pipeline_mode=pl.Buffered(1)),
            pl.BlockSpec((POOL_BUF, NB, d_pool), lambda i: (0, stile(i), 0), pipeline_mode=pl.Buffered(1)),
        ] + [_const_spec(c.shape) for c in consts],
        out_specs=[
            pl.BlockSpec((TB, D), lambda i: (blk(i), 0)),
            pl.BlockSpec((TB, D), lambda i: (blk(i), 0)),
            pl.BlockSpec((1, HALO, d_pool), lambda i: (pt(i) // tps, 0, 0)),
            pl.BlockSpec((TS, NB, d_pool), lambda i: (0, stile(i), 0)),
            pl.BlockSpec((TS, NB, d_sgu), lambda i: (0, stile(i), 0)),
            pl.BlockSpec((1, TOP_K, TB), lambda i: (routed(i), 0, 0)),
            pl.BlockSpec((1, TOP_K, TB), lambda i: (routed(i), 0, 0)),
            pl.BlockSpec((1, N_EXPERTS, 128), lambda i: (routed(i), 0, 0)),
        ],
        out_shape=[
            jax.ShapeDtypeStruct((n, D), F32),
            jax.ShapeDtypeStruct((n, D), BF16),
            jax.ShapeDtypeStruct((B, HALO, d_pool), F32),
            jax.ShapeDtypeStruct((TS, NS, d_pool), F32),
            jax.ShapeDtypeStruct((TS, NS, d_sgu), F32),
            jax.ShapeDtypeStruct((nblk, TOP_K, TB), I32),
            jax.ShapeDtypeStruct((nblk, TOP_K, TB), F32),
            jax.ShapeDtypeStruct((nblk, N_EXPERTS, 128), I32),
        ],
        scratch_shapes=[
            pltpu.VMEM((TB + HALO, d_pool), BF16),
            pltpu.VMEM((TB, D), BF16),
            pltpu.VMEM((TB, D), F32),
        ],
        compiler_params=pltpu.CompilerParams(
            dimension_semantics=("arbitrary",), vmem_limit_bytes=VMEM_LIMIT),
        name="mixer",
    )(xp, xs_t, st_t, *consts)


def _route_tile(h1, h1b, rw_ref, rb_ref, upper_ref, lstrict_ref, dest_ref, wsel_ref, cnt_ref):
    per_group = N_EXPERTS // N_EXPERT_GROUPS
    neg = -jnp.inf
    nt = (((1,), (1,)), ((), ()))
    h_lo = (h1 - h1b.astype(F32)).astype(BF16)
    by_h_hi = lax.dot_general(rw_ref[...], h1b, nt, preferred_element_type=F32)
    logits_t = (by_h_hi[:N_EXPERTS] + by_h_hi[N_EXPERTS:]
                + lax.dot_general(rw_ref[0:N_EXPERTS, :], h_lo, nt, preferred_element_type=F32))
    s_t = jax.nn.sigmoid(logits_t)
    b_t = s_t + rb_ref[...]

    io_g = lax.broadcasted_iota(I32, (per_group, TB), 0)
    gs = []
    for g in range(N_EXPERT_GROUPS):
        xg = b_t[g * per_group:(g + 1) * per_group, :]
        m1 = jnp.max(xg, axis=0, keepdims=True)
        i1 = jnp.min(jnp.where(xg == m1, io_g, per_group), axis=0, keepdims=True)
        m2 = jnp.max(jnp.where(io_g == i1, neg, xg), axis=0, keepdims=True)
        gs.append(m1 + m2)
    masked = []
    for g in range(N_EXPERT_GROUPS):
        rank = jnp.zeros((1, TB), F32)
        for g2 in range(N_EXPERT_GROUPS):
            if g2 != g:
                ahead = (gs[g2] >= gs[g]) if g2 < g else (gs[g2] > gs[g])
                rank = rank + jnp.where(ahead, 1.0, 0.0)
        keep = rank < float(TOPK_GROUPS)
        masked.append(jnp.where(keep, b_t[g * per_group:(g + 1) * per_group, :], neg))
    xm = jnp.concatenate(masked, axis=0)

    io_e = lax.broadcasted_iota(I32, (N_EXPERTS, TB), 0)
    onehots = []
    sel = jnp.zeros((N_EXPERTS, TB), F32)
    for _ in range(TOP_K):
        m = jnp.max(xm, axis=0, keepdims=True)
        idx = jnp.min(jnp.where(xm == m, io_e, N_EXPERTS), axis=0, keepdims=True)
        oh = io_e == idx
        onehots.append(oh)
        sel = jnp.where(oh, 1.0, sel)
        xm = jnp.where(oh, neg, xm)

    ssel = sel * s_t
    denom = jnp.sum(ssel, axis=0, keepdims=True)
    comb = ssel / denom * ROUTE_SCALE

    rank_t = _dot(sel.astype(BF16), upper_ref[...])
    cnt = jnp.sum(sel, axis=1, keepdims=True)
    cnt_i = cnt.astype(I32)
    seg16 = ((cnt_i + (SEG - 1)) // SEG).astype(F32)
    off16 = _dot(lstrict_ref[...], jnp.broadcast_to(seg16, (N_EXPERTS, 128)).astype(BF16))
    d_t = rank_t + off16[:, 0:1] * float(SEG)
    for k in range(TOP_K):
        dest_ref[0, k:k + 1, :] = jnp.sum(jnp.where(onehots[k], d_t, 0.0), axis=0, keepdims=True).astype(I32)
        wsel_ref[0, k:k + 1, :] = jnp.sum(jnp.where(onehots[k], comb, 0.0), axis=0, keepdims=True)
    cnt_ref[0] = jnp.broadcast_to(cnt_i, (N_EXPERTS, 128))


N_LIST_REFS = 3 * len(COPY_ROWS)


def _start_block_copies(b, copy_lists, make_copy):
    for k, rows in enumerate(COPY_ROWS):
        table_l, table_g, counts = copy_lists[3 * k:3 * k + 3]
        width = table_l.shape[0] // counts.shape[0]
        n = counts[b]

        def start(j, table_l=table_l, table_g=table_g, width=width, rows=rows):
            idx = b * width + j
            make_copy(pl.multiple_of(table_l[idx], min(rows, SEG)), pl.multiple_of(table_g[idx], SEGG), rows).start()

        def eight(q, c, start=start):
            for u in range(8):
                start(8 * q + u)
            return c

        def one(j, c, start=start):
            start(j)
            return c

        lax.fori_loop(0, n // 8, eight, 0)
        lax.fori_loop(n // 8 * 8, n, one, 0)


def _wait_row_units(count, max_count, make_wait_copy):
    for k in range(int(max_count).bit_length()):
        @pl.when(((count >> k) & 1) == 1)
        def _():
            make_wait_copy(SEGG << k).wait()


def _dispatch_body(*refs):
    copy_lists = refs[:N_LIST_REFS]
    (units, nch, tail_g, tail_n, dest_ref, wsel_ref, h_ref, tokid_ref, ones_ref,
     xs_hbm, src_ref, wrow_ref, xloc_ref, zero_ref, sems, tail_sem) = refs[N_LIST_REFS:]
    b = pl.program_id(0)
    nb = pl.num_programs(0)
    max_units = xloc_ref.shape[1] // SEGG
    slot = b % 2

    def seg_copy(sl):
        return lambda l, g, rows: pltpu.make_async_copy(
            xloc_ref.at[sl, pl.ds(l, rows), :], xs_hbm.at[pl.ds(g, rows), :], sems.at[sl])

    def wait_block(count, sl):
        _wait_row_units(count, max_units, lambda rows: pltpu.make_async_copy(
            xloc_ref.at[sl, pl.ds(0, rows), :], xs_hbm.at[pl.ds(0, rows), :], sems.at[sl]))

    @pl.when(b >= 2)
    def _():
        wait_block(units[jnp.maximum(b - 2, 0)], slot)

    dest = dest_ref[0]
    wsel = wsel_ref[0]
    h = h_ref[...]
    src_ref[...] = jnp.zeros(src_ref.shape, F32)
    wrow_ref[...] = jnp.zeros(wrow_ref.shape, F32)

    row_in_chunk = lax.broadcasted_iota(I32, (KC, TB), 0).astype(F32).astype(BF16)
    wsel_b = wsel.astype(BF16)
    one = jnp.ones((1, TB), BF16)

    def chunk(c):
        r0 = pl.multiple_of(c * KC, KC)
        rel = (dest - r0).astype(F32).astype(BF16)
        sb = jnp.zeros((KC, TB), BF16)
        wm = jnp.zeros((KC, TB), BF16)
        for k in range(TOP_K):
            hit = rel[k:k + 1, :] == row_in_chunk
            sb = jnp.where(hit, one, sb)
            wm = jnp.where(hit, wsel_b[k:k + 1, :], wm)
        xloc_ref[slot, pl.ds(r0, KC), :] = _dot(sb, h).astype(BF16)
        src_ref[0, pl.ds(r0, KC), :] = _dot(sb, tokid_ref[...])
        wrow_ref[0, pl.ds(r0, KC), :] = _dot(wm, ones_ref[...])

    def chunk_pair(cp, carry):
        chunk(2 * cp)
        chunk(2 * cp + 1)
        return carry

    lax.fori_loop(0, nch[b] // 2, chunk_pair, 0)

    @pl.when(nch[b] % 2 == 1)
    def _():
        chunk(nch[b] - 1)

    _start_block_copies(b, copy_lists, seg_copy(slot))

    def tail_copy(g):
        return pltpu.make_async_copy(zero_ref, xs_hbm.at[pl.ds(g, SEGG), :], tail_sem)

    def tails(wait):
        def per_expert(e, carry):
            def per_granule(j, c):
                cp = tail_copy(pl.multiple_of(tail_g[e] + j * SEGG, SEGG))
                if wait:
                    cp.wait()
                else:
                    cp.start()
                return c
            lax.fori_loop(0, tail_n[e], per_granule, 0)
            return carry
        lax.fori_loop(0, N_EXPERTS, per_expert, 0)

    @pl.when(b == nb - 1)
    def _():
        zero_ref[...] = jnp.zeros(zero_ref.shape, BF16)
        tails(False)

        @pl.when(b >= 1)
        def _():
            wait_block(units[jnp.maximum(b - 1, 0)], 1 - slot)

        wait_block(units[b], slot)
        tails(True)


def _dispatch(meta, dest, wsel, h1b_all, n_rows_sorted, m_out_max):
    n, D = h1b_all.shape
    nb = n // TB
    tokid = jnp.asarray(np.broadcast_to(np.arange(TB, dtype=np.float32)[:, None], (TB, 128)), BF16)
    ones = jnp.ones((TB, 128), BF16)
    cs = lambda shape: pl.BlockSpec(shape, lambda i, *_: (0,) * len(shape), pipeline_mode=pl.Buffered(1))
    grid_spec = pltpu.PrefetchScalarGridSpec(
        num_scalar_prefetch=N_LIST_REFS + 4,
        grid=(nb,),
        in_specs=[
            pl.BlockSpec((1, TOP_K, TB), lambda i, *_: (i, 0, 0)),
            pl.BlockSpec((1, TOP_K, TB), lambda i, *_: (i, 0, 0)),
            pl.BlockSpec((TB, D), lambda i, *_: (i, 0)),
            cs(tokid.shape), cs(ones.shape),
        ],
        out_specs=[
            pl.BlockSpec(memory_space=pl.ANY),
            pl.BlockSpec((1, m_out_max, 128), lambda i, *_: (i, 0, 0)),
            pl.BlockSpec((1, m_out_max, 128), lambda i, *_: (i, 0, 0)),
        ],
        scratch_shapes=[
            pltpu.VMEM((2, m_out_max, D), BF16),
            pltpu.VMEM((SEGG, D), BF16),
            pltpu.SemaphoreType.DMA((2,)),
            pltpu.SemaphoreType.DMA(()),
        ],
    )
    return pl.pallas_call(
        _dispatch_body,
        grid_spec=grid_spec,
        out_shape=[
            jax.ShapeDtypeStruct((n_rows_sorted, D), BF16),
            jax.ShapeDtypeStruct((nb, m_out_max, 128), F32),
            jax.ShapeDtypeStruct((nb, m_out_max, 128), F32),
        ],
        compiler_params=pltpu.CompilerParams(
            dimension_semantics=("arbitrary",), vmem_limit_bytes=VMEM_LIMIT, has_side_effects=True),
        name="dispatch",
    )(*meta["copy_lists"], meta["units"], meta["nch"], meta["tail_g"], meta["tail_n"], dest, wsel, h1b_all, tokid,
      ones)


def _ffn_body(tile_e, tile_src, tile_rows, e_slot, e_next, x_ref, wg_hbm, wu_hbm, wd_hbm, y_ref,
              wg_f, wu_f, wd_f, wgu_b, wd_b, sems):
    i = pl.program_id(0)
    f = wg_f.shape[2]
    e = tile_e[i]
    e_prev = tile_e[jnp.maximum(i - 1, 0)]

    def weight_copies(expert, slot):
        return (pltpu.make_async_copy(wg_hbm.at[expert], wg_f.at[slot], sems.at[slot, 0]),
                pltpu.make_async_copy(wu_hbm.at[expert], wu_f.at[slot], sems.at[slot, 1]),
                pltpu.make_async_copy(wd_hbm.at[expert], wd_f.at[slot], sems.at[slot, 2]))

    @pl.when((i == 0) | (e != e_prev))
    def _():
        slot = e_slot[e]

        @pl.when(i == 0)
        def _():
            for cp in weight_copies(e, slot):
                cp.start()

        for cp in weight_copies(e, slot):
            cp.wait()
        nxt = e_next[e]

        @pl.when(nxt >= 0)
        def _():
            for cp in weight_copies(nxt, 1 - slot):
                cp.start(priority=1)

        wgu_b[:, :f] = wg_f[slot].astype(BF16)
        wgu_b[:, f:] = wu_f[slot].astype(BF16)
        wd_b[...] = wd_f[slot].astype(BF16)

    def swiglu(x):
        gu = _dot(x, wgu_b[...])
        act = (jax.nn.silu(gu[:, :f]) * gu[:, f:]).astype(BF16)
        return _dot(act, wd_b[...]).astype(BF16)

    for m in range(TRC, TR + 1, TRC):
        @pl.when(tile_rows[i] == m)
        def _(m=m):
            y_ref[0:m, :] = swiglu(x_ref[0:m, :])
            if m < TR:
                y_ref[m:, :] = jnp.zeros((TR - m, y_ref.shape[1]), BF16)


def _expert_ffn(meta, xs, w_gate, w_up, w_down):
    R, D = xs.shape
    E, _, f = w_gate.shape
    nt = R // TR
    grid_spec = pltpu.PrefetchScalarGridSpec(
        num_scalar_prefetch=5,
        grid=(nt,),
        in_specs=[
            pl.BlockSpec((TR, D), lambda i, te, ts, *_: (ts[i], 0)),
            pl.BlockSpec(memory_space=pl.ANY),
            pl.BlockSpec(memory_space=pl.ANY),
            pl.BlockSpec(memory_space=pl.ANY),
        ],
        out_specs=pl.BlockSpec((TR, D), lambda i, te, ts, *_: (ts[i], 0)),
        scratch_shapes=[
            pltpu.VMEM((2, D, f), F32),
            pltpu.VMEM((2, D, f), F32),
            pltpu.VMEM((2, f, D), F32),
            pltpu.VMEM((D, 2 * f), BF16),
            pltpu.VMEM((f, D), BF16),
            pltpu.SemaphoreType.DMA((2, 3)),
        ],
    )
    return pl.pallas_call(
        _ffn_body,
        grid_spec=grid_spec,
        out_shape=jax.ShapeDtypeStruct((R, D), BF16),
        compiler_params=pltpu.CompilerParams(
            dimension_semantics=("arbitrary",), vmem_limit_bytes=VMEM_LIMIT),
        name="expert_ffn",
    )(meta["tile_e"], meta["tile_src"], meta["tile_rows"], meta["e_slot"], meta["e_next"],
      xs, w_gate, w_up, w_down)


def _combine_body(alpha, n_pb, *refs):
    copy_lists = refs[:N_LIST_REFS]
    (chunk_units, nch, src_ref, wrow_ref, h1_ref, h1b_ref, pp_ref, ps_ref,
     wsg_ref, wsu_ref, wsd_ref, g2_ref, b2_ref, wpg_ref, bpg_ref, wpe_ref, ys_hbm,
     yp_ref, ysm_ref, yloc_ref, acc_ref, sems) = refs[N_LIST_REFS:]
    b = pl.program_id(0)
    n_chunks = nch[b]
    max_chunks = sems.shape[0]

    def seg_copy(l, g, rows):
        return pltpu.make_async_copy(ys_hbm.at[pl.ds(g, rows), :], yloc_ref.at[pl.ds(l, rows), :],
                                     sems.at[l // KCC])

    def fetch_block(blk):
        _start_block_copies(blk, copy_lists, seg_copy)

    @pl.when(b == 0)
    def _():
        yloc_ref[...] = jnp.zeros(yloc_ref.shape, BF16)
        fetch_block(b)

    hb = h1b_ref[...]
    sh = (jax.nn.silu(_dot(hb, wsg_ref[...])) * _dot(hb, wsu_ref[...])).astype(BF16)
    acc_ref[...] = _dot(sh, wsd_ref[...])

    def wait_chunk(c):
        def wait_rows(rows):
            return pltpu.make_async_copy(ys_hbm.at[pl.ds(0, rows), :], yloc_ref.at[pl.ds(0, rows), :], sems.at[c])

        _wait_row_units(chunk_units[b * max_chunks + c], KCC // SEGG, wait_rows)

    def accumulate(c, n):
        r0 = pl.multiple_of(c * KCC, KCC)
        src = src_ref[0, pl.ds(r0, n * KCC), :]
        w = wrow_ref[0, pl.ds(r0, n * KCC), :]
        lane = lax.broadcasted_iota(I32, src.shape, 1).astype(F32)
        wm = jnp.concatenate([jnp.where(src == lane + float(o), w, 0.0) for o in range(0, TB, src.shape[1])],
                             axis=1).astype(BF16)
        acc_ref[...] += lax.dot_general(wm, yloc_ref[pl.ds(r0, n * KCC), :],
                                        (((0,), (0,)), ((), ())), preferred_element_type=F32)

    def chunk_pair(q, carry):
        wait_chunk(2 * q)
        wait_chunk(2 * q + 1)
        accumulate(2 * q, 2)
        return carry

    lax.fori_loop(0, n_chunks // 2, chunk_pair, 0)

    @pl.when(n_chunks % 2 == 1)
    def _():
        wait_chunk(n_chunks - 1)
        accumulate(n_chunks - 1, 1)

    @pl.when(b + 1 < pl.num_programs(0))
    def _():
        fetch_block(b + 1)

    h2 = _ln_rows(alpha * h1_ref[...] + acc_ref[...], g2_ref[...], b2_ref[...])
    gate = jax.nn.sigmoid(_dot(h2.astype(BF16), wpg_ref[...]) + bpg_ref[...])
    p = jnp.where(b < n_pb, pp_ref[...], ps_ref[...])
    pe = _dot(p.astype(BF16), wpe_ref[...])
    y = h2 + gate * pe

    @pl.when(b < n_pb)
    def _():
        yp_ref[...] = y

    @pl.when(b >= n_pb)
    def _():
        ysm_ref[...] = y


def _combine(meta, src_rep, w_rep, h1_all, h1b_all, p_prompt, p_sample, wsg_b, wsu_b, wsd_b, g2, b2, wpg_b, bpg,
             wpe_b, ys, m_out_max, n_prompt, alpha):
    n, D = h1_all.shape
    nb = n // TB
    n_pb = n_prompt // TB
    d_pe = p_prompt.shape[1]
    cs = lambda shape: pl.BlockSpec(shape, lambda i, *_: (0,) * len(shape), pipeline_mode=pl.Buffered(1))
    grid_spec = pltpu.PrefetchScalarGridSpec(
        num_scalar_prefetch=N_LIST_REFS + 2,
        grid=(nb,),
        in_specs=[
            pl.BlockSpec((1, m_out_max, 128), lambda i, *_: (i, 0, 0)),
            pl.BlockSpec((1, m_out_max, 128), lambda i, *_: (i, 0, 0)),
            pl.BlockSpec((TB, D), lambda i, *_: (i, 0)),
            pl.BlockSpec((TB, D), lambda i, *_: (i, 0)),
            pl.BlockSpec((TB, d_pe), lambda i, *_: (jnp.minimum(i, n_pb - 1), 0)),
            pl.BlockSpec((TB, d_pe), lambda i, *_: (jnp.maximum(i - n_pb, 0), 0)),
            cs(wsg_b.shape), cs(wsu_b.shape), cs(wsd_b.shape), cs(g2.shape), cs(b2.shape),
            cs(wpg_b.shape), cs(bpg.shape), cs(wpe_b.shape),
            pl.BlockSpec(memory_space=pl.ANY),
        ],
        out_specs=[
            pl.BlockSpec((TB, D), lambda i, *_: (jnp.minimum(i, n_pb - 1), 0)),
            pl.BlockSpec((TB, D), lambda i, *_: (jnp.maximum(i - n_pb, 0), 0)),
        ],
        scratch_shapes=[
            pltpu.VMEM((m_out_max, D), BF16),
            pltpu.VMEM((TB, D), F32),
            pltpu.SemaphoreType.DMA((m_out_max // KCC,)),
        ],
    )
    return pl.pallas_call(
        functools.partial(_combine_body, alpha, n_pb),
        grid_spec=grid_spec,
        out_shape=[jax.ShapeDtypeStruct((n_prompt, D), F32), jax.ShapeDtypeStruct((n - n_prompt, D), F32)],
        compiler_params=pltpu.CompilerParams(
            dimension_semantics=("arbitrary",), vmem_limit_bytes=VMEM_LIMIT),
        name="combine",
    )(*meta["copy_lists"], meta["chunk_units"], meta["nchc"], src_rep, w_rep, h1_all, h1b_all, p_prompt, p_sample,
      wsg_b, wsu_b, wsd_b, g2, b2, wpg_b, bpg, wpe_b, ys)


def _sort_meta(cnt, n_tiles_max, m_out_max):
    segl = (cnt + (SEG - 1)) // SEG * SEG
    loc = jnp.cumsum(segl, axis=1) - segl
    n8 = (cnt + (SEGG - 1)) // SEGG
    segp = n8 * SEGG
    before = jnp.cumsum(segp, axis=0) - segp
    total = jnp.sum(segp, axis=0)
    total_c = (total + (TRC - 1)) // TRC * TRC
    total_p = (total + (TR - 1)) // TR * TR
    ends = jnp.cumsum(total_p)
    base = ends - total_p
    glob = base[None, :] + before
    n_valid = ends[-1] // TR
    tiles = jnp.arange(n_tiles_max, dtype=I32)
    tile_src = jnp.minimum(tiles, n_valid - 1)
    tile_e = jnp.minimum(jnp.sum((ends[None, :] <= (tile_src * TR)[:, None]).astype(I32), axis=1), N_EXPERTS - 1)
    left = jnp.sum(jnp.where(tile_e[:, None] == jnp.arange(N_EXPERTS, dtype=I32)[None, :],
                             (base + total_c)[None, :], 0), axis=1) - tile_src * TR
    tile_rows = jnp.where(tiles < n_valid, jnp.clip(left, 0, TR), 0)
    experts = jnp.arange(N_EXPERTS, dtype=I32)
    active = total_p > 0
    later = active[None, :] & (experts[None, :] > experts[:, None])
    e_next = jnp.min(jnp.where(later, experts[None, :], N_EXPERTS), axis=1)
    def copy_list(count, first_l, first_g, rows, width):
        end = jnp.cumsum(count, axis=1)
        start = end - count
        j = jnp.arange(width, dtype=I32)[None, :, None]
        in_seg = (start[:, None, :] <= j) & (j < end[:, None, :])
        step = (j - start[:, None, :]) * rows
        lst_l = jnp.sum(jnp.where(in_seg, first_l[:, None, :] + step, 0), axis=2)
        lst_g = jnp.sum(jnp.where(in_seg, first_g[:, None, :] + step, 0), axis=2)
        return lst_l, lst_g, end[:, -1], j[:, :, 0] < end[:, -1:]

    chunks = jnp.arange(m_out_max // KCC, dtype=I32)[None, :, None]
    chunk_units = 0
    lists = {}
    left_units, sent_rows, larger = n8, jnp.zeros_like(n8), None
    for rows in COPY_ROWS:
        per = rows // SEGG
        count = left_units // per
        width = m_out_max // rows if larger is None else N_EXPERTS * (larger // rows - 1)
        larger = rows
        lst_l, lst_g, lst_n, ok = copy_list(count, loc + sent_rows, glob + sent_rows, rows, width)
        chunk_units = chunk_units + jnp.sum(jnp.where(ok[:, None, :] & (lst_l[:, None, :] // KCC == chunks), per, 0), axis=2)
        lists[rows] = (lst_l.reshape(-1).astype(I32), lst_g.reshape(-1).astype(I32), lst_n.astype(I32))
        left_units, sent_rows = left_units - count * per, sent_rows + count * rows
    local_rows = jnp.sum(segl, axis=1)
    return {
        "copy_lists": tuple(a for rows in COPY_ROWS for a in lists[rows]),
        "units": jnp.sum(n8, axis=1).astype(I32),
        "chunk_units": chunk_units.reshape(-1).astype(I32),
        "nch": ((local_rows + (KC - 1)) // KC).astype(I32),
        "nchc": ((local_rows + (KCC - 1)) // KCC).astype(I32),
        "tail_g": (base + total).astype(I32),
        "tail_n": ((total_c - total) // SEGG).astype(I32),
        "tile_e": tile_e.astype(I32),
        "tile_src": tile_src.astype(I32),
        "tile_rows": tile_rows.astype(I32),
        "e_slot": ((jnp.cumsum(active.astype(I32)) - 1) % 2).astype(I32),
        "e_next": jnp.where(e_next < N_EXPERTS, e_next, -1).astype(I32),
    }


def _layer(xp, xs, st, pp, ps, w_in, w_pool, pool_scale, sgu_ln_g, sgu_ln_b, w_s, b_s, w_out, ln1_g, ln1_b,
           router_w, router_bias, w_gate, w_up, w_down, ws_gate, ws_up, ws_down, ln2_g, ln2_b, w_pe, w_pgate,
           b_pgate, alpha):
    B, T, D = xp.shape
    NS, TS, _ = xs.shape
    n_prompt, n_sample = B * T, NS * TS
    n = n_prompt + n_sample
    n_heads = w_s.shape[0]
    row = lambda v: v.reshape(1, -1)

    x_t = jnp.transpose(xs, (1, 0, 2))
    st_t = jnp.transpose(st, (1, 0, 2))
    bsb = jnp.broadcast_to(b_s[:, :, None], (n_heads, CHUNK, SGU_HEAD_DIM))
    wrow = jnp.repeat(jnp.transpose(w_s[:, :TS, :TS], (1, 2, 0)), SGU_HEAD_DIM, axis=2)
    brow = jnp.repeat(jnp.transpose(b_s[:, :TS], (1, 0)), SGU_HEAD_DIM, axis=1)
    h1_all, h1b_all, pool_p, a_s, vn_s, dest, wsel, cnt = _mixer(
        xp, x_t, st_t, w_in.astype(BF16), w_pool.astype(BF16), row(pool_scale), sgu_ln_g, sgu_ln_b, w_s, bsb,
        wrow, brow, w_out.astype(BF16), row(ln1_g), row(ln1_b), jnp.transpose(router_w),
        router_bias.reshape(-1, 1), alpha)
    nb = n // TB
    m_out_max = TB * TOP_K + N_EXPERTS * SEG
    rows_max = n * TOP_K + nb * N_EXPERTS * (SEGG - 1) + N_EXPERTS * (TR - SEGG)
    n_tiles_max = -(-rows_max // TR)
    meta = _sort_meta(cnt[:, :, 0], n_tiles_max, m_out_max)
    xsort, src_rep, w_rep = _dispatch(meta, dest, wsel, h1b_all, n_tiles_max * TR, m_out_max)
    ysort = _expert_ffn(meta, xsort, w_gate, w_up, w_down)

    nbs = TB // TS
    ps_t = jnp.transpose(ps.reshape(NS // nbs, nbs, TS, -1), (0, 2, 1, 3)).reshape(n_sample, -1)
    y_p, y_s = _combine(meta, src_rep, w_rep, h1_all, h1b_all, pp.reshape(n_prompt, -1), ps_t,
                        ws_gate.astype(BF16), ws_up.astype(BF16),
                        ws_down.astype(BF16), row(ln2_g), row(ln2_b), w_pgate.astype(BF16), row(b_pgate),
                        w_pe.astype(BF16), ysort, m_out_max, n_prompt, alpha)

    yp = y_p.reshape(B, T, D)
    ys = jnp.transpose(y_s.reshape(NS // nbs, TS, nbs, D), (0, 2, 1, 3)).reshape(NS, TS, D)
    new_pool_p = pool_p[:, HALO - POOL_BUF:, :]
    new_pool_s = jnp.concatenate([st, jnp.transpose(a_s, (1, 0, 2))], axis=1)[:, -POOL_BUF:]
    vn = jnp.transpose(vn_s, (1, 0, 2)).reshape(NS, TS, n_heads, SGU_HEAD_DIM)
    return yp, ys, new_pool_p, new_pool_s, vn


def kernel(x_prompt, x_sample, state_pool, p_prompt, p_sample, w_in, w_pool, pool_scale, sgu_ln_g, sgu_ln_b, w_s, b_s, w_out, ln1_g, ln1_b, router_w, router_bias, w_gate, w_up, w_down, ws_gate, ws_up, ws_down, ln2_g, ln2_b, w_pe, w_pgate, b_pgate):
    depth = w_in.shape[0]
    alpha = (2.0 * depth) ** 0.25
    hp, hs = x_prompt, x_sample
    pool_p, pool_s, v_s = [], [], []
    for i in range(depth):
        hp, hs, bp, bs, vs = _layer(
            hp, hs, state_pool[i], p_prompt[i], p_sample[i], w_in[i], w_pool[i], pool_scale[i], sgu_ln_g[i],
            sgu_ln_b[i], w_s[i], b_s[i], w_out[i], ln1_g[i], ln1_b[i], router_w[i], router_bias[i], w_gate[i],
            w_up[i], w_down[i], ws_gate[i], ws_up[i], ws_down[i], ln2_g[i], ln2_b[i], w_pe[i], w_pgate[i],
            b_pgate[i], alpha)
        pool_p.append(bp)
        pool_s.append(bs)
        v_s.append(vs)
    return hp, hs, jnp.stack(pool_p), jnp.stack(pool_s), jnp.stack(v_s)
```
